```python
import jax, jax.numpy as jnp
from jax import lax
import numpy as np

D_MODEL = 1024
BATCH = 16
SEQ = 256
DEPTH = 1
DEC_BATCH = 4
DEC_SEQ = 2048
PAST_LEN = 256

GRID_W = 64
CHUNK = GRID_W
HG_H = 8
HG_DK = D_MODEL // HG_H
HG_DV = D_MODEL // HG_H
HG_KW = HG_H * HG_DK
HG_VW = HG_H * HG_DV
GLA_H = 4
GLA_DK = D_MODEL // 2 // GLA_H
GLA_DV = D_MODEL // GLA_H
GLA_KW = GLA_H * GLA_DK
GLA_VW = GLA_H * GLA_DV
GLA_RANK = 16
GLA_GATE_NORM = 16.0
N_GROUPS = 4
EXPERTS_PER_GROUP = 8
N_EXPERTS = N_GROUPS * EXPERTS_PER_GROUP
TOP_K_IN_GROUP = 2
D_EXPERT = D_MODEL // 4
EPS = 1e-6
IN_SIZES = (HG_KW, HG_KW, HG_KW, HG_VW, HG_VW,
            GLA_KW, GLA_KW, GLA_VW, GLA_VW, GLA_RANK, GLA_RANK,
            D_MODEL, D_MODEL)
IN_WIDTH = sum(IN_SIZES)
IN_SPLIT_POINTS = tuple(int(v) for v in np.cumsum(IN_SIZES)[:-1])

kernel_name = "hybrid_hgrn2_gla_hmoe_diffusion_step"


def rmsnorm(x, g):
    xf = x.astype(jnp.float32)
    y = xf * lax.rsqrt(jnp.mean(xf * xf, axis=-1, keepdims=True) + EPS)
    return (y * g.astype(jnp.float32)).astype(x.dtype)


def gated_chunk_scan(q, k, v, log_a, s0, n_chunks):
    b, L, h, _ = q.shape
    dv = v.shape[-1]
    c = L // n_chunks

    def to_chunks(t):
        return t.astype(jnp.float32).reshape(b, n_chunks, c, h, t.shape[-1]).transpose(1, 0, 3, 2, 4)

    causal = jnp.tril(jnp.ones((c, c), dtype=bool))[:, :, None]

    def step(s, inp):
        qc, kc, vc, gc = inp
        cum = jnp.cumsum(gc, axis=2)
        last = cum[:, :, -1:, :]
        o_inter = jnp.einsum('bhtk,bhkv->bhtv', qc * jnp.exp(cum), s)
        diff = cum[:, :, :, None, :] - cum[:, :, None, :, :]
        decay = jnp.where(causal, jnp.exp(jnp.where(causal, diff, 0.0)), 0.0)
        scores = jnp.einsum('bhtk,bhsk,bhtsk->bhts', qc, kc, decay)
        o_intra = jnp.einsum('bhts,bhsv->bhtv', scores, vc)
        s_new = jnp.exp(last[:, :, 0, :, None]) * s + jnp.einsum('bhsk,bhsv->bhkv', kc * jnp.exp(last - cum), vc)
        return s_new, o_inter + o_intra

    s_fin, o = lax.scan(step, s0.astype(jnp.float32),
                        (to_chunks(q), to_chunks(k), to_chunks(v), to_chunks(log_a)))
    o = o.transpose(1, 0, 3, 2, 4).reshape(b, L, h, dv)
    return o, s_fin


def bidirectional_scan(q, k_f, k_b, v, g_f, g_b, s0, n_chunks):
    flip = lambda t: jnp.flip(t, axis=1)
    o_f, s_f = gated_chunk_scan(q, k_f, v, g_f, s0[:, 0], n_chunks)
    o_b, s_b = gated_chunk_scan(flip(q), flip(k_b), flip(v), flip(g_b), s0[:, 1], n_chunks)
    return o_f + flip(o_b), jnp.stack([s_f, s_b], axis=1)


def head_norm_gate(o, gate, g, dtype):
    y = o * lax.rsqrt(jnp.mean(o * o, axis=-1, keepdims=True) + EPS) * g.astype(jnp.float32)
    b, L = o.shape[:2]
    return (y.reshape(b, L, -1) * jax.nn.silu(gate.astype(jnp.float32))).astype(dtype)


def token_mixer(h, s0_hg, s0_gla, n_chunks, lb, w_in, hg_norm_g, gla_wa_up, gla_ba, gla_norm_g,
                w_br_a, w_br_b, w_out):
    b, L, _ = h.shape
    proj = h @ w_in
    (hq, hf_f, hf_b, hi, hgate, gq, gk, gv, gr, ga_f, ga_b, m_a, m_b) = jnp.split(proj, IN_SPLIT_POINTS, axis=-1)
    heads = lambda t, n: t.reshape(b, L, n, -1)
    f_f = lb + (1.0 - lb) * jax.nn.sigmoid(hf_f.astype(jnp.float32))
    f_b = lb + (1.0 - lb) * jax.nn.sigmoid(hf_b.astype(jnp.float32))
    o_hg, st_hg = bidirectional_scan(heads(hq, HG_H) * HG_DK ** -0.5,
                                     heads(1.0 - f_f, HG_H), heads(1.0 - f_b, HG_H), heads(hi, HG_H),
                                     heads(jnp.log(f_f), HG_H), heads(jnp.log(f_b), HG_H), s0_hg, n_chunks)
    log_a_f = jax.nn.log_sigmoid((ga_f @ gla_wa_up[0] + gla_ba[0]).astype(jnp.float32)) / GLA_GATE_NORM
    log_a_b = jax.nn.log_sigmoid((ga_b @ gla_wa_up[1] + gla_ba[1]).astype(jnp.float32)) / GLA_GATE_NORM
    kg = heads(gk, GLA_H)
    o_gla, st_gla = bidirectional_scan(heads(gq, GLA_H) * GLA_DK ** -0.5, kg, kg, heads(gv, GLA_H),
                                       heads(log_a_f, GLA_H), heads(log_a_b, GLA_H), s0_gla, n_chunks)
    y_hg = head_norm_gate(o_hg, hgate, hg_norm_g, h.dtype)
    y_gla = head_norm_gate(o_gla, gr, gla_norm_g, h.dtype)
    merged = jax.nn.sigmoid(m_a) * (y_hg @ w_br_a) + jax.nn.sigmoid(m_b) * (y_gla @ w_br_b)
    return merged @ w_out, st_hg, st_gla


def hierarchical_moe(h, w_router_group, w_router_expert, w_exp_gate, w_exp_up, w_exp_down):
    b, L, d = h.shape
    t = h.reshape(-1, d)
    p_group = jax.nn.softmax((t @ w_router_group).astype(jnp.float32), axis=-1)
    g_idx = jnp.argmax(p_group, axis=-1)
    p_top = jnp.take_along_axis(p_group, g_idx[:, None], axis=-1)
    exp_logits = jnp.einsum('nd,gde->nge', t, w_router_expert).astype(jnp.float32)
    sel = jnp.take_along_axis(exp_logits, g_idx[:, None, None], axis=1)[:, 0]
    w_top, e_idx = lax.top_k(jax.nn.softmax(sel, axis=-1), TOP_K_IN_GROUP)
    w_top = w_top / jnp.sum(w_top, axis=-1, keepdims=True) * p_top
    expert_id = g_idx[:, None] * EXPERTS_PER_GROUP + e_idx
    combine = jnp.sum(jax.nn.one_hot(expert_id, N_EXPERTS, dtype=jnp.float32) * w_top[..., None], axis=1)
    combine = combine.astype(t.dtype)
    out = jnp.zeros_like(t)
    for g in range(N_GROUPS):
        sl = slice(g * EXPERTS_PER_GROUP, (g + 1) * EXPERTS_PER_GROUP)
        a = jnp.einsum('nd,edf->nef', t, w_exp_gate[sl])
        u = jnp.einsum('nd,edf->nef', t, w_exp_up[sl])
        hid = jax.nn.silu(a) * u * combine[:, sl, None]
        out = out + jnp.einsum('nef,efd->nd', hid, w_exp_down[sl])
    return out.reshape(b, L, d)


def trunk_layer(x, cond, s0_hg, s0_gla, n_chunks, lb, ada_w, ada_b, norm1_g, norm2_g, w_in, hg_norm_g,
                gla_wa_up, gla_ba, gla_norm_g, w_br_a, w_br_b, w_out,
                w_router_group, w_router_expert, w_exp_gate, w_exp_up, w_exp_down):
    mod = (jax.nn.silu(cond) @ ada_w + ada_b).reshape(-1, 1, 6 * D_MODEL)
    shift1, scale1, gate1, shift2, scale2, gate2 = jnp.split(mod, 6, axis=-1)
    h = rmsnorm(x, norm1_g) * (1.0 + scale1) + shift1
    mix, st_hg, st_gla = token_mixer(h, s0_hg, s0_gla, n_chunks, lb, w_in, hg_norm_g, gla_wa_up, gla_ba,
                                     gla_norm_g, w_br_a, w_br_b, w_out)
    x = x + gate1 * mix
    h2 = rmsnorm(x, norm2_g) * (1.0 + scale2) + shift2
    x = x + gate2 * hierarchical_moe(h2, w_router_group, w_router_expert, w_exp_gate, w_exp_up, w_exp_down)
    return x, st_hg, st_gla


def setup_inputs(seed: int = 0) -> dict:
    key = jax.random.key(seed)
    ks = jax.random.split(key, 26)
    nrm = lambda k, shape, scale: jax.random.normal(k, shape, jnp.float32) * scale
    D = D_MODEL
    return {
        "x_prompt": nrm(ks[0], (BATCH, SEQ, D), 1.0),
        "x_sample": nrm(ks[1], (DEC_BATCH, DEC_SEQ, D), 1.0),
        "state_hgrn": nrm(ks[2], (DEC_BATCH, DEPTH, 2, HG_H, HG_DK, HG_DV), 0.1),
        "state_gla": nrm(ks[3], (DEC_BATCH, DEPTH, 2, GLA_H, GLA_DK, GLA_DV), 0.1),
        "c": nrm(ks[4], (DEC_BATCH, D), 1.0),
        "c_ctx": nrm(ks[5], (D,), 1.0),
        "ada_w": nrm(ks[6], (DEPTH, D, 6 * D), D ** -0.5),
        "ada_b": nrm(ks[7], (DEPTH, 6 * D), 0.01),
        "norm1_g": 1.0 + nrm(ks[8], (DEPTH, D), 0.1),
        "norm2_g": 1.0 + nrm(ks[9], (DEPTH, D), 0.1),
        "w_in": nrm(ks[10], (DEPTH, D, IN_WIDTH), D ** -0.5),
        "hg_lb_param": nrm(ks[11], (DEPTH + 1, HG_KW), 1.0),
        "hg_norm_g": 1.0 + nrm(ks[12], (DEPTH, HG_DV), 0.1),
        "gla_wa_up": nrm(ks[13], (DEPTH, 2, GLA_RANK, GLA_KW), GLA_RANK ** -0.5),
        "gla_ba": 1.0 + nrm(ks[14], (DEPTH, 2, GLA_KW), 0.5),
        "gla_norm_g": 1.0 + nrm(ks[15], (DEPTH, GLA_DV), 0.1),
        "w_br_a": nrm(ks[16], (DEPTH, HG_VW, D), HG_VW ** -0.5),
        "w_br_b": nrm(ks[17], (DEPTH, GLA_VW, D), GLA_VW ** -0.5),
        "w_out": nrm(ks[18], (DEPTH, D, D), D ** -0.5),
        "w_router_group": nrm(ks[19], (DEPTH, D, N_GROUPS), D ** -0.5),
        "w_router_expert": nrm(ks[20], (DEPTH, N_GROUPS, D, EXPERTS_PER_GROUP), D ** -0.5),
        "w_exp_gate": nrm(ks[21], (DEPTH, N_EXPERTS, D, D_EXPERT), D ** -0.5),
        "w_exp_up": nrm(ks[22], (DEPTH, N_EXPERTS, D, D_EXPERT), D ** -0.5),
        "w_exp_down": nrm(ks[23], (DEPTH, N_EXPERTS, D_EXPERT, D), D_EXPERT ** -0.5),
        "final_norm_g": 1.0 + nrm(ks[24], (D,), 0.1),
    }


def reference(x_prompt, x_sample, state_hgrn, state_gla, c, c_ctx, ada_w, ada_b, norm1_g, norm2_g, w_in,
              hg_lb_param, hg_norm_g, gla_wa_up, gla_ba, gla_norm_g, w_br_a, w_br_b, w_out,
              w_router_group, w_router_expert, w_exp_gate, w_exp_up, w_exp_down, final_norm_g):
    bp, lp, _ = x_prompt.shape
    ls = x_sample.shape[1]
    ctx_chunks = lp // CHUNK
    rows = ls // GRID_W
    lb_all = jnp.cumsum(jax.nn.softmax(hg_lb_param.astype(jnp.float32), axis=0), axis=0)
    zero_hg = jnp.zeros((bp, 2, HG_H, HG_DK, HG_DV), jnp.float32)
    zero_gla = jnp.zeros((bp, 2, GLA_H, GLA_DK, GLA_DV), jnp.float32)
    hp, hs = x_prompt, x_sample
    new_hg, new_gla = [], []
    for l in range(DEPTH):
        lp_args = (lb_all[l], ada_w[l], ada_b[l], norm1_g[l], norm2_g[l], w_in[l], hg_norm_g[l], gla_wa_up[l],
                   gla_ba[l], gla_norm_g[l], w_br_a[l], w_br_b[l], w_out[l], w_router_group[l],
                   w_router_expert[l], w_exp_gate[l], w_exp_up[l], w_exp_down[l])
        hp, st_hg, st_gla = trunk_layer(hp, c_ctx, zero_hg, zero_gla, ctx_chunks, *lp_args)
        new_hg.append(st_hg)
        new_gla.append(st_gla)
        hs, _, _ = trunk_layer(hs, c, state_hgrn[:, l], state_gla[:, l], rows, *lp_args)
    y_prompt = rmsnorm(hp, final_norm_g)
    y_sample = rmsnorm(hs, final_norm_g)
    new_state_hgrn = jnp.stack(new_hg, axis=1)
    new_state_gla = jnp.stack(new_gla, axis=1)
    return (y_prompt, y_sample, new_state_hgrn, new_state_gla)
```

```python
import functools

import numpy as np
import jax
import jax.numpy as jnp
from jax import lax
from jax.experimental import pallas as pl
from jax.experimental.pallas import tpu as pltpu

F32 = jnp.float32
BF16 = jnp.bfloat16

D_MODEL = 1024
CHUNK = 64
N_LEVELS = 6
HG_H, HG_DK, HG_DV = 8, 128, 128
GLA_H, GLA_DK, GLA_DV = 4, 128, 256
GLA_RANK = 16
GLA_GATE_NORM = 16.0
N_GROUPS = 4
EXPERTS_PER_GROUP = 8
N_EXPERTS = N_GROUPS * EXPERTS_PER_GROUP
D_EXPERT = D_MODEL // 4
EPS = 1e-6
ROUTER_LANES = 128
EXPERT_LANE0 = N_GROUPS
VMEM_LIMIT = 56 * 1024 * 1024

TM_PROJ = 256
TM_POST = 256
TM_MOE = 1024


def _sigmoid(x):
    return 1.0 / (1.0 + jnp.exp(-x))


def _silu(x):
    return x * _sigmoid(x)


def _log_sigmoid(x):
    return jnp.minimum(x, 0.0) - jnp.log(1.0 + jnp.exp(-jnp.abs(x)))


def _rmsnorm(x, g):
    return x * lax.rsqrt(jnp.mean(x * x, axis=-1, keepdims=True) + EPS) * g


def _dot(a, b):
    return jnp.dot(a, b, preferred_element_type=F32)


def _dot_exact(a, b):
    return jnp.dot(a, b, preferred_element_type=F32, precision=lax.Precision.HIGHEST)


def _dot_nt(a, b):
    return lax.dot_general(a, b, (((1,), (1,)), ((), ())), preferred_element_type=F32)


def _dot_tn(a, b):
    return lax.dot_general(a, b, (((0,), (0,)), ((), ())), preferred_element_type=F32)


def _mod_body(c_ref, w_ref, b_ref, o_ref):
    o_ref[...] = _dot_exact(_silu(c_ref[...]), w_ref[...]) + b_ref[...]


def _modulation(cond, ada_w, ada_b):
    rows, n = cond.shape[0], ada_w.shape[1]
    tn = n // 4
    return pl.pallas_call(
        _mod_body,
        grid=(n // tn,),
        in_specs=[pl.BlockSpec((rows, D_MODEL), lambda j: (0, 0)),
                  pl.BlockSpec((D_MODEL, tn), lambda j: (0, j)),
                  pl.BlockSpec((1, tn), lambda j: (0, j))],
        out_specs=pl.BlockSpec((rows, tn), lambda j: (0, j)),
        out_shape=jax.ShapeDtypeStruct((rows, n), F32),
        compiler_params=pltpu.CompilerParams(vmem_limit_bytes=VMEM_LIMIT),
        name="modulation",
    )(cond, ada_w, ada_b)


_C_HQ, _C_HFF, _C_HFB, _C_HI, _C_HGATE = 0, 1024, 2048, 3072, 4096
_C_GQ, _C_GK, _C_GV, _C_GR, _C_MA, _C_MB = 5120, 5632, 6144, 7168, 8192, 9216
W_MAIN = 10240


def _inproj_body(x_ref, mod_ref, n1_ref, lbp_ref, wmain_ref, wga_ref, waup_ref, ba_ref,
                 hq_ref, gf_ref, gb_ref, hv_ref, hgate_ref, gq_ref, gk_ref, gv_ref, gr_ref,
                 laf_ref, lab_ref, sa_ref, sb_ref):
    x = x_ref[...]
    h = _rmsnorm(x, n1_ref[...]) * (1.0 + mod_ref[1:2, :]) + mod_ref[0:1, :]
    hb = h.astype(BF16)

    def proj(c0, w):
        return _dot(hb, wmain_ref[:, c0:c0 + w])

    p0, p1 = lbp_ref[0:1, :], lbp_ref[1:2, :]
    pm = jnp.maximum(p0, p1)
    e0, e1 = jnp.exp(p0 - pm), jnp.exp(p1 - pm)
    lb = e0 / (e0 + e1)

    hq_ref[...] = (proj(_C_HQ, 1024) * HG_DK ** -0.5).astype(BF16)
    gf_ref[...] = jnp.log(lb + (1.0 - lb) * _sigmoid(proj(_C_HFF, 1024)))
    gb_ref[...] = jnp.log(lb + (1.0 - lb) * _sigmoid(proj(_C_HFB, 1024)))
    hv_ref[...] = proj(_C_HI, 1024).astype(BF16)
    hgate_ref[...] = _silu(proj(_C_HGATE, 1024)).astype(BF16)
    gq_ref[...] = (proj(_C_GQ, 512) * GLA_DK ** -0.5).astype(BF16)
    gk_ref[...] = proj(_C_GK, 512).astype(BF16)
    gv_ref[...] = proj(_C_GV, 1024).astype(BF16)
    gr_ref[...] = _silu(proj(_C_GR, 1024)).astype(BF16)
    sa_ref[...] = _sigmoid(proj(_C_MA, 1024)).astype(BF16)
    sb_ref[...] = _sigmoid(proj(_C_MB, 1024)).astype(BF16)

    ga = _dot_exact(h, wga_ref[...])
    la = _log_sigmoid(_dot_exact(ga, waup_ref[...]) + ba_ref[...]) * (1.0 / GLA_GATE_NORM)
    laf_ref[...] = la[:, :512]
    lab_ref[...] = la[:, 512:]


def _inproj(x, mod, tiles_per_row, n1, lbp, wmain, wga, waup, ba):
    t = x.shape[0]
    tm = TM_PROJ
    const = lambda shape: pl.BlockSpec(shape, lambda i: (0,) * len(shape))
    tok = lambda w: pl.BlockSpec((tm, w), lambda i: (i, 0))
    widths = (1024, 1024, 1024, 1024, 1024, 512, 512, 1024, 1024, 512, 512, 1024, 1024)
    dtypes = (BF16, F32, F32, BF16, BF16, BF16, BF16, BF16, BF16, F32, F32, BF16, BF16)
    return pl.pallas_call(
        _inproj_body,
        grid=(t // tm,),
        in_specs=[tok(D_MODEL),
                  pl.BlockSpec((None, 6, D_MODEL), lambda i: (i // tiles_per_row, 0, 0)),
                  const((1, D_MODEL)), const((2, D_MODEL)),
                  pl.BlockSpec((D_MODEL, W_MAIN), lambda i: (0, 0), pipeline_mode=pl.Buffered(1)),
                  const((D_MODEL, 2 * GLA_RANK)), const((2 * GLA_RANK, 1024)), const((1, 1024))],
        out_specs=[tok(w) for w in widths],
        out_shape=[jax.ShapeDtypeStruct((t, w), dt) for w, dt in zip(widths, dtypes)],
        compiler_params=pltpu.CompilerParams(vmem_limit_bytes=VMEM_LIMIT),
        name="inproj",
    )(x, mod, n1, lbp, wmain, wga, waup, ba)


def _scan_constants():
    c = CHUNK
    m = np.zeros((N_LEVELS + 2, c, c), np.float32)
    for l in range(N_LEVELS):
        h = 1 << l
        for t in range(c):
            r = (t // (2 * h)) * 2 * h + h - 1
            if t > r:
                m[l, t, r + 1:t + 1] = 1.0
            else:
                m[l, t, t + 1:r + 1] = 1.0
    for t in range(c):
        m[N_LEVELS, t, :t + 1] = 1.0
        m[N_LEVELS + 1, t, t + 1:] = 1.0
    lvl = np.full((c, c), N_LEVELS + 1, np.int32)
    for t in range(c):
        lvl[t, t] = N_LEVELS
        for s in range(t):
            lvl[t, s] = int(np.floor(np.log2(t ^ s)))
    m_fwd = m.reshape(-1, c)
    m_bwd = m[:, ::-1, ::-1].reshape(-1, c)
    return (jnp.asarray(np.stack([m_fwd, m_bwd]), BF16), jnp.asarray(np.stack([lvl, lvl.T])))


def _scan_direction(d, q_ref, k_ref, v_ref, g_ref, m_ref, lvl_ref, st_ref, o_ref, n_heads, dk, dv):
    c = CHUNK
    m_all = m_ref[d]
    lvl = lvl_ref[d]
    row = lax.broadcasted_iota(jnp.int32, (c, dk), 0)
    for h in range(n_heads):
        ks = slice(h * dk, (h + 1) * dk)
        vs = slice(h * dv, (h + 1) * dv)
        g = g_ref[:, ks]
        g1 = g.astype(BF16)
        r1 = g - g1.astype(F32)
        g2 = r1.astype(BF16)
        g3 = (r1 - g2.astype(F32)).astype(BF16)
        ex = _dot(m_all, g1) + _dot(m_all, g2) + _dot(m_all, g3)
        e = jnp.exp(ex)
        qb = q_ref[:, ks]
        q = qb.astype(F32)
        if k_ref is None:
            k = 1.0 - jnp.exp(g)
        else:
            k = k_ref[:, ks].astype(F32)
        a = jnp.where(lvl == N_LEVELS, _dot_nt(qb, k.astype(BF16)), 0.0)
        for l in range(N_LEVELS):
            upper = (row & (1 << l)) != 0
            use_q = upper if d == 0 else jnp.logical_not(upper)
            p = (jnp.where(use_q, q, k) * e[l * c:(l + 1) * c]).astype(BF16)
            a = jnp.where(lvl == l, _dot_nt(p, p), a)
        pq = (q * e[N_LEVELS * c:(N_LEVELS + 1) * c]).astype(BF16)
        pk = (k * e[(N_LEVELS + 1) * c:(N_LEVELS + 2) * c]).astype(BF16)
        v = v_ref[:, vs]
        st = st_ref[d, h]
        o_ref[:, vs] = _dot(a.astype(BF16), v) + _dot_nt(pq, st.astype(BF16))
        edge = N_LEVELS * c + (c - 1 if d == 0 else 0)
        st_ref[d, h] = st * e[edge:edge + 1] + _dot_tn(v, pk)


def _scan_body(*refs, n_heads, dk, dv, derive_k, zero_init, emit_state):
    refs = list(refs)
    qf_ref, qb_ref = refs.pop(0), refs.pop(0)
    kf_ref = kb_ref = None
    if not derive_k:
        kf_ref, kb_ref = refs.pop(0), refs.pop(0)
    vf_ref, vb_ref, gf_ref, gb_ref, m_ref, lvl_ref = (refs.pop(0) for _ in range(6))
    s0_ref = None if zero_init else refs.pop(0)
    of_ref, ob_ref = refs.pop(0), refs.pop(0)
    sout_ref = refs.pop(0) if emit_state else None
    st_ref = refs.pop(0)
    n = pl.program_id(1)

    @pl.when(n == 0)
    def _():
        for d in range(2):
            for h in range(n_heads):
                st_ref[d, h] = jnp.zeros((dv, dk), F32) if zero_init else s0_ref[d, h].T

    _scan_direction(0, qf_ref, kf_ref, vf_ref, gf_ref, m_ref, lvl_ref, st_ref, of_ref, n_heads, dk, dv)
    _scan_direction(1, qb_ref, kb_ref, vb_ref, gb_ref, m_ref, lvl_ref, st_ref, ob_ref, n_heads, dk, dv)

    if emit_state:
        @pl.when(n == pl.num_programs(1) - 1)
        def _():
            for d in range(2):
                for h in range(n_heads):
                    sout_ref[d, h] = st_ref[d, h].T


def _scan(q, k, v, g_f, g_b, s0, consts, n_heads, dk, dv, emit_state):
    b, l, _ = q.shape
    n = l // CHUNK
    m_all, lvl = consts
    fwd = lambda w: pl.BlockSpec((None, CHUNK, w), lambda i, j: (i, j, 0))
    bwd = lambda w: pl.BlockSpec((None, CHUNK, w), lambda i, j: (i, n - 1 - j, 0))
    const = lambda shape: pl.BlockSpec(shape, lambda i, j: (0,) * len(shape))
    state = pl.BlockSpec((None, 2, n_heads, dk, dv), lambda i, j: (i, 0, 0, 0, 0))
    wk, wv = n_heads * dk, n_heads * dv
    args, in_specs = [q, q], [fwd(wk), bwd(wk)]
    if k is not None:
        args += [k, k]
        in_specs += [fwd(wk), bwd(wk)]
    args += [v, v, g_f, g_b, m_all, lvl]
    in_specs += [fwd(wv), bwd(wv), fwd(wk), bwd(wk), const(m_all.shape), const(lvl.shape)]
    if s0 is not None:
        args.append(s0)
        in_specs.append(state)
    out_specs = [fwd(wv), bwd(wv)]
    out_shape = [jax.ShapeDtypeStruct((b, l, wv), F32)] * 2
    if emit_state:
        out_specs.append(state)
        out_shape.append(jax.ShapeDtypeStruct((b, 2, n_heads, dk, dv), F32))
    body = functools.partial(_scan_body, n_heads=n_heads, dk=dk, dv=dv, derive_k=k is None,
                             zero_init=s0 is None, emit_state=emit_state)
    return pl.pallas_call(
        body,
        grid=(b, n),
        in_specs=in_specs,
        out_specs=out_specs,
        out_shape=out_shape,
        scratch_shapes=[pltpu.VMEM((2, n_heads, dv, dk), F32)],
        compiler_params=pltpu.CompilerParams(dimension_semantics=("arbitrary", "arbitrary"),
                                             vmem_limit_bytes=VMEM_LIMIT),
        name="scan_hgrn" if k is None else "scan_gla",
    )(*args)


def _post_body(ohf_ref, ohb_ref, ogf_ref, ogb_ref, hgate_ref, gr_ref, sa_ref, sb_ref, x_ref, mod_ref,
               hgn_ref, glan_ref, n2_ref, wa_ref, wb_ref, wo_ref, wr_ref,
               x1_ref, h2_ref, comb_ref, yh_ref, yg_ref):
    for h in range(HG_H):
        s = slice(h * HG_DV, (h + 1) * HG_DV)
        o = ohf_ref[:, s] + ohb_ref[:, s]
        yh_ref[:, s] = (_rmsnorm(o, hgn_ref[...]) * hgate_ref[:, s].astype(F32)).astype(BF16)
    for h in range(GLA_H):
        s = slice(h * GLA_DV, (h + 1) * GLA_DV)
        o = ogf_ref[:, s] + ogb_ref[:, s]
        yg_ref[:, s] = (_rmsnorm(o, glan_ref[...]) * gr_ref[:, s].astype(F32)).astype(BF16)
    merged = (sa_ref[...].astype(F32) * _dot(yh_ref[...], wa_ref[...])
              + sb_ref[...].astype(F32) * _dot(yg_ref[...], wb_ref[...]))
    mix = _dot(merged.astype(BF16), wo_ref[...])
    x1 = x_ref[...] + mod_ref[2:3, :] * mix
    x1_ref[...] = x1
    h2 = _rmsnorm(x1, n2_ref[...]) * (1.0 + mod_ref[4:5, :]) + mod_ref[3:4, :]
    h2_ref[...] = h2.astype(BF16)

    logits = _dot_exact(h2, wr_ref[...])
    lane = lax.broadcasted_iota(jnp.int32, logits.shape, 1)
    neg = -jnp.inf
    first = lambda hit: jnp.min(jnp.where(hit, lane, ROUTER_LANES), axis=-1, keepdims=True)
    gl = jnp.where(lane < N_GROUPS, logits, neg)
    gmax = jnp.max(gl, axis=-1, keepdims=True)
    p_top = 1.0 / jnp.sum(jnp.exp(gl - gmax), axis=-1, keepdims=True)
    g_idx = first(gl == gmax)
    in_group = ((lane >= EXPERT_LANE0) & (lane < EXPERT_LANE0 + N_EXPERTS)
                & (((lane - EXPERT_LANE0) >> 3) == g_idx))
    sl = jnp.where(in_group, logits, neg)
    m1 = jnp.max(sl, axis=-1, keepdims=True)
    i1 = first(sl == m1)
    sl2 = jnp.where(lane == i1, neg, sl)
    m2 = jnp.max(sl2, axis=-1, keepdims=True)
    i2 = first(sl2 == m2)
    r = jnp.exp(m2 - m1)
    w1 = p_top / (1.0 + r)
    comb_ref[...] = jnp.where(lane == i1, w1, jnp.where(lane == i2, w1 * r, 0.0))


def _post(ohf, ohb, ogf, ogb, hgate, gr, sa, sb, x, mod, tiles_per_row, hgn, glan, n2, wa, wb, wo, wr):
    t = x.shape[0]
    tm = TM_POST
    const = lambda shape: pl.BlockSpec(shape, lambda i: (0,) * len(shape))
    tok = lambda w: pl.BlockSpec((tm, w), lambda i: (i, 0))
    return pl.pallas_call(
        _post_body,
        grid=(t // tm,),
        in_specs=[tok(1024)] * 9
                 + [pl.BlockSpec((None, 6, D_MODEL), lambda i: (i // tiles_per_row, 0, 0)),
                    const((1, HG_DV)), const((1, GLA_DV)), const((1, D_MODEL)),
                    const((1024, 1024)), const((1024, 1024)), const((1024, 1024)),
                    const((D_MODEL, ROUTER_LANES))],
        out_specs=[tok(1024), tok(1024), tok(ROUTER_LANES)],
        out_shape=[jax.ShapeDtypeStruct((t, D_MODEL), F32), jax.ShapeDtypeStruct((t, D_MODEL), BF16),
                   jax.ShapeDtypeStruct((t, ROUTER_LANES), F32)],
        scratch_shapes=[pltpu.VMEM((tm, 1024), BF16), pltpu.VMEM((tm, 1024), BF16)],
        compiler_params=pltpu.CompilerParams(vmem_limit_bytes=VMEM_LIMIT),
        name="post",
    )(ohf, ohb, ogf, ogb, hgate, gr, sa, sb, x, mod, hgn, glan, n2, wa, wb, wo, wr)


def _moe_body(h2_ref, comb_ref, wg_ref, wu_ref, wd_ref, x1_ref, mod_ref, fn_ref, y_ref, acc_ref):
    e = pl.program_id(1)

    @pl.when(e == 0)
    def _():
        acc_ref[...] = jnp.zeros_like(acc_ref)

    h2 = h2_ref[...]
    a = _dot(h2, wg_ref[...])
    u = _dot(h2, wu_ref[...])
    comb = comb_ref[...]
    lane = lax.broadcasted_iota(jnp.int32, comb.shape, 1)
    col = jnp.sum(jnp.where(lane == e + EXPERT_LANE0, comb, 0.0), axis=-1, keepdims=True)
    hid = (_silu(a) * u * col).astype(BF16)
    acc_ref[...] += _dot(hid, wd_ref[...])

    @pl.when(e == pl.num_programs(1) - 1)
    def _():
        x2 = x1_ref[...] + mod_ref[5:6, :] * acc_ref[...]
        y_ref[...] = _rmsnorm(x2, fn_ref[...])


def _moe(h2, comb, wg, wu, wd, x1, mod, tiles_per_row, fn):
    t = h2.shape[0]
    tm = TM_MOE
    tok = lambda w: pl.BlockSpec((tm, w), lambda i, e: (i, 0))
    return pl.pallas_call(
        _moe_body,
        grid=(t // tm, N_EXPERTS),
        in_specs=[tok(D_MODEL), tok(ROUTER_LANES),
                  pl.BlockSpec((None, D_MODEL, D_EXPERT), lambda i, e: (e, 0, 0)),
                  pl.BlockSpec((None, D_MODEL, D_EXPERT), lambda i, e: (e, 0, 0)),
                  pl.BlockSpec((None, D_EXPERT, D_MODEL), lambda i, e: (e, 0, 0)),
                  tok(D_MODEL),
                  pl.BlockSpec((None, 6, D_MODEL), lambda i, e: (i // tiles_per_row, 0, 0)),
                  pl.BlockSpec((1, D_MODEL), lambda i, e: (0, 0))],
        out_specs=tok(D_MODEL),
        out_shape=jax.ShapeDtypeStruct((t, D_MODEL), F32),
        scratch_shapes=[pltpu.VMEM((tm, D_MODEL), F32)],
        compiler_params=pltpu.CompilerParams(dimension_semantics=("arbitrary", "arbitrary"),
                                             vmem_limit_bytes=VMEM_LIMIT),
        name="moe",
    )(h2, comb, wg, wu, wd, x1, mod, fn)


def _trunk_layer(x, mod, s0_hg, s0_gla, emit_state, w):
    b, l, _ = x.shape
    xt = x.reshape(b * l, D_MODEL)
    per_row = lambda tm: (l // tm) if mod.shape[0] > 1 else (b * l // tm)
    (hq, gf, gb, hv, hgate, gq, gk, gv, gr, laf, lab, sa, sb) = _inproj(
        xt, mod, per_row(TM_PROJ), w["n1"], w["lbp"], w["wmain"], w["wga"], w["waup"], w["ba"])
    seq = lambda a: a.reshape(b, l, a.shape[-1])
    hg = _scan(seq(hq), None, seq(hv), seq(gf), seq(gb), s0_hg, w["scan_consts"], HG_H, HG_DK, HG_DV,
               emit_state)
    gla = _scan(seq(gq), seq(gk), seq(gv), seq(laf), seq(lab), s0_gla, w["scan_consts"], GLA_H, GLA_DK,
                GLA_DV, emit_state)
    flat = lambda a: a.reshape(b * l, a.shape[-1])
    x1, h2, comb = _post(flat(hg[0]), flat(hg[1]), flat(gla[0]), flat(gla[1]), hgate, gr, sa, sb, xt, mod,
                         per_row(TM_POST), w["hgn"], w["glan"], w["n2"], w["wa"], w["wb"], w["wo"], w["wr"])
    y = _moe(h2, comb, w["wg"], w["wu"], w["wd"], x1, mod, per_row(TM_MOE), w["fn"])
    states = (hg[2], gla[2]) if emit_state else (None, None)
    return y.reshape(b, l, D_MODEL), states


def kernel(x_prompt, x_sample, state_hgrn, state_gla, c, c_ctx, ada_w, ada_b, norm1_g, norm2_g, w_in,
           hg_lb_param, hg_norm_g, gla_wa_up, gla_ba, gla_norm_g, w_br_a, w_br_b, w_out,
           w_router_group, w_router_expert, w_exp_gate, w_exp_up, w_exp_down, final_norm_g):
    nb = c.shape[0]
    cond = jnp.concatenate([c, c_ctx[None, :], jnp.zeros((8 - nb - 1, D_MODEL), F32)], axis=0)
    mod = _modulation(cond, ada_w[0], ada_b).reshape(8, 6, D_MODEL)

    w_in0 = w_in[0]
    ga0 = _C_MA
    zeros = jnp.zeros((GLA_RANK, GLA_H * GLA_DK), F32)
    router = jnp.concatenate(
        [w_router_group[0], jnp.transpose(w_router_expert[0], (1, 0, 2)).reshape(D_MODEL, N_EXPERTS),
         jnp.zeros((D_MODEL, ROUTER_LANES - N_GROUPS - N_EXPERTS), F32)], axis=1)
    w = {
        "n1": norm1_g, "n2": norm2_g, "fn": final_norm_g[None, :], "lbp": hg_lb_param,
        "wmain": jnp.concatenate([w_in0[:, :ga0], w_in0[:, ga0 + 2 * GLA_RANK:]], axis=1).astype(BF16),
        "wga": w_in0[:, ga0:ga0 + 2 * GLA_RANK],
        "waup": jnp.concatenate([jnp.concatenate([gla_wa_up[0, 0], zeros], axis=1),
                                 jnp.concatenate([zeros, gla_wa_up[0, 1]], axis=1)], axis=0),
        "ba": gla_ba[0].reshape(1, 2 * GLA_H * GLA_DK),
        "hgn": hg_norm_g, "glan": gla_norm_g,
        "wa": w_br_a[0].astype(BF16), "wb": w_br_b[0].astype(BF16), "wo": w_out[0].astype(BF16),
        "wr": router,
        "wg": w_exp_gate[0].astype(BF16), "wu": w_exp_up[0].astype(BF16), "wd": w_exp_down[0].astype(BF16),
        "scan_consts": _scan_constants(),
    }
    y_prompt, (st_hg, st_gla) = _trunk_layer(x_prompt, mod[nb:nb + 1], None, None, True, w)
    y_sample, _ = _trunk_layer(x_sample, mod[:nb], state_hgrn[:, 0], state_gla[:, 0], False, w)
    return y_prompt, y_sample, st_hg[:, None], st_gla[:, None]
```

```python
import functools

import numpy as np
import jax
import jax.numpy as jnp
from jax import lax
from jax.experimental import pallas as pl
from jax.experimental.pallas import tpu as pltpu

F32 = jnp.float32
BF16 = jnp.bfloat16

D_MODEL = 1024
CHUNK = 64
N_LEVELS = 6
HG_H, HG_DK, HG_DV = 8, 128, 128
GLA_H, GLA_DK, GLA_DV = 4, 128, 256
GLA_RANK = 16
GLA_GATE_NORM = 16.0
N_GROUPS = 4
EXPERTS_PER_GROUP = 8
N_EXPERTS = N_GROUPS * EXPERTS_PER_GROUP
D_EXPERT = D_MODEL // 4
EPS = 1e-6
ROUTER_LANES = 128
EXPERT_LANE0 = N_GROUPS
VMEM_LIMIT = 56 * 1024 * 1024

TM_PROJ = 256
TM_POST = 256
TM_MOE = 1024


def _sigmoid(x):
    return 1.0 / (1.0 + jnp.exp(-x))


def _silu(x):
    return x * _sigmoid(x)


def _log_sigmoid(x):
    return jnp.minimum(x, 0.0) - jnp.log(1.0 + jnp.exp(-jnp.abs(x)))


def _rmsnorm(x, g):
    return x * lax.rsqrt(jnp.mean(x * x, axis=-1, keepdims=True) + EPS) * g


def _dot(a, b):
    return jnp.dot(a, b, preferred_element_type=F32)


def _dot_exact(a, b):
    return jnp.dot(a, b, preferred_element_type=F32, precision=lax.Precision.HIGHEST)


def _dot_nt(a, b):
    return lax.dot_general(a, b, (((1,), (1,)), ((), ())), preferred_element_type=F32)


def _dot_tn(a, b):
    return lax.dot_general(a, b, (((0,), (0,)), ((), ())), preferred_element_type=F32)


def _mod_body(c_ref, w_ref, b_ref, o_ref):
    o_ref[...] = _dot_exact(_silu(c_ref[...]), w_ref[...]) + b_ref[...]


def _modulation(cond, ada_w, ada_b):
    rows, n = cond.shape[0], ada_w.shape[1]
    tn = n // 4
    return pl.pallas_call(
        _mod_body,
        grid=(n // tn,),
        in_specs=[pl.BlockSpec((rows, D_MODEL), lambda j: (0, 0)),
                  pl.BlockSpec((D_MODEL, tn), lambda j: (0, j)),
                  pl.BlockSpec((1, tn), lambda j: (0, j))],
        out_specs=pl.BlockSpec((rows, tn), lambda j: (0, j)),
        out_shape=jax.ShapeDtypeStruct((rows, n), F32),
        compiler_params=pltpu.CompilerParams(vmem_limit_bytes=VMEM_LIMIT),
        name="modulation",
    )(cond, ada_w, ada_b)


_C_HQ, _C_HFF, _C_HFB, _C_HI, _C_HGATE = 0, 1024, 2048, 3072, 4096
_C_GQ, _C_GK, _C_GV, _C_GR, _C_MA, _C_MB = 5120, 5632, 6144, 7168, 8192, 9216
W_MAIN = 10240


def _inproj_body(x_ref, mod_ref, n1_ref, lbp_ref, wmain_ref, wga_ref, waup_ref, ba_ref,
                 hq_ref, gf_ref, gb_ref, hv_ref, hgate_ref, gq_ref, gk_ref, gv_ref, gr_ref,
                 laf_ref, lab_ref, sa_ref, sb_ref):
    x = x_ref[...]
    h = _rmsnorm(x, n1_ref[...]) * (1.0 + mod_ref[1:2, :]) + mod_ref[0:1, :]
    hb = h.astype(BF16)

    def proj(c0, w):
        return _dot(hb, wmain_ref[:, c0:c0 + w])

    p0, p1 = lbp_ref[0:1, :], lbp_ref[1:2, :]
    pm = jnp.maximum(p0, p1)
    e0, e1 = jnp.exp(p0 - pm), jnp.exp(p1 - pm)
    lb = e0 / (e0 + e1)

    hq_ref[...] = (proj(_C_HQ, 1024) * HG_DK ** -0.5).astype(BF16)
    gf_ref[...] = jnp.log(lb + (1.0 - lb) * _sigmoid(proj(_C_HFF, 1024)))
    gb_ref[...] = jnp.log(lb + (1.0 - lb) * _sigmoid(proj(_C_HFB, 1024)))
    hv_ref[...] = proj(_C_HI, 1024).astype(BF16)
    hgate_ref[...] = _silu(proj(_C_HGATE, 1024)).astype(BF16)
    gq_ref[...] = (proj(_C_GQ, 512) * GLA_DK ** -0.5).astype(BF16)
    gk_ref[...] = proj(_C_GK, 512).astype(BF16)
    gv_ref[...] = proj(_C_GV, 1024).astype(BF16)
    gr_ref[...] = _silu(proj(_C_GR, 1024)).astype(BF16)
    sa_ref[...] = _sigmoid(proj(_C_MA, 1024)).astype(BF16)
    sb_ref[...] = _sigmoid(proj(_C_MB, 1024)).astype(BF16)

    ga = _dot_exact(h, wga_ref[...])
    la = _log_sigmoid(_dot_exact(ga, waup_ref[...]) + ba_ref[...]) * (1.0 / GLA_GATE_NORM)
    laf_ref[...] = la[:, :512]
    lab_ref[...] = la[:, 512:]


def _inproj(x, mod, tiles_per_row, n1, lbp, wmain, wga, waup, ba):
    t = x.shape[0]
    tm = TM_PROJ
    const = lambda shape: pl.BlockSpec(shape, lambda i: (0,) * len(shape))
    tok = lambda w: pl.BlockSpec((tm, w), lambda i: (i, 0))
    widths = (1024, 1024, 1024, 1024, 1024, 512, 512, 1024, 1024, 512, 512, 1024, 1024)
    dtypes = (BF16, F32, F32, BF16, BF16, BF16, BF16, BF16, BF16, F32, F32, BF16, BF16)
    return pl.pallas_call(
        _inproj_body,
        grid=(t // tm,),
        in_specs=[tok(D_MODEL),
                  pl.BlockSpec((None, 6, D_MODEL), lambda i: (i // tiles_per_row, 0, 0)),
                  const((1, D_MODEL)), const((2, D_MODEL)),
                  pl.BlockSpec((D_MODEL, W_MAIN), lambda i: (0, 0), pipeline_mode=pl.Buffered(1)),
                  const((D_MODEL, 2 * GLA_RANK)), const((2 * GLA_RANK, 1024)), const((1, 1024))],
        out_specs=[tok(w) for w in widths],
        out_shape=[jax.ShapeDtypeStruct((t, w), dt) for w, dt in zip(widths, dtypes)],
        compiler_params=pltpu.CompilerParams(vmem_limit_bytes=VMEM_LIMIT),
        name="inproj",
    )(x, mod, n1, lbp, wmain, wga, waup, ba)


N_MXU_LEVELS = 2
ROW_GROUP = 16
START_SLOT, END_SLOT = N_LEVELS, N_LEVELS + 1
LOG2E = 1.4426950408889634


def _boundary_row(t, h, d):
    base = (t // (2 * h)) * 2 * h
    return base + h - 1 if d == 0 else base + h


def _scan_constants():
    c = CHUNK
    tri = np.tril(np.ones((c, c), np.float32))
    m = [tri] + [tri[[_boundary_row(t, 1 << l, 0) for t in range(c)]] for l in range(N_MXU_LEVELS)]
    m = np.stack(m)
    lvl = np.full((c, c), N_LEVELS + 1, np.int32)
    for t in range(c):
        lvl[t, t] = N_LEVELS
        for s in range(t):
            lvl[t, s] = int(np.floor(np.log2(t ^ s)))
    m_fwd = m.reshape(-1, c)
    m_bwd = m[:, ::-1, ::-1].reshape(-1, c)
    return (jnp.asarray(np.stack([m_fwd, m_bwd]), BF16), jnp.asarray(np.stack([lvl, lvl.T])))


def _scan_prepare(d, q_ref, k_ref, g_ref, m_ref, cb_ref, qf_ref, kf_ref, p_ref):
    c = CHUNK
    w = g_ref.shape[-1]
    g = g_ref[...]
    g1 = g.astype(BF16)
    r1 = g - g1.astype(F32)
    g2 = r1.astype(BF16)
    g3 = (r1 - g2.astype(F32)).astype(BF16)
    m = m_ref[d]
    cb_ref[d] = (_dot(m, g1) + _dot(m, g2) + _dot(m, g3)) * LOG2E
    qf_ref[d] = q_ref[...].astype(F32)
    kf_ref[d] = (1.0 - jnp.exp(g)) if k_ref is None else k_ref[...].astype(F32)
    bcast = lambda r, n: jnp.broadcast_to(cb_ref[d, r:r + 1, :], (n, w))
    for i in range(c // ROW_GROUP):
        t0 = ROW_GROUP * i
        rows = slice(t0, t0 + ROW_GROUP)
        cum = cb_ref[d, rows, :]
        q = qf_ref[d, rows, :]
        k = kf_ref[d, rows, :]
        rowid = t0 + lax.broadcasted_iota(jnp.int32, (ROW_GROUP, w), 0)
        for l in range(N_LEVELS):
            h = 1 << l
            if l < N_MXU_LEVELS:
                bnd = cb_ref[d, c * (l + 1) + t0:c * (l + 1) + t0 + ROW_GROUP, :]
            else:
                r0, r1 = _boundary_row(t0, h, d), _boundary_row(t0 + 8, h, d)
                bnd = bcast(r0, ROW_GROUP) if r0 == r1 else jnp.concatenate([bcast(r0, 8), bcast(r1, 8)], axis=0)
            e = jnp.exp2(-jnp.abs(cum - bnd))
            if h >= ROW_GROUP:
                src = q if ((t0 & h) != 0) == (d == 0) else k
            else:
                upper = (rowid & h) != 0
                src = jnp.where(upper if d == 0 else jnp.logical_not(upper), q, k)
            p_ref[d, l, rows, :] = (src * e).astype(BF16)
        p_ref[d, START_SLOT, rows, :] = (q * jnp.exp2(cum)).astype(BF16)
        edge = c - 1 if d == 0 else 0
        p_ref[d, END_SLOT, rows, :] = (k * jnp.exp2(-jnp.abs(cum - bcast(edge, ROW_GROUP)))).astype(BF16)


def _scan_scores(d, q_ref, lvl_ref, kf_ref, p_ref, a_ref, n_heads, dk):
    lvl = lvl_ref[d]
    for h in range(n_heads):
        ks = slice(h * dk, (h + 1) * dk)
        a = jnp.where(lvl == N_LEVELS, _dot_nt(q_ref[:, ks], kf_ref[d, :, ks].astype(BF16)), 0.0)
        for l in range(N_LEVELS):
            p = p_ref[d, l, :, ks]
            a = jnp.where(lvl == l, _dot_nt(p, p), a)
        a_ref[d, h] = a.astype(BF16)


def _scan_outputs(d, v_ref, cb_ref, p_ref, a_ref, st_ref, o_ref, n_heads, dk, dv):
    edge = CHUNK - 1 if d == 0 else 0
    for h in range(n_heads):
        ks = slice(h * dk, (h + 1) * dk)
        vs = slice(h * dv, (h + 1) * dv)
        v = v_ref[:, vs]
        st = st_ref[d, h]
        o_ref[:, vs] = _dot(a_ref[d, h], v) + _dot_nt(p_ref[d, START_SLOT, :, ks], st.astype(BF16))
        st_ref[d, h] = st * jnp.exp2(cb_ref[d, edge:edge + 1, ks]) + _dot_tn(v, p_ref[d, END_SLOT, :, ks])


def _scan_body(*refs, n_heads, dk, dv, derive_k, zero_init, emit_state):
    refs = list(refs)
    q_refs = (refs.pop(0), refs.pop(0))
    k_refs = (None, None) if derive_k else (refs.pop(0), refs.pop(0))
    v_refs = (refs.pop(0), refs.pop(0))
    g_refs = (refs.pop(0), refs.pop(0))
    m_ref, lvl_ref = refs.pop(0), refs.pop(0)
    s0_ref = None if zero_init else refs.pop(0)
    o_refs = (refs.pop(0), refs.pop(0))
    sout_ref = refs.pop(0) if emit_state else None
    st_ref, cb_ref, qf_ref, kf_ref, p_ref, a_ref = refs
    n = pl.program_id(1)

    @pl.when(n == 0)
    def _():
        for d in range(2):
            for h in range(n_heads):
                st_ref[d, h] = jnp.zeros((dv, dk), F32) if zero_init else s0_ref[d, h].T

    for d in range(2):
        _scan_prepare(d, q_refs[d], k_refs[d], g_refs[d], m_ref, cb_ref, qf_ref, kf_ref, p_ref)
    for d in range(2):
        _scan_scores(d, q_refs[d], lvl_ref, kf_ref, p_ref, a_ref, n_heads, dk)
    for d in range(2):
        _scan_outputs(d, v_refs[d], cb_ref, p_ref, a_ref, st_ref, o_refs[d], n_heads, dk, dv)

    if emit_state:
        @pl.when(n == pl.num_programs(1) - 1)
        def _():
            for d in range(2):
                for h in range(n_heads):
                    sout_ref[d, h] = st_ref[d, h].T


def _scan(q, k, v, g_f, g_b, s0, consts, n_heads, dk, dv, emit_state):
    b, l, _ = q.shape
    n = l // CHUNK
    m_all, lvl = consts
    fwd = lambda w: pl.BlockSpec((None, CHUNK, w), lambda i, j: (i, j, 0))
    bwd = lambda w: pl.BlockSpec((None, CHUNK, w), lambda i, j: (i, n - 1 - j, 0))
    const = lambda shape: pl.BlockSpec(shape, lambda i, j: (0,) * len(shape))
    state = pl.BlockSpec((None, 2, n_heads, dk, dv), lambda i, j: (i, 0, 0, 0, 0))
    wk, wv = n_heads * dk, n_heads * dv
    args, in_specs = [q, q], [fwd(wk), bwd(wk)]
    if k is not None:
        args += [k, k]
        in_specs += [fwd(wk), bwd(wk)]
    args += [v, v, g_f, g_b, m_all, lvl]
    in_specs += [fwd(wv), bwd(wv), fwd(wk), bwd(wk), const(m_all.shape), const(lvl.shape)]
    if s0 is not None:
        args.append(s0)
        in_specs.append(state)
    out_specs = [fwd(wv), bwd(wv)]
    out_shape = [jax.ShapeDtypeStruct((b, l, wv), F32)] * 2
    if emit_state:
        out_specs.append(state)
        out_shape.append(jax.ShapeDtypeStruct((b, 2, n_heads, dk, dv), F32))
    body = functools.partial(_scan_body, n_heads=n_heads, dk=dk, dv=dv, derive_k=k is None,
                             zero_init=s0 is None, emit_state=emit_state)
    return pl.pallas_call(
        body,
        grid=(b, n),
        in_specs=in_specs,
        out_specs=out_specs,
        out_shape=out_shape,
        scratch_shapes=[pltpu.VMEM((2, n_heads, dv, dk), F32),
                        pltpu.VMEM((2, (1 + N_MXU_LEVELS) * CHUNK, wk), F32),
                        pltpu.VMEM((2, CHUNK, wk), F32), pltpu.VMEM((2, CHUNK, wk), F32),
                        pltpu.VMEM((2, N_LEVELS + 2, CHUNK, wk), BF16),
                        pltpu.VMEM((2, n_heads, CHUNK, CHUNK), BF16)],
        compiler_params=pltpu.CompilerParams(dimension_semantics=("arbitrary", "arbitrary"),
                                             vmem_limit_bytes=VMEM_LIMIT),
        name="scan_hgrn" if k is None else "scan_gla",
    )(*args)


def _post_body(ohf_ref, ohb_ref, ogf_ref, ogb_ref, hgate_ref, gr_ref, sa_ref, sb_ref, x_ref, mod_ref,
               hgn_ref, glan_ref, n2_ref, wa_ref, wb_ref, wo_ref, wr_ref,
               x1_ref, h2_ref, comb_ref, yh_ref, yg_ref):
    for h in range(HG_H):
        s = slice(h * HG_DV, (h + 1) * HG_DV)
        o = ohf_ref[:, s] + ohb_ref[:, s]
        yh_ref[:, s] = (_rmsnorm(o, hgn_ref[...]) * hgate_ref[:, s].astype(F32)).astype(BF16)
    for h in range(GLA_H):
        s = slice(h * GLA_DV, (h + 1) * GLA_DV)
        o = ogf_ref[:, s] + ogb_ref[:, s]
        yg_ref[:, s] = (_rmsnorm(o, glan_ref[...]) * gr_ref[:, s].astype(F32)).astype(BF16)
    merged = (sa_ref[...].astype(F32) * _dot(yh_ref[...], wa_ref[...])
              + sb_ref[...].astype(F32) * _dot(yg_ref[...], wb_ref[...]))
    mix = _dot(merged.astype(BF16), wo_ref[...])
    x1 = x_ref[...] + mod_ref[2:3, :] * mix
    x1_ref[...] = x1
    h2 = _rmsnorm(x1, n2_ref[...]) * (1.0 + mod_ref[4:5, :]) + mod_ref[3:4, :]
    h2_ref[...] = h2.astype(BF16)

    logits = _dot_exact(h2, wr_ref[...])
    lane = lax.broadcasted_iota(jnp.int32, logits.shape, 1)
    neg = -jnp.inf
    first = lambda hit: jnp.min(jnp.where(hit, lane, ROUTER_LANES), axis=-1, keepdims=True)
    gl = jnp.where(lane < N_GROUPS, logits, neg)
    gmax = jnp.max(gl, axis=-1, keepdims=True)
    p_top = 1.0 / jnp.sum(jnp.exp(gl - gmax), axis=-1, keepdims=True)
    g_idx = first(gl == gmax)
    in_group = ((lane >= EXPERT_LANE0) & (lane < EXPERT_LANE0 + N_EXPERTS)
                & (((lane - EXPERT_LANE0) >> 3) == g_idx))
    sl = jnp.where(in_group, logits, neg)
    m1 = jnp.max(sl, axis=-1, keepdims=True)
    i1 = first(sl == m1)
    sl2 = jnp.where(lane == i1, neg, sl)
    m2 = jnp.max(sl2, axis=-1, keepdims=True)
    i2 = first(sl2 == m2)
    r = jnp.exp(m2 - m1)
    w1 = p_top / (1.0 + r)
    comb_ref[...] = jnp.where(lane == i1, w1, jnp.where(lane == i2, w1 * r, 0.0))


def _post(ohf, ohb, ogf, ogb, hgate, gr, sa, sb, x, mod, tiles_per_row, hgn, glan, n2, wa, wb, wo, wr):
    t = x.shape[0]
    tm = TM_POST
    const = lambda shape: pl.BlockSpec(shape, lambda i: (0,) * len(shape))
    tok = lambda w: pl.BlockSpec((tm, w), lambda i: (i, 0))
    return pl.pallas_call(
        _post_body,
        grid=(t // tm,),
        in_specs=[tok(1024)] * 9
                 + [pl.BlockSpec((None, 6, D_MODEL), lambda i: (i // tiles_per_row, 0, 0)),
                    const((1, HG_DV)), const((1, GLA_DV)), const((1, D_MODEL)),
                    const((1024, 1024)), const((1024, 1024)), const((1024, 1024)),
                    const((D_MODEL, ROUTER_LANES))],
        out_specs=[tok(1024), tok(1024), tok(ROUTER_LANES)],
        out_shape=[jax.ShapeDtypeStruct((t, D_MODEL), F32), jax.ShapeDtypeStruct((t, D_MODEL), BF16),
                   jax.ShapeDtypeStruct((t, ROUTER_LANES), F32)],
        scratch_shapes=[pltpu.VMEM((tm, 1024), BF16), pltpu.VMEM((tm, 1024), BF16)],
        compiler_params=pltpu.CompilerParams(vmem_limit_bytes=VMEM_LIMIT),
        name="post",
    )(ohf, ohb, ogf, ogb, hgate, gr, sa, sb, x, mod, hgn, glan, n2, wa, wb, wo, wr)


def _moe_body(h2_ref, comb_ref, wg_ref, wu_ref, wd_ref, x1_ref, mod_ref, fn_ref, y_ref, acc_ref):
    e = pl.program_id(1)

    @pl.when(e == 0)
    def _():
        acc_ref[...] = jnp.zeros_like(acc_ref)

    h2 = h2_ref[...]
    a = _dot(h2, wg_ref[...])
    u = _dot(h2, wu_ref[...])
    comb = comb_ref[...]
    lane = lax.broadcasted_iota(jnp.int32, comb.shape, 1)
    col = jnp.sum(jnp.where(lane == e + EXPERT_LANE0, comb, 0.0), axis=-1, keepdims=True)
    hid = (_silu(a) * u * col).astype(BF16)
    acc_ref[...] += _dot(hid, wd_ref[...])

    @pl.when(e == pl.num_programs(1) - 1)
    def _():
        x2 = x1_ref[...] + mod_ref[5:6, :] * acc_ref[...]
        y_ref[...] = _rmsnorm(x2, fn_ref[...])


def _moe(h2, comb, wg, wu, wd, x1, mod, tiles_per_row, fn):
    t = h2.shape[0]
    tm = TM_MOE
    tok = lambda w: pl.BlockSpec((tm, w), lambda i, e: (i, 0))
    return pl.pallas_call(
        _moe_body,
        grid=(t // tm, N_EXPERTS),
        in_specs=[tok(D_MODEL), tok(ROUTER_LANES),
                  pl.BlockSpec((None, D_MODEL, D_EXPERT), lambda i, e: (e, 0, 0)),
                  pl.BlockSpec((None, D_MODEL, D_EXPERT), lambda i, e: (e, 0, 0)),
                  pl.BlockSpec((None, D_EXPERT, D_MODEL), lambda i, e: (e, 0, 0)),
                  tok(D_MODEL),
                  pl.BlockSpec((None, 6, D_MODEL), lambda i, e: (i // tiles_per_row, 0, 0)),
                  pl.BlockSpec((1, D_MODEL), lambda i, e: (0, 0))],
        out_specs=tok(D_MODEL),
        out_shape=jax.ShapeDtypeStruct((t, D_MODEL), F32),
        scratch_shapes=[pltpu.VMEM((tm, D_MODEL), F32)],
        compiler_params=pltpu.CompilerParams(dimension_semantics=("arbitrary", "arbitrary"),
                                             vmem_limit_bytes=VMEM_LIMIT),
        name="moe",
    )(h2, comb, wg, wu, wd, x1, mod, fn)


def _trunk_layer(x, mod, s0_hg, s0_gla, emit_state, w):
    b, l, _ = x.shape
    xt = x.reshape(b * l, D_MODEL)
    per_row = lambda tm: (l // tm) if mod.shape[0] > 1 else (b * l // tm)
    (hq, gf, gb, hv, hgate, gq, gk, gv, gr, laf, lab, sa, sb) = _inproj(
        xt, mod, per_row(TM_PROJ), w["n1"], w["lbp"], w["wmain"], w["wga"], w["waup"], w["ba"])
    seq = lambda a: a.reshape(b, l, a.shape[-1])
    hg = _scan(seq(hq), None, seq(hv), seq(gf), seq(gb), s0_hg, w["scan_consts"], HG_H, HG_DK, HG_DV,
               emit_state)
    gla = _scan(seq(gq), seq(gk), seq(gv), seq(laf), seq(lab), s0_gla, w["scan_consts"], GLA_H, GLA_DK,
                GLA_DV, emit_state)
    flat = lambda a: a.reshape(b * l, a.shape[-1])
    x1, h2, comb = _post(flat(hg[0]), flat(hg[1]), flat(gla[0]), flat(gla[1]), hgate, gr, sa, sb, xt, mod,
                         per_row(TM_POST), w["hgn"], w["glan"], w["n2"], w["wa"], w["wb"], w["wo"], w["wr"])
    y = _moe(h2, comb, w["wg"], w["wu"], w["wd"], x1, mod, per_row(TM_MOE), w["fn"])
    states = (hg[2], gla[2]) if emit_state else (None, None)
    return y.reshape(b, l, D_MODEL), states


def kernel(x_prompt, x_sample, state_hgrn, state_gla, c, c_ctx, ada_w, ada_b, norm1_g, norm2_g, w_in,
           hg_lb_param, hg_norm_g, gla_wa_up, gla_ba, gla_norm_g, w_br_a, w_br_b, w_out,
           w_router_group, w_router_expert, w_exp_gate, w_exp_up, w_exp_down, final_norm_g):
    nb = c.shape[0]
    cond = jnp.concatenate([c, c_ctx[None, :], jnp.zeros((8 - nb - 1, D_MODEL), F32)], axis=0)
    mod = _modulation(cond, ada_w[0], ada_b).reshape(8, 6, D_MODEL)

    w_in0 = w_in[0]
    ga0 = _C_MA
    zeros = jnp.zeros((GLA_RANK, GLA_H * GLA_DK), F32)
    router = jnp.concatenate(
        [w_router_group[0], jnp.transpose(w_router_expert[0], (1, 0, 2)).reshape(D_MODEL, N_EXPERTS),
         jnp.zeros((D_MODEL, ROUTER_LANES - N_GROUPS - N_EXPERTS), F32)], axis=1)
    w = {
        "n1": norm1_g, "n2": norm2_g, "fn": final_norm_g[None, :], "lbp": hg_lb_param,
        "wmain": jnp.concatenate([w_in0[:, :ga0], w_in0[:, ga0 + 2 * GLA_RANK:]], axis=1).astype(BF16),
        "wga": w_in0[:, ga0:ga0 + 2 * GLA_RANK],
        "waup": jnp.concatenate([jnp.concatenate([gla_wa_up[0, 0], zeros], axis=1),
                                 jnp.concatenate([zeros, gla_wa_up[0, 1]], axis=1)], axis=0),
        "ba": gla_ba[0].reshape(1, 2 * GLA_H * GLA_DK),
        "hgn": hg_norm_g, "glan": gla_norm_g,
        "wa": w_br_a[0].astype(BF16), "wb": w_br_b[0].astype(BF16), "wo": w_out[0].astype(BF16),
        "wr": router,
        "wg": w_exp_gate[0].astype(BF16), "wu": w_exp_up[0].astype(BF16), "wd": w_exp_down[0].astype(BF16),
        "scan_consts": _scan_constants(),
    }
    y_prompt, (st_hg, st_gla) = _trunk_layer(x_prompt, mod[nb:nb + 1], None, None, True, w)
    y_sample, _ = _trunk_layer(x_sample, mod[:nb], state_hgrn[:, 0], state_gla[:, 0], False, w)
    return y_prompt, y_sample, st_hg[:, None], st_gla[:, None]
```

```python
import functools

import numpy as np
import jax
import jax.numpy as jnp
from jax import lax
from jax.experimental import pallas as pl
from jax.experimental.pallas import tpu as pltpu

F32 = jnp.float32
BF16 = jnp.bfloat16

D_MODEL = 1024
CHUNK = 64
N_LEVELS = 6
HG_H, HG_DK, HG_DV = 8, 128, 128
GLA_H, GLA_DK, GLA_DV = 4, 128, 256
GLA_RANK = 16
GLA_GATE_NORM = 16.0
N_GROUPS = 4
EXPERTS_PER_GROUP = 8
N_EXPERTS = N_GROUPS * EXPERTS_PER_GROUP
D_EXPERT = D_MODEL // 4
EPS = 1e-6
ROUTER_LANES = 128
EXPERT_LANE0 = N_GROUPS
VMEM_LIMIT = 56 * 1024 * 1024

TM_PROJ = 256
TM_POST = 256
TM_MOE = 1024


def _sigmoid(x):
    return 1.0 / (1.0 + jnp.exp(-x))


def _silu(x):
    return x * _sigmoid(x)


def _log_sigmoid(x):
    return jnp.minimum(x, 0.0) - jnp.log(1.0 + jnp.exp(-jnp.abs(x)))


def _rmsnorm(x, g):
    return x * lax.rsqrt(jnp.mean(x * x, axis=-1, keepdims=True) + EPS) * g


def _dot(a, b):
    return jnp.dot(a, b, preferred_element_type=F32)


def _dot_exact(a, b):
    return jnp.dot(a, b, preferred_element_type=F32, precision=lax.Precision.HIGHEST)


def _dot_nt(a, b):
    return lax.dot_general(a, b, (((1,), (1,)), ((), ())), preferred_element_type=F32)


def _dot_tn(a, b):
    return lax.dot_general(a, b, (((0,), (0,)), ((), ())), preferred_element_type=F32)


def _mod_body(c_ref, w_ref, b_ref, o_ref):
    o_ref[...] = _dot_exact(_silu(c_ref[...]), w_ref[...]) + b_ref[...]


def _modulation(cond, ada_w, ada_b):
    rows, n = cond.shape[0], ada_w.shape[1]
    tn = n // 4
    return pl.pallas_call(
        _mod_body,
        grid=(n // tn,),
        in_specs=[pl.BlockSpec((rows, D_MODEL), lambda j: (0, 0)),
                  pl.BlockSpec((D_MODEL, tn), lambda j: (0, j)),
                  pl.BlockSpec((1, tn), lambda j: (0, j))],
        out_specs=pl.BlockSpec((rows, tn), lambda j: (0, j)),
        out_shape=jax.ShapeDtypeStruct((rows, n), F32),
        compiler_params=pltpu.CompilerParams(vmem_limit_bytes=VMEM_LIMIT),
        name="modulation",
    )(cond, ada_w, ada_b)


_C_HQ, _C_HFF, _C_HFB, _C_HI, _C_HGATE = 0, 1024, 2048, 3072, 4096
_C_GQ, _C_GK, _C_GV, _C_GR, _C_MA, _C_MB = 5120, 5632, 6144, 7168, 8192, 9216
_C_GA = 10240
GA_LANES = 128
W_MAIN = _C_GA + GA_LANES


def _inproj_body(x_ref, mod_ref, n1_ref, lbp_ref, wmain_ref, waup_ref, ba_ref,
                 hq_ref, gf_ref, gb_ref, hv_ref, hgate_ref, gq_ref, gk_ref, gv_ref, gr_ref,
                 laf_ref, lab_ref, sa_ref, sb_ref):
    x = x_ref[...]
    h = _rmsnorm(x, n1_ref[...]) * (1.0 + mod_ref[1:2, :]) + mod_ref[0:1, :]
    hb = h.astype(BF16)

    def proj(c0, w):
        return _dot(hb, wmain_ref[:, c0:c0 + w])

    p0, p1 = lbp_ref[0:1, :], lbp_ref[1:2, :]
    pm = jnp.maximum(p0, p1)
    e0, e1 = jnp.exp(p0 - pm), jnp.exp(p1 - pm)
    lb = e0 / (e0 + e1)

    hq_ref[...] = (proj(_C_HQ, 1024) * HG_DK ** -0.5).astype(BF16)
    gf_ref[...] = jnp.log(lb + (1.0 - lb) * _sigmoid(proj(_C_HFF, 1024)))
    gb_ref[...] = jnp.log(lb + (1.0 - lb) * _sigmoid(proj(_C_HFB, 1024)))
    hv_ref[...] = proj(_C_HI, 1024).astype(BF16)
    hgate_ref[...] = _silu(proj(_C_HGATE, 1024)).astype(BF16)
    gq_ref[...] = (proj(_C_GQ, 512) * GLA_DK ** -0.5).astype(BF16)
    gk_ref[...] = proj(_C_GK, 512).astype(BF16)
    gv_ref[...] = proj(_C_GV, 1024).astype(BF16)
    gr_ref[...] = _silu(proj(_C_GR, 1024)).astype(BF16)
    sa_ref[...] = _sigmoid(proj(_C_MA, 1024)).astype(BF16)
    sb_ref[...] = _sigmoid(proj(_C_MB, 1024)).astype(BF16)

    ga = proj(_C_GA, GA_LANES).astype(BF16)
    la = _log_sigmoid(_dot(ga, waup_ref[...]) + ba_ref[...]) * (1.0 / GLA_GATE_NORM)
    laf_ref[...] = la[:, :512]
    lab_ref[...] = la[:, 512:]


def _inproj(x, mod, tiles_per_row, n1, lbp, wmain, waup, ba):
    t = x.shape[0]
    tm = TM_PROJ
    const = lambda shape: pl.BlockSpec(shape, lambda i: (0,) * len(shape))
    tok = lambda w: pl.BlockSpec((tm, w), lambda i: (i, 0))
    widths = (1024, 1024, 1024, 1024, 1024, 512, 512, 1024, 1024, 512, 512, 1024, 1024)
    dtypes = (BF16, F32, F32, BF16, BF16, BF16, BF16, BF16, BF16, F32, F32, BF16, BF16)
    return pl.pallas_call(
        _inproj_body,
        grid=(t // tm,),
        in_specs=[tok(D_MODEL),
                  pl.BlockSpec((None, 6, D_MODEL), lambda i: (i // tiles_per_row, 0, 0)),
                  const((1, D_MODEL)), const((2, D_MODEL)),
                  pl.BlockSpec((D_MODEL, W_MAIN), lambda i: (0, 0), pipeline_mode=pl.Buffered(1)),
                  const((GA_LANES, 1024)), const((1, 1024))],
        out_specs=[tok(w) for w in widths],
        out_shape=[jax.ShapeDtypeStruct((t, w), dt) for w, dt in zip(widths, dtypes)],
        compiler_params=pltpu.CompilerParams(vmem_limit_bytes=VMEM_LIMIT),
        name="inproj",
    )(x, mod, n1, lbp, wmain, waup, ba)


N_MXU_LEVELS = 2
ROW_GROUP = 16
START_SLOT, END_SLOT = N_LEVELS, N_LEVELS + 1
LOG2E = 1.4426950408889634


def _boundary_row(t, h, d):
    base = (t // (2 * h)) * 2 * h
    return base + h - 1 if d == 0 else base + h


def _scan_constants():
    c = CHUNK
    tri = np.tril(np.ones((c, c), np.float32))
    m = [tri] + [tri[[_boundary_row(t, 1 << l, 0) for t in range(c)]] for l in range(N_MXU_LEVELS)]
    m = np.stack(m)
    lvl = np.full((c, c), N_LEVELS + 1, np.int32)
    for t in range(c):
        lvl[t, t] = N_LEVELS
        for s in range(t):
            lvl[t, s] = int(np.floor(np.log2(t ^ s)))
    m_fwd = m.reshape(-1, c)
    m_bwd = m[:, ::-1, ::-1].reshape(-1, c)
    return (jnp.asarray(np.stack([m_fwd, m_bwd]), BF16), jnp.asarray(np.stack([lvl, lvl.T])))


def _scan_prepare(d, q_ref, k_ref, g_ref, m_ref, cb_ref, qf_ref, kf_ref, p_ref):
    c = CHUNK
    w = g_ref.shape[-1]
    g = g_ref[...]
    g1 = g.astype(BF16)
    r1 = g - g1.astype(F32)
    g2 = r1.astype(BF16)
    g3 = (r1 - g2.astype(F32)).astype(BF16)
    m = m_ref[d]
    cb_ref[d] = (_dot(m, g1) + _dot(m, g2) + _dot(m, g3)) * LOG2E
    qf_ref[d] = q_ref[...].astype(F32)
    kf_ref[d] = (1.0 - jnp.exp(g)) if k_ref is None else k_ref[...].astype(F32)
    bcast = lambda r, n: jnp.broadcast_to(cb_ref[d, r:r + 1, :], (n, w))
    for i in range(c // ROW_GROUP):
        t0 = ROW_GROUP * i
        rows = slice(t0, t0 + ROW_GROUP)
        cum = cb_ref[d, rows, :]
        q = qf_ref[d, rows, :]
        k = kf_ref[d, rows, :]
        rowid = t0 + lax.broadcasted_iota(jnp.int32, (ROW_GROUP, w), 0)
        for l in range(N_LEVELS):
            h = 1 << l
            if l < N_MXU_LEVELS:
                bnd = cb_ref[d, c * (l + 1) + t0:c * (l + 1) + t0 + ROW_GROUP, :]
            else:
                r0, r1 = _boundary_row(t0, h, d), _boundary_row(t0 + 8, h, d)
                bnd = bcast(r0, ROW_GROUP) if r0 == r1 else jnp.concatenate([bcast(r0, 8), bcast(r1, 8)], axis=0)
            e = jnp.exp2(-jnp.abs(cum - bnd))
            if h >= ROW_GROUP:
                src = q if ((t0 & h) != 0) == (d == 0) else k
            else:
                upper = (rowid & h) != 0
                src = jnp.where(upper if d == 0 else jnp.logical_not(upper), q, k)
            p_ref[d, l, rows, :] = (src * e).astype(BF16)
        p_ref[d, START_SLOT, rows, :] = (q * jnp.exp2(cum)).astype(BF16)
        edge = c - 1 if d == 0 else 0
        p_ref[d, END_SLOT, rows, :] = (k * jnp.exp2(-jnp.abs(cum - bcast(edge, ROW_GROUP)))).astype(BF16)


def _scan_scores(d, q_ref, lvl_ref, kf_ref, p_ref, a_ref, n_heads, dk):
    lvl = lvl_ref[d]
    for h in range(n_heads):
        ks = slice(h * dk, (h + 1) * dk)
        a = jnp.where(lvl == N_LEVELS, _dot_nt(q_ref[:, ks], kf_ref[d, :, ks].astype(BF16)), 0.0)
        for l in range(N_LEVELS):
            p = p_ref[d, l, :, ks]
            a = jnp.where(lvl == l, _dot_nt(p, p), a)
        a_ref[d, h] = a.astype(BF16)


def _scan_outputs(d, v_ref, cb_ref, p_ref, a_ref, st_ref, o_ref, n_heads, dk, dv):
    edge = CHUNK - 1 if d == 0 else 0
    for h in range(n_heads):
        ks = slice(h * dk, (h + 1) * dk)
        vs = slice(h * dv, (h + 1) * dv)
        v = v_ref[:, vs]
        st = st_ref[d, h]
        o_ref[:, vs] = _dot(a_ref[d, h], v) + _dot_nt(p_ref[d, START_SLOT, :, ks], st.astype(BF16))
        st_ref[d, h] = st * jnp.exp2(cb_ref[d, edge:edge + 1, ks]) + _dot_tn(v, p_ref[d, END_SLOT, :, ks])


def _scan_body(*refs, n_heads, dk, dv, derive_k, zero_init, emit_state):
    refs = list(refs)
    q_refs = (refs.pop(0), refs.pop(0))
    k_refs = (None, None) if derive_k else (refs.pop(0), refs.pop(0))
    v_refs = (refs.pop(0), refs.pop(0))
    g_refs = (refs.pop(0), refs.pop(0))
    m_ref, lvl_ref = refs.pop(0), refs.pop(0)
    s0_ref = None if zero_init else refs.pop(0)
    o_refs = (refs.pop(0), refs.pop(0))
    sout_ref = refs.pop(0) if emit_state else None
    st_ref, cb_ref, qf_ref, kf_ref, p_ref, a_ref = refs
    n = pl.program_id(1)

    @pl.when(n == 0)
    def _():
        for d in range(2):
            for h in range(n_heads):
                st_ref[d, h] = jnp.zeros((dv, dk), F32) if zero_init else s0_ref[d, h].T

    for d in range(2):
        _scan_prepare(d, q_refs[d], k_refs[d], g_refs[d], m_ref, cb_ref, qf_ref, kf_ref, p_ref)
    for d in range(2):
        _scan_scores(d, q_refs[d], lvl_ref, kf_ref, p_ref, a_ref, n_heads, dk)
    for d in range(2):
        _scan_outputs(d, v_refs[d], cb_ref, p_ref, a_ref, st_ref, o_refs[d], n_heads, dk, dv)

    if emit_state:
        @pl.when(n == pl.num_programs(1) - 1)
        def _():
            for d in range(2):
                for h in range(n_heads):
                    sout_ref[d, h] = st_ref[d, h].T


def _scan(q, k, v, g_f, g_b, s0, consts, n_heads, dk, dv, emit_state):
    b, l, _ = q.shape
    n = l // CHUNK
    m_all, lvl = consts
    fwd = lambda w: pl.BlockSpec((None, CHUNK, w), lambda i, j: (i, j, 0))
    bwd = lambda w: pl.BlockSpec((None, CHUNK, w), lambda i, j: (i, n - 1 - j, 0))
    const = lambda shape: pl.BlockSpec(shape, lambda i, j: (0,) * len(shape))
    state = pl.BlockSpec((None, 2, n_heads, dk, dv), lambda i, j: (i, 0, 0, 0, 0))
    wk, wv = n_heads * dk, n_heads * dv
    args, in_specs = [q, q], [fwd(wk), bwd(wk)]
    if k is not None:
        args += [k, k]
        in_specs += [fwd(wk), bwd(wk)]
    args += [v, v, g_f, g_b, m_all, lvl]
    in_specs += [fwd(wv), bwd(wv), fwd(wk), bwd(wk), const(m_all.shape), const(lvl.shape)]
    if s0 is not None:
        args.append(s0)
        in_specs.append(state)
    out_specs = [fwd(wv), bwd(wv)]
    out_shape = [jax.ShapeDtypeStruct((b, l, wv), F32)] * 2
    if emit_state:
        out_specs.append(state)
        out_shape.append(jax.ShapeDtypeStruct((b, 2, n_heads, dk, dv), F32))
    body = functools.partial(_scan_body, n_heads=n_heads, dk=dk, dv=dv, derive_k=k is None,
                             zero_init=s0 is None, emit_state=emit_state)
    return pl.pallas_call(
        body,
        grid=(b, n),
        in_specs=in_specs,
        out_specs=out_specs,
        out_shape=out_shape,
        scratch_shapes=[pltpu.VMEM((2, n_heads, dv, dk), F32),
                        pltpu.VMEM((2, (1 + N_MXU_LEVELS) * CHUNK, wk), F32),
                        pltpu.VMEM((2, CHUNK, wk), F32), pltpu.VMEM((2, CHUNK, wk), F32),
                        pltpu.VMEM((2, N_LEVELS + 2, CHUNK, wk), BF16),
                        pltpu.VMEM((2, n_heads, CHUNK, CHUNK), BF16)],
        compiler_params=pltpu.CompilerParams(dimension_semantics=("arbitrary", "arbitrary"),
                                             vmem_limit_bytes=VMEM_LIMIT),
        name="scan_hgrn" if k is None else "scan_gla",
    )(*args)


def _post_body(ohf_ref, ohb_ref, ogf_ref, ogb_ref, hgate_ref, gr_ref, sa_ref, sb_ref, x_ref, mod_ref,
               hgn_ref, glan_ref, n2_ref, wa_ref, wb_ref, wo_ref, wr_ref,
               x1_ref, h2_ref, comb_ref, yh_ref, yg_ref):
    for h in range(HG_H):
        s = slice(h * HG_DV, (h + 1) * HG_DV)
        o = ohf_ref[:, s] + ohb_ref[:, s]
        yh_ref[:, s] = (_rmsnorm(o, hgn_ref[...]) * hgate_ref[:, s].astype(F32)).astype(BF16)
    for h in range(GLA_H):
        s = slice(h * GLA_DV, (h + 1) * GLA_DV)
        o = ogf_ref[:, s] + ogb_ref[:, s]
        yg_ref[:, s] = (_rmsnorm(o, glan_ref[...]) * gr_ref[:, s].astype(F32)).astype(BF16)
    merged = (sa_ref[...].astype(F32) * _dot(yh_ref[...], wa_ref[...])
              + sb_ref[...].astype(F32) * _dot(yg_ref[...], wb_ref[...]))
    mix = _dot(merged.astype(BF16), wo_ref[...])
    x1 = x_ref[...] + mod_ref[2:3, :] * mix
    x1_ref[...] = x1
    h2 = _rmsnorm(x1, n2_ref[...]) * (1.0 + mod_ref[4:5, :]) + mod_ref[3:4, :]
    h2_hi = h2.astype(BF16)
    h2_ref[...] = h2_hi

    h2_lo = (h2 - h2_hi.astype(F32)).astype(BF16)
    logits = _dot(h2_hi, wr_ref[0]) + _dot(h2_hi, wr_ref[1]) + _dot(h2_lo, wr_ref[0])
    lane = lax.broadcasted_iota(jnp.int32, logits.shape, 1)
    neg = -jnp.inf
    first = lambda hit: jnp.min(jnp.where(hit, lane, ROUTER_LANES), axis=-1, keepdims=True)
    gl = jnp.where(lane < N_GROUPS, logits, neg)
    gmax = jnp.max(gl, axis=-1, keepdims=True)
    p_top = 1.0 / jnp.sum(jnp.exp(gl - gmax), axis=-1, keepdims=True)
    g_idx = first(gl == gmax)
    in_group = ((lane >= EXPERT_LANE0) & (lane < EXPERT_LANE0 + N_EXPERTS)
                & (((lane - EXPERT_LANE0) >> 3) == g_idx))
    sl = jnp.where(in_group, logits, neg)
    m1 = jnp.max(sl, axis=-1, keepdims=True)
    i1 = first(sl == m1)
    sl2 = jnp.where(lane == i1, neg, sl)
    m2 = jnp.max(sl2, axis=-1, keepdims=True)
    i2 = first(sl2 == m2)
    r = jnp.exp(m2 - m1)
    w1 = p_top / (1.0 + r)
    comb_ref[...] = jnp.where(lane == i1, w1, jnp.where(lane == i2, w1 * r, 0.0))


def _post(ohf, ohb, ogf, ogb, hgate, gr, sa, sb, x, mod, tiles_per_row, hgn, glan, n2, wa, wb, wo, wr):
    t = x.shape[0]
    tm = TM_POST
    const = lambda shape: pl.BlockSpec(shape, lambda i: (0,) * len(shape))
    tok = lambda w: pl.BlockSpec((tm, w), lambda i: (i, 0))
    return pl.pallas_call(
        _post_body,
        grid=(t // tm,),
        in_specs=[tok(1024)] * 9
                 + [pl.BlockSpec((None, 6, D_MODEL), lambda i: (i // tiles_per_row, 0, 0)),
                    const((1, HG_DV)), const((1, GLA_DV)), const((1, D_MODEL)),
                    const((1024, 1024)), const((1024, 1024)), const((1024, 1024)),
                    const((2, D_MODEL, ROUTER_LANES))],
        out_specs=[tok(1024), tok(1024), tok(ROUTER_LANES)],
        out_shape=[jax.ShapeDtypeStruct((t, D_MODEL), F32), jax.ShapeDtypeStruct((t, D_MODEL), BF16),
                   jax.ShapeDtypeStruct((t, ROUTER_LANES), F32)],
        scratch_shapes=[pltpu.VMEM((tm, 1024), BF16), pltpu.VMEM((tm, 1024), BF16)],
        compiler_params=pltpu.CompilerParams(vmem_limit_bytes=VMEM_LIMIT),
        name="post",
    )(ohf, ohb, ogf, ogb, hgate, gr, sa, sb, x, mod, hgn, glan, n2, wa, wb, wo, wr)


def _moe_body(h2_ref, comb_ref, wg_ref, wu_ref, wd_ref, x1_ref, mod_ref, fn_ref, y_ref, acc_ref):
    e = pl.program_id(1)

    @pl.when(e == 0)
    def _():
        acc_ref[...] = jnp.zeros_like(acc_ref)

    h2 = h2_ref[...]
    a = _dot(h2, wg_ref[...])
    u = _dot(h2, wu_ref[...])
    comb = comb_ref[...]
    lane = lax.broadcasted_iota(jnp.int32, comb.shape, 1)
    col = jnp.sum(jnp.where(lane == e + EXPERT_LANE0, comb, 0.0), axis=-1, keepdims=True)
    hid = (_silu(a) * u * col).astype(BF16)
    acc_ref[...] += _dot(hid, wd_ref[...])

    @pl.when(e == pl.num_programs(1) - 1)
    def _():
        x2 = x1_ref[...] + mod_ref[5:6, :] * acc_ref[...]
        y_ref[...] = _rmsnorm(x2, fn_ref[...])


def _moe(h2, comb, wg, wu, wd, x1, mod, tiles_per_row, fn):
    t = h2.shape[0]
    tm = TM_MOE
    tok = lambda w: pl.BlockSpec((tm, w), lambda i, e: (i, 0))
    return pl.pallas_call(
        _moe_body,
        grid=(t // tm, N_EXPERTS),
        in_specs=[tok(D_MODEL), tok(ROUTER_LANES),
                  pl.BlockSpec((None, D_MODEL, D_EXPERT), lambda i, e: (e, 0, 0)),
                  pl.BlockSpec((None, D_MODEL, D_EXPERT), lambda i, e: (e, 0, 0)),
                  pl.BlockSpec((None, D_EXPERT, D_MODEL), lambda i, e: (e, 0, 0)),
                  tok(D_MODEL),
                  pl.BlockSpec((None, 6, D_MODEL), lambda i, e: (i // tiles_per_row, 0, 0)),
                  pl.BlockSpec((1, D_MODEL), lambda i, e: (0, 0))],
        out_specs=tok(D_MODEL),
        out_shape=jax.ShapeDtypeStruct((t, D_MODEL), F32),
        scratch_shapes=[pltpu.VMEM((tm, D_MODEL), F32)],
        compiler_params=pltpu.CompilerParams(dimension_semantics=("arbitrary", "arbitrary"),
                                             vmem_limit_bytes=VMEM_LIMIT),
        name="moe",
    )(h2, comb, wg, wu, wd, x1, mod, fn)


def _trunk_layer(x, mod, s0_hg, s0_gla, emit_state, w):
    b, l, _ = x.shape
    xt = x.reshape(b * l, D_MODEL)
    per_row = lambda tm: (l // tm) if mod.shape[0] > 1 else (b * l // tm)
    (hq, gf, gb, hv, hgate, gq, gk, gv, gr, laf, lab, sa, sb) = _inproj(
        xt, mod, per_row(TM_PROJ), w["n1"], w["lbp"], w["wmain"], w["waup"], w["ba"])
    seq = lambda a: a.reshape(b, l, a.shape[-1])
    hg = _scan(seq(hq), None, seq(hv), seq(gf), seq(gb), s0_hg, w["scan_consts"], HG_H, HG_DK, HG_DV,
               emit_state)
    gla = _scan(seq(gq), seq(gk), seq(gv), seq(laf), seq(lab), s0_gla, w["scan_consts"], GLA_H, GLA_DK,
                GLA_DV, emit_state)
    flat = lambda a: a.reshape(b * l, a.shape[-1])
    x1, h2, comb = _post(flat(hg[0]), flat(hg[1]), flat(gla[0]), flat(gla[1]), hgate, gr, sa, sb, xt, mod,
                         per_row(TM_POST), w["hgn"], w["glan"], w["n2"], w["wa"], w["wb"], w["wo"], w["wr"])
    y = _moe(h2, comb, w["wg"], w["wu"], w["wd"], x1, mod, per_row(TM_MOE), w["fn"])
    states = (hg[2], gla[2]) if emit_state else (None, None)
    return y.reshape(b, l, D_MODEL), states


def kernel(x_prompt, x_sample, state_hgrn, state_gla, c, c_ctx, ada_w, ada_b, norm1_g, norm2_g, w_in,
           hg_lb_param, hg_norm_g, gla_wa_up, gla_ba, gla_norm_g, w_br_a, w_br_b, w_out,
           w_router_group, w_router_expert, w_exp_gate, w_exp_up, w_exp_down, final_norm_g):
    nb = c.shape[0]
    cond = jnp.concatenate([c, c_ctx[None, :], jnp.zeros((8 - nb - 1, D_MODEL), F32)], axis=0)
    mod = _modulation(cond, ada_w[0], ada_b).reshape(8, 6, D_MODEL)

    w_in0 = w_in[0]
    ga0 = _C_MA
    zeros = jnp.zeros((GLA_RANK, GLA_H * GLA_DK), F32)
    router = jnp.concatenate(
        [w_router_group[0], jnp.transpose(w_router_expert[0], (1, 0, 2)).reshape(D_MODEL, N_EXPERTS),
         jnp.zeros((D_MODEL, ROUTER_LANES - N_GROUPS - N_EXPERTS), F32)], axis=1)
    w = {
        "n1": norm1_g, "n2": norm2_g, "fn": final_norm_g[None, :], "lbp": hg_lb_param,
        "wmain": jnp.concatenate([w_in0[:, :ga0], w_in0[:, ga0 + 2 * GLA_RANK:], w_in0[:, ga0:ga0 + 2 * GLA_RANK],
                                  jnp.zeros((D_MODEL, GA_LANES - 2 * GLA_RANK), F32)], axis=1).astype(BF16),
        "waup": jnp.concatenate([jnp.concatenate([gla_wa_up[0, 0], zeros], axis=1),
                                 jnp.concatenate([zeros, gla_wa_up[0, 1]], axis=1),
                                 jnp.zeros((GA_LANES - 2 * GLA_RANK, 2 * GLA_H * GLA_DK), F32)],
                                axis=0).astype(BF16),
        "ba": gla_ba[0].reshape(1, 2 * GLA_H * GLA_DK),
        "hgn": hg_norm_g, "glan": gla_norm_g,
        "wa": w_br_a[0].astype(BF16), "wb": w_br_b[0].astype(BF16), "wo": w_out[0].astype(BF16),
        "wr": jnp.stack([router.astype(BF16), (router - router.astype(BF16).astype(F32)).astype(BF16)]),
        "wg": w_exp_gate[0].astype(BF16), "wu": w_exp_up[0].astype(BF16), "wd": w_exp_down[0].astype(BF16),
        "scan_consts": _scan_constants(),
    }
    y_prompt, (st_hg, st_gla) = _trunk_layer(x_prompt, mod[nb:nb + 1], None, None, True, w)
    y_sample, _ = _trunk_layer(x_sample, mod[:nb], state_hgrn[:, 0], state_gla[:, 0], False, w)
    return y_prompt, y_sample, st_hg[:, None], st_gla[:, None]
```

```python
import functools

import numpy as np
import jax
import jax.numpy as jnp
from jax import lax
from jax.experimental import pallas as pl
from jax.experimental.pallas import tpu as pltpu

F32 = jnp.float32
BF16 = jnp.bfloat16

D_MODEL = 1024
CHUNK = 64
N_LEVELS = 6
HG_H, HG_DK, HG_DV = 8, 128, 128
GLA_H, GLA_DK, GLA_DV = 4, 128, 256
GLA_RANK = 16
GLA_GATE_NORM = 16.0
N_GROUPS = 4
EXPERTS_PER_GROUP = 8
N_EXPERTS = N_GROUPS * EXPERTS_PER_GROUP
D_EXPERT = D_MODEL // 4
EPS = 1e-6
ROUTER_LANES = 128
EXPERT_LANE0 = N_GROUPS
VMEM_LIMIT = 56 * 1024 * 1024

TM_PROJ = 256
TM_POST = 256
TM_FINAL = 512
TR_MOE = 256
OUT_SUBLANES = D_MODEL // 128
COMB_SUBLANE = OUT_SUBLANES
ROW_SUBLANES = 16


def _sigmoid(x):
    return 1.0 / (1.0 + jnp.exp(-x))


def _silu(x):
    return x * _sigmoid(x)


def _log_sigmoid(x):
    return jnp.minimum(x, 0.0) - jnp.log(1.0 + jnp.exp(-jnp.abs(x)))


def _rmsnorm(x, g):
    return x * lax.rsqrt(jnp.mean(x * x, axis=-1, keepdims=True) + EPS) * g


def _dot(a, b):
    return jnp.dot(a, b, preferred_element_type=F32)


def _dot_exact(a, b):
    return jnp.dot(a, b, preferred_element_type=F32, precision=lax.Precision.HIGHEST)


def _dot_nt(a, b):
    return lax.dot_general(a, b, (((1,), (1,)), ((), ())), preferred_element_type=F32)


def _dot_tn(a, b):
    return lax.dot_general(a, b, (((0,), (0,)), ((), ())), preferred_element_type=F32)


def _mod_body(c_ref, w_ref, b_ref, o_ref):
    o_ref[...] = _dot_exact(_silu(c_ref[...]), w_ref[...]) + b_ref[...]


def _modulation(cond, ada_w, ada_b):
    rows, n = cond.shape[0], ada_w.shape[1]
    tn = n // 4
    return pl.pallas_call(
        _mod_body,
        grid=(n // tn,),
        in_specs=[pl.BlockSpec((rows, D_MODEL), lambda j: (0, 0)),
                  pl.BlockSpec((D_MODEL, tn), lambda j: (0, j)),
                  pl.BlockSpec((1, tn), lambda j: (0, j))],
        out_specs=pl.BlockSpec((rows, tn), lambda j: (0, j)),
        out_shape=jax.ShapeDtypeStruct((rows, n), F32),
        compiler_params=pltpu.CompilerParams(vmem_limit_bytes=VMEM_LIMIT),
        name="modulation",
    )(cond, ada_w, ada_b)


_C_HQ, _C_HFF, _C_HFB, _C_HI, _C_HGATE = 0, 1024, 2048, 3072, 4096
_C_GQ, _C_GK, _C_GV, _C_GR, _C_MA, _C_MB = 5120, 5632, 6144, 7168, 8192, 9216
_C_GA = 10240
GA_LANES = 128
W_MAIN = _C_GA + GA_LANES


def _inproj_body(x_ref, mod_ref, n1_ref, lbp_ref, wmain_ref, waup_ref, ba_ref,
                 hq_ref, gf_ref, gb_ref, hv_ref, hgate_ref, gq_ref, gk_ref, gv_ref, gr_ref,
                 laf_ref, lab_ref, sa_ref, sb_ref):
    x = x_ref[...]
    h = _rmsnorm(x, n1_ref[...]) * (1.0 + mod_ref[1:2, :]) + mod_ref[0:1, :]
    hb = h.astype(BF16)

    def proj(c0, w):
        return _dot(hb, wmain_ref[:, c0:c0 + w])

    p0, p1 = lbp_ref[0:1, :], lbp_ref[1:2, :]
    pm = jnp.maximum(p0, p1)
    e0, e1 = jnp.exp(p0 - pm), jnp.exp(p1 - pm)
    lb = e0 / (e0 + e1)

    hq_ref[...] = (proj(_C_HQ, 1024) * HG_DK ** -0.5).astype(BF16)
    gf_ref[...] = jnp.log(lb + (1.0 - lb) * _sigmoid(proj(_C_HFF, 1024)))
    gb_ref[...] = jnp.log(lb + (1.0 - lb) * _sigmoid(proj(_C_HFB, 1024)))
    hv_ref[...] = proj(_C_HI, 1024).astype(BF16)
    hgate_ref[...] = _silu(proj(_C_HGATE, 1024)).astype(BF16)
    gq_ref[...] = (proj(_C_GQ, 512) * GLA_DK ** -0.5).astype(BF16)
    gk_ref[...] = proj(_C_GK, 512).astype(BF16)
    gv_ref[...] = proj(_C_GV, 1024).astype(BF16)
    gr_ref[...] = _silu(proj(_C_GR, 1024)).astype(BF16)
    sa_ref[...] = _sigmoid(proj(_C_MA, 1024)).astype(BF16)
    sb_ref[...] = _sigmoid(proj(_C_MB, 1024)).astype(BF16)

    ga = proj(_C_GA, GA_LANES).astype(BF16)
    la = _log_sigmoid(_dot(ga, waup_ref[...]) + ba_ref[...]) * (1.0 / GLA_GATE_NORM)
    laf_ref[...] = la[:, :512]
    lab_ref[...] = la[:, 512:]


def _inproj(x, mod, tiles_per_row, n1, lbp, wmain, waup, ba):
    t = x.shape[0]
    tm = TM_PROJ
    const = lambda shape: pl.BlockSpec(shape, lambda i: (0,) * len(shape))
    tok = lambda w: pl.BlockSpec((tm, w), lambda i: (i, 0))
    widths = (1024, 1024, 1024, 1024, 1024, 512, 512, 1024, 1024, 512, 512, 1024, 1024)
    dtypes = (BF16, F32, F32, BF16, BF16, BF16, BF16, BF16, BF16, F32, F32, BF16, BF16)
    return pl.pallas_call(
        _inproj_body,
        grid=(t // tm,),
        in_specs=[tok(D_MODEL),
                  pl.BlockSpec((None, 6, D_MODEL), lambda i: (i // tiles_per_row, 0, 0)),
                  const((1, D_MODEL)), const((2, D_MODEL)),
                  pl.BlockSpec((D_MODEL, W_MAIN), lambda i: (0, 0), pipeline_mode=pl.Buffered(1)),
                  const((GA_LANES, 1024)), const((1, 1024))],
        out_specs=[tok(w) for w in widths],
        out_shape=[jax.ShapeDtypeStruct((t, w), dt) for w, dt in zip(widths, dtypes)],
        compiler_params=pltpu.CompilerParams(vmem_limit_bytes=VMEM_LIMIT),
        name="inproj",
    )(x, mod, n1, lbp, wmain, waup, ba)


N_MXU_LEVELS = 2
ROW_GROUP = 16
START_SLOT, END_SLOT = N_LEVELS, N_LEVELS + 1
LOG2E = 1.4426950408889634


def _boundary_row(t, h, d):
    base = (t // (2 * h)) * 2 * h
    return base + h - 1 if d == 0 else base + h


def _scan_constants():
    c = CHUNK
    tri = np.tril(np.ones((c, c), np.float32))
    m = [tri] + [tri[[_boundary_row(t, 1 << l, 0) for t in range(c)]] for l in range(N_MXU_LEVELS)]
    m = np.stack(m)
    lvl = np.full((c, c), N_LEVELS + 1, np.int32)
    for t in range(c):
        lvl[t, t] = N_LEVELS
        for s in range(t):
            lvl[t, s] = int(np.floor(np.log2(t ^ s)))
    m_fwd = m.reshape(-1, c)
    m_bwd = m[:, ::-1, ::-1].reshape(-1, c)
    return (jnp.asarray(np.stack([m_fwd, m_bwd]), BF16), jnp.asarray(np.stack([lvl, lvl.T])))


def _scan_prepare(d, q_ref, k_ref, g_ref, m_ref, cb_ref, qf_ref, kf_ref, p_ref):
    c = CHUNK
    w = g_ref.shape[-1]
    g = g_ref[...]
    g1 = g.astype(BF16)
    r1 = g - g1.astype(F32)
    g2 = r1.astype(BF16)
    g3 = (r1 - g2.astype(F32)).astype(BF16)
    m = m_ref[d]
    cb_ref[d] = (_dot(m, g1) + _dot(m, g2) + _dot(m, g3)) * LOG2E
    qf_ref[d] = q_ref[...].astype(F32)
    kf_ref[d] = (1.0 - jnp.exp(g)) if k_ref is None else k_ref[...].astype(F32)
    bcast = lambda r, n: jnp.broadcast_to(cb_ref[d, r:r + 1, :], (n, w))
    for i in range(c // ROW_GROUP):
        t0 = ROW_GROUP * i
        rows = slice(t0, t0 + ROW_GROUP)
        cum = cb_ref[d, rows, :]
        q = qf_ref[d, rows, :]
        k = kf_ref[d, rows, :]
        rowid = t0 + lax.broadcasted_iota(jnp.int32, (ROW_GROUP, w), 0)
        for l in range(N_LEVELS):
            h = 1 << l
            if l < N_MXU_LEVELS:
                bnd = cb_ref[d, c * (l + 1) + t0:c * (l + 1) + t0 + ROW_GROUP, :]
            else:
                r0, r1 = _boundary_row(t0, h, d), _boundary_row(t0 + 8, h, d)
                bnd = bcast(r0, ROW_GROUP) if r0 == r1 else jnp.concatenate([bcast(r0, 8), bcast(r1, 8)], axis=0)
            e = jnp.exp2(-jnp.abs(cum - bnd))
            if h >= ROW_GROUP:
                src = q if ((t0 & h) != 0) == (d == 0) else k
            else:
                upper = (rowid & h) != 0
                src = jnp.where(upper if d == 0 else jnp.logical_not(upper), q, k)
            p_ref[d, l, rows, :] = (src * e).astype(BF16)
        p_ref[d, START_SLOT, rows, :] = (q * jnp.exp2(cum)).astype(BF16)
        edge = c - 1 if d == 0 else 0
        p_ref[d, END_SLOT, rows, :] = (k * jnp.exp2(-jnp.abs(cum - bcast(edge, ROW_GROUP)))).astype(BF16)


def _scan_scores(d, q_ref, lvl_ref, kf_ref, p_ref, a_ref, n_heads, dk):
    lvl = lvl_ref[d]
    for h in range(n_heads):
        ks = slice(h * dk, (h + 1) * dk)
        a = jnp.where(lvl == N_LEVELS, _dot_nt(q_ref[:, ks], kf_ref[d, :, ks].astype(BF16)), 0.0)
        for l in range(N_LEVELS):
            p = p_ref[d, l, :, ks]
            a = jnp.where(lvl == l, _dot_nt(p, p), a)
        a_ref[d, h] = a.astype(BF16)


def _scan_outputs(d, v_ref, cb_ref, p_ref, a_ref, st_ref, o_ref, n_heads, dk, dv):
    edge = CHUNK - 1 if d == 0 else 0
    for h in range(n_heads):
        ks = slice(h * dk, (h + 1) * dk)
        vs = slice(h * dv, (h + 1) * dv)
        v = v_ref[:, vs]
        st = st_ref[d, h]
        o_ref[:, vs] = _dot(a_ref[d, h], v) + _dot_nt(p_ref[d, START_SLOT, :, ks], st.astype(BF16))
        st_ref[d, h] = st * jnp.exp2(cb_ref[d, edge:edge + 1, ks]) + _dot_tn(v, p_ref[d, END_SLOT, :, ks])


def _scan_body(*refs, n_heads, dk, dv, derive_k, zero_init, emit_state):
    refs = list(refs)
    q_refs = (refs.pop(0), refs.pop(0))
    k_refs = (None, None) if derive_k else (refs.pop(0), refs.pop(0))
    v_refs = (refs.pop(0), refs.pop(0))
    g_refs = (refs.pop(0), refs.pop(0))
    m_ref, lvl_ref = refs.pop(0), refs.pop(0)
    s0_ref = None if zero_init else refs.pop(0)
    o_refs = (refs.pop(0), refs.pop(0))
    sout_ref = refs.pop(0) if emit_state else None
    st_ref, cb_ref, qf_ref, kf_ref, p_ref, a_ref = refs
    n = pl.program_id(1)

    @pl.when(n == 0)
    def _():
        for d in range(2):
            for h in range(n_heads):
                st_ref[d, h] = jnp.zeros((dv, dk), F32) if zero_init else s0_ref[d, h].T

    for d in range(2):
        _scan_prepare(d, q_refs[d], k_refs[d], g_refs[d], m_ref, cb_ref, qf_ref, kf_ref, p_ref)
    for d in range(2):
        _scan_scores(d, q_refs[d], lvl_ref, kf_ref, p_ref, a_ref, n_heads, dk)
    for d in range(2):
        _scan_outputs(d, v_refs[d], cb_ref, p_ref, a_ref, st_ref, o_refs[d], n_heads, dk, dv)

    if emit_state:
        @pl.when(n == pl.num_programs(1) - 1)
        def _():
            for d in range(2):
                for h in range(n_heads):
                    sout_ref[d, h] = st_ref[d, h].T


def _scan(q, k, v, g_f, g_b, s0, consts, n_heads, dk, dv, emit_state):
    b, l, _ = q.shape
    n = l // CHUNK
    m_all, lvl = consts
    fwd = lambda w: pl.BlockSpec((None, CHUNK, w), lambda i, j: (i, j, 0))
    bwd = lambda w: pl.BlockSpec((None, CHUNK, w), lambda i, j: (i, n - 1 - j, 0))
    const = lambda shape: pl.BlockSpec(shape, lambda i, j: (0,) * len(shape))
    state = pl.BlockSpec((None, 2, n_heads, dk, dv), lambda i, j: (i, 0, 0, 0, 0))
    wk, wv = n_heads * dk, n_heads * dv
    args, in_specs = [q, q], [fwd(wk), bwd(wk)]
    if k is not None:
        args += [k, k]
        in_specs += [fwd(wk), bwd(wk)]
    args += [v, v, g_f, g_b, m_all, lvl]
    in_specs += [fwd(wv), bwd(wv), fwd(wk), bwd(wk), const(m_all.shape), const(lvl.shape)]
    if s0 is not None:
        args.append(s0)
        in_specs.append(state)
    out_specs = [fwd(wv), bwd(wv)]
    out_shape = [jax.ShapeDtypeStruct((b, l, wv), F32)] * 2
    if emit_state:
        out_specs.append(state)
        out_shape.append(jax.ShapeDtypeStruct((b, 2, n_heads, dk, dv), F32))
    body = functools.partial(_scan_body, n_heads=n_heads, dk=dk, dv=dv, derive_k=k is None,
                             zero_init=s0 is None, emit_state=emit_state)
    return pl.pallas_call(
        body,
        grid=(b, n),
        in_specs=in_specs,
        out_specs=out_specs,
        out_shape=out_shape,
        scratch_shapes=[pltpu.VMEM((2, n_heads, dv, dk), F32),
                        pltpu.VMEM((2, (1 + N_MXU_LEVELS) * CHUNK, wk), F32),
                        pltpu.VMEM((2, CHUNK, wk), F32), pltpu.VMEM((2, CHUNK, wk), F32),
                        pltpu.VMEM((2, N_LEVELS + 2, CHUNK, wk), BF16),
                        pltpu.VMEM((2, n_heads, CHUNK, CHUNK), BF16)],
        compiler_params=pltpu.CompilerParams(dimension_semantics=("arbitrary", "arbitrary"),
                                             vmem_limit_bytes=VMEM_LIMIT),
        name="scan_hgrn" if k is None else "scan_gla",
    )(*args)


def _post_body(ohf_ref, ohb_ref, ogf_ref, ogb_ref, hgate_ref, gr_ref, sa_ref, sb_ref, x_ref, mod_ref,
               hgn_ref, glan_ref, n2_ref, wa_ref, wb_ref, wo_ref, wr_ref,
               x1_ref, rows_ref, comb_ref, yh_ref, yg_ref):
    for h in range(HG_H):
        s = slice(h * HG_DV, (h + 1) * HG_DV)
        o = ohf_ref[:, s] + ohb_ref[:, s]
        yh_ref[:, s] = (_rmsnorm(o, hgn_ref[...]) * hgate_ref[:, s].astype(F32)).astype(BF16)
    for h in range(GLA_H):
        s = slice(h * GLA_DV, (h + 1) * GLA_DV)
        o = ogf_ref[:, s] + ogb_ref[:, s]
        yg_ref[:, s] = (_rmsnorm(o, glan_ref[...]) * gr_ref[:, s].astype(F32)).astype(BF16)
    merged = (sa_ref[...].astype(F32) * _dot(yh_ref[...], wa_ref[...])
              + sb_ref[...].astype(F32) * _dot(yg_ref[...], wb_ref[...]))
    mix = _dot(merged.astype(BF16), wo_ref[...])
    x1 = x_ref[...] + mod_ref[2:3, :] * mix
    x1_ref[...] = x1
    h2 = _rmsnorm(x1, n2_ref[...]) * (1.0 + mod_ref[4:5, :]) + mod_ref[3:4, :]
    h2_hi = h2.astype(BF16)
    tm = h2.shape[0]
    for s in range(D_MODEL // 128):
        rows_ref[pl.ds(s, tm, stride=ROW_SUBLANES), :] = h2[:, s * 128:(s + 1) * 128]
    for s in range(COMB_SUBLANE + 1, ROW_SUBLANES):
        rows_ref[pl.ds(s, tm, stride=ROW_SUBLANES), :] = jnp.zeros((tm, 128), F32)

    h2_lo = (h2 - h2_hi.astype(F32)).astype(BF16)
    logits = _dot(h2_hi, wr_ref[0]) + _dot(h2_hi, wr_ref[1]) + _dot(h2_lo, wr_ref[0])
    lane = lax.broadcasted_iota(jnp.int32, logits.shape, 1)
    neg = -jnp.inf
    first = lambda hit: jnp.min(jnp.where(hit, lane, ROUTER_LANES), axis=-1, keepdims=True)
    gl = jnp.where(lane < N_GROUPS, logits, neg)
    gmax = jnp.max(gl, axis=-1, keepdims=True)
    p_top = 1.0 / jnp.sum(jnp.exp(gl - gmax), axis=-1, keepdims=True)
    g_idx = first(gl == gmax)
    in_group = ((lane >= EXPERT_LANE0) & (lane < EXPERT_LANE0 + N_EXPERTS)
                & (((lane - EXPERT_LANE0) >> 3) == g_idx))
    sl = jnp.where(in_group, logits, neg)
    m1 = jnp.max(sl, axis=-1, keepdims=True)
    i1 = first(sl == m1)
    sl2 = jnp.where(lane == i1, neg, sl)
    m2 = jnp.max(sl2, axis=-1, keepdims=True)
    i2 = first(sl2 == m2)
    r = jnp.exp(m2 - m1)
    w1 = p_top / (1.0 + r)
    comb = jnp.where(lane == i1, w1, jnp.where(lane == i2, w1 * r, 0.0))
    rows_ref[pl.ds(COMB_SUBLANE, tm, stride=ROW_SUBLANES), :] = comb
    comb_ref[...] = jnp.where(lane == 0, g_idx.astype(F32), comb)


def _post(ohf, ohb, ogf, ogb, hgate, gr, sa, sb, x, mod, tiles_per_row, hgn, glan, n2, wa, wb, wo, wr):
    t = x.shape[0]
    tm = TM_POST
    const = lambda shape: pl.BlockSpec(shape, lambda i: (0,) * len(shape))
    tok = lambda w: pl.BlockSpec((tm, w), lambda i: (i, 0))
    return pl.pallas_call(
        _post_body,
        grid=(t // tm,),
        in_specs=[tok(1024)] * 9
                 + [pl.BlockSpec((None, 6, D_MODEL), lambda i: (i // tiles_per_row, 0, 0)),
                    const((1, HG_DV)), const((1, GLA_DV)), const((1, D_MODEL)),
                    const((1024, 1024)), const((1024, 1024)), const((1024, 1024)),
                    const((2, D_MODEL, ROUTER_LANES))],
        out_specs=[tok(1024), pl.BlockSpec((tm * ROW_SUBLANES, 128), lambda i: (i, 0)), tok(ROUTER_LANES)],
        out_shape=[jax.ShapeDtypeStruct((t, D_MODEL), F32), jax.ShapeDtypeStruct((t * ROW_SUBLANES, 128), F32),
                   jax.ShapeDtypeStruct((t, ROUTER_LANES), F32)],
        scratch_shapes=[pltpu.VMEM((tm, 1024), BF16), pltpu.VMEM((tm, 1024), BF16)],
        compiler_params=pltpu.CompilerParams(vmem_limit_bytes=VMEM_LIMIT),
        name="post",
    )(ohf, ohb, ogf, ogb, hgate, gr, sa, sb, x, mod, hgn, glan, n2, wa, wb, wo, wr)


def _route_plan(group, t):
    tr = TR_MOE
    nt = t // tr + N_GROUPS
    onehot = (group[:, None] == jnp.arange(N_GROUPS, dtype=jnp.int32)[None, :]).astype(jnp.int32)
    tiles = (jnp.sum(onehot, axis=0) + tr - 1) // tr
    tile_end = jnp.cumsum(tiles)
    rank = jnp.sum((jnp.cumsum(onehot, axis=0) - onehot) * onehot, axis=1)
    dest = jnp.sum((tile_end - tiles)[None, :] * onehot, axis=1) * tr + rank
    tile_group = jnp.minimum(jnp.sum(jnp.arange(nt, dtype=jnp.int32)[:, None] >= tile_end[None, :], axis=1),
                             N_GROUPS - 1).astype(jnp.int32)
    row = jnp.arange(nt * tr, dtype=jnp.int32)
    spare = t + ((row // tr) % 2) * tr + row % tr
    row_src = jnp.zeros((nt * tr,), jnp.int32).at[dest].set(jnp.arange(t, dtype=jnp.int32))
    row_dst = spare.at[dest].set(jnp.arange(t, dtype=jnp.int32))
    return tile_group, row_src, row_dst


def _moe_body(tg_ref, src_ref, dst_ref, rows_hbm, wg_ref, wu_ref, wd_ref, out_hbm, xbuf, ybuf, gsem, ssem):
    tr = TR_MOE
    i = pl.program_id(0)
    nt = pl.num_programs(0)
    slot = i % 2

    def gather_row(tile, slot_, r):
        tok = src_ref[tile * tr + r]
        return pltpu.make_async_copy(rows_hbm.at[pl.ds(pl.multiple_of(tok * ROW_SUBLANES, ROW_SUBLANES), ROW_SUBLANES)],
                                     xbuf.at[slot_, pl.ds(r * ROW_SUBLANES, ROW_SUBLANES)], gsem.at[slot_])

    def scatter_row(tile, slot_, r):
        tok = dst_ref[tile * tr + r]
        return pltpu.make_async_copy(ybuf.at[slot_, pl.ds(r * OUT_SUBLANES, OUT_SUBLANES)],
                                     out_hbm.at[pl.ds(pl.multiple_of(tok * OUT_SUBLANES, OUT_SUBLANES), OUT_SUBLANES)],
                                     ssem.at[slot_])

    def for_rows(fn):
        def body(r, carry):
            fn(r)
            return carry
        lax.fori_loop(0, tr, body, 0, unroll=8)

    @pl.when(i == 0)
    def _():
        for_rows(lambda r: gather_row(0, 0, r).start())
        n_tok = out_hbm.shape[0] // OUT_SUBLANES - 2 * tr
        ybuf[...] = jnp.zeros_like(ybuf)
        for s in range(2):
            spare = pltpu.make_async_copy(
                ybuf.at[s], out_hbm.at[pl.ds((n_tok + s * tr) * OUT_SUBLANES, tr * OUT_SUBLANES)], ssem.at[s])
            spare.start()
            spare.wait()

    @pl.when(i + 1 < nt)
    def _():
        for_rows(lambda r: gather_row(i + 1, 1 - slot, r).start())

    def gathered(slot_):
        return pltpu.make_async_copy(rows_hbm.at[pl.ds(0, tr * ROW_SUBLANES)], xbuf.at[slot_], gsem.at[slot_])

    def scattered(slot_):
        return pltpu.make_async_copy(ybuf.at[slot_], out_hbm.at[pl.ds(0, tr * OUT_SUBLANES)], ssem.at[slot_])

    gathered(slot).wait()

    @pl.when(i >= 2)
    def _():
        scattered(slot).wait()

    x = jnp.concatenate([xbuf[slot, pl.ds(s, tr, stride=ROW_SUBLANES), :] for s in range(D_MODEL // 128)],
                        axis=1).astype(BF16)
    comb = xbuf[slot, pl.ds(COMB_SUBLANE, tr, stride=ROW_SUBLANES), :]
    lane = lax.broadcasted_iota(jnp.int32, comb.shape, 1)
    lane0 = EXPERT_LANE0 + tg_ref[i] * EXPERTS_PER_GROUP
    acc = jnp.zeros((tr, D_MODEL), F32)
    for e in range(EXPERTS_PER_GROUP):
        col = jnp.sum(jnp.where(lane == lane0 + e, comb, 0.0), axis=-1, keepdims=True)
        hid = (_silu(_dot(x, wg_ref[e])) * _dot(x, wu_ref[e]) * col).astype(BF16)
        acc = acc + _dot(hid, wd_ref[e])
    for s in range(D_MODEL // 128):
        ybuf[slot, pl.ds(s, tr, stride=OUT_SUBLANES), :] = acc[:, s * 128:(s + 1) * 128]

    for_rows(lambda r: scatter_row(i, slot, r).start())

    @pl.when(i == nt - 1)
    def _():
        scattered(slot).wait()
        scattered(1 - slot).wait()


def _moe(rows, plan, wg, wu, wd, t):
    tr = TR_MOE
    tile_group, row_src, row_dst = plan
    nt = tile_group.shape[0]
    group_w = lambda shape: pl.BlockSpec((EXPERTS_PER_GROUP,) + shape, lambda i, tg, src, dst: (tg[i], 0, 0))
    grid_spec = pltpu.PrefetchScalarGridSpec(
        num_scalar_prefetch=3,
        grid=(nt,),
        in_specs=[pl.BlockSpec(memory_space=pl.ANY),
                  group_w((D_MODEL, D_EXPERT)), group_w((D_MODEL, D_EXPERT)), group_w((D_EXPERT, D_MODEL))],
        out_specs=pl.BlockSpec(memory_space=pl.ANY),
        scratch_shapes=[pltpu.VMEM((2, tr * ROW_SUBLANES, 128), F32), pltpu.VMEM((2, tr * OUT_SUBLANES, 128), F32),
                        pltpu.SemaphoreType.DMA((2,)), pltpu.SemaphoreType.DMA((2,))])
    return pl.pallas_call(
        _moe_body,
        grid_spec=grid_spec,
        out_shape=jax.ShapeDtypeStruct(((t + 2 * tr) * OUT_SUBLANES, 128), F32),
        compiler_params=pltpu.CompilerParams(dimension_semantics=("arbitrary",), vmem_limit_bytes=VMEM_LIMIT),
        name="moe",
    )(tile_group, row_src, row_dst, rows, wg, wu, wd)


def _final_body(x1_ref, moe_ref, mod_ref, fn_ref, y_ref):
    tm = x1_ref.shape[0]
    moe = jnp.concatenate([moe_ref[pl.ds(s, tm, stride=OUT_SUBLANES), :] for s in range(D_MODEL // 128)], axis=1)
    y_ref[...] = _rmsnorm(x1_ref[...] + mod_ref[5:6, :] * moe, fn_ref[...])


def _final(x1, moe, mod, tiles_per_row, fn):
    t = x1.shape[0]
    tm = TM_FINAL
    return pl.pallas_call(
        _final_body,
        grid=(t // tm,),
        in_specs=[pl.BlockSpec((tm, D_MODEL), lambda i: (i, 0)),
                  pl.BlockSpec((tm * OUT_SUBLANES, 128), lambda i: (i, 0)),
                  pl.BlockSpec((None, 6, D_MODEL), lambda i: (i // tiles_per_row, 0, 0)),
                  pl.BlockSpec((1, D_MODEL), lambda i: (0, 0))],
        out_specs=pl.BlockSpec((tm, D_MODEL), lambda i: (i, 0)),
        out_shape=jax.ShapeDtypeStruct((t, D_MODEL), F32),
        compiler_params=pltpu.CompilerParams(vmem_limit_bytes=VMEM_LIMIT),
        name="final",
    )(x1, moe, mod, fn)


def _trunk_layer(x, mod, s0_hg, s0_gla, emit_state, w):
    b, l, _ = x.shape
    xt = x.reshape(b * l, D_MODEL)
    per_row = lambda tm: (l // tm) if mod.shape[0] > 1 else (b * l // tm)
    (hq, gf, gb, hv, hgate, gq, gk, gv, gr, laf, lab, sa, sb) = _inproj(
        xt, mod, per_row(TM_PROJ), w["n1"], w["lbp"], w["wmain"], w["waup"], w["ba"])
    seq = lambda a: a.reshape(b, l, a.shape[-1])
    hg = _scan(seq(hq), None, seq(hv), seq(gf), seq(gb), s0_hg, w["scan_consts"], HG_H, HG_DK, HG_DV,
               emit_state)
    gla = _scan(seq(gq), seq(gk), seq(gv), seq(laf), seq(lab), s0_gla, w["scan_consts"], GLA_H, GLA_DK,
                GLA_DV, emit_state)
    flat = lambda a: a.reshape(b * l, a.shape[-1])
    x1, rows, comb = _post(flat(hg[0]), flat(hg[1]), flat(gla[0]), flat(gla[1]), hgate, gr, sa, sb, xt, mod,
                           per_row(TM_POST), w["hgn"], w["glan"], w["n2"], w["wa"], w["wb"], w["wo"], w["wr"])
    plan = _route_plan(comb[:, 0].astype(jnp.int32), b * l)
    moe = _moe(rows, plan, w["wg"], w["wu"], w["wd"], b * l)
    y = _final(x1, moe, mod, per_row(TM_FINAL), w["fn"])
    states = (hg[2], gla[2]) if emit_state else (None, None)
    return y.reshape(b, l, D_MODEL), states


def kernel(x_prompt, x_sample, state_hgrn, state_gla, c, c_ctx, ada_w, ada_b, norm1_g, norm2_g, w_in,
           hg_lb_param, hg_norm_g, gla_wa_up, gla_ba, gla_norm_g, w_br_a, w_br_b, w_out,
           w_router_group, w_router_expert, w_exp_gate, w_exp_up, w_exp_down, final_norm_g):
    nb = c.shape[0]
    cond = jnp.concatenate([c, c_ctx[None, :], jnp.zeros((8 - nb - 1, D_MODEL), F32)], axis=0)
    mod = _modulation(cond, ada_w[0], ada_b).reshape(8, 6, D_MODEL)

    w_in0 = w_in[0]
    ga0 = _C_MA
    zeros = jnp.zeros((GLA_RANK, GLA_H * GLA_DK), F32)
    router = jnp.concatenate(
        [w_router_group[0], jnp.transpose(w_router_expert[0], (1, 0, 2)).reshape(D_MODEL, N_EXPERTS),
         jnp.zeros((D_MODEL, ROUTER_LANES - N_GROUPS - N_EXPERTS), F32)], axis=1)
    w = {
        "n1": norm1_g, "n2": norm2_g, "fn": final_norm_g[None, :], "lbp": hg_lb_param,
        "wmain": jnp.concatenate([w_in0[:, :ga0], w_in0[:, ga0 + 2 * GLA_RANK:], w_in0[:, ga0:ga0 + 2 * GLA_RANK],
                                  jnp.zeros((D_MODEL, GA_LANES - 2 * GLA_RANK), F32)], axis=1).astype(BF16),
        "waup": jnp.concatenate([jnp.concatenate([gla_wa_up[0, 0], zeros], axis=1),
                                 jnp.concatenate([zeros, gla_wa_up[0, 1]], axis=1),
                                 jnp.zeros((GA_LANES - 2 * GLA_RANK, 2 * GLA_H * GLA_DK), F32)],
                                axis=0).astype(BF16),
        "ba": gla_ba[0].reshape(1, 2 * GLA_H * GLA_DK),
        "hgn": hg_norm_g, "glan": gla_norm_g,
        "wa": w_br_a[0].astype(BF16), "wb": w_br_b[0].astype(BF16), "wo": w_out[0].astype(BF16),
        "wr": jnp.stack([router.astype(BF16), (router - router.astype(BF16).astype(F32)).astype(BF16)]),
        "wg": w_exp_gate[0].astype(BF16), "wu": w_exp_up[0].astype(BF16), "wd": w_exp_down[0].astype(BF16),
        "scan_consts": _scan_constants(),
    }
    y_prompt, (st_hg, st_gla) = _trunk_layer(x_prompt, mod[nb:nb + 1], None, None, True, w)
    y_sample, _ = _trunk_layer(x_sample, mod[:nb], state_hgrn[:, 0], state_gla[:, 0], False, w)
    return y_prompt, y_sample, st_hg[:, None], st_gla[:, None]
```

```python
import functools

import numpy as np
import jax
import jax.numpy as jnp
from jax import lax
from jax.experimental import pallas as pl
from jax.experimental.pallas import tpu as pltpu

F32 = jnp.float32
BF16 = jnp.bfloat16

D_MODEL = 1024
CHUNK = 64
N_LEVELS = 6
CHUNKS_PER_STEP = 2
HG_H, HG_DK, HG_DV = 8, 128, 128
GLA_H, GLA_DK, GLA_DV = 4, 128, 256
GLA_RANK = 16
GLA_GATE_NORM = 16.0
N_GROUPS = 4
EXPERTS_PER_GROUP = 8
N_EXPERTS = N_GROUPS * EXPERTS_PER_GROUP
D_EXPERT = D_MODEL // 4
EPS = 1e-6
ROUTER_LANES = 128
EXPERT_LANE0 = N_GROUPS
VMEM_LIMIT = 56 * 1024 * 1024

TM_PROJ = 256
TM_POST = 256
TM_FINAL = 512
TR_MOE = 256
PLAN_BLOCK = 256
OUT_SUBLANES = D_MODEL // 128
COMB_SUBLANE = OUT_SUBLANES
ROW_SUBLANES = 16


def _sigmoid(x):
    return 1.0 / (1.0 + jnp.exp(-x))


def _silu(x):
    return x * _sigmoid(x)


def _log_sigmoid(x):
    return jnp.minimum(x, 0.0) - jnp.log(1.0 + jnp.exp(-jnp.abs(x)))


def _rmsnorm(x, g):
    return x * lax.rsqrt(jnp.mean(x * x, axis=-1, keepdims=True) + EPS) * g


def _dot(a, b):
    return jnp.dot(a, b, preferred_element_type=F32)


def _dot_exact(a, b):
    return jnp.dot(a, b, preferred_element_type=F32, precision=lax.Precision.HIGHEST)


def _dot_nt(a, b):
    return lax.dot_general(a, b, (((1,), (1,)), ((), ())), preferred_element_type=F32)


def _dot_tn(a, b):
    return lax.dot_general(a, b, (((0,), (0,)), ((), ())), preferred_element_type=F32)


def _mod_body(c_ref, w_ref, b_ref, o_ref):
    o_ref[...] = _dot_exact(_silu(c_ref[...]), w_ref[...]) + b_ref[...]


def _modulation(cond, ada_w, ada_b):
    rows, n = cond.shape[0], ada_w.shape[1]
    tn = n // 4
    return pl.pallas_call(
        _mod_body,
        grid=(n // tn,),
        in_specs=[pl.BlockSpec((rows, D_MODEL), lambda j: (0, 0)),
                  pl.BlockSpec((D_MODEL, tn), lambda j: (0, j)),
                  pl.BlockSpec((1, tn), lambda j: (0, j))],
        out_specs=pl.BlockSpec((rows, tn), lambda j: (0, j)),
        out_shape=jax.ShapeDtypeStruct((rows, n), F32),
        compiler_params=pltpu.CompilerParams(vmem_limit_bytes=VMEM_LIMIT),
        name="modulation",
    )(cond, ada_w, ada_b)


_C_HQ, _C_HFF, _C_HFB, _C_HI, _C_HGATE = 0, 1024, 2048, 3072, 4096
_C_GQ, _C_GK, _C_GV, _C_GR, _C_MA, _C_MB = 5120, 5632, 6144, 7168, 8192, 9216
_C_GA = 10240
GA_LANES = 128
W_MAIN = _C_GA + GA_LANES


def _inproj_body(x_ref, mod_ref, n1_ref, lbp_ref, wmain_ref, waup_ref, ba_ref,
                 hq_ref, gf_ref, gb_ref, hv_ref, hgate_ref, gq_ref, gk_ref, gv_ref, gr_ref,
                 laf_ref, lab_ref, sa_ref, sb_ref):
    x = x_ref[...]
    h = _rmsnorm(x, n1_ref[...]) * (1.0 + mod_ref[1:2, :]) + mod_ref[0:1, :]
    hb = h.astype(BF16)

    def proj(c0, w):
        return _dot(hb, wmain_ref[:, c0:c0 + w])

    p0, p1 = lbp_ref[0:1, :], lbp_ref[1:2, :]
    pm = jnp.maximum(p0, p1)
    e0, e1 = jnp.exp(p0 - pm), jnp.exp(p1 - pm)
    lb = e0 / (e0 + e1)

    hq_ref[...] = (proj(_C_HQ, 1024) * HG_DK ** -0.5).astype(BF16)
    gf_ref[...] = jnp.log(lb + (1.0 - lb) * _sigmoid(proj(_C_HFF, 1024)))
    gb_ref[...] = jnp.log(lb + (1.0 - lb) * _sigmoid(proj(_C_HFB, 1024)))
    hv_ref[...] = proj(_C_HI, 1024).astype(BF16)
    hgate_ref[...] = _silu(proj(_C_HGATE, 1024)).astype(BF16)
    gq_ref[...] = (proj(_C_GQ, 512) * GLA_DK ** -0.5).astype(BF16)
    gk_ref[...] = proj(_C_GK, 512).astype(BF16)
    gv_ref[...] = proj(_C_GV, 1024).astype(BF16)
    gr_ref[...] = _silu(proj(_C_GR, 1024)).astype(BF16)
    sa_ref[...] = _sigmoid(proj(_C_MA, 1024)).astype(BF16)
    sb_ref[...] = _sigmoid(proj(_C_MB, 1024)).astype(BF16)

    ga = proj(_C_GA, GA_LANES).astype(BF16)
    la = _log_sigmoid(_dot(ga, waup_ref[...]) + ba_ref[...]) * (1.0 / GLA_GATE_NORM)
    laf_ref[...] = la[:, :512]
    lab_ref[...] = la[:, 512:]


def _inproj(x, mod, tiles_per_row, n1, lbp, wmain, waup, ba):
    t = x.shape[0]
    tm = TM_PROJ
    const = lambda shape: pl.BlockSpec(shape, lambda i: (0,) * len(shape))
    tok = lambda w: pl.BlockSpec((tm, w), lambda i: (i, 0))
    widths = (1024, 1024, 1024, 1024, 1024, 512, 512, 1024, 1024, 512, 512, 1024, 1024)
    dtypes = (BF16, F32, F32, BF16, BF16, BF16, BF16, BF16, BF16, F32, F32, BF16, BF16)
    return pl.pallas_call(
        _inproj_body,
        grid=(t // tm,),
        in_specs=[tok(D_MODEL),
                  pl.BlockSpec((None, 6, D_MODEL), lambda i: (i // tiles_per_row, 0, 0)),
                  const((1, D_MODEL)), const((2, D_MODEL)),
                  pl.BlockSpec((D_MODEL, W_MAIN), lambda i: (0, 0), pipeline_mode=pl.Buffered(1)),
                  const((GA_LANES, 1024)), const((1, 1024))],
        out_specs=[tok(w) for w in widths],
        out_shape=[jax.ShapeDtypeStruct((t, w), dt) for w, dt in zip(widths, dtypes)],
        compiler_params=pltpu.CompilerParams(vmem_limit_bytes=VMEM_LIMIT),
        name="inproj",
    )(x, mod, n1, lbp, wmain, waup, ba)


N_MXU_LEVELS = 2
ROW_GROUP = 16
START_SLOT, END_SLOT = N_LEVELS, N_LEVELS + 1
LOG2E = 1.4426950408889634


def _boundary_row(t, h, d):
    base = (t // (2 * h)) * 2 * h
    return base + h - 1 if d == 0 else base + h


def _scan_constants():
    c = CHUNK
    tri = np.tril(np.ones((c, c), np.float32))
    m = [tri] + [tri[[_boundary_row(t, 1 << l, 0) for t in range(c)]] for l in range(N_MXU_LEVELS)]
    m = np.stack(m)
    lvl = np.full((c, c), N_LEVELS + 1, np.int32)
    for t in range(c):
        lvl[t, t] = N_LEVELS
        for s in range(t):
            lvl[t, s] = int(np.floor(np.log2(t ^ s)))
    m_fwd = m.reshape(-1, c)
    m_bwd = m[:, ::-1, ::-1].reshape(-1, c)
    return (jnp.asarray(np.stack([m_fwd, m_bwd]), BF16), jnp.asarray(np.stack([lvl, lvl.T])))


def _neg_abs(x):
    bits = lax.bitcast_convert_type(x, jnp.uint32) | jnp.uint32(0x80000000)
    return lax.bitcast_convert_type(bits, F32)


def _scan_prepare(u, d, rows, q_ref, k_ref, g_ref, m_ref, cb_ref, qf_ref, kf_ref, p_ref):
    c = CHUNK
    w = g_ref.shape[-1]
    g = g_ref[rows, :]
    g1 = g.astype(BF16)
    g2 = (g - g1.astype(F32)).astype(BF16)
    m = m_ref[d]
    cb_ref[u] = (_dot(m, g1) + _dot(m, g2)) * LOG2E
    qf_ref[u] = q_ref[rows, :].astype(F32)
    kf_ref[u] = (1.0 - jnp.exp(g)) if k_ref is None else k_ref[rows, :].astype(F32)
    bcast = lambda r, n: jnp.broadcast_to(cb_ref[u, r:r + 1, :], (n, w))
    for i in range(c // ROW_GROUP):
        t0 = ROW_GROUP * i
        grp = slice(t0, t0 + ROW_GROUP)
        cum = cb_ref[u, grp, :]
        q = qf_ref[u, grp, :]
        k = kf_ref[u, grp, :]
        rowid = t0 + lax.broadcasted_iota(jnp.int32, (ROW_GROUP, w), 0)
        for l in range(N_LEVELS):
            h = 1 << l
            if l < N_MXU_LEVELS:
                bnd = cb_ref[u, c * (l + 1) + t0:c * (l + 1) + t0 + ROW_GROUP, :]
            else:
                r0, r1 = _boundary_row(t0, h, d), _boundary_row(t0 + 8, h, d)
                bnd = bcast(r0, ROW_GROUP) if r0 == r1 else jnp.concatenate([bcast(r0, 8), bcast(r1, 8)], axis=0)
            e = jnp.exp2(_neg_abs(cum - bnd))
            if h >= ROW_GROUP:
                src = q if ((t0 & h) != 0) == (d == 0) else k
            else:
                upper = (rowid & h) != 0
                src = jnp.where(upper if d == 0 else jnp.logical_not(upper), q, k)
            p_ref[u, l, grp, :] = (src * e).astype(BF16)
        p_ref[u, START_SLOT, grp, :] = (q * jnp.exp2(cum)).astype(BF16)
        edge = c - 1 if d == 0 else 0
        p_ref[u, END_SLOT, grp, :] = (k * jnp.exp2(_neg_abs(cum - bcast(edge, ROW_GROUP)))).astype(BF16)


def _scan_scores(u, rows, q_ref, masks, kf_ref, p_ref, a_ref, n_heads, dk):
    for h in range(n_heads):
        ks = slice(h * dk, (h + 1) * dk)
        a = jnp.where(masks[N_LEVELS], _dot_nt(q_ref[rows, ks], kf_ref[u, :, ks].astype(BF16)), 0.0)
        for l in range(N_LEVELS):
            p = p_ref[u, l, :, ks]
            a = jnp.where(masks[l], _dot_nt(p, p), a)
        a_ref[u, h] = a.astype(BF16)


def _scan_outputs(u, d, rows, v_ref, cb_ref, p_ref, a_ref, st_ref, o_ref, n_heads, dk, dv):
    edge = CHUNK - 1 if d == 0 else 0
    for h in range(n_heads):
        ks = slice(h * dk, (h + 1) * dk)
        vs = slice(h * dv, (h + 1) * dv)
        v = v_ref[rows, vs]
        st = st_ref[d, h]
        o_ref[rows, vs] = _dot(a_ref[u, h], v) + _dot_nt(p_ref[u, START_SLOT, :, ks], st.astype(BF16))
        st_ref[d, h] = st * jnp.exp2(cb_ref[u, edge:edge + 1, ks]) + _dot_tn(v, p_ref[u, END_SLOT, :, ks])


def _scan_body(*refs, n_heads, dk, dv, derive_k, zero_init, emit_state):
    refs = list(refs)
    q_refs = (refs.pop(0), refs.pop(0))
    k_refs = (None, None) if derive_k else (refs.pop(0), refs.pop(0))
    v_refs = (refs.pop(0), refs.pop(0))
    g_refs = (refs.pop(0), refs.pop(0))
    m_ref, lvl_ref = refs.pop(0), refs.pop(0)
    s0_ref = None if zero_init else refs.pop(0)
    o_refs = (refs.pop(0), refs.pop(0))
    sout_ref = refs.pop(0) if emit_state else None
    st_ref, cb_ref, qf_ref, kf_ref, p_ref, a_ref = refs
    n = pl.program_id(1)

    @pl.when(n == 0)
    def _():
        for d in range(2):
            for h in range(n_heads):
                st_ref[d, h] = jnp.zeros((dv, dk), F32) if zero_init else s0_ref[d, h].T

    cps = CHUNKS_PER_STEP
    rows = lambda c: slice(c * CHUNK, (c + 1) * CHUNK)
    for d in range(2):
        for c in range(cps):
            _scan_prepare(d * cps + c, d, rows(c), q_refs[d], k_refs[d], g_refs[d], m_ref, cb_ref, qf_ref, kf_ref,
                          p_ref)
    for d in range(2):
        masks = [lvl_ref[d] == l for l in range(N_LEVELS + 1)]
        for c in range(cps):
            _scan_scores(d * cps + c, rows(c), q_refs[d], masks, kf_ref, p_ref, a_ref, n_heads, dk)
    for d in range(2):
        for c in (range(cps) if d == 0 else reversed(range(cps))):
            _scan_outputs(d * cps + c, d, rows(c), v_refs[d], cb_ref, p_ref, a_ref, st_ref, o_refs[d], n_heads,
                          dk, dv)

    if emit_state:
        @pl.when(n == pl.num_programs(1) - 1)
        def _():
            for d in range(2):
                for h in range(n_heads):
                    sout_ref[d, h] = st_ref[d, h].T


def _scan(q, k, v, g_f, g_b, s0, consts, n_heads, dk, dv, emit_state):
    b, l, _ = q.shape
    step = CHUNKS_PER_STEP * CHUNK
    n = l // step
    slots = 2 * CHUNKS_PER_STEP
    m_all, lvl = consts
    fwd = lambda w: pl.BlockSpec((None, step, w), lambda i, j: (i, j, 0))
    bwd = lambda w: pl.BlockSpec((None, step, w), lambda i, j: (i, n - 1 - j, 0))
    const = lambda shape: pl.BlockSpec(shape, lambda i, j: (0,) * len(shape))
    state = pl.BlockSpec((None, 2, n_heads, dk, dv), lambda i, j: (i, 0, 0, 0, 0))
    wk, wv = n_heads * dk, n_heads * dv
    args, in_specs = [q, q], [fwd(wk), bwd(wk)]
    if k is not None:
        args += [k, k]
        in_specs += [fwd(wk), bwd(wk)]
    args += [v, v, g_f, g_b, m_all, lvl]
    in_specs += [fwd(wv), bwd(wv), fwd(wk), bwd(wk), const(m_all.shape), const(lvl.shape)]
    if s0 is not None:
        args.append(s0)
        in_specs.append(state)
    out_specs = [fwd(wv), bwd(wv)]
    out_shape = [jax.ShapeDtypeStruct((b, l, wv), F32)] * 2
    if emit_state:
        out_specs.append(state)
        out_shape.append(jax.ShapeDtypeStruct((b, 2, n_heads, dk, dv), F32))
    body = functools.partial(_scan_body, n_heads=n_heads, dk=dk, dv=dv, derive_k=k is None,
                             zero_init=s0 is None, emit_state=emit_state)
    return pl.pallas_call(
        body,
        grid=(b, n),
        in_specs=in_specs,
        out_specs=out_specs,
        out_shape=out_shape,
        scratch_shapes=[pltpu.VMEM((2, n_heads, dv, dk), F32),
                        pltpu.VMEM((slots, (1 + N_MXU_LEVELS) * CHUNK, wk), F32),
                        pltpu.VMEM((slots, CHUNK, wk), F32), pltpu.VMEM((slots, CHUNK, wk), F32),
                        pltpu.VMEM((slots, N_LEVELS + 2, CHUNK, wk), BF16),
                        pltpu.VMEM((slots, n_heads, CHUNK, CHUNK), BF16)],
        compiler_params=pltpu.CompilerParams(dimension_semantics=("arbitrary", "arbitrary"),
                                             vmem_limit_bytes=VMEM_LIMIT),
        name="scan_hgrn" if k is None else "scan_gla",
    )(*args)


def _post_body(ohf_ref, ohb_ref, ogf_ref, ogb_ref, hgate_ref, gr_ref, sa_ref, sb_ref, x_ref, mod_ref,
               hgn_ref, glan_ref, n2_ref, wa_ref, wb_ref, wo_ref, wr_ref,
               x1_ref, rows_ref, comb_ref, yh_ref, yg_ref):
    for h in range(HG_H):
        s = slice(h * HG_DV, (h + 1) * HG_DV)
        o = ohf_ref[:, s] + ohb_ref[:, s]
        yh_ref[:, s] = (_rmsnorm(o, hgn_ref[...]) * hgate_ref[:, s].astype(F32)).astype(BF16)
    for h in range(GLA_H):
        s = slice(h * GLA_DV, (h + 1) * GLA_DV)
        o = ogf_ref[:, s] + ogb_ref[:, s]
        yg_ref[:, s] = (_rmsnorm(o, glan_ref[...]) * gr_ref[:, s].astype(F32)).astype(BF16)
    merged = (sa_ref[...].astype(F32) * _dot(yh_ref[...], wa_ref[...])
              + sb_ref[...].astype(F32) * _dot(yg_ref[...], wb_ref[...]))
    mix = _dot(merged.astype(BF16), wo_ref[...])
    x1 = x_ref[...] + mod_ref[2:3, :] * mix
    x1_ref[...] = x1
    h2 = _rmsnorm(x1, n2_ref[...]) * (1.0 + mod_ref[4:5, :]) + mod_ref[3:4, :]
    h2_hi = h2.astype(BF16)
    tm = h2.shape[0]
    for s in range(D_MODEL // 128):
        rows_ref[pl.ds(s, tm, stride=ROW_SUBLANES), :] = h2[:, s * 128:(s + 1) * 128]
    for s in range(COMB_SUBLANE + 1, ROW_SUBLANES):
        rows_ref[pl.ds(s, tm, stride=ROW_SUBLANES), :] = jnp.zeros((tm, 128), F32)

    h2_lo = (h2 - h2_hi.astype(F32)).astype(BF16)
    logits = _dot(h2_hi, wr_ref[0]) + _dot(h2_hi, wr_ref[1]) + _dot(h2_lo, wr_ref[0])
    lane = lax.broadcasted_iota(jnp.int32, logits.shape, 1)
    neg = -jnp.inf
    first = lambda hit: jnp.min(jnp.where(hit, lane, ROUTER_LANES), axis=-1, keepdims=True)
    gl = jnp.where(lane < N_GROUPS, logits, neg)
    gmax = jnp.max(gl, axis=-1, keepdims=True)
    p_top = 1.0 / jnp.sum(jnp.exp(gl - gmax), axis=-1, keepdims=True)
    g_idx = first(gl == gmax)
    in_group = ((lane >= EXPERT_LANE0) & (lane < EXPERT_LANE0 + N_EXPERTS)
                & (((lane - EXPERT_LANE0) >> 3) == g_idx))
    sl = jnp.where(in_group, logits, neg)
    m1 = jnp.max(sl, axis=-1, keepdims=True)
    i1 = first(sl == m1)
    sl2 = jnp.where(lane == i1, neg, sl)
    m2 = jnp.max(sl2, axis=-1, keepdims=True)
    i2 = first(sl2 == m2)
    r = jnp.exp(m2 - m1)
    w1 = p_top / (1.0 + r)
    comb = jnp.where(lane == i1, w1, jnp.where(lane == i2, w1 * r, 0.0))
    rows_ref[pl.ds(COMB_SUBLANE, tm, stride=ROW_SUBLANES), :] = comb
    comb_ref[...] = jnp.where(lane == 0, g_idx.astype(F32), comb)


def _post(ohf, ohb, ogf, ogb, hgate, gr, sa, sb, x, mod, tiles_per_row, hgn, glan, n2, wa, wb, wo, wr):
    t = x.shape[0]
    tm = TM_POST
    const = lambda shape: pl.BlockSpec(shape, lambda i: (0,) * len(shape))
    tok = lambda w: pl.BlockSpec((tm, w), lambda i: (i, 0))
    return pl.pallas_call(
        _post_body,
        grid=(t // tm,),
        in_specs=[tok(1024)] * 9
                 + [pl.BlockSpec((None, 6, D_MODEL), lambda i: (i // tiles_per_row, 0, 0)),
                    const((1, HG_DV)), const((1, GLA_DV)), const((1, D_MODEL)),
                    const((1024, 1024)), const((1024, 1024)), const((1024, 1024)),
                    const((2, D_MODEL, ROUTER_LANES))],
        out_specs=[tok(1024), pl.BlockSpec((tm * ROW_SUBLANES, 128), lambda i: (i, 0)), tok(ROUTER_LANES)],
        out_shape=[jax.ShapeDtypeStruct((t, D_MODEL), F32), jax.ShapeDtypeStruct((t * ROW_SUBLANES, 128), F32),
                   jax.ShapeDtypeStruct((t, ROUTER_LANES), F32)],
        scratch_shapes=[pltpu.VMEM((tm, 1024), BF16), pltpu.VMEM((tm, 1024), BF16)],
        compiler_params=pltpu.CompilerParams(vmem_limit_bytes=VMEM_LIMIT),
        name="post",
    )(ohf, ohb, ogf, ogb, hgate, gr, sa, sb, x, mod, hgn, glan, n2, wa, wb, wo, wr)


def _route_plan(group, t):
    tr, blk = TR_MOE, PLAN_BLOCK
    nt, nb = t // tr + N_GROUPS, t // blk
    i32 = jnp.int32
    gids = jnp.arange(N_GROUPS, dtype=i32)
    onehot = (group[:, None] == gids[None, :]).astype(i32).reshape(nb, blk, N_GROUPS)
    within = jnp.cumsum(onehot, axis=1)
    blk_end = jnp.cumsum(within[:, -1, :], axis=0)
    blk_start = blk_end - within[:, -1, :]
    counts = blk_end[-1]
    tiles = (counts + tr - 1) // tr
    tile_end = jnp.cumsum(tiles)
    tile_group = jnp.minimum(jnp.sum(jnp.arange(nt, dtype=i32)[:, None] >= tile_end[None, :], axis=1),
                             N_GROUPS - 1).astype(i32)
    row = jnp.arange(nt * tr, dtype=i32)
    gsel = (jnp.repeat(tile_group, tr)[:, None] == gids[None, :]).astype(i32)
    pick = lambda per_group: jnp.sum(gsel * per_group[None, :], axis=1)
    k = row - pick(tile_end - tiles) * tr
    valid = k < pick(counts)
    per_block = lambda table: jnp.sum(gsel[:, None, :] * table[None, :, :], axis=2)
    b_row = jnp.minimum(jnp.sum(per_block(blk_end) <= k[:, None], axis=1), nb - 1)
    bsel = (b_row[:, None] == jnp.arange(nb, dtype=i32)[None, :]).astype(i32)
    k_local = k - jnp.sum(bsel * per_block(blk_start), axis=1)
    sel = (bsel[:, :, None] * gsel[:, None, :]).reshape(nt * tr, nb * N_GROUPS).astype(F32)
    table = jnp.transpose(within, (0, 2, 1)).reshape(nb * N_GROUPS, blk).astype(F32)
    ranks = jnp.dot(sel, table)
    t_local = jnp.sum(ranks <= k_local[:, None].astype(F32), axis=1).astype(i32)
    row_src = jnp.where(valid, b_row * blk + t_local, 0)
    spare = t + ((row // tr) % 2) * tr + row % tr
    return tile_group, row_src, jnp.where(valid, row_src, spare)


def _moe_body(tg_ref, src_ref, dst_ref, rows_hbm, wg_ref, wu_ref, wd_ref, out_hbm, xbuf, ybuf, gsem, ssem):
    tr = TR_MOE
    i = pl.program_id(0)
    nt = pl.num_programs(0)
    slot = i % 2

    def gather_row(tile, slot_, r):
        tok = src_ref[tile * tr + r]
        return pltpu.make_async_copy(rows_hbm.at[pl.ds(pl.multiple_of(tok * ROW_SUBLANES, ROW_SUBLANES), ROW_SUBLANES)],
                                     xbuf.at[slot_, pl.ds(r * ROW_SUBLANES, ROW_SUBLANES)], gsem.at[slot_])

    def scatter_row(tile, slot_, r):
        tok = dst_ref[tile * tr + r]
        return pltpu.make_async_copy(ybuf.at[slot_, pl.ds(r * OUT_SUBLANES, OUT_SUBLANES)],
                                     out_hbm.at[pl.ds(pl.multiple_of(tok * OUT_SUBLANES, OUT_SUBLANES), OUT_SUBLANES)],
                                     ssem.at[slot_])

    def for_rows(fn):
        def body(r, carry):
            fn(r)
            return carry
        lax.fori_loop(0, tr, body, 0, unroll=8)

    @pl.when(i == 0)
    def _():
        for_rows(lambda r: gather_row(0, 0, r).start())
        n_tok = out_hbm.shape[0] // OUT_SUBLANES - 2 * tr
        ybuf[...] = jnp.zeros_like(ybuf)
        for s in range(2):
            spare = pltpu.make_async_copy(
                ybuf.at[s], out_hbm.at[pl.ds((n_tok + s * tr) * OUT_SUBLANES, tr * OUT_SUBLANES)], ssem.at[s])
            spare.start()
            spare.wait()

    @pl.when(i + 1 < nt)
    def _():
        for_rows(lambda r: gather_row(i + 1, 1 - slot, r).start())

    def gathered(slot_):
        return pltpu.make_async_copy(rows_hbm.at[pl.ds(0, tr * ROW_SUBLANES)], xbuf.at[slot_], gsem.at[slot_])

    def scattered(slot_):
        return pltpu.make_async_copy(ybuf.at[slot_], out_hbm.at[pl.ds(0, tr * OUT_SUBLANES)], ssem.at[slot_])

    gathered(slot).wait()

    @pl.when(i >= 2)
    def _():
        scattered(slot).wait()

    x = jnp.concatenate([xbuf[slot, pl.ds(s, tr, stride=ROW_SUBLANES), :] for s in range(D_MODEL // 128)],
                        axis=1).astype(BF16)
    comb = xbuf[slot, pl.ds(COMB_SUBLANE, tr, stride=ROW_SUBLANES), :]
    lane = lax.broadcasted_iota(jnp.int32, comb.shape, 1)
    lane0 = EXPERT_LANE0 + tg_ref[i] * EXPERTS_PER_GROUP
    acc = jnp.zeros((tr, D_MODEL), F32)
    for e in range(EXPERTS_PER_GROUP):
        col = jnp.sum(jnp.where(lane == lane0 + e, comb, 0.0), axis=-1, keepdims=True)
        hid = (_silu(_dot(x, wg_ref[e])) * _dot(x, wu_ref[e]) * col).astype(BF16)
        acc = acc + _dot(hid, wd_ref[e])
    for s in range(D_MODEL // 128):
        ybuf[slot, pl.ds(s, tr, stride=OUT_SUBLANES), :] = acc[:, s * 128:(s + 1) * 128]

    for_rows(lambda r: scatter_row(i, slot, r).start())

    @pl.when(i == nt - 1)
    def _():
        scattered(slot).wait()
        scattered(1 - slot).wait()


def _moe(rows, plan, wg, wu, wd, t):
    tr = TR_MOE
    tile_group, row_src, row_dst = plan
    nt = tile_group.shape[0]
    group_w = lambda shape: pl.BlockSpec((EXPERTS_PER_GROUP,) + shape, lambda i, tg, src, dst: (tg[i], 0, 0))
    grid_spec = pltpu.PrefetchScalarGridSpec(
        num_scalar_prefetch=3,
        grid=(nt,),
        in_specs=[pl.BlockSpec(memory_space=pl.ANY),
                  group_w((D_MODEL, D_EXPERT)), group_w((D_MODEL, D_EXPERT)), group_w((D_EXPERT, D_MODEL))],
        out_specs=pl.BlockSpec(memory_space=pl.ANY),
        scratch_shapes=[pltpu.VMEM((2, tr * ROW_SUBLANES, 128), F32), pltpu.VMEM((2, tr * OUT_SUBLANES, 128), F32),
                        pltpu.SemaphoreType.DMA((2,)), pltpu.SemaphoreType.DMA((2,))])
    return pl.pallas_call(
        _moe_body,
        grid_spec=grid_spec,
        out_shape=jax.ShapeDtypeStruct(((t + 2 * tr) * OUT_SUBLANES, 128), F32),
        compiler_params=pltpu.CompilerParams(dimension_semantics=("arbitrary",), vmem_limit_bytes=VMEM_LIMIT),
        name="moe",
    )(tile_group, row_src, row_dst, rows, wg, wu, wd)


def _final_body(x1_ref, moe_ref, mod_ref, fn_ref, y_ref):
    tm = x1_ref.shape[0]
    moe = jnp.concatenate([moe_ref[pl.ds(s, tm, stride=OUT_SUBLANES), :] for s in range(D_MODEL // 128)], axis=1)
    y_ref[...] = _rmsnorm(x1_ref[...] + mod_ref[5:6, :] * moe, fn_ref[...])


def _final(x1, moe, mod, tiles_per_row, fn):
    t = x1.shape[0]
    tm = TM_FINAL
    return pl.pallas_call(
        _final_body,
        grid=(t // tm,),
        in_specs=[pl.BlockSpec((tm, D_MODEL), lambda i: (i, 0)),
                  pl.BlockSpec((tm * OUT_SUBLANES, 128), lambda i: (i, 0)),
                  pl.BlockSpec((None, 6, D_MODEL), lambda i: (i // tiles_per_row, 0, 0)),
                  pl.BlockSpec((1, D_MODEL), lambda i: (0, 0))],
        out_specs=pl.BlockSpec((tm, D_MODEL), lambda i: (i, 0)),
        out_shape=jax.ShapeDtypeStruct((t, D_MODEL), F32),
        compiler_params=pltpu.CompilerParams(vmem_limit_bytes=VMEM_LIMIT),
        name="final",
    )(x1, moe, mod, fn)


def _trunk_layer(x, mod, s0_hg, s0_gla, emit_state, w):
    b, l, _ = x.shape
    xt = x.reshape(b * l, D_MODEL)
    per_row = lambda tm: (l // tm) if mod.shape[0] > 1 else (b * l // tm)
    (hq, gf, gb, hv, hgate, gq, gk, gv, gr, laf, lab, sa, sb) = _inproj(
        xt, mod, per_row(TM_PROJ), w["n1"], w["lbp"], w["wmain"], w["waup"], w["ba"])
    seq = lambda a: a.reshape(b, l, a.shape[-1])
    hg = _scan(seq(hq), None, seq(hv), seq(gf), seq(gb), s0_hg, w["scan_consts"], HG_H, HG_DK, HG_DV,
               emit_state)
    gla = _scan(seq(gq), seq(gk), seq(gv), seq(laf), seq(lab), s0_gla, w["scan_consts"], GLA_H, GLA_DK,
                GLA_DV, emit_state)
    flat = lambda a: a.reshape(b * l, a.shape[-1])
    x1, rows, comb = _post(flat(hg[0]), flat(hg[1]), flat(gla[0]), flat(gla[1]), hgate, gr, sa, sb, xt, mod,
                           per_row(TM_POST), w["hgn"], w["glan"], w["n2"], w["wa"], w["wb"], w["wo"], w["wr"])
    plan = _route_plan(comb[:, 0].astype(jnp.int32), b * l)
    moe = _moe(rows, plan, w["wg"], w["wu"], w["wd"], b * l)
    y = _final(x1, moe, mod, per_row(TM_FINAL), w["fn"])
    states = (hg[2], gla[2]) if emit_state else (None, None)
    return y.reshape(b, l, D_MODEL), states


def kernel(x_prompt, x_sample, state_hgrn, state_gla, c, c_ctx, ada_w, ada_b, norm1_g, norm2_g, w_in,
           hg_lb_param, hg_norm_g, gla_wa_up, gla_ba, gla_norm_g, w_br_a, w_br_b, w_out,
           w_router_group, w_router_expert, w_exp_gate, w_exp_up, w_exp_down, final_norm_g):
    nb = c.shape[0]
    cond = jnp.concatenate([c, c_ctx[None, :], jnp.zeros((8 - nb - 1, D_MODEL), F32)], axis=0)
    mod = _modulation(cond, ada_w[0], ada_b).reshape(8, 6, D_MODEL)

    w_in0 = w_in[0]
    ga0 = _C_MA
    zeros = jnp.zeros((GLA_RANK, GLA_H * GLA_DK), F32)
    router = jnp.concatenate(
        [w_router_group[0], jnp.transpose(w_router_expert[0], (1, 0, 2)).reshape(D_MODEL, N_EXPERTS),
         jnp.zeros((D_MODEL, ROUTER_LANES - N_GROUPS - N_EXPERTS), F32)], axis=1)
    w = {
        "n1": norm1_g, "n2": norm2_g, "fn": final_norm_g[None, :], "lbp": hg_lb_param,
        "wmain": jnp.concatenate([w_in0[:, :ga0], w_in0[:, ga0 + 2 * GLA_RANK:], w_in0[:, ga0:ga0 + 2 * GLA_RANK],
                                  jnp.zeros((D_MODEL, GA_LANES - 2 * GLA_RANK), F32)], axis=1).astype(BF16),
        "waup": jnp.concatenate([jnp.concatenate([gla_wa_up[0, 0], zeros], axis=1),
                                 jnp.concatenate([zeros, gla_wa_up[0, 1]], axis=1),
                                 jnp.zeros((GA_LANES - 2 * GLA_RANK, 2 * GLA_H * GLA_DK), F32)],
                                axis=0).astype(BF16),
        "ba": gla_ba[0].reshape(1, 2 * GLA_H * GLA_DK),
        "hgn": hg_norm_g, "glan": gla_norm_g,
        "wa": w_br_a[0].astype(BF16), "wb": w_br_b[0].astype(BF16), "wo": w_out[0].astype(BF16),
        "wr": jnp.stack([router.astype(BF16), (router - router.astype(BF16).astype(F32)).astype(BF16)]),
        "wg": w_exp_gate[0].astype(BF16), "wu": w_exp_up[0].astype(BF16), "wd": w_exp_down[0].astype(BF16),
        "scan_consts": _scan_constants(),
    }
    y_prompt, (st_hg, st_gla) = _trunk_layer(x_prompt, mod[nb:nb + 1], None, None, True, w)
    y_sample, _ = _trunk_layer(x_sample, mod[:nb], state_hgrn[:, 0], state_gla[:, 0], False, w)
    return y_prompt, y_sample, st_hg[:, None], st_gla[:, None]
```

```python
import functools

import numpy as np
import jax
import jax.numpy as jnp
from jax import lax
from jax.experimental import pallas as pl
from jax.experimental.pallas import tpu as pltpu

F32 = jnp.float32
BF16 = jnp.bfloat16

D_MODEL = 1024
CHUNK = 64
N_LEVELS = 6
CHUNKS_PER_STEP = 2
HG_H, HG_DK, HG_DV = 8, 128, 128
GLA_H, GLA_DK, GLA_DV = 4, 128, 256
GLA_RANK = 16
GLA_GATE_NORM = 16.0
N_GROUPS = 4
EXPERTS_PER_GROUP = 8
N_EXPERTS = N_GROUPS * EXPERTS_PER_GROUP
D_EXPERT = D_MODEL // 4
EPS = 1e-6
ROUTER_LANES = 128
EXPERT_LANE0 = N_GROUPS
VMEM_LIMIT = 56 * 1024 * 1024

TM_PROJ = 256
TM_POST = 256
TM_FINAL = 512
TR_MOE = 256
PLAN_BLOCK = 256
OUT_SUBLANES = D_MODEL // 128
COMB_SUBLANE = OUT_SUBLANES
ROW_SUBLANES = 16


def _sigmoid(x):
    return 1.0 / (1.0 + jnp.exp(-x))


def _silu(x):
    return x * _sigmoid(x)


def _log_sigmoid(x):
    return jnp.minimum(x, 0.0) - jnp.log(1.0 + jnp.exp(-jnp.abs(x)))


def _rmsnorm(x, g):
    return x * lax.rsqrt(jnp.mean(x * x, axis=-1, keepdims=True) + EPS) * g


def _dot(a, b):
    return jnp.dot(a, b, preferred_element_type=F32)


def _dot_exact(a, b):
    return jnp.dot(a, b, preferred_element_type=F32, precision=lax.Precision.HIGHEST)


def _dot_nt(a, b):
    return lax.dot_general(a, b, (((1,), (1,)), ((), ())), preferred_element_type=F32)


def _dot_tn(a, b):
    return lax.dot_general(a, b, (((0,), (0,)), ((), ())), preferred_element_type=F32)


def _mod_body(c_ref, w_ref, b_ref, o_ref):
    o_ref[...] = _dot_exact(_silu(c_ref[...]), w_ref[...]) + b_ref[...]


def _modulation(cond, ada_w, ada_b):
    rows, n = cond.shape[0], ada_w.shape[1]
    tn = n // 4
    return pl.pallas_call(
        _mod_body,
        grid=(n // tn,),
        in_specs=[pl.BlockSpec((rows, D_MODEL), lambda j: (0, 0)),
                  pl.BlockSpec((D_MODEL, tn), lambda j: (0, j)),
                  pl.BlockSpec((1, tn), lambda j: (0, j))],
        out_specs=pl.BlockSpec((rows, tn), lambda j: (0, j)),
        out_shape=jax.ShapeDtypeStruct((rows, n), F32),
        compiler_params=pltpu.CompilerParams(vmem_limit_bytes=VMEM_LIMIT),
        name="modulation",
    )(cond, ada_w, ada_b)


_C_HQ, _C_HFF, _C_HFB, _C_HI, _C_HGATE = 0, 1024, 2048, 3072, 4096
_C_GQ, _C_GK, _C_GV, _C_GR, _C_MA, _C_MB = 5120, 5632, 6144, 7168, 8192, 9216
_C_GA = 10240
GA_LANES = 128
W_MAIN = _C_GA + GA_LANES


def _inproj_body(x_ref, mod_ref, n1_ref, lbp_ref, wmain_ref, waup_ref, ba_ref,
                 hq_ref, gf_ref, gb_ref, hv_ref, hgate_ref, gq_ref, gk_ref, gv_ref, gr_ref,
                 laf_ref, lab_ref, sa_ref, sb_ref):
    x = x_ref[...]
    h = _rmsnorm(x, n1_ref[...]) * (1.0 + mod_ref[1:2, :]) + mod_ref[0:1, :]
    hb = h.astype(BF16)

    def proj(c0, w):
        return _dot(hb, wmain_ref[:, c0:c0 + w])

    p0, p1 = lbp_ref[0:1, :], lbp_ref[1:2, :]
    pm = jnp.maximum(p0, p1)
    e0, e1 = jnp.exp(p0 - pm), jnp.exp(p1 - pm)
    lb = e0 / (e0 + e1)

    hq_ref[...] = (proj(_C_HQ, 1024) * HG_DK ** -0.5).astype(BF16)
    gf_ref[...] = jnp.log(lb + (1.0 - lb) * _sigmoid(proj(_C_HFF, 1024)))
    gb_ref[...] = jnp.log(lb + (1.0 - lb) * _sigmoid(proj(_C_HFB, 1024)))
    hv_ref[...] = proj(_C_HI, 1024).astype(BF16)
    hgate_ref[...] = _silu(proj(_C_HGATE, 1024)).astype(BF16)
    gq_ref[...] = (proj(_C_GQ, 512) * GLA_DK ** -0.5).astype(BF16)
    gk_ref[...] = proj(_C_GK, 512).astype(BF16)
    gv_ref[...] = proj(_C_GV, 1024).astype(BF16)
    gr_ref[...] = _silu(proj(_C_GR, 1024)).astype(BF16)
    sa_ref[...] = _sigmoid(proj(_C_MA, 1024)).astype(BF16)
    sb_ref[...] = _sigmoid(proj(_C_MB, 1024)).astype(BF16)

    ga = proj(_C_GA, GA_LANES).astype(BF16)
    la = _log_sigmoid(_dot(ga, waup_ref[...]) + ba_ref[...]) * (1.0 / GLA_GATE_NORM)
    laf_ref[...] = la[:, :512]
    lab_ref[...] = la[:, 512:]


def _inproj(x, mod, tiles_per_row, n1, lbp, wmain, waup, ba):
    t = x.shape[0]
    tm = TM_PROJ
    const = lambda shape: pl.BlockSpec(shape, lambda i: (0,) * len(shape))
    tok = lambda w: pl.BlockSpec((tm, w), lambda i: (i, 0))
    widths = (1024, 1024, 1024, 1024, 1024, 512, 512, 1024, 1024, 512, 512, 1024, 1024)
    dtypes = (BF16, F32, F32, BF16, BF16, BF16, BF16, BF16, BF16, F32, F32, BF16, BF16)
    return pl.pallas_call(
        _inproj_body,
        grid=(t // tm,),
        in_specs=[tok(D_MODEL),
                  pl.BlockSpec((None, 6, D_MODEL), lambda i: (i // tiles_per_row, 0, 0)),
                  const((1, D_MODEL)), const((2, D_MODEL)),
                  pl.BlockSpec((D_MODEL, W_MAIN), lambda i: (0, 0), pipeline_mode=pl.Buffered(1)),
                  const((GA_LANES, 1024)), const((1, 1024))],
        out_specs=[tok(w) for w in widths],
        out_shape=[jax.ShapeDtypeStruct((t, w), dt) for w, dt in zip(widths, dtypes)],
        compiler_params=pltpu.CompilerParams(vmem_limit_bytes=VMEM_LIMIT),
        name="inproj",
    )(x, mod, n1, lbp, wmain, waup, ba)


N_MXU_LEVELS = 3
ROW_GROUP = 16
START_SLOT, END_SLOT = N_LEVELS, N_LEVELS + 1
LOG2E = 1.4426950408889634


def _boundary_row(t, h, d):
    base = (t // (2 * h)) * 2 * h
    return base + h - 1 if d == 0 else base + h


def _scan_constants():
    c = CHUNK
    tri = np.tril(np.ones((c, c), np.float32))
    m = [tri] + [np.abs(tri - tri[[_boundary_row(t, 1 << l, 0) for t in range(c)]]) for l in range(N_MXU_LEVELS)]
    m = np.stack(m)
    lvl = np.full((c, c), N_LEVELS + 1, np.int32)
    for t in range(c):
        lvl[t, t] = N_LEVELS
        for s in range(t):
            lvl[t, s] = int(np.floor(np.log2(t ^ s)))
    m_fwd = m.reshape(-1, c)
    m_bwd = m[:, ::-1, ::-1].reshape(-1, c)
    m_both = np.stack([m_fwd, m_bwd])
    return (jnp.asarray(np.concatenate([m_both, m_both], axis=2), BF16), jnp.asarray(np.stack([lvl, lvl.T])))


def _scan_prepare(u, d, rows, q_ref, k_ref, g_ref, m_ref, cb_ref, qf_ref, kf_ref, p_ref):
    c = CHUNK
    w = g_ref.shape[-1]
    g = g_ref[rows, :]
    g1 = g.astype(BF16)
    g2 = (g - g1.astype(F32)).astype(BF16)
    cb_ref[u] = _dot(m_ref[d], jnp.concatenate([g1, g2], axis=0)) * LOG2E
    qf_ref[u] = q_ref[rows, :].astype(F32)
    kf_ref[u] = (1.0 - jnp.exp(g)) if k_ref is None else k_ref[rows, :].astype(F32)
    bcast = lambda r, n: jnp.broadcast_to(cb_ref[u, r:r + 1, :], (n, w))
    for i in range(c // ROW_GROUP):
        t0 = ROW_GROUP * i
        grp = slice(t0, t0 + ROW_GROUP)
        cum = cb_ref[u, grp, :]
        q = qf_ref[u, grp, :]
        k = kf_ref[u, grp, :]
        rowid = t0 + lax.broadcasted_iota(jnp.int32, (ROW_GROUP, w), 0)
        for l in range(N_LEVELS):
            h = 1 << l
            if l < N_MXU_LEVELS:
                ex = cb_ref[u, c * (l + 1) + t0:c * (l + 1) + t0 + ROW_GROUP, :]
                upper = (rowid & h) != 0
                src = jnp.where(upper if d == 0 else jnp.logical_not(upper), q, k)
            else:
                ex, src = [], []
                for t in (t0, t0 + 8):
                    half = slice(t - t0, t - t0 + 8)
                    bnd = bcast(_boundary_row(t, h, d), 8)
                    query = ((t & h) != 0) == (d == 0)
                    ex.append(cum[half] - bnd if query else bnd - cum[half])
                    src.append(q[half] if query else k[half])
                ex, src = jnp.concatenate(ex, axis=0), jnp.concatenate(src, axis=0)
            p_ref[u, l, grp, :] = (src * jnp.exp2(ex)).astype(BF16)
        p_ref[u, START_SLOT, grp, :] = (q * jnp.exp2(cum)).astype(BF16)
        edge = c - 1 if d == 0 else 0
        p_ref[u, END_SLOT, grp, :] = (k * jnp.exp2(bcast(edge, ROW_GROUP) - cum)).astype(BF16)


def _scan_scores(u, rows, q_ref, masks, kf_ref, p_ref, a_ref, n_heads, dk):
    for h in range(n_heads):
        ks = slice(h * dk, (h + 1) * dk)
        a = jnp.where(masks[N_LEVELS], _dot_nt(q_ref[rows, ks], kf_ref[u, :, ks].astype(BF16)), 0.0)
        for l in range(N_LEVELS):
            p = p_ref[u, l, :, ks]
            a = jnp.where(masks[l], _dot_nt(p, p), a)
        a_ref[u, h] = a.astype(BF16)


def _scan_outputs(u, d, rows, v_ref, cb_ref, p_ref, a_ref, st_ref, o_ref, n_heads, dk, dv):
    edge = CHUNK - 1 if d == 0 else 0
    for h in range(n_heads):
        ks = slice(h * dk, (h + 1) * dk)
        vs = slice(h * dv, (h + 1) * dv)
        v = v_ref[rows, vs]
        st = st_ref[d, h]
        o_ref[rows, vs] = _dot(a_ref[u, h], v) + _dot_nt(p_ref[u, START_SLOT, :, ks], st.astype(BF16))
        st_ref[d, h] = st * jnp.exp2(cb_ref[u, edge:edge + 1, ks]) + _dot_tn(v, p_ref[u, END_SLOT, :, ks])


def _scan_body(*refs, n_heads, dk, dv, derive_k, zero_init, emit_state):
    refs = list(refs)
    q_refs = (refs.pop(0), refs.pop(0))
    k_refs = (None, None) if derive_k else (refs.pop(0), refs.pop(0))
    v_refs = (refs.pop(0), refs.pop(0))
    g_refs = (refs.pop(0), refs.pop(0))
    m_ref, lvl_ref = refs.pop(0), refs.pop(0)
    s0_ref = None if zero_init else refs.pop(0)
    o_refs = (refs.pop(0), refs.pop(0))
    sout_ref = refs.pop(0) if emit_state else None
    st_ref, cb_ref, qf_ref, kf_ref, p_ref, a_ref = refs
    n = pl.program_id(1)

    @pl.when(n == 0)
    def _():
        for d in range(2):
            for h in range(n_heads):
                st_ref[d, h] = jnp.zeros((dv, dk), F32) if zero_init else s0_ref[d, h].T

    cps = CHUNKS_PER_STEP
    rows = lambda c: slice(c * CHUNK, (c + 1) * CHUNK)
    masks = [[lvl_ref[d] == l for l in range(N_LEVELS + 1)] for d in range(2)]
    units = [(d, c) for c in range(cps) for d in range(2)]
    for i in range(len(units) + 1):
        if i < len(units):
            d, c = units[i]
            _scan_prepare(d * cps + c, d, rows(c), q_refs[d], k_refs[d], g_refs[d], m_ref, cb_ref, qf_ref, kf_ref,
                          p_ref)
        if i >= 1:
            d, c = units[i - 1]
            _scan_scores(d * cps + c, rows(c), q_refs[d], masks[d], kf_ref, p_ref, a_ref, n_heads, dk)
    for d in range(2):
        for c in (range(cps) if d == 0 else reversed(range(cps))):
            _scan_outputs(d * cps + c, d, rows(c), v_refs[d], cb_ref, p_ref, a_ref, st_ref, o_refs[d], n_heads,
                          dk, dv)

    if emit_state:
        @pl.when(n == pl.num_programs(1) - 1)
        def _():
            for d in range(2):
                for h in range(n_heads):
                    sout_ref[d, h] = st_ref[d, h].T


def _scan(q, k, v, g_f, g_b, s0, consts, n_heads, dk, dv, emit_state):
    b, l, _ = q.shape
    step = CHUNKS_PER_STEP * CHUNK
    n = l // step
    slots = 2 * CHUNKS_PER_STEP
    m_all, lvl = consts
    fwd = lambda w: pl.BlockSpec((None, step, w), lambda i, j: (i, j, 0))
    bwd = lambda w: pl.BlockSpec((None, step, w), lambda i, j: (i, n - 1 - j, 0))
    const = lambda shape: pl.BlockSpec(shape, lambda i, j: (0,) * len(shape))
    state = pl.BlockSpec((None, 2, n_heads, dk, dv), lambda i, j: (i, 0, 0, 0, 0))
    wk, wv = n_heads * dk, n_heads * dv
    args, in_specs = [q, q], [fwd(wk), bwd(wk)]
    if k is not None:
        args += [k, k]
        in_specs += [fwd(wk), bwd(wk)]
    args += [v, v, g_f, g_b, m_all, lvl]
    in_specs += [fwd(wv), bwd(wv), fwd(wk), bwd(wk), const(m_all.shape), const(lvl.shape)]
    if s0 is not None:
        args.append(s0)
        in_specs.append(state)
    out_specs = [fwd(wv), bwd(wv)]
    out_shape = [jax.ShapeDtypeStruct((b, l, wv), F32)] * 2
    if emit_state:
        out_specs.append(state)
        out_shape.append(jax.ShapeDtypeStruct((b, 2, n_heads, dk, dv), F32))
    body = functools.partial(_scan_body, n_heads=n_heads, dk=dk, dv=dv, derive_k=k is None,
                             zero_init=s0 is None, emit_state=emit_state)
    return pl.pallas_call(
        body,
        grid=(b, n),
        in_specs=in_specs,
        out_specs=out_specs,
        out_shape=out_shape,
        scratch_shapes=[pltpu.VMEM((2, n_heads, dv, dk), F32),
                        pltpu.VMEM((slots, (1 + N_MXU_LEVELS) * CHUNK, wk), F32),
                        pltpu.VMEM((slots, CHUNK, wk), F32), pltpu.VMEM((slots, CHUNK, wk), F32),
                        pltpu.VMEM((slots, N_LEVELS + 2, CHUNK, wk), BF16),
                        pltpu.VMEM((slots, n_heads, CHUNK, CHUNK), BF16)],
        compiler_params=pltpu.CompilerParams(dimension_semantics=("arbitrary", "arbitrary"),
                                             vmem_limit_bytes=VMEM_LIMIT),
        name="scan_hgrn" if k is None else "scan_gla",
    )(*args)


def _post_body(ohf_ref, ohb_ref, ogf_ref, ogb_ref, hgate_ref, gr_ref, sa_ref, sb_ref, x_ref, mod_ref,
               hgn_ref, glan_ref, n2_ref, wa_ref, wb_ref, wo_ref, wr_ref,
               x1_ref, rows_ref, comb_ref, yh_ref, yg_ref):
    for h in range(HG_H):
        s = slice(h * HG_DV, (h + 1) * HG_DV)
        o = ohf_ref[:, s] + ohb_ref[:, s]
        yh_ref[:, s] = (_rmsnorm(o, hgn_ref[...]) * hgate_ref[:, s].astype(F32)).astype(BF16)
    for h in range(GLA_H):
        s = slice(h * GLA_DV, (h + 1) * GLA_DV)
        o = ogf_ref[:, s] + ogb_ref[:, s]
        yg_ref[:, s] = (_rmsnorm(o, glan_ref[...]) * gr_ref[:, s].astype(F32)).astype(BF16)
    merged = (sa_ref[...].astype(F32) * _dot(yh_ref[...], wa_ref[...])
              + sb_ref[...].astype(F32) * _dot(yg_ref[...], wb_ref[...]))
    mix = _dot(merged.astype(BF16), wo_ref[...])
    x1 = x_ref[...] + mod_ref[2:3, :] * mix
    x1_ref[...] = x1
    h2 = _rmsnorm(x1, n2_ref[...]) * (1.0 + mod_ref[4:5, :]) + mod_ref[3:4, :]
    h2_hi = h2.astype(BF16)
    tm = h2.shape[0]
    for s in range(D_MODEL // 128):
        rows_ref[pl.ds(s, tm, stride=ROW_SUBLANES), :] = h2[:, s * 128:(s + 1) * 128]
    for s in range(COMB_SUBLANE + 1, ROW_SUBLANES):
        rows_ref[pl.ds(s, tm, stride=ROW_SUBLANES), :] = jnp.zeros((tm, 128), F32)

    h2_lo = (h2 - h2_hi.astype(F32)).astype(BF16)
    logits = _dot(h2_hi, wr_ref[0]) + _dot(h2_hi, wr_ref[1]) + _dot(h2_lo, wr_ref[0])
    lane = lax.broadcasted_iota(jnp.int32, logits.shape, 1)
    neg = -jnp.inf
    first = lambda hit: jnp.min(jnp.where(hit, lane, ROUTER_LANES), axis=-1, keepdims=True)
    gl = jnp.where(lane < N_GROUPS, logits, neg)
    gmax = jnp.max(gl, axis=-1, keepdims=True)
    p_top = 1.0 / jnp.sum(jnp.exp(gl - gmax), axis=-1, keepdims=True)
    g_idx = first(gl == gmax)
    in_group = ((lane >= EXPERT_LANE0) & (lane < EXPERT_LANE0 + N_EXPERTS)
                & (((lane - EXPERT_LANE0) >> 3) == g_idx))
    sl = jnp.where(in_group, logits, neg)
    m1 = jnp.max(sl, axis=-1, keepdims=True)
    i1 = first(sl == m1)
    sl2 = jnp.where(lane == i1, neg, sl)
    m2 = jnp.max(sl2, axis=-1, keepdims=True)
    i2 = first(sl2 == m2)
    r = jnp.exp(m2 - m1)
    w1 = p_top / (1.0 + r)
    comb = jnp.where(lane == i1, w1, jnp.where(lane == i2, w1 * r, 0.0))
    rows_ref[pl.ds(COMB_SUBLANE, tm, stride=ROW_SUBLANES), :] = comb
    comb_ref[...] = jnp.where(lane == 0, g_idx.astype(F32), comb)


def _post(ohf, ohb, ogf, ogb, hgate, gr, sa, sb, x, mod, tiles_per_row, hgn, glan, n2, wa, wb, wo, wr):
    t = x.shape[0]
    tm = TM_POST
    const = lambda shape: pl.BlockSpec(shape, lambda i: (0,) * len(shape))
    tok = lambda w: pl.BlockSpec((tm, w), lambda i: (i, 0))
    return pl.pallas_call(
        _post_body,
        grid=(t // tm,),
        in_specs=[tok(1024)] * 9
                 + [pl.BlockSpec((None, 6, D_MODEL), lambda i: (i // tiles_per_row, 0, 0)),
                    const((1, HG_DV)), const((1, GLA_DV)), const((1, D_MODEL)),
                    const((1024, 1024)), const((1024, 1024)), const((1024, 1024)),
                    const((2, D_MODEL, ROUTER_LANES))],
        out_specs=[tok(1024), pl.BlockSpec((tm * ROW_SUBLANES, 128), lambda i: (i, 0)), tok(ROUTER_LANES)],
        out_shape=[jax.ShapeDtypeStruct((t, D_MODEL), F32), jax.ShapeDtypeStruct((t * ROW_SUBLANES, 128), F32),
                   jax.ShapeDtypeStruct((t, ROUTER_LANES), F32)],
        scratch_shapes=[pltpu.VMEM((tm, 1024), BF16), pltpu.VMEM((tm, 1024), BF16)],
        compiler_params=pltpu.CompilerParams(vmem_limit_bytes=VMEM_LIMIT),
        name="post",
    )(ohf, ohb, ogf, ogb, hgate, gr, sa, sb, x, mod, hgn, glan, n2, wa, wb, wo, wr)


def _route_plan(group, t):
    tr, blk = TR_MOE, PLAN_BLOCK
    nt, nb = t // tr + N_GROUPS, t // blk
    i32 = jnp.int32
    gids = jnp.arange(N_GROUPS, dtype=i32)
    onehot = (group[:, None] == gids[None, :]).astype(i32).reshape(nb, blk, N_GROUPS)
    within = jnp.cumsum(onehot, axis=1)
    blk_end = jnp.cumsum(within[:, -1, :], axis=0)
    blk_start = blk_end - within[:, -1, :]
    counts = blk_end[-1]
    tiles = (counts + tr - 1) // tr
    tile_end = jnp.cumsum(tiles)
    tile_group = jnp.minimum(jnp.sum(jnp.arange(nt, dtype=i32)[:, None] >= tile_end[None, :], axis=1),
                             N_GROUPS - 1).astype(i32)
    row = jnp.arange(nt * tr, dtype=i32)
    gsel = (jnp.repeat(tile_group, tr)[:, None] == gids[None, :]).astype(i32)
    pick = lambda per_group: jnp.sum(gsel * per_group[None, :], axis=1)
    k = row - pick(tile_end - tiles) * tr
    valid = k < pick(counts)
    per_block = lambda table: jnp.sum(gsel[:, None, :] * table[None, :, :], axis=2)
    b_row = jnp.minimum(jnp.sum(per_block(blk_end) <= k[:, None], axis=1), nb - 1)
    bsel = (b_row[:, None] == jnp.arange(nb, dtype=i32)[None, :]).astype(i32)
    k_local = k - jnp.sum(bsel * per_block(blk_start), axis=1)
    sel = (bsel[:, :, None] * gsel[:, None, :]).reshape(nt * tr, nb * N_GROUPS).astype(F32)
    table = jnp.transpose(within, (0, 2, 1)).reshape(nb * N_GROUPS, blk).astype(F32)
    ranks = jnp.dot(sel, table)
    t_local = jnp.sum(ranks <= k_local[:, None].astype(F32), axis=1).astype(i32)
    row_src = jnp.where(valid, b_row * blk + t_local, 0)
    spare = t + ((row // tr) % 2) * tr + row % tr
    row_dst = jnp.where(valid, row_src, spare)
    return tile_group, row_src, jnp.concatenate([t + tr + jnp.arange(tr, dtype=i32), row_dst])


def _moe_body(tg_ref, src_ref, dst_ref, rows_hbm, wg_ref, wu_ref, wd_ref, out_hbm, xbuf, ybuf, gsem, ssem):
    tr = TR_MOE
    i = pl.program_id(0)
    nt = pl.num_programs(0)
    slot = i % 2

    def gather_row(tile, slot_, r):
        tok = src_ref[tile * tr + r]
        return pltpu.make_async_copy(rows_hbm.at[pl.ds(pl.multiple_of(tok * ROW_SUBLANES, ROW_SUBLANES), ROW_SUBLANES)],
                                     xbuf.at[slot_, pl.ds(r * ROW_SUBLANES, ROW_SUBLANES)], gsem.at[slot_])

    def scatter_row(tile, slot_, r):
        tok = dst_ref[(tile + 1) * tr + r]
        return pltpu.make_async_copy(ybuf.at[slot_, pl.ds(r * OUT_SUBLANES, OUT_SUBLANES)],
                                     out_hbm.at[pl.ds(pl.multiple_of(tok * OUT_SUBLANES, OUT_SUBLANES), OUT_SUBLANES)],
                                     ssem.at[slot_])

    def for_rows(fn):
        def body(r, carry):
            fn(r)
            return carry
        lax.fori_loop(0, tr, body, 0, unroll=8)

    def gathered(slot_):
        return pltpu.make_async_copy(rows_hbm.at[pl.ds(0, tr * ROW_SUBLANES)], xbuf.at[slot_], gsem.at[slot_])

    def scattered(slot_):
        return pltpu.make_async_copy(ybuf.at[slot_], out_hbm.at[pl.ds(0, tr * OUT_SUBLANES)], ssem.at[slot_])

    @pl.when(i == 0)
    def _():
        for_rows(lambda r: gather_row(0, 0, r).start())
        n_tok = out_hbm.shape[0] // OUT_SUBLANES - 2 * tr
        ybuf[...] = jnp.zeros_like(ybuf)
        for s in range(2):
            spare = pltpu.make_async_copy(
                ybuf.at[s], out_hbm.at[pl.ds((n_tok + s * tr) * OUT_SUBLANES, tr * OUT_SUBLANES)], ssem.at[s])
            spare.start()
            spare.wait()

    gathered(slot).wait()

    nxt = jnp.minimum(i + 1, nt - 1)
    per_expert = tr // EXPERTS_PER_GROUP
    x = jnp.concatenate([xbuf[slot, pl.ds(s, tr, stride=ROW_SUBLANES), :] for s in range(D_MODEL // 128)],
                        axis=1).astype(BF16)
    comb = xbuf[slot, pl.ds(COMB_SUBLANE, tr, stride=ROW_SUBLANES), :]
    lane = lax.broadcasted_iota(jnp.int32, comb.shape, 1)
    lane0 = EXPERT_LANE0 + tg_ref[i] * EXPERTS_PER_GROUP
    acc = jnp.zeros((tr, D_MODEL), F32)
    for e in range(EXPERTS_PER_GROUP):
        for r in range(e * per_expert, (e + 1) * per_expert):
            gather_row(nxt, 1 - slot, r).start()
            scatter_row(i - 1, 1 - slot, r).start()
        col = jnp.sum(jnp.where(lane == lane0 + e, comb, 0.0), axis=-1, keepdims=True)
        hid = (_silu(_dot(x, wg_ref[e])) * _dot(x, wu_ref[e]) * col).astype(BF16)
        acc = acc + _dot(hid, wd_ref[e])

    @pl.when(i >= 1)
    def _():
        scattered(slot).wait()

    for s in range(D_MODEL // 128):
        ybuf[slot, pl.ds(s, tr, stride=OUT_SUBLANES), :] = acc[:, s * 128:(s + 1) * 128]

    @pl.when(i == nt - 1)
    def _():
        for_rows(lambda r: scatter_row(i, slot, r).start())
        scattered(slot).wait()
        scattered(1 - slot).wait()
        gathered(1 - slot).wait()


def _moe(rows, plan, wg, wu, wd, t):
    tr = TR_MOE
    tile_group, row_src, row_dst = plan
    nt = tile_group.shape[0]
    group_w = lambda shape: pl.BlockSpec((EXPERTS_PER_GROUP,) + shape, lambda i, tg, src, dst: (tg[i], 0, 0))
    grid_spec = pltpu.PrefetchScalarGridSpec(
        num_scalar_prefetch=3,
        grid=(nt,),
        in_specs=[pl.BlockSpec(memory_space=pl.ANY),
                  group_w((D_MODEL, D_EXPERT)), group_w((D_MODEL, D_EXPERT)), group_w((D_EXPERT, D_MODEL))],
        out_specs=pl.BlockSpec(memory_space=pl.ANY),
        scratch_shapes=[pltpu.VMEM((2, tr * ROW_SUBLANES, 128), F32), pltpu.VMEM((2, tr * OUT_SUBLANES, 128), F32),
                        pltpu.SemaphoreType.DMA((2,)), pltpu.SemaphoreType.DMA((2,))])
    return pl.pallas_call(
        _moe_body,
        grid_spec=grid_spec,
        out_shape=jax.ShapeDtypeStruct(((t + 2 * tr) * OUT_SUBLANES, 128), F32),
        compiler_params=pltpu.CompilerParams(dimension_semantics=("arbitrary",), vmem_limit_bytes=VMEM_LIMIT),
        name="moe",
    )(tile_group, row_src, row_dst, rows, wg, wu, wd)


def _final_body(x1_ref, moe_ref, mod_ref, fn_ref, y_ref):
    tm = x1_ref.shape[0]
    moe = jnp.concatenate([moe_ref[pl.ds(s, tm, stride=OUT_SUBLANES), :] for s in range(D_MODEL // 128)], axis=1)
    y_ref[...] = _rmsnorm(x1_ref[...] + mod_ref[5:6, :] * moe, fn_ref[...])


def _final(x1, moe, mod, tiles_per_row, fn):
    t = x1.shape[0]
    tm = TM_FINAL
    return pl.pallas_call(
        _final_body,
        grid=(t // tm,),
        in_specs=[pl.BlockSpec((tm, D_MODEL), lambda i: (i, 0)),
                  pl.BlockSpec((tm * OUT_SUBLANES, 128), lambda i: (i, 0)),
                  pl.BlockSpec((None, 6, D_MODEL), lambda i: (i // tiles_per_row, 0, 0)),
                  pl.BlockSpec((1, D_MODEL), lambda i: (0, 0))],
        out_specs=pl.BlockSpec((tm, D_MODEL), lambda i: (i, 0)),
        out_shape=jax.ShapeDtypeStruct((t, D_MODEL), F32),
        compiler_params=pltpu.CompilerParams(vmem_limit_bytes=VMEM_LIMIT),
        name="final",
    )(x1, moe, mod, fn)


def _trunk_layer(x, mod, s0_hg, s0_gla, emit_state, w):
    b, l, _ = x.shape
    xt = x.reshape(b * l, D_MODEL)
    per_row = lambda tm: (l // tm) if mod.shape[0] > 1 else (b * l // tm)
    (hq, gf, gb, hv, hgate, gq, gk, gv, gr, laf, lab, sa, sb) = _inproj(
        xt, mod, per_row(TM_PROJ), w["n1"], w["lbp"], w["wmain"], w["waup"], w["ba"])
    seq = lambda a: a.reshape(b, l, a.shape[-1])
    hg = _scan(seq(hq), None, seq(hv), seq(gf), seq(gb), s0_hg, w["scan_consts"], HG_H, HG_DK, HG_DV,
               emit_state)
    gla = _scan(seq(gq), seq(gk), seq(gv), seq(laf), seq(lab), s0_gla, w["scan_consts"], GLA_H, GLA_DK,
                GLA_DV, emit_state)
    flat = lambda a: a.reshape(b * l, a.shape[-1])
    x1, rows, comb = _post(flat(hg[0]), flat(hg[1]), flat(gla[0]), flat(gla[1]), hgate, gr, sa, sb, xt, mod,
                           per_row(TM_POST), w["hgn"], w["glan"], w["n2"], w["wa"], w["wb"], w["wo"], w["wr"])
    plan = _route_plan(comb[:, 0].astype(jnp.int32), b * l)
    moe = _moe(rows, plan, w["wg"], w["wu"], w["wd"], b * l)
    y = _final(x1, moe, mod, per_row(TM_FINAL), w["fn"])
    states = (hg[2], gla[2]) if emit_state else (None, None)
    return y.reshape(b, l, D_MODEL), states


def kernel(x_prompt, x_sample, state_hgrn, state_gla, c, c_ctx, ada_w, ada_b, norm1_g, norm2_g, w_in,
           hg_lb_param, hg_norm_g, gla_wa_up, gla_ba, gla_norm_g, w_br_a, w_br_b, w_out,
           w_router_group, w_router_expert, w_exp_gate, w_exp_up, w_exp_down, final_norm_g):
    nb = c.shape[0]
    cond = jnp.concatenate([c, c_ctx[None, :], jnp.zeros((8 - nb - 1, D_MODEL), F32)], axis=0)
    mod = _modulation(cond, ada_w[0], ada_b).reshape(8, 6, D_MODEL)

    w_in0 = w_in[0]
    ga0 = _C_MA
    zeros = jnp.zeros((GLA_RANK, GLA_H * GLA_DK), F32)
    router = jnp.concatenate(
        [w_router_group[0], jnp.transpose(w_router_expert[0], (1, 0, 2)).reshape(D_MODEL, N_EXPERTS),
         jnp.zeros((D_MODEL, ROUTER_LANES - N_GROUPS - N_EXPERTS), F32)], axis=1)
    w = {
        "n1": norm1_g, "n2": norm2_g, "fn": final_norm_g[None, :], "lbp": hg_lb_param,
        "wmain": jnp.concatenate([w_in0[:, :ga0], w_in0[:, ga0 + 2 * GLA_RANK:], w_in0[:, ga0:ga0 + 2 * GLA_RANK],
                                  jnp.zeros((D_MODEL, GA_LANES - 2 * GLA_RANK), F32)], axis=1).astype(BF16),
        "waup": jnp.concatenate([jnp.concatenate([gla_wa_up[0, 0], zeros], axis=1),
                                 jnp.concatenate([zeros, gla_wa_up[0, 1]], axis=1),
                                 jnp.zeros((GA_LANES - 2 * GLA_RANK, 2 * GLA_H * GLA_DK), F32)],
                                axis=0).astype(BF16),
        "ba": gla_ba[0].reshape(1, 2 * GLA_H * GLA_DK),
        "hgn": hg_norm_g, "glan": gla_norm_g,
        "wa": w_br_a[0].astype(BF16), "wb": w_br_b[0].astype(BF16), "wo": w_out[0].astype(BF16),
        "wr": jnp.stack([router.astype(BF16), (router - router.astype(BF16).astype(F32)).astype(BF16)]),
        "wg": w_exp_gate[0].astype(BF16), "wu": w_exp_up[0].astype(BF16), "wd": w_exp_down[0].astype(BF16),
        "scan_consts": _scan_constants(),
    }
    y_prompt, (st_hg, st_gla) = _trunk_layer(x_prompt, mod[nb:nb + 1], None, None, True, w)
    y_sample, _ = _trunk_layer(x_sample, mod[:nb], state_hgrn[:, 0], state_gla[:, 0], False, w)
    return y_prompt, y_sample, st_hg[:, None], st_gla[:, None]
```

```python
import functools

import numpy as np
import jax
import jax.numpy as jnp
from jax import lax
from jax.experimental import pallas as pl
from jax.experimental.pallas import tpu as pltpu

F32 = jnp.float32
BF16 = jnp.bfloat16

D_MODEL = 1024
CHUNK = 64
N_LEVELS = 6
CHUNKS_PER_STEP = 2
HG_H, HG_DK, HG_DV = 8, 128, 128
GLA_H, GLA_DK, GLA_DV = 4, 128, 256
GLA_RANK = 16
GLA_GATE_NORM = 16.0
N_GROUPS = 4
EXPERTS_PER_GROUP = 8
N_EXPERTS = N_GROUPS * EXPERTS_PER_GROUP
D_EXPERT = D_MODEL // 4
EPS = 1e-6
ROUTER_LANES = 128
EXPERT_LANE0 = N_GROUPS
VMEM_LIMIT = 56 * 1024 * 1024

TM_PROJ = 256
TM_POST = 256
TM_FINAL = 512
TR_MOE = 256
PLAN_BLOCK = 256
OUT_SUBLANES = D_MODEL // 128
COMB_SUBLANE = OUT_SUBLANES
ROW_SUBLANES = 16


def _sigmoid(x):
    return 1.0 / (1.0 + jnp.exp(-x))


def _silu(x):
    return x * _sigmoid(x)


def _log_sigmoid(x):
    return jnp.minimum(x, 0.0) - jnp.log(1.0 + jnp.exp(-jnp.abs(x)))


def _rmsnorm(x, g):
    return x * lax.rsqrt(jnp.mean(x * x, axis=-1, keepdims=True) + EPS) * g


def _dot(a, b):
    return jnp.dot(a, b, preferred_element_type=F32)


def _dot_exact(a, b):
    return jnp.dot(a, b, preferred_element_type=F32, precision=lax.Precision.HIGHEST)


def _dot_nt(a, b):
    return lax.dot_general(a, b, (((1,), (1,)), ((), ())), preferred_element_type=F32)


def _dot_tn(a, b):
    return lax.dot_general(a, b, (((0,), (0,)), ((), ())), preferred_element_type=F32)


def _mod_body(c_ref, w_ref, b_ref, o_ref):
    o_ref[...] = _dot_exact(_silu(c_ref[...]), w_ref[...]) + b_ref[...]


def _modulation(cond, ada_w, ada_b):
    rows, n = cond.shape[0], ada_w.shape[1]
    tn = n // 4
    return pl.pallas_call(
        _mod_body,
        grid=(n // tn,),
        in_specs=[pl.BlockSpec((rows, D_MODEL), lambda j: (0, 0)),
                  pl.BlockSpec((D_MODEL, tn), lambda j: (0, j)),
                  pl.BlockSpec((1, tn), lambda j: (0, j))],
        out_specs=pl.BlockSpec((rows, tn), lambda j: (0, j)),
        out_shape=jax.ShapeDtypeStruct((rows, n), F32),
        compiler_params=pltpu.CompilerParams(vmem_limit_bytes=VMEM_LIMIT),
        name="modulation",
    )(cond, ada_w, ada_b)


_C_HQ, _C_HFF, _C_HFB, _C_HI, _C_HGATE = 0, 1024, 2048, 3072, 4096
_C_GQ, _C_GK, _C_GV, _C_GR, _C_MA, _C_MB = 5120, 5632, 6144, 7168, 8192, 9216
_C_GA = 10240
GA_LANES = 128
W_MAIN = _C_GA + GA_LANES


def _inproj_body(x_ref, mod_ref, n1_ref, lbp_ref, wmain_ref, waup_ref, ba_ref,
                 hq_ref, gf_ref, gb_ref, hv_ref, hgate_ref, gq_ref, gk_ref, gv_ref, gr_ref,
                 laf_ref, lab_ref, sa_ref, sb_ref):
    x = x_ref[...]
    h = _rmsnorm(x, n1_ref[...]) * (1.0 + mod_ref[1:2, :]) + mod_ref[0:1, :]
    hb = h.astype(BF16)

    def proj(c0, w):
        return _dot(hb, wmain_ref[:, c0:c0 + w])

    p0, p1 = lbp_ref[0:1, :], lbp_ref[1:2, :]
    pm = jnp.maximum(p0, p1)
    e0, e1 = jnp.exp(p0 - pm), jnp.exp(p1 - pm)
    lb = e0 / (e0 + e1)

    hq_ref[...] = (proj(_C_HQ, 1024) * HG_DK ** -0.5).astype(BF16)
    gf_ref[...] = jnp.log(lb + (1.0 - lb) * _sigmoid(proj(_C_HFF, 1024)))
    gb_ref[...] = jnp.log(lb + (1.0 - lb) * _sigmoid(proj(_C_HFB, 1024)))
    hv_ref[...] = proj(_C_HI, 1024).astype(BF16)
    hgate_ref[...] = _silu(proj(_C_HGATE, 1024)).astype(BF16)
    gq_ref[...] = (proj(_C_GQ, 512) * GLA_DK ** -0.5).astype(BF16)
    gk_ref[...] = proj(_C_GK, 512).astype(BF16)
    gv_ref[...] = proj(_C_GV, 1024).astype(BF16)
    gr_ref[...] = _silu(proj(_C_GR, 1024)).astype(BF16)
    sa_ref[...] = _sigmoid(proj(_C_MA, 1024)).astype(BF16)
    sb_ref[...] = _sigmoid(proj(_C_MB, 1024)).astype(BF16)

    ga = proj(_C_GA, GA_LANES).astype(BF16)
    la = _log_sigmoid(_dot(ga, waup_ref[...]) + ba_ref[...]) * (1.0 / GLA_GATE_NORM)
    laf_ref[...] = la[:, :512]
    lab_ref[...] = la[:, 512:]


def _inproj(x, mod, tiles_per_row, n1, lbp, wmain, waup, ba):
    t = x.shape[0]
    tm = TM_PROJ
    const = lambda shape: pl.BlockSpec(shape, lambda i: (0,) * len(shape))
    tok = lambda w: pl.BlockSpec((tm, w), lambda i: (i, 0))
    widths = (1024, 1024, 1024, 1024, 1024, 512, 512, 1024, 1024, 512, 512, 1024, 1024)
    dtypes = (BF16, F32, F32, BF16, BF16, BF16, BF16, BF16, BF16, F32, F32, BF16, BF16)
    return pl.pallas_call(
        _inproj_body,
        grid=(t // tm,),
        in_specs=[tok(D_MODEL),
                  pl.BlockSpec((None, 6, D_MODEL), lambda i: (i // tiles_per_row, 0, 0)),
                  const((1, D_MODEL)), const((2, D_MODEL)),
                  pl.BlockSpec((D_MODEL, W_MAIN), lambda i: (0, 0), pipeline_mode=pl.Buffered(1)),
                  const((GA_LANES, 1024)), const((1, 1024))],
        out_specs=[tok(w) for w in widths],
        out_shape=[jax.ShapeDtypeStruct((t, w), dt) for w, dt in zip(widths, dtypes)],
        compiler_params=pltpu.CompilerParams(vmem_limit_bytes=VMEM_LIMIT),
        name="inproj",
    )(x, mod, n1, lbp, wmain, waup, ba)


N_MXU_LEVELS = 3
ROW_GROUP = 16
START_SLOT, END_SLOT = N_LEVELS, N_LEVELS + 1
LOG2E = 1.4426950408889634


def _boundary_row(t, h, d):
    base = (t // (2 * h)) * 2 * h
    return base + h - 1 if d == 0 else base + h


def _scan_constants():
    c = CHUNK
    tri = np.tril(np.ones((c, c), np.float32))
    m = [tri] + [np.abs(tri - tri[[_boundary_row(t, 1 << l, 0) for t in range(c)]]) for l in range(N_MXU_LEVELS)]
    m = np.stack(m)
    lvl = np.full((c, c), N_LEVELS + 1, np.int32)
    for t in range(c):
        lvl[t, t] = N_LEVELS
        for s in range(t):
            lvl[t, s] = int(np.floor(np.log2(t ^ s)))
    m_fwd = m.reshape(-1, c)
    m_bwd = m[:, ::-1, ::-1].reshape(-1, c)
    m_both = np.stack([m_fwd, m_bwd])
    return (jnp.asarray(np.concatenate([m_both, m_both], axis=2), BF16), jnp.asarray(np.stack([lvl, lvl.T])))


def _scan_prepare(u, d, rows, q_ref, k_ref, g_ref, m_ref, cb_ref, qf_ref, kf_ref, p_ref):
    c = CHUNK
    w = g_ref.shape[-1]
    g = g_ref[rows, :]
    g1 = g.astype(BF16)
    g2 = (g - g1.astype(F32)).astype(BF16)
    cb_ref[u] = _dot(m_ref[d], jnp.concatenate([g1, g2], axis=0)) * LOG2E
    qf_ref[u] = q_ref[rows, :].astype(F32)
    kf_ref[u] = (1.0 - jnp.exp(g)) if k_ref is None else k_ref[rows, :].astype(F32)
    bcast = lambda r, n: jnp.broadcast_to(cb_ref[u, r:r + 1, :], (n, w))
    for i in range(c // ROW_GROUP):
        t0 = ROW_GROUP * i
        grp = slice(t0, t0 + ROW_GROUP)
        cum = cb_ref[u, grp, :]
        q = qf_ref[u, grp, :]
        k = kf_ref[u, grp, :]
        rowid = t0 + lax.broadcasted_iota(jnp.int32, (ROW_GROUP, w), 0)
        for l in range(N_LEVELS):
            h = 1 << l
            if l < N_MXU_LEVELS:
                ex = cb_ref[u, c * (l + 1) + t0:c * (l + 1) + t0 + ROW_GROUP, :]
                upper = (rowid & h) != 0
                src = jnp.where(upper if d == 0 else jnp.logical_not(upper), q, k)
            else:
                ex, src = [], []
                for t in (t0, t0 + 8):
                    half = slice(t - t0, t - t0 + 8)
                    bnd = bcast(_boundary_row(t, h, d), 8)
                    query = ((t & h) != 0) == (d == 0)
                    ex.append(cum[half] - bnd if query else bnd - cum[half])
                    src.append(q[half] if query else k[half])
                ex, src = jnp.concatenate(ex, axis=0), jnp.concatenate(src, axis=0)
            p_ref[u, l, grp, :] = (src * jnp.exp2(ex)).astype(BF16)
        p_ref[u, START_SLOT, grp, :] = (q * jnp.exp2(cum)).astype(BF16)
        edge = c - 1 if d == 0 else 0
        p_ref[u, END_SLOT, grp, :] = (k * jnp.exp2(bcast(edge, ROW_GROUP) - cum)).astype(BF16)


def _scan_scores(u, rows, q_ref, masks, kf_ref, p_ref, a_ref, n_heads, dk):
    for h in range(n_heads):
        ks = slice(h * dk, (h + 1) * dk)
        a = jnp.where(masks[N_LEVELS], _dot_nt(q_ref[rows, ks], kf_ref[u, :, ks].astype(BF16)), 0.0)
        for l in range(N_LEVELS):
            p = p_ref[u, l, :, ks]
            a = jnp.where(masks[l], _dot_nt(p, p), a)
        a_ref[u, h] = a.astype(BF16)


def _scan_outputs(u, d, rows, v_ref, cb_ref, p_ref, a_ref, st_ref, o_ref, n_heads, dk, dv):
    edge = CHUNK - 1 if d == 0 else 0
    for h in range(n_heads):
        ks = slice(h * dk, (h + 1) * dk)
        vs = slice(h * dv, (h + 1) * dv)
        v = v_ref[rows, vs]
        st = st_ref[d, h]
        o = _dot(a_ref[u, h], v) + _dot_nt(p_ref[u, START_SLOT, :, ks], st.astype(BF16))
        o_ref[rows, vs] = o.astype(o_ref.dtype)
        st_ref[d, h] = st * jnp.exp2(cb_ref[u, edge:edge + 1, ks]) + _dot_tn(v, p_ref[u, END_SLOT, :, ks])


def _scan_body(*refs, n_heads, dk, dv, derive_k, zero_init, emit_state):
    refs = list(refs)
    q_refs = (refs.pop(0), refs.pop(0))
    k_refs = (None, None) if derive_k else (refs.pop(0), refs.pop(0))
    v_refs = (refs.pop(0), refs.pop(0))
    g_refs = (refs.pop(0), refs.pop(0))
    m_ref, lvl_ref = refs.pop(0), refs.pop(0)
    s0_ref = None if zero_init else refs.pop(0)
    o_refs = (refs.pop(0), refs.pop(0))
    sout_ref = refs.pop(0) if emit_state else None
    st_ref, cb_ref, qf_ref, kf_ref, p_ref, a_ref = refs
    n = pl.program_id(1)

    @pl.when(n == 0)
    def _():
        for d in range(2):
            for h in range(n_heads):
                st_ref[d, h] = jnp.zeros((dv, dk), F32) if zero_init else s0_ref[d, h].T

    cps = CHUNKS_PER_STEP
    rows = lambda c: slice(c * CHUNK, (c + 1) * CHUNK)
    masks = [[lvl_ref[d] == l for l in range(N_LEVELS + 1)] for d in range(2)]
    units = [(d, c) for c in range(cps) for d in range(2)]
    for i in range(len(units) + 1):
        if i < len(units):
            d, c = units[i]
            _scan_prepare(d * cps + c, d, rows(c), q_refs[d], k_refs[d], g_refs[d], m_ref, cb_ref, qf_ref, kf_ref,
                          p_ref)
        if i >= 1:
            d, c = units[i - 1]
            _scan_scores(d * cps + c, rows(c), q_refs[d], masks[d], kf_ref, p_ref, a_ref, n_heads, dk)
    for d in range(2):
        for c in (range(cps) if d == 0 else reversed(range(cps))):
            _scan_outputs(d * cps + c, d, rows(c), v_refs[d], cb_ref, p_ref, a_ref, st_ref, o_refs[d], n_heads,
                          dk, dv)

    if emit_state:
        @pl.when(n == pl.num_programs(1) - 1)
        def _():
            for d in range(2):
                for h in range(n_heads):
                    sout_ref[d, h] = st_ref[d, h].T


def _scan(q, k, v, g_f, g_b, s0, consts, n_heads, dk, dv, emit_state):
    b, l, _ = q.shape
    step = CHUNKS_PER_STEP * CHUNK
    n = l // step
    slots = 2 * CHUNKS_PER_STEP
    m_all, lvl = consts
    fwd = lambda w: pl.BlockSpec((None, step, w), lambda i, j: (i, j, 0))
    bwd = lambda w: pl.BlockSpec((None, step, w), lambda i, j: (i, n - 1 - j, 0))
    const = lambda shape: pl.BlockSpec(shape, lambda i, j: (0,) * len(shape))
    state = pl.BlockSpec((None, 2, n_heads, dk, dv), lambda i, j: (i, 0, 0, 0, 0))
    wk, wv = n_heads * dk, n_heads * dv
    args, in_specs = [q, q], [fwd(wk), bwd(wk)]
    if k is not None:
        args += [k, k]
        in_specs += [fwd(wk), bwd(wk)]
    args += [v, v, g_f, g_b, m_all, lvl]
    in_specs += [fwd(wv), bwd(wv), fwd(wk), bwd(wk), const(m_all.shape), const(lvl.shape)]
    if s0 is not None:
        args.append(s0)
        in_specs.append(state)
    out_specs = [fwd(wv), bwd(wv)]
    out_shape = [jax.ShapeDtypeStruct((b, l, wv), BF16)] * 2
    if emit_state:
        out_specs.append(state)
        out_shape.append(jax.ShapeDtypeStruct((b, 2, n_heads, dk, dv), F32))
    body = functools.partial(_scan_body, n_heads=n_heads, dk=dk, dv=dv, derive_k=k is None,
                             zero_init=s0 is None, emit_state=emit_state)
    return pl.pallas_call(
        body,
        grid=(b, n),
        in_specs=in_specs,
        out_specs=out_specs,
        out_shape=out_shape,
        scratch_shapes=[pltpu.VMEM((2, n_heads, dv, dk), F32),
                        pltpu.VMEM((slots, (1 + N_MXU_LEVELS) * CHUNK, wk), F32),
                        pltpu.VMEM((slots, CHUNK, wk), F32), pltpu.VMEM((slots, CHUNK, wk), F32),
                        pltpu.VMEM((slots, N_LEVELS + 2, CHUNK, wk), BF16),
                        pltpu.VMEM((slots, n_heads, CHUNK, CHUNK), BF16)],
        compiler_params=pltpu.CompilerParams(dimension_semantics=("arbitrary", "arbitrary"),
                                             vmem_limit_bytes=VMEM_LIMIT),
        name="scan_hgrn" if k is None else "scan_gla",
    )(*args)


def _post_body(ohf_ref, ohb_ref, ogf_ref, ogb_ref, hgate_ref, gr_ref, sa_ref, sb_ref, x_ref, mod_ref,
               hgn_ref, glan_ref, n2_ref, wa_ref, wb_ref, wo_ref, wr_ref,
               x1_ref, rows_ref, comb_ref, yh_ref, yg_ref):
    for h in range(HG_H):
        s = slice(h * HG_DV, (h + 1) * HG_DV)
        o = ohf_ref[:, s].astype(F32) + ohb_ref[:, s].astype(F32)
        yh_ref[:, s] = (_rmsnorm(o, hgn_ref[...]) * hgate_ref[:, s].astype(F32)).astype(BF16)
    for h in range(GLA_H):
        s = slice(h * GLA_DV, (h + 1) * GLA_DV)
        o = ogf_ref[:, s].astype(F32) + ogb_ref[:, s].astype(F32)
        yg_ref[:, s] = (_rmsnorm(o, glan_ref[...]) * gr_ref[:, s].astype(F32)).astype(BF16)
    merged = (sa_ref[...].astype(F32) * _dot(yh_ref[...], wa_ref[...])
              + sb_ref[...].astype(F32) * _dot(yg_ref[...], wb_ref[...]))
    mix = _dot(merged.astype(BF16), wo_ref[...])
    x1 = x_ref[...] + mod_ref[2:3, :] * mix
    x1_ref[...] = x1
    h2 = _rmsnorm(x1, n2_ref[...]) * (1.0 + mod_ref[4:5, :]) + mod_ref[3:4, :]
    h2_hi = h2.astype(BF16)
    tm = h2.shape[0]
    for s in range(D_MODEL // 128):
        rows_ref[pl.ds(s, tm, stride=ROW_SUBLANES), :] = h2[:, s * 128:(s + 1) * 128]
    for s in range(COMB_SUBLANE + 1, ROW_SUBLANES):
        rows_ref[pl.ds(s, tm, stride=ROW_SUBLANES), :] = jnp.zeros((tm, 128), F32)

    h2_lo = (h2 - h2_hi.astype(F32)).astype(BF16)
    logits = _dot(h2_hi, wr_ref[0]) + _dot(h2_hi, wr_ref[1]) + _dot(h2_lo, wr_ref[0])
    lane = lax.broadcasted_iota(jnp.int32, logits.shape, 1)
    neg = -jnp.inf
    first = lambda hit: jnp.min(jnp.where(hit, lane, ROUTER_LANES), axis=-1, keepdims=True)
    gl = jnp.where(lane < N_GROUPS, logits, neg)
    gmax = jnp.max(gl, axis=-1, keepdims=True)
    p_top = 1.0 / jnp.sum(jnp.exp(gl - gmax), axis=-1, keepdims=True)
    g_idx = first(gl == gmax)
    in_group = ((lane >= EXPERT_LANE0) & (lane < EXPERT_LANE0 + N_EXPERTS)
                & (((lane - EXPERT_LANE0) >> 3) == g_idx))
    sl = jnp.where(in_group, logits, neg)
    m1 = jnp.max(sl, axis=-1, keepdims=True)
    i1 = first(sl == m1)
    sl2 = jnp.where(lane == i1, neg, sl)
    m2 = jnp.max(sl2, axis=-1, keepdims=True)
    i2 = first(sl2 == m2)
    r = jnp.exp(m2 - m1)
    w1 = p_top / (1.0 + r)
    comb = jnp.where(lane == i1, w1, jnp.where(lane == i2, w1 * r, 0.0))
    rows_ref[pl.ds(COMB_SUBLANE, tm, stride=ROW_SUBLANES), :] = comb
    comb_ref[...] = jnp.where(lane == 0, g_idx.astype(F32), comb)


def _post(ohf, ohb, ogf, ogb, hgate, gr, sa, sb, x, mod, tiles_per_row, hgn, glan, n2, wa, wb, wo, wr):
    t = x.shape[0]
    tm = TM_POST
    const = lambda shape: pl.BlockSpec(shape, lambda i: (0,) * len(shape))
    tok = lambda w: pl.BlockSpec((tm, w), lambda i: (i, 0))
    return pl.pallas_call(
        _post_body,
        grid=(t // tm,),
        in_specs=[tok(1024)] * 9
                 + [pl.BlockSpec((None, 6, D_MODEL), lambda i: (i // tiles_per_row, 0, 0)),
                    const((1, HG_DV)), const((1, GLA_DV)), const((1, D_MODEL)),
                    const((1024, 1024)), const((1024, 1024)), const((1024, 1024)),
                    const((2, D_MODEL, ROUTER_LANES))],
        out_specs=[tok(1024), pl.BlockSpec((tm * ROW_SUBLANES, 128), lambda i: (i, 0)), tok(ROUTER_LANES)],
        out_shape=[jax.ShapeDtypeStruct((t, D_MODEL), F32), jax.ShapeDtypeStruct((t * ROW_SUBLANES, 128), F32),
                   jax.ShapeDtypeStruct((t, ROUTER_LANES), F32)],
        scratch_shapes=[pltpu.VMEM((tm, 1024), BF16), pltpu.VMEM((tm, 1024), BF16)],
        compiler_params=pltpu.CompilerParams(vmem_limit_bytes=VMEM_LIMIT),
        name="post",
    )(ohf, ohb, ogf, ogb, hgate, gr, sa, sb, x, mod, hgn, glan, n2, wa, wb, wo, wr)


def _route_plan(group, t):
    tr, blk = TR_MOE, PLAN_BLOCK
    nt, nb = t // tr + N_GROUPS, t // blk
    i32 = jnp.int32
    gids = jnp.arange(N_GROUPS, dtype=i32)
    onehot = (group[:, None] == gids[None, :]).astype(i32).reshape(nb, blk, N_GROUPS)
    within = jnp.cumsum(onehot, axis=1)
    blk_end = jnp.cumsum(within[:, -1, :], axis=0)
    blk_start = blk_end - within[:, -1, :]
    counts = blk_end[-1]
    tiles = (counts + tr - 1) // tr
    tile_end = jnp.cumsum(tiles)
    tile_group = jnp.minimum(jnp.sum(jnp.arange(nt, dtype=i32)[:, None] >= tile_end[None, :], axis=1),
                             N_GROUPS - 1).astype(i32)
    row = jnp.arange(nt * tr, dtype=i32)
    gsel = (jnp.repeat(tile_group, tr)[:, None] == gids[None, :]).astype(i32)
    pick = lambda per_group: jnp.sum(gsel * per_group[None, :], axis=1)
    k = row - pick(tile_end - tiles) * tr
    valid = k < pick(counts)
    per_block = lambda table: jnp.sum(gsel[:, None, :] * table[None, :, :], axis=2)
    b_row = jnp.minimum(jnp.sum(per_block(blk_end) <= k[:, None], axis=1), nb - 1)
    bsel = (b_row[:, None] == jnp.arange(nb, dtype=i32)[None, :]).astype(i32)
    k_local = k - jnp.sum(bsel * per_block(blk_start), axis=1)
    sel = (bsel[:, :, None] * gsel[:, None, :]).reshape(nt * tr, nb * N_GROUPS).astype(F32)
    table = jnp.transpose(within, (0, 2, 1)).reshape(nb * N_GROUPS, blk).astype(F32)
    ranks = jnp.dot(sel, table)
    t_local = jnp.sum(ranks <= k_local[:, None].astype(F32), axis=1).astype(i32)
    row_src = jnp.where(valid, b_row * blk + t_local, 0)
    spare = t + ((row // tr) % 2) * tr + row % tr
    row_dst = jnp.where(valid, row_src, spare)
    return tile_group, row_src, jnp.concatenate([t + tr + jnp.arange(tr, dtype=i32), row_dst])


def _moe_body(tg_ref, src_ref, dst_ref, rows_hbm, wg_ref, wu_ref, wd_ref, out_hbm, xbuf, ybuf, gsem, ssem):
    tr = TR_MOE
    i = pl.program_id(0)
    nt = pl.num_programs(0)
    slot = i % 2

    def gather_row(tile, slot_, r):
        tok = src_ref[tile * tr + r]
        return pltpu.make_async_copy(rows_hbm.at[pl.ds(pl.multiple_of(tok * ROW_SUBLANES, ROW_SUBLANES), ROW_SUBLANES)],
                                     xbuf.at[slot_, pl.ds(r * ROW_SUBLANES, ROW_SUBLANES)], gsem.at[slot_])

    def scatter_row(tile, slot_, r):
        tok = dst_ref[(tile + 1) * tr + r]
        return pltpu.make_async_copy(ybuf.at[slot_, pl.ds(r * OUT_SUBLANES, OUT_SUBLANES)],
                                     out_hbm.at[pl.ds(pl.multiple_of(tok * OUT_SUBLANES, OUT_SUBLANES), OUT_SUBLANES)],
                                     ssem.at[slot_])

    def for_rows(fn):
        def body(r, carry):
            fn(r)
            return carry
        lax.fori_loop(0, tr, body, 0, unroll=8)

    def gathered(slot_):
        return pltpu.make_async_copy(rows_hbm.at[pl.ds(0, tr * ROW_SUBLANES)], xbuf.at[slot_], gsem.at[slot_])

    def scattered(slot_):
        return pltpu.make_async_copy(ybuf.at[slot_], out_hbm.at[pl.ds(0, tr * OUT_SUBLANES)], ssem.at[slot_])

    @pl.when(i == 0)
    def _():
        for_rows(lambda r: gather_row(0, 0, r).start())
        n_tok = out_hbm.shape[0] // OUT_SUBLANES - 2 * tr
        ybuf[...] = jnp.zeros_like(ybuf)
        for s in range(2):
            spare = pltpu.make_async_copy(
                ybuf.at[s], out_hbm.at[pl.ds((n_tok + s * tr) * OUT_SUBLANES, tr * OUT_SUBLANES)], ssem.at[s])
            spare.start()
            spare.wait()

    gathered(slot).wait()

    nxt = jnp.minimum(i + 1, nt - 1)
    per_expert = tr // EXPERTS_PER_GROUP

    def copy_burst(r0):
        def body(j, carry):
            for p in range(2):
                r = r0 + 2 * j + p
                gather_row(nxt, 1 - slot, r).start(priority=p)
                scatter_row(i - 1, 1 - slot, r).start(priority=p)
            return carry
        lax.fori_loop(0, per_expert // 2, body, 0, unroll=4)

    x = jnp.concatenate([xbuf[slot, pl.ds(s, tr, stride=ROW_SUBLANES), :] for s in range(D_MODEL // 128)],
                        axis=1).astype(BF16)
    comb = xbuf[slot, pl.ds(COMB_SUBLANE, tr, stride=ROW_SUBLANES), :]
    lane = lax.broadcasted_iota(jnp.int32, comb.shape, 1)
    lane0 = EXPERT_LANE0 + tg_ref[i] * EXPERTS_PER_GROUP
    acc = jnp.zeros((tr, D_MODEL), F32)
    for e in range(EXPERTS_PER_GROUP):
        copy_burst(e * per_expert)
        col =jnp.sum(jnp.where(lane == lane0 + e, comb, 0.0), axis=-1, keepdims=True)
        hid = (_silu(_dot(x, wg_ref[e])) * _dot(x, wu_ref[e]) * col).astype(BF16)
        acc = acc + _dot(hid, wd_ref[e])

    @pl.when(i >= 1)
    def _():
        scattered(slot).wait()

    for s in range(D_MODEL // 128):
        ybuf[slot, pl.ds(s, tr, stride=OUT_SUBLANES), :] = acc[:, s * 128:(s + 1) * 128]

    @pl.when(i == nt - 1)
    def _():
        for_rows(lambda r: scatter_row(i, slot, r).start())
        scattered(slot).wait()
        scattered(1 - slot).wait()
        gathered(1 - slot).wait()


def _moe(rows, plan, wg, wu, wd, t):
    tr = TR_MOE
    tile_group, row_src, row_dst = plan
    nt = tile_group.shape[0]
    group_w = lambda shape: pl.BlockSpec((EXPERTS_PER_GROUP,) + shape, lambda i, tg, src, dst: (tg[i], 0, 0))
    grid_spec = pltpu.PrefetchScalarGridSpec(
        num_scalar_prefetch=3,
        grid=(nt,),
        in_specs=[pl.BlockSpec(memory_space=pl.ANY),
                  group_w((D_MODEL, D_EXPERT)), group_w((D_MODEL, D_EXPERT)), group_w((D_EXPERT, D_MODEL))],
        out_specs=pl.BlockSpec(memory_space=pl.ANY),
        scratch_shapes=[pltpu.VMEM((2, tr * ROW_SUBLANES, 128), F32), pltpu.VMEM((2, tr * OUT_SUBLANES, 128), F32),
                        pltpu.SemaphoreType.DMA((2,)), pltpu.SemaphoreType.DMA((2,))])
    return pl.pallas_call(
        _moe_body,
        grid_spec=grid_spec,
        out_shape=jax.ShapeDtypeStruct(((t + 2 * tr) * OUT_SUBLANES, 128), F32),
        compiler_params=pltpu.CompilerParams(dimension_semantics=("arbitrary",), vmem_limit_bytes=VMEM_LIMIT),
        name="moe",
    )(tile_group, row_src, row_dst, rows, wg, wu, wd)


def _final_body(x1_ref, moe_ref, mod_ref, fn_ref, y_ref):
    tm = x1_ref.shape[0]
    moe = jnp.concatenate([moe_ref[pl.ds(s, tm, stride=OUT_SUBLANES), :] for s in range(D_MODEL // 128)], axis=1)
    y_ref[...] = _rmsnorm(x1_ref[...] + mod_ref[5:6, :] * moe, fn_ref[...])


def _final(x1, moe, mod, tiles_per_row, fn):
    t = x1.shape[0]
    tm = TM_FINAL
    return pl.pallas_call(
        _final_body,
        grid=(t // tm,),
        in_specs=[pl.BlockSpec((tm, D_MODEL), lambda i: (i, 0)),
                  pl.BlockSpec((tm * OUT_SUBLANES, 128), lambda i: (i, 0)),
                  pl.BlockSpec((None, 6, D_MODEL), lambda i: (i // tiles_per_row, 0, 0)),
                  pl.BlockSpec((1, D_MODEL), lambda i: (0, 0))],
        out_specs=pl.BlockSpec((tm, D_MODEL), lambda i: (i, 0)),
        out_shape=jax.ShapeDtypeStruct((t, D_MODEL), F32),
        compiler_params=pltpu.CompilerParams(vmem_limit_bytes=VMEM_LIMIT),
        name="final",
    )(x1, moe, mod, fn)


def _trunk_layer(x, mod, s0_hg, s0_gla, emit_state, w):
    b, l, _ = x.shape
    xt = x.reshape(b * l, D_MODEL)
    per_row = lambda tm: (l // tm) if mod.shape[0] > 1 else (b * l // tm)
    (hq, gf, gb, hv, hgate, gq, gk, gv, gr, laf, lab, sa, sb) = _inproj(
        xt, mod, per_row(TM_PROJ), w["n1"], w["lbp"], w["wmain"], w["waup"], w["ba"])
    seq = lambda a: a.reshape(b, l, a.shape[-1])
    hg = _scan(seq(hq), None, seq(hv), seq(gf), seq(gb), s0_hg, w["scan_consts"], HG_H, HG_DK, HG_DV,
               emit_state)
    gla = _scan(seq(gq), seq(gk), seq(gv), seq(laf), seq(lab), s0_gla, w["scan_consts"], GLA_H, GLA_DK,
                GLA_DV, emit_state)
    flat = lambda a: a.reshape(b * l, a.shape[-1])
    x1, rows, comb = _post(flat(hg[0]), flat(hg[1]), flat(gla[0]), flat(gla[1]), hgate, gr, sa, sb, xt, mod,
                           per_row(TM_POST), w["hgn"], w["glan"], w["n2"], w["wa"], w["wb"], w["wo"], w["wr"])
    plan = _route_plan(comb[:, 0].astype(jnp.int32), b * l)
    moe = _moe(rows, plan, w["wg"], w["wu"], w["wd"], b * l)
    y = _final(x1, moe, mod, per_row(TM_FINAL), w["fn"])
    states = (hg[2], gla[2]) if emit_state else (None, None)
    return y.reshape(b, l, D_MODEL), states


def kernel(x_prompt, x_sample, state_hgrn, state_gla, c, c_ctx, ada_w, ada_b, norm1_g, norm2_g, w_in,
           hg_lb_param, hg_norm_g, gla_wa_up, gla_ba, gla_norm_g, w_br_a, w_br_b, w_out,
           w_router_group, w_router_expert, w_exp_gate, w_exp_up, w_exp_down, final_norm_g):
    nb = c.shape[0]
    cond = jnp.concatenate([c, c_ctx[None, :], jnp.zeros((8 - nb - 1, D_MODEL), F32)], axis=0)
    mod = _modulation(cond, ada_w[0], ada_b).reshape(8, 6, D_MODEL)

    w_in0 = w_in[0]
    ga0 = _C_MA
    zeros = jnp.zeros((GLA_RANK, GLA_H * GLA_DK), F32)
    router = jnp.concatenate(
        [w_router_group[0], jnp.transpose(w_router_expert[0], (1, 0, 2)).reshape(D_MODEL, N_EXPERTS),
         jnp.zeros((D_MODEL, ROUTER_LANES - N_GROUPS - N_EXPERTS), F32)], axis=1)
    w = {
        "n1": norm1_g, "n2": norm2_g, "fn": final_norm_g[None, :], "lbp": hg_lb_param,
        "wmain": jnp.concatenate([w_in0[:, :ga0], w_in0[:, ga0 + 2 * GLA_RANK:], w_in0[:, ga0:ga0 + 2 * GLA_RANK],
                                  jnp.zeros((D_MODEL, GA_LANES - 2 * GLA_RANK), F32)], axis=1).astype(BF16),
        "waup": jnp.concatenate([jnp.concatenate([gla_wa_up[0, 0], zeros], axis=1),
                                 jnp.concatenate([zeros, gla_wa_up[0, 1]], axis=1),
                                 jnp.zeros((GA_LANES - 2 * GLA_RANK, 2 * GLA_H * GLA_DK), F32)],
                                axis=0).astype(BF16),
        "ba": gla_ba[0].reshape(1, 2 * GLA_H * GLA_DK),
        "hgn": hg_norm_g, "glan": gla_norm_g,
        "wa": w_br_a[0].astype(BF16), "wb": w_br_b[0].astype(BF16), "wo": w_out[0].astype(BF16),
        "wr": jnp.stack([router.astype(BF16), (router - router.astype(BF16).astype(F32)).astype(BF16)]),
        "wg": w_exp_gate[0].astype(BF16), "wu": w_exp_up[0].astype(BF16), "wd": w_exp_down[0].astype(BF16),
        "scan_consts": _scan_constants(),
    }
    y_prompt, (st_hg, st_gla) = _trunk_layer(x_prompt, mod[nb:nb + 1], None, None, True, w)
    y_sample, _ = _trunk_layer(x_sample, mod[:nb], state_hgrn[:, 0], state_gla[:, 0], False, w)
    return y_prompt, y_sample, st_hg[:, None], st_gla[:, None]
```

```python
import functools

import numpy as np
import jax
import jax.numpy as jnp
from jax import lax
from jax.experimental import pallas as pl
from jax.experimental.pallas import tpu as pltpu

F32 = jnp.float32
BF16 = jnp.bfloat16

D_MODEL = 1024
CHUNK = 64
N_LEVELS = 6
CHUNKS_PER_STEP = 4
HG_H, HG_DK, HG_DV = 8, 128, 128
GLA_H, GLA_DK, GLA_DV = 4, 128, 256
GLA_RANK = 16
GLA_GATE_NORM = 16.0
N_GROUPS = 4
EXPERTS_PER_GROUP = 8
N_EXPERTS = N_GROUPS * EXPERTS_PER_GROUP
D_EXPERT = D_MODEL // 4
EPS = 1e-6
ROUTER_LANES = 128
EXPERT_LANE0 = N_GROUPS
VMEM_LIMIT = 56 * 1024 * 1024

TM_PROJ = 256
TM_POST = 512
TM_FINAL = 512
TR_MOE = 256
PLAN_BLOCK = 256
OUT_SUBLANES = D_MODEL // 128
COMB_SUBLANE = OUT_SUBLANES
ROW_SUBLANES = 16


def _sigmoid(x):
    return 1.0 / (1.0 + jnp.exp(-x))


def _silu(x):
    return x * _sigmoid(x)


def _log_sigmoid(x):
    return jnp.minimum(x, 0.0) - jnp.log(1.0 + jnp.exp(-jnp.abs(x)))


def _rmsnorm(x, g):
    return x * lax.rsqrt(jnp.mean(x * x, axis=-1, keepdims=True) + EPS) * g


def _dot(a, b):
    return jnp.dot(a, b, preferred_element_type=F32)


def _dot_exact(a, b):
    return jnp.dot(a, b, preferred_element_type=F32, precision=lax.Precision.HIGHEST)


def _dot_nt(a, b):
    return lax.dot_general(a, b, (((1,), (1,)), ((), ())), preferred_element_type=F32)


def _dot_tn(a, b):
    return lax.dot_general(a, b, (((0,), (0,)), ((), ())), preferred_element_type=F32)


def _mod_body(c_ref, w_ref, b_ref, o_ref):
    o_ref[...] = _dot_exact(_silu(c_ref[...]), w_ref[...]) + b_ref[...]


def _modulation(cond, ada_w, ada_b):
    rows, n = cond.shape[0], ada_w.shape[1]
    tn = n // 4
    return pl.pallas_call(
        _mod_body,
        grid=(n // tn,),
        in_specs=[pl.BlockSpec((rows, D_MODEL), lambda j: (0, 0)),
                  pl.BlockSpec((D_MODEL, tn), lambda j: (0, j)),
                  pl.BlockSpec((1, tn), lambda j: (0, j))],
        out_specs=pl.BlockSpec((rows, tn), lambda j: (0, j)),
        out_shape=jax.ShapeDtypeStruct((rows, n), F32),
        compiler_params=pltpu.CompilerParams(vmem_limit_bytes=VMEM_LIMIT),
        name="modulation",
    )(cond, ada_w, ada_b)


_C_HQ, _C_HFF, _C_HFB, _C_HI, _C_HGATE = 0, 1024, 2048, 3072, 4096
_C_GQ, _C_GK, _C_GV, _C_GR, _C_TAIL = 5120, 5632, 6144, 7168, 8192
GA_WIDTH = 2 * GLA_RANK
W_TAIL = GA_WIDTH + 2 * D_MODEL
W_MAIN = _C_TAIL + W_TAIL


def _inproj_body(x_ref, mod_ref, n1_ref, lbp_ref, wmain_ref, waup_ref, ba_ref,
                 hq_ref, gf_ref, gb_ref, hv_ref, hgate_ref, gq_ref, gk_ref, gv_ref, gr_ref,
                 laf_ref, lab_ref, sa_ref, sb_ref):
    x = x_ref[...]
    h = _rmsnorm(x, n1_ref[...]) * (1.0 + mod_ref[1:2, :]) + mod_ref[0:1, :]
    hb = h.astype(BF16)

    def proj(c0, w):
        return _dot(hb, wmain_ref[:, c0:c0 + w])

    p0, p1 = lbp_ref[0:1, :], lbp_ref[1:2, :]
    pm = jnp.maximum(p0, p1)
    e0, e1 = jnp.exp(p0 - pm), jnp.exp(p1 - pm)
    lb = e0 / (e0 + e1)

    hq_ref[...] = (proj(_C_HQ, 1024) * HG_DK ** -0.5).astype(BF16)
    gf_ref[...] = jnp.log(lb + (1.0 - lb) * _sigmoid(proj(_C_HFF, 1024)))
    gb_ref[...] = jnp.log(lb + (1.0 - lb) * _sigmoid(proj(_C_HFB, 1024)))
    hv_ref[...] = proj(_C_HI, 1024).astype(BF16)
    hgate_ref[...] = _silu(proj(_C_HGATE, 1024)).astype(BF16)
    gq_ref[...] = (proj(_C_GQ, 512) * GLA_DK ** -0.5).astype(BF16)
    gk_ref[...] = proj(_C_GK, 512).astype(BF16)
    gv_ref[...] = proj(_C_GV, 1024).astype(BF16)
    gr_ref[...] = _silu(proj(_C_GR, 1024)).astype(BF16)
    tail = proj(_C_TAIL, W_TAIL)
    sa_ref[...] = _sigmoid(tail[:, GA_WIDTH:GA_WIDTH + D_MODEL]).astype(BF16)
    sb_ref[...] = _sigmoid(tail[:, GA_WIDTH + D_MODEL:]).astype(BF16)

    ga = tail[:, :GA_WIDTH].astype(BF16)
    la = _log_sigmoid(_dot(ga, waup_ref[...]) + ba_ref[...]) * (1.0 / GLA_GATE_NORM)
    laf_ref[...] = la[:, :512]
    lab_ref[...] = la[:, 512:]


def _inproj(x, mod, tiles_per_row, n1, lbp, wmain, waup, ba):
    t = x.shape[0]
    tm = TM_PROJ
    const = lambda shape: pl.BlockSpec(shape, lambda i: (0,) * len(shape))
    tok = lambda w: pl.BlockSpec((tm, w), lambda i: (i, 0))
    widths = (1024, 1024, 1024, 1024, 1024, 512, 512, 1024, 1024, 512, 512, 1024, 1024)
    dtypes = (BF16, F32, F32, BF16, BF16, BF16, BF16, BF16, BF16, F32, F32, BF16, BF16)
    return pl.pallas_call(
        _inproj_body,
        grid=(t // tm,),
        in_specs=[tok(D_MODEL),
                  pl.BlockSpec((None, 6, D_MODEL), lambda i: (i // tiles_per_row, 0, 0)),
                  const((1, D_MODEL)), const((2, D_MODEL)),
                  pl.BlockSpec((D_MODEL, W_MAIN), lambda i: (0, 0), pipeline_mode=pl.Buffered(1)),
                  const((GA_WIDTH, 1024)), const((1, 1024))],
        out_specs=[tok(w) for w in widths],
        out_shape=[jax.ShapeDtypeStruct((t, w), dt) for w, dt in zip(widths, dtypes)],
        compiler_params=pltpu.CompilerParams(vmem_limit_bytes=VMEM_LIMIT),
        name="inproj",
    )(x, mod, n1, lbp, wmain, waup, ba)


N_MXU_LEVELS = 3
ROW_GROUP = 16
START_SLOT, END_SLOT = N_LEVELS, N_LEVELS + 1
LOG2E = 1.4426950408889634


def _boundary_row(t, h, d):
    base = (t // (2 * h)) * 2 * h
    return base + h - 1 if d == 0 else base + h


def _scan_constants():
    c = CHUNK
    tri = np.tril(np.ones((c, c), np.float32))
    m = [tri] + [np.abs(tri - tri[[_boundary_row(t, 1 << l, 0) for t in range(c)]]) for l in range(N_MXU_LEVELS)]
    m = np.stack(m)
    lvl = np.full((c, c), N_LEVELS + 1, np.int32)
    for t in range(c):
        lvl[t, t] = N_LEVELS
        for s in range(t):
            lvl[t, s] = int(np.floor(np.log2(t ^ s)))
    m_fwd = m.reshape(-1, c)
    m_bwd = m[:, ::-1, ::-1].reshape(-1, c)
    m_both = np.stack([m_fwd, m_bwd])
    return (jnp.asarray(np.concatenate([m_both, m_both], axis=2), BF16), jnp.asarray(np.stack([lvl, lvl.T])))


def _scan_prepare(u, d, rows, q_ref, k_ref, g_ref, m_ref, cb_ref, qf_ref, kf_ref, p_ref):
    c = CHUNK
    w = g_ref.shape[-1]
    g = g_ref[rows, :]
    g1 = g.astype(BF16)
    g2 = (g - g1.astype(F32)).astype(BF16)
    cb_ref[u] = _dot(m_ref[d], jnp.concatenate([g1, g2], axis=0)) * LOG2E
    qf_ref[u] = q_ref[rows, :].astype(F32)
    kf_ref[u] = (1.0 - jnp.exp(g)) if k_ref is None else k_ref[rows, :].astype(F32)
    bcast = lambda r, n: jnp.broadcast_to(cb_ref[u, r:r + 1, :], (n, w))
    for i in range(c // ROW_GROUP):
        t0 = ROW_GROUP * i
        grp = slice(t0, t0 + ROW_GROUP)
        cum = cb_ref[u, grp, :]
        q = qf_ref[u, grp, :]
        k = kf_ref[u, grp, :]
        rowid = t0 + lax.broadcasted_iota(jnp.int32, (ROW_GROUP, w), 0)
        for l in range(N_LEVELS):
            h = 1 << l
            if l < N_MXU_LEVELS:
                ex = cb_ref[u, c * (l + 1) + t0:c * (l + 1) + t0 + ROW_GROUP, :]
                upper = (rowid & h) != 0
                src = jnp.where(upper if d == 0 else jnp.logical_not(upper), q, k)
            else:
                ex, src = [], []
                for t in (t0, t0 + 8):
                    half = slice(t - t0, t - t0 + 8)
                    bnd = bcast(_boundary_row(t, h, d), 8)
                    query = ((t & h) != 0) == (d == 0)
                    ex.append(cum[half] - bnd if query else bnd - cum[half])
                    src.append(q[half] if query else k[half])
                ex, src = jnp.concatenate(ex, axis=0), jnp.concatenate(src, axis=0)
            p_ref[u, l, grp, :] = (src * jnp.exp2(ex)).astype(BF16)
        p_ref[u, START_SLOT, grp, :] = (q * jnp.exp2(cum)).astype(BF16)
        edge = c - 1 if d == 0 else 0
        p_ref[u, END_SLOT, grp, :] = (k * jnp.exp2(bcast(edge, ROW_GROUP) - cum)).astype(BF16)


def _scan_scores(u, rows, q_ref, masks, kf_ref, p_ref, a_ref, n_heads, dk):
    for h in range(n_heads):
        ks = slice(h * dk, (h + 1) * dk)
        a = jnp.where(masks[N_LEVELS], _dot_nt(q_ref[rows, ks], kf_ref[u, :, ks].astype(BF16)), 0.0)
        for l in range(N_LEVELS):
            p = p_ref[u, l, :, ks]
            a = jnp.where(masks[l], _dot_nt(p, p), a)
        a_ref[u, h] = a.astype(BF16)


def _scan_outputs(u, d, rows, v_ref, cb_ref, p_ref, a_ref, st_ref, o_ref, n_heads, dk, dv):
    edge = CHUNK - 1 if d == 0 else 0
    for h in range(n_heads):
        ks = slice(h * dk, (h + 1) * dk)
        vs = slice(h * dv, (h + 1) * dv)
        v = v_ref[rows, vs]
        st = st_ref[d, h]
        o = _dot(a_ref[u, h], v) + _dot_nt(p_ref[u, START_SLOT, :, ks], st.astype(BF16))
        o_ref[rows, vs] = o.astype(o_ref.dtype)
        st_ref[d, h] = st * jnp.exp2(cb_ref[u, edge:edge + 1, ks]) + _dot_tn(v, p_ref[u, END_SLOT, :, ks])


def _scan_body(*refs, n_heads, dk, dv, derive_k, zero_init, emit_state):
    refs = list(refs)
    q_refs = (refs.pop(0), refs.pop(0))
    k_refs = (None, None) if derive_k else (refs.pop(0), refs.pop(0))
    v_refs = (refs.pop(0), refs.pop(0))
    g_refs = (refs.pop(0), refs.pop(0))
    m_ref, lvl_ref = refs.pop(0), refs.pop(0)
    s0_ref = None if zero_init else refs.pop(0)
    o_refs = (refs.pop(0), refs.pop(0))
    sout_ref = refs.pop(0) if emit_state else None
    st_ref, cb_ref, qf_ref, kf_ref, p_ref, a_ref = refs
    n = pl.program_id(1)

    @pl.when(n == 0)
    def _():
        for d in range(2):
            for h in range(n_heads):
                st_ref[d, h] = jnp.zeros((dv, dk), F32) if zero_init else s0_ref[d, h].T

    cps = CHUNKS_PER_STEP
    rows = lambda c: slice(c * CHUNK, (c + 1) * CHUNK)
    masks = [[lvl_ref[d] == l for l in range(N_LEVELS + 1)] for d in range(2)]
    units = [(d, c) for c in range(cps) for d in range(2)]
    for i in range(len(units) + 1):
        if i < len(units):
            d, c = units[i]
            _scan_prepare(d * cps + c, d, rows(c), q_refs[d], k_refs[d], g_refs[d], m_ref, cb_ref, qf_ref, kf_ref,
                          p_ref)
        if i >= 1:
            d, c = units[i - 1]
            _scan_scores(d * cps + c, rows(c), q_refs[d], masks[d], kf_ref, p_ref, a_ref, n_heads, dk)
    for d in range(2):
        for c in (range(cps) if d == 0 else reversed(range(cps))):
            _scan_outputs(d * cps + c, d, rows(c), v_refs[d], cb_ref, p_ref, a_ref, st_ref, o_refs[d], n_heads,
                          dk, dv)

    if emit_state:
        @pl.when(n == pl.num_programs(1) - 1)
        def _():
            for d in range(2):
                for h in range(n_heads):
                    sout_ref[d, h] = st_ref[d, h].T


def _scan(q, k, v, g_f, g_b, s0, consts, n_heads, dk, dv, emit_state):
    b, l, _ = q.shape
    step = CHUNKS_PER_STEP * CHUNK
    n = l // step
    slots = 2 * CHUNKS_PER_STEP
    m_all, lvl = consts
    fwd = lambda w: pl.BlockSpec((None, step, w), lambda i, j: (i, j, 0))
    bwd = lambda w: pl.BlockSpec((None, step, w), lambda i, j: (i, n - 1 - j, 0))
    const = lambda shape: pl.BlockSpec(shape, lambda i, j: (0,) * len(shape))
    state = pl.BlockSpec((None, 2, n_heads, dk, dv), lambda i, j: (i, 0, 0, 0, 0))
    wk, wv = n_heads * dk, n_heads * dv
    args, in_specs = [q, q], [fwd(wk), bwd(wk)]
    if k is not None:
        args += [k, k]
        in_specs += [fwd(wk), bwd(wk)]
    args += [v, v, g_f, g_b, m_all, lvl]
    in_specs += [fwd(wv), bwd(wv), fwd(wk), bwd(wk), const(m_all.shape), const(lvl.shape)]
    if s0 is not None:
        args.append(s0)
        in_specs.append(state)
    out_specs = [fwd(wv), bwd(wv)]
    out_shape = [jax.ShapeDtypeStruct((b, l, wv), BF16)] * 2
    if emit_state:
        out_specs.append(state)
        out_shape.append(jax.ShapeDtypeStruct((b, 2, n_heads, dk, dv), F32))
    body = functools.partial(_scan_body, n_heads=n_heads, dk=dk, dv=dv, derive_k=k is None,
                             zero_init=s0 is None, emit_state=emit_state)
    return pl.pallas_call(
        body,
        grid=(b, n),
        in_specs=in_specs,
        out_specs=out_specs,
        out_shape=out_shape,
        scratch_shapes=[pltpu.VMEM((2, n_heads, dv, dk), F32),
                        pltpu.VMEM((slots, (1 + N_MXU_LEVELS) * CHUNK, wk), F32),
                        pltpu.VMEM((slots, CHUNK, wk), F32), pltpu.VMEM((slots, CHUNK, wk), F32),
                        pltpu.VMEM((slots, N_LEVELS + 2, CHUNK, wk), BF16),
                        pltpu.VMEM((slots, n_heads, CHUNK, CHUNK), BF16)],
        compiler_params=pltpu.CompilerParams(dimension_semantics=("arbitrary", "arbitrary"),
                                             vmem_limit_bytes=VMEM_LIMIT),
        name="scan_hgrn" if k is None else "scan_gla",
    )(*args)


def _post_body(ohf_ref, ohb_ref, ogf_ref, ogb_ref, hgate_ref, gr_ref, sa_ref, sb_ref, x_ref, mod_ref,
               hgn_ref, glan_ref, n2_ref, wa_ref, wb_ref, wo_ref, wr_ref,
               x1_ref, rows_ref, comb_ref, yh_ref, yg_ref):
    for h in range(HG_H):
        s = slice(h * HG_DV, (h + 1) * HG_DV)
        o = ohf_ref[:, s].astype(F32) + ohb_ref[:, s].astype(F32)
        yh_ref[:, s] = (_rmsnorm(o, hgn_ref[...]) * hgate_ref[:, s].astype(F32)).astype(BF16)
    for h in range(GLA_H):
        s = slice(h * GLA_DV, (h + 1) * GLA_DV)
        o = ogf_ref[:, s].astype(F32) + ogb_ref[:, s].astype(F32)
        yg_ref[:, s] = (_rmsnorm(o, glan_ref[...]) * gr_ref[:, s].astype(F32)).astype(BF16)
    merged = (sa_ref[...].astype(F32) * _dot(yh_ref[...], wa_ref[...])
              + sb_ref[...].astype(F32) * _dot(yg_ref[...], wb_ref[...]))
    mix = _dot(merged.astype(BF16), wo_ref[...])
    x1 = x_ref[...] + mod_ref[2:3, :] * mix
    x1_ref[...] = x1
    h2 = _rmsnorm(x1, n2_ref[...]) * (1.0 + mod_ref[4:5, :]) + mod_ref[3:4, :]
    h2_hi = h2.astype(BF16)
    tm = h2.shape[0]
    for s in range(D_MODEL // 128):
        rows_ref[pl.ds(s, tm, stride=ROW_SUBLANES), :] = h2[:, s * 128:(s + 1) * 128]
    for s in range(COMB_SUBLANE + 1, ROW_SUBLANES):
        rows_ref[pl.ds(s, tm, stride=ROW_SUBLANES), :] = jnp.zeros((tm, 128), F32)

    h2_lo = (h2 - h2_hi.astype(F32)).astype(BF16)
    logits = _dot(h2_hi, wr_ref[0]) + _dot(h2_hi, wr_ref[1]) + _dot(h2_lo, wr_ref[0])
    lane = lax.broadcasted_iota(jnp.int32, logits.shape, 1)
    neg = -jnp.inf
    first = lambda hit: jnp.min(jnp.where(hit, lane, ROUTER_LANES), axis=-1, keepdims=True)
    gl = jnp.where(lane < N_GROUPS, logits, neg)
    gmax = jnp.max(gl, axis=-1, keepdims=True)
    p_top = 1.0 / jnp.sum(jnp.exp(gl - gmax), axis=-1, keepdims=True)
    g_idx = first(gl == gmax)
    in_group = ((lane >= EXPERT_LANE0) & (lane < EXPERT_LANE0 + N_EXPERTS)
                & (((lane - EXPERT_LANE0) >> 3) == g_idx))
    sl = jnp.where(in_group, logits, neg)
    m1 = jnp.max(sl, axis=-1, keepdims=True)
    i1 = first(sl == m1)
    sl2 = jnp.where(lane == i1, neg, sl)
    m2 = jnp.max(sl2, axis=-1, keepdims=True)
    i2 = first(sl2 == m2)
    r = jnp.exp(m2 - m1)
    w1 = p_top / (1.0 + r)
    comb = jnp.where(lane == i1, w1, jnp.where(lane == i2, w1 * r, 0.0))
    rows_ref[pl.ds(COMB_SUBLANE, tm, stride=ROW_SUBLANES), :] = comb
    comb_ref[...] = jnp.where(lane == 0, g_idx.astype(F32), comb)


def _post(ohf, ohb, ogf, ogb, hgate, gr, sa, sb, x, mod, tiles_per_row, hgn, glan, n2, wa, wb, wo, wr):
    t = x.shape[0]
    tm = TM_POST
    const = lambda shape: pl.BlockSpec(shape, lambda i: (0,) * len(shape))
    tok = lambda w: pl.BlockSpec((tm, w), lambda i: (i, 0))
    return pl.pallas_call(
        _post_body,
        grid=(t // tm,),
        in_specs=[tok(1024)] * 9
                 + [pl.BlockSpec((None, 6, D_MODEL), lambda i: (i // tiles_per_row, 0, 0)),
                    const((1, HG_DV)), const((1, GLA_DV)), const((1, D_MODEL)),
                    const((1024, 1024)), const((1024, 1024)), const((1024, 1024)),
                    const((2, D_MODEL, ROUTER_LANES))],
        out_specs=[tok(1024), pl.BlockSpec((tm * ROW_SUBLANES, 128), lambda i: (i, 0)), tok(ROUTER_LANES)],
        out_shape=[jax.ShapeDtypeStruct((t, D_MODEL), F32), jax.ShapeDtypeStruct((t * ROW_SUBLANES, 128), F32),
                   jax.ShapeDtypeStruct((t, ROUTER_LANES), F32)],
        scratch_shapes=[pltpu.VMEM((tm, 1024), BF16), pltpu.VMEM((tm, 1024), BF16)],
        compiler_params=pltpu.CompilerParams(vmem_limit_bytes=VMEM_LIMIT),
        name="post",
    )(ohf, ohb, ogf, ogb, hgate, gr, sa, sb, x, mod, hgn, glan, n2, wa, wb, wo, wr)


def _route_plan(group, t):
    tr, blk = TR_MOE, PLAN_BLOCK
    nt, nb = t // tr + N_GROUPS, t // blk
    i32 = jnp.int32
    gids = jnp.arange(N_GROUPS, dtype=i32)
    onehot = (group[:, None] == gids[None, :]).astype(i32).reshape(nb, blk, N_GROUPS)
    within = jnp.cumsum(onehot, axis=1)
    blk_end = jnp.cumsum(within[:, -1, :], axis=0)
    blk_start = blk_end - within[:, -1, :]
    counts = blk_end[-1]
    tiles = (counts + tr - 1) // tr
    tile_end = jnp.cumsum(tiles)
    tile_group = jnp.minimum(jnp.sum(jnp.arange(nt, dtype=i32)[:, None] >= tile_end[None, :], axis=1),
                             N_GROUPS - 1).astype(i32)
    row = jnp.arange(nt * tr, dtype=i32)
    gsel = (jnp.repeat(tile_group, tr)[:, None] == gids[None, :]).astype(i32)
    pick = lambda per_group: jnp.sum(gsel * per_group[None, :], axis=1)
    k = row - pick(tile_end - tiles) * tr
    valid = k < pick(counts)
    per_block = lambda table: jnp.sum(gsel[:, None, :] * table[None, :, :], axis=2)
    b_row = jnp.minimum(jnp.sum(per_block(blk_end) <= k[:, None], axis=1), nb - 1)
    bsel = (b_row[:, None] == jnp.arange(nb, dtype=i32)[None, :]).astype(i32)
    k_local = k - jnp.sum(bsel * per_block(blk_start), axis=1)
    sel = (bsel[:, :, None] * gsel[:, None, :]).reshape(nt * tr, nb * N_GROUPS).astype(F32)
    table = jnp.transpose(within, (0, 2, 1)).reshape(nb * N_GROUPS, blk).astype(F32)
    ranks = jnp.dot(sel, table)
    t_local = jnp.sum(ranks <= k_local[:, None].astype(F32), axis=1).astype(i32)
    row_src = jnp.where(valid, b_row * blk + t_local, 0)
    spare = t + ((row // tr) % 2) * tr + row % tr
    return tile_group, row_src, jnp.where(valid, row_src, spare)


def _moe_body(tg_ref, src_ref, dst_ref, rows_hbm, wg_ref, wu_ref, wd_ref, out_hbm, xbuf, ybuf, gsem, ssem):
    tr = TR_MOE
    i = pl.program_id(0)
    nt = pl.num_programs(0)
    slot = i % 2

    def gather_row(tile, slot_, r):
        tok = src_ref[tile * tr + r]
        return pltpu.make_async_copy(rows_hbm.at[pl.ds(pl.multiple_of(tok * ROW_SUBLANES, ROW_SUBLANES), ROW_SUBLANES)],
                                     xbuf.at[slot_, pl.ds(r * ROW_SUBLANES, ROW_SUBLANES)], gsem.at[slot_])

    def scatter_row(tile, slot_, r):
        tok = dst_ref[tile * tr + r]
        return pltpu.make_async_copy(ybuf.at[slot_, pl.ds(r * OUT_SUBLANES, OUT_SUBLANES)],
                                     out_hbm.at[pl.ds(pl.multiple_of(tok * OUT_SUBLANES, OUT_SUBLANES), OUT_SUBLANES)],
                                     ssem.at[slot_])

    def for_rows(fn):
        def body(r, carry):
            fn(r)
            return carry
        lax.fori_loop(0, tr, body, 0, unroll=8)

    def gathered(slot_):
        return pltpu.make_async_copy(rows_hbm.at[pl.ds(0, tr * ROW_SUBLANES)], xbuf.at[slot_], gsem.at[slot_])

    def scattered(slot_):
        return pltpu.make_async_copy(ybuf.at[slot_], out_hbm.at[pl.ds(0, tr * OUT_SUBLANES)], ssem.at[slot_])

    @pl.when(i == 0)
    def _():
        for_rows(lambda r: gather_row(0, 0, r).start())
        n_tok = out_hbm.shape[0] // OUT_SUBLANES - 2 * tr
        ybuf[...] = jnp.zeros_like(ybuf)
        for s in range(2):
            spare = pltpu.make_async_copy(
                ybuf.at[s], out_hbm.at[pl.ds((n_tok + s * tr) * OUT_SUBLANES, tr * OUT_SUBLANES)], ssem.at[s])
            spare.start()
            spare.wait()

    @pl.when(i + 1 < nt)
    def _():
        for_rows(lambda r: gather_row(i + 1, 1 - slot, r).start())

    gathered(slot).wait()

    @pl.when(i >= 2)
    def _():
        scattered(slot).wait()

    x = jnp.concatenate([xbuf[slot, pl.ds(s, tr, stride=ROW_SUBLANES), :] for s in range(D_MODEL // 128)],
                        axis=1).astype(BF16)
    comb = xbuf[slot, pl.ds(COMB_SUBLANE, tr, stride=ROW_SUBLANES), :]
    lane = lax.broadcasted_iota(jnp.int32, comb.shape, 1)
    lane0 = EXPERT_LANE0 + tg_ref[i] * EXPERTS_PER_GROUP
    acc = jnp.zeros((tr, D_MODEL), F32)
    for e in range(EXPERTS_PER_GROUP):
        col = jnp.sum(jnp.where(lane == lane0 + e, comb, 0.0), axis=-1, keepdims=True)
        hid = (_silu(_dot(x, wg_ref[e])) * _dot(x, wu_ref[e]) * col).astype(BF16)
        acc = acc + _dot(hid, wd_ref[e])

    for s in range(D_MODEL // 128):
        ybuf[slot, pl.ds(s, tr, stride=OUT_SUBLANES), :] = acc[:, s * 128:(s + 1) * 128]

    for_rows(lambda r: scatter_row(i, slot, r).start())

    @pl.when(i == nt - 1)
    def _():
        scattered(slot).wait()
        scattered(1 - slot).wait()


def _moe(rows, plan, wg, wu, wd, t):
    tr = TR_MOE
    tile_group, row_src, row_dst = plan
    nt = tile_group.shape[0]
    group_w = lambda shape: pl.BlockSpec((EXPERTS_PER_GROUP,) + shape, lambda i, tg, src, dst: (tg[i], 0, 0))
    grid_spec = pltpu.PrefetchScalarGridSpec(
        num_scalar_prefetch=3,
        grid=(nt,),
        in_specs=[pl.BlockSpec(memory_space=pl.ANY),
                  group_w((D_MODEL, D_EXPERT)), group_w((D_MODEL, D_EXPERT)), group_w((D_EXPERT, D_MODEL))],
        out_specs=pl.BlockSpec(memory_space=pl.ANY),
        scratch_shapes=[pltpu.VMEM((2, tr * ROW_SUBLANES, 128), F32), pltpu.VMEM((2, tr * OUT_SUBLANES, 128), F32),
                        pltpu.SemaphoreType.DMA((2,)), pltpu.SemaphoreType.DMA((2,))])
    return pl.pallas_call(
        _moe_body,
        grid_spec=grid_spec,
        out_shape=jax.ShapeDtypeStruct(((t + 2 * tr) * OUT_SUBLANES, 128), F32),
        compiler_params=pltpu.CompilerParams(dimension_semantics=("arbitrary",), vmem_limit_bytes=VMEM_LIMIT),
        name="moe",
    )(tile_group, row_src, row_dst, rows, wg, wu, wd)


def _final_body(x1_ref, moe_ref, mod_ref, fn_ref, y_ref):
    tm = x1_ref.shape[0]
    moe = jnp.concatenate([moe_ref[pl.ds(s, tm, stride=OUT_SUBLANES), :] for s in range(D_MODEL // 128)], axis=1)
    y_ref[...] = _rmsnorm(x1_ref[...] + mod_ref[5:6, :] * moe, fn_ref[...])


def _final(x1, moe, mod, tiles_per_row, fn):
    t = x1.shape[0]
    tm = TM_FINAL
    return pl.pallas_call(
        _final_body,
        grid=(t // tm,),
        in_specs=[pl.BlockSpec((tm, D_MODEL), lambda i: (i, 0)),
                  pl.BlockSpec((tm * OUT_SUBLANES, 128), lambda i: (i, 0)),
                  pl.BlockSpec((None, 6, D_MODEL), lambda i: (i // tiles_per_row, 0, 0)),
                  pl.BlockSpec((1, D_MODEL), lambda i: (0, 0))],
        out_specs=pl.BlockSpec((tm, D_MODEL), lambda i: (i, 0)),
        out_shape=jax.ShapeDtypeStruct((t, D_MODEL), F32),
        compiler_params=pltpu.CompilerParams(vmem_limit_bytes=VMEM_LIMIT),
        name="final",
    )(x1, moe, mod, fn)


def _trunk_layer(x, mod, s0_hg, s0_gla, emit_state, w):
    b, l, _ = x.shape
    xt = x.reshape(b * l, D_MODEL)
    per_row = lambda tm: (l // tm) if mod.shape[0] > 1 else (b * l // tm)
    (hq, gf, gb, hv, hgate, gq, gk, gv, gr, laf, lab, sa, sb) = _inproj(
        xt, mod, per_row(TM_PROJ), w["n1"], w["lbp"], w["wmain"], w["waup"], w["ba"])
    seq = lambda a: a.reshape(b, l, a.shape[-1])
    hg = _scan(seq(hq), None, seq(hv), seq(gf), seq(gb), s0_hg, w["scan_consts"], HG_H, HG_DK, HG_DV,
               emit_state)
    gla = _scan(seq(gq), seq(gk), seq(gv), seq(laf), seq(lab), s0_gla, w["scan_consts"], GLA_H, GLA_DK,
                GLA_DV, emit_state)
    flat = lambda a: a.reshape(b * l, a.shape[-1])
    x1, rows, comb = _post(flat(hg[0]), flat(hg[1]), flat(gla[0]), flat(gla[1]), hgate, gr, sa, sb, xt, mod,
                           per_row(TM_POST), w["hgn"], w["glan"], w["n2"], w["wa"], w["wb"], w["wo"], w["wr"])
    plan = _route_plan(comb[:, 0].astype(jnp.int32), b * l)
    moe = _moe(rows, plan, w["wg"], w["wu"], w["wd"], b * l)
    y = _final(x1, moe, mod, per_row(TM_FINAL), w["fn"])
    states = (hg[2], gla[2]) if emit_state else (None, None)
    return y.reshape(b, l, D_MODEL), states


def kernel(x_prompt, x_sample, state_hgrn, state_gla, c, c_ctx, ada_w, ada_b, norm1_g, norm2_g, w_in,
           hg_lb_param, hg_norm_g, gla_wa_up, gla_ba, gla_norm_g, w_br_a, w_br_b, w_out,
           w_router_group, w_router_expert, w_exp_gate, w_exp_up, w_exp_down, final_norm_g):
    nb = c.shape[0]
    cond = jnp.concatenate([c, c_ctx[None, :], jnp.zeros((8 - nb - 1, D_MODEL), F32)], axis=0)
    mod = _modulation(cond, ada_w[0], ada_b).reshape(8, 6, D_MODEL)

    zeros = jnp.zeros((GLA_RANK, GLA_H * GLA_DK), F32)
    router = jnp.concatenate(
        [w_router_group[0], jnp.transpose(w_router_expert[0], (1, 0, 2)).reshape(D_MODEL, N_EXPERTS),
         jnp.zeros((D_MODEL, ROUTER_LANES - N_GROUPS - N_EXPERTS), F32)], axis=1)
    w = {
        "n1": norm1_g, "n2": norm2_g, "fn": final_norm_g[None, :], "lbp": hg_lb_param,
        "wmain": w_in[0].astype(BF16),
        "waup": jnp.concatenate([jnp.concatenate([gla_wa_up[0, 0], zeros], axis=1),
                                 jnp.concatenate([zeros, gla_wa_up[0, 1]], axis=1)], axis=0).astype(BF16),
        "ba": gla_ba[0].reshape(1, 2 * GLA_H * GLA_DK),
        "hgn": hg_norm_g, "glan": gla_norm_g,
        "wa": w_br_a[0].astype(BF16), "wb": w_br_b[0].astype(BF16), "wo": w_out[0].astype(BF16),
        "wr": jnp.stack([router.astype(BF16), (router - router.astype(BF16).astype(F32)).astype(BF16)]),
        "wg": w_exp_gate[0].astype(BF16), "wu": w_exp_up[0].astype(BF16), "wd": w_exp_down[0].astype(BF16),
        "scan_consts": _scan_constants(),
    }
    y_prompt, (st_hg, st_gla) = _trunk_layer(x_prompt, mod[nb:nb + 1], None, None, True, w)
    y_sample, _ = _trunk_layer(x_sample, mod[:nb], state_hgrn[:, 0], state_gla[:, 0], False, w)
    return y_prompt, y_sample, st_hg[:, None], st_gla[:, None]
```

```python
import functools

import numpy as np
import jax
import jax.numpy as jnp
from jax import lax
from jax.experimental import pallas as pl
from jax.experimental.pallas import tpu as pltpu

F32 = jnp.float32
BF16 = jnp.bfloat16

D_MODEL = 1024
CHUNK = 64
N_LEVELS = 6
CHUNKS_PER_STEP = 4
HG_H, HG_DK, HG_DV = 8, 128, 128
GLA_H, GLA_DK, GLA_DV = 4, 128, 256
GLA_RANK = 16
GLA_GATE_NORM = 16.0
N_GROUPS = 4
EXPERTS_PER_GROUP = 8
N_EXPERTS = N_GROUPS * EXPERTS_PER_GROUP
D_EXPERT = D_MODEL // 4
EPS = 1e-6
ROUTER_LANES = 128
EXPERT_LANE0 = N_GROUPS
VMEM_LIMIT = 56 * 1024 * 1024

TM_PROJ = 256
TM_POST = 256
TM_FINAL = 512
TR_MOE = 256
PLAN_BLOCK = 256
OUT_SUBLANES = D_MODEL // 128
COMB_SUBLANE = OUT_SUBLANES
ROW_SUBLANES = 16


def _sigmoid(x):
    return 1.0 / (1.0 + jnp.exp(-x))


def _silu(x):
    return x * _sigmoid(x)


def _log_sigmoid(x):
    return jnp.minimum(x, 0.0) - jnp.log(1.0 + jnp.exp(-jnp.abs(x)))


def _rmsnorm(x, g):
    return x * lax.rsqrt(jnp.mean(x * x, axis=-1, keepdims=True) + EPS) * g


def _dot(a, b):
    return jnp.dot(a, b, preferred_element_type=F32)


def _dot_exact(a, b):
    return jnp.dot(a, b, preferred_element_type=F32, precision=lax.Precision.HIGHEST)


def _dot_nt(a, b):
    return lax.dot_general(a, b, (((1,), (1,)), ((), ())), preferred_element_type=F32)


def _dot_tn(a, b):
    return lax.dot_general(a, b, (((0,), (0,)), ((), ())), preferred_element_type=F32)


def _mod_body(c_ref, w_ref, b_ref, o_ref):
    o_ref[...] = _dot_exact(_silu(c_ref[...]), w_ref[...]) + b_ref[...]


def _modulation(cond, ada_w, ada_b):
    rows, n = cond.shape[0], ada_w.shape[1]
    tn = n // 4
    return pl.pallas_call(
        _mod_body,
        grid=(n // tn,),
        in_specs=[pl.BlockSpec((rows, D_MODEL), lambda j: (0, 0)),
                  pl.BlockSpec((D_MODEL, tn), lambda j: (0, j)),
                  pl.BlockSpec((1, tn), lambda j: (0, j))],
        out_specs=pl.BlockSpec((rows, tn), lambda j: (0, j)),
        out_shape=jax.ShapeDtypeStruct((rows, n), F32),
        compiler_params=pltpu.CompilerParams(vmem_limit_bytes=VMEM_LIMIT),
        name="modulation",
    )(cond, ada_w, ada_b)


_C_HQ, _C_HFF, _C_HFB, _C_HI, _C_HGATE = 0, 1024, 2048, 3072, 4096
_C_GQ, _C_GK, _C_GV, _C_GR, _C_TAIL = 5120, 5632, 6144, 7168, 8192
GA_WIDTH = 2 * GLA_RANK
W_TAIL = GA_WIDTH + 2 * D_MODEL
W_MAIN = _C_TAIL + W_TAIL


def _inproj_body(x_ref, mod_ref, n1_ref, lbp_ref, wmain_ref, waup_ref, ba_ref,
                 hq_ref, gf_ref, gb_ref, hv_ref, hgate_ref, gq_ref, gk_ref, gv_ref, gr_ref,
                 laf_ref, lab_ref, sa_ref, sb_ref):
    x = x_ref[...]
    h = _rmsnorm(x, n1_ref[...]) * (1.0 + mod_ref[1:2, :]) + mod_ref[0:1, :]
    hb = h.astype(BF16)

    def proj(c0, w):
        return _dot(hb, wmain_ref[:, c0:c0 + w])

    p0, p1 = lbp_ref[0:1, :], lbp_ref[1:2, :]
    pm = jnp.maximum(p0, p1)
    e0, e1 = jnp.exp(p0 - pm), jnp.exp(p1 - pm)
    lb = e0 / (e0 + e1)

    hq_ref[...] = (proj(_C_HQ, 1024) * HG_DK ** -0.5).astype(BF16)
    gf_ref[...] = jnp.log(lb + (1.0 - lb) * _sigmoid(proj(_C_HFF, 1024)))
    gb_ref[...] = jnp.log(lb + (1.0 - lb) * _sigmoid(proj(_C_HFB, 1024)))
    hv_ref[...] = proj(_C_HI, 1024).astype(BF16)
    hgate_ref[...] = _silu(proj(_C_HGATE, 1024)).astype(BF16)
    gq_ref[...] = (proj(_C_GQ, 512) * GLA_DK ** -0.5).astype(BF16)
    gk_ref[...] = proj(_C_GK, 512).astype(BF16)
    gv_ref[...] = proj(_C_GV, 1024).astype(BF16)
    gr_ref[...] = _silu(proj(_C_GR, 1024)).astype(BF16)
    tail = proj(_C_TAIL, W_TAIL)
    sa_ref[...] = _sigmoid(tail[:, GA_WIDTH:GA_WIDTH + D_MODEL]).astype(BF16)
    sb_ref[...] = _sigmoid(tail[:, GA_WIDTH + D_MODEL:]).astype(BF16)

    ga = tail[:, :GA_WIDTH].astype(BF16)
    la = _log_sigmoid(_dot(ga, waup_ref[...]) + ba_ref[...]) * (1.0 / GLA_GATE_NORM)
    laf_ref[...] = la[:, :512]
    lab_ref[...] = la[:, 512:]


def _inproj(x, mod, tiles_per_row, n1, lbp, wmain, waup, ba):
    t = x.shape[0]
    tm = TM_PROJ
    const = lambda shape: pl.BlockSpec(shape, lambda i: (0,) * len(shape))
    tok = lambda w: pl.BlockSpec((tm, w), lambda i: (i, 0))
    widths = (1024, 1024, 1024, 1024, 1024, 512, 512, 1024, 1024, 512, 512, 1024, 1024)
    dtypes = (BF16, F32, F32, BF16, BF16, BF16, BF16, BF16, BF16, F32, F32, BF16, BF16)
    return pl.pallas_call(
        _inproj_body,
        grid=(t // tm,),
        in_specs=[tok(D_MODEL),
                  pl.BlockSpec((None, 6, D_MODEL), lambda i: (i // tiles_per_row, 0, 0)),
                  const((1, D_MODEL)), const((2, D_MODEL)),
                  pl.BlockSpec((D_MODEL, W_MAIN), lambda i: (0, 0), pipeline_mode=pl.Buffered(1)),
                  const((GA_WIDTH, 1024)), const((1, 1024))],
        out_specs=[tok(w) for w in widths],
        out_shape=[jax.ShapeDtypeStruct((t, w), dt) for w, dt in zip(widths, dtypes)],
        compiler_params=pltpu.CompilerParams(vmem_limit_bytes=VMEM_LIMIT),
        name="inproj",
    )(x, mod, n1, lbp, wmain, waup, ba)


N_MXU_LEVELS = 3
ROW_GROUP = 16
START_SLOT, END_SLOT = N_LEVELS, N_LEVELS + 1
LOG2E = 1.4426950408889634


def _boundary_row(t, h, d):
    base = (t // (2 * h)) * 2 * h
    return base + h - 1 if d == 0 else base + h


def _scan_constants():
    c = CHUNK
    tri = np.tril(np.ones((c, c), np.float32))
    m = [tri] + [np.abs(tri - tri[[_boundary_row(t, 1 << l, 0) for t in range(c)]]) for l in range(N_MXU_LEVELS)]
    m = np.stack(m)
    lvl = np.full((c, c), N_LEVELS + 1, np.int32)
    for t in range(c):
        lvl[t, t] = N_LEVELS
        for s in range(t):
            lvl[t, s] = int(np.floor(np.log2(t ^ s)))
    m_fwd = m.reshape(-1, c)
    m_bwd = m[:, ::-1, ::-1].reshape(-1, c)
    m_both = np.stack([m_fwd, m_bwd])
    return (jnp.asarray(np.concatenate([m_both, m_both], axis=2), BF16), jnp.asarray(np.stack([lvl, lvl.T])))


def _scan_prepare(u, d, rows, q_ref, k_ref, g_ref, m_ref, side, cb_ref, kf_ref, kb_ref, p_ref):
    c = CHUNK
    w = g_ref.shape[-1]
    g = g_ref[rows, :]
    gs = g * LOG2E
    g1 = gs.astype(BF16)
    g2 = (gs - g1.astype(F32)).astype(BF16)
    cb_ref[u] = _dot(m_ref[d], jnp.concatenate([g1, g2], axis=0))
    k32 = (1.0 - jnp.exp(g)) if k_ref is None else k_ref[rows, :].astype(F32)
    kf_ref[u] = k32
    kb_ref[u] = k32.astype(BF16)
    bcast = lambda r, n: jnp.broadcast_to(cb_ref[u, r:r + 1, :], (n, w))
    for i in range(c // ROW_GROUP):
        t0 = ROW_GROUP * i
        grp = slice(t0, t0 + ROW_GROUP)
        cum = cb_ref[u, grp, :]
        q = q_ref[rows.start + t0:rows.start + t0 + ROW_GROUP, :]
        k = kb_ref[u, grp, :]
        for l in range(N_LEVELS):
            h = 1 << l
            if l < N_MXU_LEVELS:
                ex = cb_ref[u, c * (l + 1) + t0:c * (l + 1) + t0 + ROW_GROUP, :]
            else:
                ex = []
                for t in (t0, t0 + 8):
                    half = slice(t - t0, t - t0 + 8)
                    bnd = bcast(_boundary_row(t, h, d), 8)
                    ex.append(cum[half] - bnd if ((t & h) != 0) == (d == 0) else bnd - cum[half])
                ex = jnp.concatenate(ex, axis=0)
            if h >= ROW_GROUP:
                src = q if ((t0 & h) != 0) == (d == 0) else k
            else:
                upper = side[l] > 0
                src = jnp.where(upper, q, k) if d == 0 else jnp.where(upper, k, q)
            p_ref[u, l, grp, :] = src * jnp.exp2(ex).astype(BF16)
        p_ref[u, START_SLOT, grp, :] = q * jnp.exp2(cum).astype(BF16)
        edge = c - 1 if d == 0 else 0
        p_ref[u, END_SLOT, grp, :] = (kf_ref[u, grp, :] * jnp.exp2(bcast(edge, ROW_GROUP) - cum)).astype(BF16)


def _scan_scores(u, rows, q_ref, masks, kb_ref, p_ref, a_ref, n_heads, dk):
    for h in range(n_heads):
        ks = slice(h * dk, (h + 1) * dk)
        diag = _dot_nt(q_ref[rows, ks], kb_ref[u, :, ks]).astype(BF16)
        a = jnp.where(masks[N_LEVELS], diag, jnp.zeros_like(diag))
        for l in range(N_LEVELS):
            p = p_ref[u, l, :, ks]
            a = jnp.where(masks[l], _dot_nt(p, p).astype(BF16), a)
        a_ref[u, h] = a


def _scan_outputs(u, d, rows, v_ref, cb_ref, p_ref, a_ref, st_ref, o_ref, n_heads, dk, dv):
    edge = CHUNK - 1 if d == 0 else 0
    for h in range(n_heads):
        ks = slice(h * dk, (h + 1) * dk)
        vs = slice(h * dv, (h + 1) * dv)
        v = v_ref[rows, vs]
        st = st_ref[d, h]
        o = _dot(a_ref[u, h], v) + _dot_nt(p_ref[u, START_SLOT, :, ks], st.astype(BF16))
        o_ref[rows, vs] = o.astype(o_ref.dtype)
        st_ref[d, h] = st * jnp.exp2(cb_ref[u, edge:edge + 1, ks]) + _dot_tn(v, p_ref[u, END_SLOT, :, ks])


def _scan_body(*refs, n_heads, dk, dv, derive_k, zero_init, emit_state):
    refs = list(refs)
    q_refs = (refs.pop(0), refs.pop(0))
    k_refs = (None, None) if derive_k else (refs.pop(0), refs.pop(0))
    v_refs = (refs.pop(0), refs.pop(0))
    g_refs = (refs.pop(0), refs.pop(0))
    m_ref, lvl_ref = refs.pop(0), refs.pop(0)
    s0_ref = None if zero_init else refs.pop(0)
    o_refs = (refs.pop(0), refs.pop(0))
    sout_ref = refs.pop(0) if emit_state else None
    st_ref, cb_ref, kf_ref, kb_ref, p_ref, a_ref = refs
    n = pl.program_id(1)

    @pl.when(n == 0)
    def _():
        for d in range(2):
            for h in range(n_heads):
                st_ref[d, h] = jnp.zeros((dv, dk), F32) if zero_init else s0_ref[d, h].T

    cps = CHUNKS_PER_STEP
    rows = lambda c: slice(c * CHUNK, (c + 1) * CHUNK)
    masks = [[lvl_ref[d].astype(F32).astype(BF16) == l for l in range(N_LEVELS + 1)] for d in range(2)]
    row = lax.broadcasted_iota(jnp.int32, (ROW_GROUP, q_refs[0].shape[-1]), 0)
    side = [((row >> l) & 1).astype(F32).astype(BF16) for l in range(N_LEVELS) if (1 << l) < ROW_GROUP]
    units = [(d, c) for c in range(cps) for d in range(2)]
    for i in range(len(units) + 1):
        if i < len(units):
            d, c = units[i]
            _scan_prepare(d * cps + c, d, rows(c), q_refs[d], k_refs[d], g_refs[d], m_ref, side, cb_ref, kf_ref,
                          kb_ref, p_ref)
        if i >= 1:
            d, c = units[i - 1]
            _scan_scores(d * cps + c, rows(c), q_refs[d], masks[d], kb_ref, p_ref, a_ref, n_heads, dk)
    for d in range(2):
        for c in (range(cps) if d == 0 else reversed(range(cps))):
            _scan_outputs(d * cps + c, d, rows(c), v_refs[d], cb_ref, p_ref, a_ref, st_ref, o_refs[d], n_heads,
                          dk, dv)

    if emit_state:
        @pl.when(n == pl.num_programs(1) - 1)
        def _():
            for d in range(2):
                for h in range(n_heads):
                    sout_ref[d, h] = st_ref[d, h].T


def _scan(q, k, v, g_f, g_b, s0, consts, n_heads, dk, dv, emit_state):
    b, l, _ = q.shape
    step = CHUNKS_PER_STEP * CHUNK
    n = l // step
    slots = 2 * CHUNKS_PER_STEP
    m_all, lvl = consts
    fwd = lambda w: pl.BlockSpec((None, step, w), lambda i, j: (i, j, 0))
    bwd = lambda w: pl.BlockSpec((None, step, w), lambda i, j: (i, n - 1 - j, 0))
    const = lambda shape: pl.BlockSpec(shape, lambda i, j: (0,) * len(shape))
    state = pl.BlockSpec((None, 2, n_heads, dk, dv), lambda i, j: (i, 0, 0, 0, 0))
    wk, wv = n_heads * dk, n_heads * dv
    args, in_specs = [q, q], [fwd(wk), bwd(wk)]
    if k is not None:
        args += [k, k]
        in_specs += [fwd(wk), bwd(wk)]
    args += [v, v, g_f, g_b, m_all, lvl]
    in_specs += [fwd(wv), bwd(wv), fwd(wk), bwd(wk), const(m_all.shape), const(lvl.shape)]
    if s0 is not None:
        args.append(s0)
        in_specs.append(state)
    out_specs = [fwd(wv), bwd(wv)]
    out_shape = [jax.ShapeDtypeStruct((b, l, wv), BF16)] * 2
    if emit_state:
        out_specs.append(state)
        out_shape.append(jax.ShapeDtypeStruct((b, 2, n_heads, dk, dv), F32))
    body = functools.partial(_scan_body, n_heads=n_heads, dk=dk, dv=dv, derive_k=k is None,
                             zero_init=s0 is None, emit_state=emit_state)
    return pl.pallas_call(
        body,
        grid=(b, n),
        in_specs=in_specs,
        out_specs=out_specs,
        out_shape=out_shape,
        scratch_shapes=[pltpu.VMEM((2, n_heads, dv, dk), F32),
                        pltpu.VMEM((slots, (1 + N_MXU_LEVELS) * CHUNK, wk), F32),
                        pltpu.VMEM((slots, CHUNK, wk), F32), pltpu.VMEM((slots, CHUNK, wk), BF16),
                        pltpu.VMEM((slots, N_LEVELS + 2, CHUNK, wk), BF16),
                        pltpu.VMEM((slots, n_heads, CHUNK, CHUNK), BF16)],
        compiler_params=pltpu.CompilerParams(dimension_semantics=("arbitrary", "arbitrary"),
                                             vmem_limit_bytes=VMEM_LIMIT),
        name="scan_hgrn" if k is None else "scan_gla",
    )(*args)


def _post_body(ohf_ref, ohb_ref, ogf_ref, ogb_ref, hgate_ref, gr_ref, sa_ref, sb_ref, x_ref, mod_ref,
               hgn_ref, glan_ref, n2_ref, wa_ref, wb_ref, wo_ref, wr_ref,
               x1_ref, rows_ref, comb_ref, yh_ref, yg_ref):
    for h in range(HG_H):
        s = slice(h * HG_DV, (h + 1) * HG_DV)
        o = ohf_ref[:, s].astype(F32) + ohb_ref[:, s].astype(F32)
        yh_ref[:, s] = (_rmsnorm(o, hgn_ref[...]) * hgate_ref[:, s].astype(F32)).astype(BF16)
    for h in range(GLA_H):
        s = slice(h * GLA_DV, (h + 1) * GLA_DV)
        o = ogf_ref[:, s].astype(F32) + ogb_ref[:, s].astype(F32)
        yg_ref[:, s] = (_rmsnorm(o, glan_ref[...]) * gr_ref[:, s].astype(F32)).astype(BF16)
    merged = (sa_ref[...].astype(F32) * _dot(yh_ref[...], wa_ref[...])
              + sb_ref[...].astype(F32) * _dot(yg_ref[...], wb_ref[...]))
    mix = _dot(merged.astype(BF16), wo_ref[...])
    x1 = x_ref[...] + mod_ref[2:3, :] * mix
    x1_ref[...] = x1
    h2 = _rmsnorm(x1, n2_ref[...]) * (1.0 + mod_ref[4:5, :]) + mod_ref[3:4, :]
    h2_hi = h2.astype(BF16)
    tm = h2.shape[0]
    for s in range(D_MODEL // 128):
        rows_ref[pl.ds(s, tm, stride=ROW_SUBLANES), :] = h2[:, s * 128:(s + 1) * 128]
    for s in range(COMB_SUBLANE + 1, ROW_SUBLANES):
        rows_ref[pl.ds(s, tm, stride=ROW_SUBLANES), :] = jnp.zeros((tm, 128), F32)

    h2_lo = (h2 - h2_hi.astype(F32)).astype(BF16)
    logits = _dot(h2_hi, wr_ref[0]) + _dot(h2_hi, wr_ref[1]) + _dot(h2_lo, wr_ref[0])
    lane = lax.broadcasted_iota(jnp.int32, logits.shape, 1)
    neg = -jnp.inf
    first = lambda hit: jnp.min(jnp.where(hit, lane, ROUTER_LANES), axis=-1, keepdims=True)
    gl = jnp.where(lane < N_GROUPS, logits, neg)
    gmax = jnp.max(gl, axis=-1, keepdims=True)
    p_top = 1.0 / jnp.sum(jnp.exp(gl - gmax), axis=-1, keepdims=True)
    g_idx = first(gl == gmax)
    in_group = ((lane >= EXPERT_LANE0) & (lane < EXPERT_LANE0 + N_EXPERTS)
                & (((lane - EXPERT_LANE0) >> 3) == g_idx))
    sl = jnp.where(in_group, logits, neg)
    m1 = jnp.max(sl, axis=-1, keepdims=True)
    i1 = first(sl == m1)
    sl2 = jnp.where(lane == i1, neg, sl)
    m2 = jnp.max(sl2, axis=-1, keepdims=True)
    i2 = first(sl2 == m2)
    r = jnp.exp(m2 - m1)
    w1 = p_top / (1.0 + r)
    comb = jnp.where(lane == i1, w1, jnp.where(lane == i2, w1 * r, 0.0))
    rows_ref[pl.ds(COMB_SUBLANE, tm, stride=ROW_SUBLANES), :] = comb
    comb_ref[...] = jnp.where(lane == 0, g_idx.astype(F32), comb)


def _post(ohf, ohb, ogf, ogb, hgate, gr, sa, sb, x, mod, tiles_per_row, hgn, glan, n2, wa, wb, wo, wr):
    t = x.shape[0]
    tm = TM_POST
    const = lambda shape: pl.BlockSpec(shape, lambda i: (0,) * len(shape))
    tok = lambda w: pl.BlockSpec((tm, w), lambda i: (i, 0))
    return pl.pallas_call(
        _post_body,
        grid=(t // tm,),
        in_specs=[tok(1024)] * 9
                 + [pl.BlockSpec((None, 6, D_MODEL), lambda i: (i // tiles_per_row, 0, 0)),
                    const((1, HG_DV)), const((1, GLA_DV)), const((1, D_MODEL)),
                    const((1024, 1024)), const((1024, 1024)), const((1024, 1024)),
                    const((2, D_MODEL, ROUTER_LANES))],
        out_specs=[tok(1024), pl.BlockSpec((tm * ROW_SUBLANES, 128), lambda i: (i, 0)), tok(ROUTER_LANES)],
        out_shape=[jax.ShapeDtypeStruct((t, D_MODEL), F32), jax.ShapeDtypeStruct((t * ROW_SUBLANES, 128), F32),
                   jax.ShapeDtypeStruct((t, ROUTER_LANES), F32)],
        scratch_shapes=[pltpu.VMEM((tm, 1024), BF16), pltpu.VMEM((tm, 1024), BF16)],
        compiler_params=pltpu.CompilerParams(vmem_limit_bytes=VMEM_LIMIT),
        name="post",
    )(ohf, ohb, ogf, ogb, hgate, gr, sa, sb, x, mod, hgn, glan, n2, wa, wb, wo, wr)


def _route_plan(group, t):
    tr, blk = TR_MOE, PLAN_BLOCK
    nt, nb = t // tr + N_GROUPS, t // blk
    i32 = jnp.int32
    gids = jnp.arange(N_GROUPS, dtype=i32)
    onehot = (group[:, None] == gids[None, :]).astype(i32).reshape(nb, blk, N_GROUPS)
    within = jnp.cumsum(onehot, axis=1)
    blk_end = jnp.cumsum(within[:, -1, :], axis=0)
    blk_start = blk_end - within[:, -1, :]
    counts = blk_end[-1]
    tiles = (counts + tr - 1) // tr
    tile_end = jnp.cumsum(tiles)
    tile_group = jnp.minimum(jnp.sum(jnp.arange(nt, dtype=i32)[:, None] >= tile_end[None, :], axis=1),
                             N_GROUPS - 1).astype(i32)
    row = jnp.arange(nt * tr, dtype=i32)
    gsel = (jnp.repeat(tile_group, tr)[:, None] == gids[None, :]).astype(i32)
    pick = lambda per_group: jnp.sum(gsel * per_group[None, :], axis=1)
    k = row - pick(tile_end - tiles) * tr
    valid = k < pick(counts)
    per_block = lambda table: jnp.sum(gsel[:, None, :] * table[None, :, :], axis=2)
    b_row = jnp.minimum(jnp.sum(per_block(blk_end) <= k[:, None], axis=1), nb - 1)
    bsel = (b_row[:, None] == jnp.arange(nb, dtype=i32)[None, :]).astype(i32)
    k_local = k - jnp.sum(bsel * per_block(blk_start), axis=1)
    sel = (bsel[:, :, None] * gsel[:, None, :]).reshape(nt * tr, nb * N_GROUPS).astype(F32)
    table = jnp.transpose(within, (0, 2, 1)).reshape(nb * N_GROUPS, blk).astype(F32)
    ranks = jnp.dot(sel, table)
    t_local = jnp.sum(ranks <= k_local[:, None].astype(F32), axis=1).astype(i32)
    row_src = jnp.where(valid, b_row * blk + t_local, 0)
    spare = t + ((row // tr) % 2) * tr + row % tr
    return tile_group, row_src, jnp.where(valid, row_src, spare)


def _moe_body(tg_ref, src_ref, dst_ref, rows_hbm, wg_ref, wu_ref, wd_ref, out_hbm, xbuf, ybuf, gsem, ssem):
    tr = TR_MOE
    i = pl.program_id(0)
    nt = pl.num_programs(0)
    slot = i % 2

    def gather_row(tile, slot_, r):
        tok = src_ref[tile * tr + r]
        return pltpu.make_async_copy(rows_hbm.at[pl.ds(pl.multiple_of(tok * ROW_SUBLANES, ROW_SUBLANES), ROW_SUBLANES)],
                                     xbuf.at[slot_, pl.ds(r * ROW_SUBLANES, ROW_SUBLANES)], gsem.at[slot_])

    def scatter_row(tile, slot_, r):
        tok = dst_ref[tile * tr + r]
        return pltpu.make_async_copy(ybuf.at[slot_, pl.ds(r * OUT_SUBLANES, OUT_SUBLANES)],
                                     out_hbm.at[pl.ds(pl.multiple_of(tok * OUT_SUBLANES, OUT_SUBLANES), OUT_SUBLANES)],
                                     ssem.at[slot_])

    def for_rows(fn):
        def body(r, carry):
            fn(r)
            return carry
        lax.fori_loop(0, tr, body, 0, unroll=8)

    def gathered(slot_):
        return pltpu.make_async_copy(rows_hbm.at[pl.ds(0, tr * ROW_SUBLANES)], xbuf.at[slot_], gsem.at[slot_])

    def scattered(slot_):
        return pltpu.make_async_copy(ybuf.at[slot_], out_hbm.at[pl.ds(0, tr * OUT_SUBLANES)], ssem.at[slot_])

    @pl.when(i == 0)
    def _():
        for_rows(lambda r: gather_row(0, 0, r).start())
        n_tok = out_hbm.shape[0] // OUT_SUBLANES - 2 * tr
        ybuf[...] = jnp.zeros_like(ybuf)
        for s in range(2):
            spare = pltpu.make_async_copy(
                ybuf.at[s], out_hbm.at[pl.ds((n_tok + s * tr) * OUT_SUBLANES, tr * OUT_SUBLANES)], ssem.at[s])
            spare.start()
            spare.wait()

    @pl.when(i + 1 < nt)
    def _():
        for_rows(lambda r: gather_row(i + 1, 1 - slot, r).start())

    gathered(slot).wait()

    @pl.when(i >= 2)
    def _():
        scattered(slot).wait()

    x = jnp.concatenate([xbuf[slot, pl.ds(s, tr, stride=ROW_SUBLANES), :] for s in range(D_MODEL // 128)],
                        axis=1).astype(BF16)
    comb = xbuf[slot, pl.ds(COMB_SUBLANE, tr, stride=ROW_SUBLANES), :]
    lane = lax.broadcasted_iota(jnp.int32, comb.shape, 1)
    lane0 = EXPERT_LANE0 + tg_ref[i] * EXPERTS_PER_GROUP
    acc = jnp.zeros((tr, D_MODEL), F32)
    for e in range(EXPERTS_PER_GROUP):
        col = jnp.sum(jnp.where(lane == lane0 + e, comb, 0.0), axis=-1, keepdims=True)
        hid = (_silu(_dot(x, wg_ref[e])) * _dot(x, wu_ref[e]) * col).astype(BF16)
        acc = acc + _dot(hid, wd_ref[e])

    for s in range(D_MODEL // 128):
        ybuf[slot, pl.ds(s, tr, stride=OUT_SUBLANES), :] = acc[:, s * 128:(s + 1) * 128]

    for_rows(lambda r: scatter_row(i, slot, r).start())

    @pl.when(i == nt - 1)
    def _():
        scattered(slot).wait()
        scattered(1 - slot).wait()


def _moe(rows, plan, wg, wu, wd, t):
    tr = TR_MOE
    tile_group, row_src, row_dst = plan
    nt = tile_group.shape[0]
    group_w = lambda shape: pl.BlockSpec((EXPERTS_PER_GROUP,) + shape, lambda i, tg, src, dst: (tg[i], 0, 0))
    grid_spec = pltpu.PrefetchScalarGridSpec(
        num_scalar_prefetch=3,
        grid=(nt,),
        in_specs=[pl.BlockSpec(memory_space=pl.ANY),
                  group_w((D_MODEL, D_EXPERT)), group_w((D_MODEL, D_EXPERT)), group_w((D_EXPERT, D_MODEL))],
        out_specs=pl.BlockSpec(memory_space=pl.ANY),
        scratch_shapes=[pltpu.VMEM((2, tr * ROW_SUBLANES, 128), F32), pltpu.VMEM((2, tr * OUT_SUBLANES, 128), F32),
                        pltpu.SemaphoreType.DMA((2,)), pltpu.SemaphoreType.DMA((2,))])
    return pl.pallas_call(
        _moe_body,
        grid_spec=grid_spec,
        out_shape=jax.ShapeDtypeStruct(((t + 2 * tr) * OUT_SUBLANES, 128), F32),
        compiler_params=pltpu.CompilerParams(dimension_semantics=("arbitrary",), vmem_limit_bytes=VMEM_LIMIT),
        name="moe",
    )(tile_group, row_src, row_dst, rows, wg, wu, wd)


def _final_body(x1_ref, moe_ref, mod_ref, fn_ref, y_ref):
    tm = x1_ref.shape[0]
    moe = jnp.concatenate([moe_ref[pl.ds(s, tm, stride=OUT_SUBLANES), :] for s in range(D_MODEL // 128)], axis=1)
    y_ref[...] = _rmsnorm(x1_ref[...] + mod_ref[5:6, :] * moe, fn_ref[...])


def _final(x1, moe, mod, tiles_per_row, fn):
    t = x1.shape[0]
    tm = TM_FINAL
    return pl.pallas_call(
        _final_body,
        grid=(t // tm,),
        in_specs=[pl.BlockSpec((tm, D_MODEL), lambda i: (i, 0)),
                  pl.BlockSpec((tm * OUT_SUBLANES, 128), lambda i: (i, 0)),
                  pl.BlockSpec((None, 6, D_MODEL), lambda i: (i // tiles_per_row, 0, 0)),
                  pl.BlockSpec((1, D_MODEL), lambda i: (0, 0))],
        out_specs=pl.BlockSpec((tm, D_MODEL), lambda i: (i, 0)),
        out_shape=jax.ShapeDtypeStruct((t, D_MODEL), F32),
        compiler_params=pltpu.CompilerParams(vmem_limit_bytes=VMEM_LIMIT),
        name="final",
    )(x1, moe, mod, fn)


def _trunk_layer(x, mod, s0_hg, s0_gla, emit_state, w):
    b, l, _ = x.shape
    xt = x.reshape(b * l, D_MODEL)
    per_row = lambda tm: (l // tm) if mod.shape[0] > 1 else (b * l // tm)
    (hq, gf, gb, hv, hgate, gq, gk, gv, gr, laf, lab, sa, sb) = _inproj(
        xt, mod, per_row(TM_PROJ), w["n1"], w["lbp"], w["wmain"], w["waup"], w["ba"])
    seq = lambda a: a.reshape(b, l, a.shape[-1])
    hg = _scan(seq(hq), None, seq(hv), seq(gf), seq(gb), s0_hg, w["scan_consts"], HG_H, HG_DK, HG_DV,
               emit_state)
    gla = _scan(seq(gq), seq(gk), seq(gv), seq(laf), seq(lab), s0_gla, w["scan_consts"], GLA_H, GLA_DK,
                GLA_DV, emit_state)
    flat = lambda a: a.reshape(b * l, a.shape[-1])
    x1, rows, comb = _post(flat(hg[0]), flat(hg[1]), flat(gla[0]), flat(gla[1]), hgate, gr, sa, sb, xt, mod,
                           per_row(TM_POST), w["hgn"], w["glan"], w["n2"], w["wa"], w["wb"], w["wo"], w["wr"])
    plan = _route_plan(comb[:, 0].astype(jnp.int32), b * l)
    moe = _moe(rows, plan, w["wg"], w["wu"], w["wd"], b * l)
    y = _final(x1, moe, mod, per_row(TM_FINAL), w["fn"])
    states = (hg[2], gla[2]) if emit_state else (None, None)
    return y.reshape(b, l, D_MODEL), states


def kernel(x_prompt, x_sample, state_hgrn, state_gla, c, c_ctx, ada_w, ada_b, norm1_g, norm2_g, w_in,
           hg_lb_param, hg_norm_g, gla_wa_up, gla_ba, gla_norm_g, w_br_a, w_br_b, w_out,
           w_router_group, w_router_expert, w_exp_gate, w_exp_up, w_exp_down, final_norm_g):
    nb = c.shape[0]
    cond = jnp.concatenate([c, c_ctx[None, :], jnp.zeros((8 - nb - 1, D_MODEL), F32)], axis=0)
    mod = _modulation(cond, ada_w[0], ada_b).reshape(8, 6, D_MODEL)

    zeros = jnp.zeros((GLA_RANK, GLA_H * GLA_DK), F32)
    router = jnp.concatenate(
        [w_router_group[0], jnp.transpose(w_router_expert[0], (1, 0, 2)).reshape(D_MODEL, N_EXPERTS),
         jnp.zeros((D_MODEL, ROUTER_LANES - N_GROUPS - N_EXPERTS), F32)], axis=1)
    w = {
        "n1": norm1_g, "n2": norm2_g, "fn": final_norm_g[None, :], "lbp": hg_lb_param,
        "wmain": w_in[0].astype(BF16),
        "waup": jnp.concatenate([jnp.concatenate([gla_wa_up[0, 0], zeros], axis=1),
                                 jnp.concatenate([zeros, gla_wa_up[0, 1]], axis=1)], axis=0).astype(BF16),
        "ba": gla_ba[0].reshape(1, 2 * GLA_H * GLA_DK),
        "hgn": hg_norm_g, "glan": gla_norm_g,
        "wa": w_br_a[0].astype(BF16), "wb": w_br_b[0].astype(BF16), "wo": w_out[0].astype(BF16),
        "wr": jnp.stack([router.astype(BF16), (router - router.astype(BF16).astype(F32)).astype(BF16)]),
        "wg": w_exp_gate[0].astype(BF16), "wu": w_exp_up[0].astype(BF16), "wd": w_exp_down[0].astype(BF16),
        "scan_consts": _scan_constants(),
    }
    y_prompt, (st_hg, st_gla) = _trunk_layer(x_prompt, mod[nb:nb + 1], None, None, True, w)
    y_sample, _ = _trunk_layer(x_sample, mod[:nb], state_hgrn[:, 0], state_gla[:, 0], False, w)
    return y_prompt, y_sample, st_hg[:, None], st_gla[:, None]
```

```python
import functools

import numpy as np
import jax
import jax.numpy as jnp
from jax import lax
from jax.experimental import pallas as pl
from jax.experimental.pallas import tpu as pltpu

F32 = jnp.float32
BF16 = jnp.bfloat16

D_MODEL = 1024
CHUNK = 64
N_LEVELS = 6
CHUNKS_PER_STEP = 4
HG_H, HG_DK, HG_DV = 8, 128, 128
GLA_H, GLA_DK, GLA_DV = 4, 128, 256
GLA_RANK = 16
GLA_GATE_NORM = 16.0
N_GROUPS = 4
EXPERTS_PER_GROUP = 8
N_EXPERTS = N_GROUPS * EXPERTS_PER_GROUP
D_EXPERT = D_MODEL // 4
EPS = 1e-6
ROUTER_LANES = 128
EXPERT_LANE0 = N_GROUPS
VMEM_LIMIT = 56 * 1024 * 1024

TM_PROJ = 256
TM_POST = 256
POST_PART = 256
TM_FINAL = 512
TR_MOE = 256
PLAN_BLOCK = 256
OUT_SUBLANES = D_MODEL // 128
COMB_SUBLANE = OUT_SUBLANES
ROW_SUBLANES = 16


def _sigmoid(x):
    return 1.0 / (1.0 + jnp.exp(-x))


def _silu(x):
    return x * _sigmoid(x)


def _log_sigmoid(x):
    return jnp.minimum(x, 0.0) - jnp.log(1.0 + jnp.exp(-jnp.abs(x)))


def _rmsnorm(x, g):
    return x * lax.rsqrt(jnp.mean(x * x, axis=-1, keepdims=True) + EPS) * g


def _dot(a, b):
    return jnp.dot(a, b, preferred_element_type=F32)


def _dot_nt(a, b):
    return lax.dot_general(a, b, (((1,), (1,)), ((), ())), preferred_element_type=F32)


def _dot_tn(a, b):
    return lax.dot_general(a, b, (((0,), (0,)), ((), ())), preferred_element_type=F32)


def _mod_body(c_ref, w_ref, b_ref, o_ref):
    s, w = _silu(c_ref[...]), w_ref[...]
    s_hi, w_hi = s.astype(BF16), w.astype(BF16)
    s_lo, w_lo = (s - s_hi.astype(F32)).astype(BF16), (w - w_hi.astype(F32)).astype(BF16)
    o_ref[...] = _dot(s_hi, w_hi) + _dot(s_hi, w_lo) + _dot(s_lo, w_hi) + b_ref[...]


def _modulation(cond, ada_w, ada_b):
    rows, n = cond.shape[0], ada_w.shape[1]
    tn = n // 4
    return pl.pallas_call(
        _mod_body,
        grid=(n // tn,),
        in_specs=[pl.BlockSpec((rows, D_MODEL), lambda j: (0, 0)),
                  pl.BlockSpec((D_MODEL, tn), lambda j: (0, j)),
                  pl.BlockSpec((1, tn), lambda j: (0, j))],
        out_specs=pl.BlockSpec((rows, tn), lambda j: (0, j)),
        out_shape=jax.ShapeDtypeStruct((rows, n), F32),
        compiler_params=pltpu.CompilerParams(vmem_limit_bytes=VMEM_LIMIT),
        name="modulation",
    )(cond, ada_w, ada_b)


_C_HQ, _C_HFF, _C_HFB, _C_HI, _C_HGATE = 0, 1024, 2048, 3072, 4096
_C_GQ, _C_GK, _C_GV, _C_GR, _C_TAIL = 5120, 5632, 6144, 7168, 8192
GA_WIDTH = 2 * GLA_RANK
W_TAIL = GA_WIDTH + 2 * D_MODEL
W_MAIN = _C_TAIL + W_TAIL


def _inproj_body(x_ref, mod_ref, n1_ref, lbp_ref, wmain_ref, waup_ref, ba_ref,
                 hq_ref, gf_ref, gb_ref, hv_ref, hgate_ref, gq_ref, gk_ref, gv_ref, gr_ref,
                 laf_ref, lab_ref, sa_ref, sb_ref):
    x = x_ref[...]
    h = _rmsnorm(x, n1_ref[...]) * (1.0 + mod_ref[1:2, :]) + mod_ref[0:1, :]
    hb = h.astype(BF16)

    def proj(c0, w):
        return _dot(hb, wmain_ref[:, c0:c0 + w])

    p0, p1 = lbp_ref[0:1, :], lbp_ref[1:2, :]
    pm = jnp.maximum(p0, p1)
    e0, e1 = jnp.exp(p0 - pm), jnp.exp(p1 - pm)
    lb = e0 / (e0 + e1)

    hq_ref[...] = (proj(_C_HQ, 1024) * HG_DK ** -0.5).astype(BF16)
    gf_ref[...] = jnp.log(lb + (1.0 - lb) * _sigmoid(proj(_C_HFF, 1024)))
    gb_ref[...] = jnp.log(lb + (1.0 - lb) * _sigmoid(proj(_C_HFB, 1024)))
    hv_ref[...] = proj(_C_HI, 1024).astype(BF16)
    hgate_ref[...] = _silu(proj(_C_HGATE, 1024)).astype(BF16)
    gq_ref[...] = (proj(_C_GQ, 512) * GLA_DK ** -0.5).astype(BF16)
    gk_ref[...] = proj(_C_GK, 512).astype(BF16)
    gv_ref[...] = proj(_C_GV, 1024).astype(BF16)
    gr_ref[...] = _silu(proj(_C_GR, 1024)).astype(BF16)
    tail = proj(_C_TAIL, W_TAIL)
    sa_ref[...] = _sigmoid(tail[:, GA_WIDTH:GA_WIDTH + D_MODEL]).astype(BF16)
    sb_ref[...] = _sigmoid(tail[:, GA_WIDTH + D_MODEL:]).astype(BF16)

    ga = tail[:, :GA_WIDTH].astype(BF16)
    la = _log_sigmoid(_dot(ga, waup_ref[...]) + ba_ref[...]) * (1.0 / GLA_GATE_NORM)
    laf_ref[...] = la[:, :512]
    lab_ref[...] = la[:, 512:]


def _inproj(x, mod, tiles_per_row, n1, lbp, wmain, waup, ba):
    t = x.shape[0]
    tm = TM_PROJ
    const = lambda shape: pl.BlockSpec(shape, lambda i: (0,) * len(shape))
    tok = lambda w: pl.BlockSpec((tm, w), lambda i: (i, 0))
    widths = (1024, 1024, 1024, 1024, 1024, 512, 512, 1024, 1024, 512, 512, 1024, 1024)
    dtypes = (BF16, F32, F32, BF16, BF16, BF16, BF16, BF16, BF16, F32, F32, BF16, BF16)
    return pl.pallas_call(
        _inproj_body,
        grid=(t // tm,),
        in_specs=[tok(D_MODEL),
                  pl.BlockSpec((None, 6, D_MODEL), lambda i: (i // tiles_per_row, 0, 0)),
                  const((1, D_MODEL)), const((2, D_MODEL)),
                  pl.BlockSpec((D_MODEL, W_MAIN), lambda i: (0, 0), pipeline_mode=pl.Buffered(1)),
                  const((GA_WIDTH, 1024)), const((1, 1024))],
        out_specs=[tok(w) for w in widths],
        out_shape=[jax.ShapeDtypeStruct((t, w), dt) for w, dt in zip(widths, dtypes)],
        compiler_params=pltpu.CompilerParams(vmem_limit_bytes=VMEM_LIMIT),
        name="inproj",
    )(x, mod, n1, lbp, wmain, waup, ba)


N_MXU_LEVELS = 3
ROW_GROUP = 16
START_SLOT, END_SLOT = N_LEVELS, N_LEVELS + 1
LOG2E = 1.4426950408889634


def _boundary_row(t, h, d):
    base = (t // (2 * h)) * 2 * h
    return base + h - 1 if d == 0 else base + h


def _scan_constants():
    c = CHUNK
    tri = np.tril(np.ones((c, c), np.float32))
    m = [tri] + [np.abs(tri - tri[[_boundary_row(t, 1 << l, 0) for t in range(c)]]) for l in range(N_MXU_LEVELS)]
    m = np.stack(m)
    lvl = np.full((c, c), N_LEVELS + 1, np.int32)
    for t in range(c):
        lvl[t, t] = N_LEVELS
        for s in range(t):
            lvl[t, s] = int(np.floor(np.log2(t ^ s)))
    m_fwd = m.reshape(-1, c)
    m_bwd = m[:, ::-1, ::-1].reshape(-1, c)
    m_both = np.stack([m_fwd, m_bwd])
    return (jnp.asarray(np.concatenate([m_both, m_both], axis=2), BF16), jnp.asarray(np.stack([lvl, lvl.T])))


def _scan_prepare(u, d, rows, q_ref, k_ref, g_ref, m_ref, side, cb_ref, kf_ref, kb_ref, p_ref):
    c = CHUNK
    w = g_ref.shape[-1]
    g = g_ref[rows, :]
    gs = g * LOG2E
    g1 = gs.astype(BF16)
    g2 = (gs - g1.astype(F32)).astype(BF16)
    cb_ref[u] = _dot(m_ref[d], jnp.concatenate([g1, g2], axis=0))
    k32 = (1.0 - jnp.exp(g)) if k_ref is None else k_ref[rows, :].astype(F32)
    kf_ref[u] = k32
    kb_ref[u] = k32.astype(BF16)
    bcast = lambda r, n: jnp.broadcast_to(cb_ref[u, r:r + 1, :], (n, w))
    for i in range(c // ROW_GROUP):
        t0 = ROW_GROUP * i
        grp = slice(t0, t0 + ROW_GROUP)
        cum = cb_ref[u, grp, :]
        q = q_ref[rows.start + t0:rows.start + t0 + ROW_GROUP, :]
        k = kb_ref[u, grp, :]
        for l in range(N_LEVELS):
            h = 1 << l
            if l < N_MXU_LEVELS:
                ex = cb_ref[u, c * (l + 1) + t0:c * (l + 1) + t0 + ROW_GROUP, :]
            else:
                ex = []
                for t in (t0, t0 + 8):
                    half = slice(t - t0, t - t0 + 8)
                    bnd = bcast(_boundary_row(t, h, d), 8)
                    ex.append(cum[half] - bnd if ((t & h) != 0) == (d == 0) else bnd - cum[half])
                ex = jnp.concatenate(ex, axis=0)
            if h >= ROW_GROUP:
                src = q if ((t0 & h) != 0) == (d == 0) else k
            else:
                upper = side[l] > 0
                src = jnp.where(upper, q, k) if d == 0 else jnp.where(upper, k, q)
            p_ref[u, l, grp, :] = src * jnp.exp2(ex).astype(BF16)
        p_ref[u, START_SLOT, grp, :] = q * jnp.exp2(cum).astype(BF16)
        edge = c - 1 if d == 0 else 0
        p_ref[u, END_SLOT, grp, :] = (kf_ref[u, grp, :] * jnp.exp2(bcast(edge, ROW_GROUP) - cum)).astype(BF16)


def _scan_scores(u, d, rows, q_ref, masks, kb_ref, p_ref, a_ref, n_heads, dk):
    groups = range(CHUNK // ROW_GROUP)
    piece = lambda x, g: x[g * ROW_GROUP:(g + 1) * ROW_GROUP]
    mask_pieces = [[piece(m, g) for g in groups] for m in masks]
    for h in range(n_heads):
        ks = slice(h * dk, (h + 1) * dk)
        diag = _dot_nt(q_ref[rows, ks], kb_ref[u, :, ks]).astype(BF16)
        a = [jnp.where(mask_pieces[N_LEVELS][g], piece(diag, g), jnp.zeros((), BF16)) for g in groups]
        for l in range(N_LEVELS):
            p = p_ref[u, l, :, ks]
            if (1 << l) < ROW_GROUP:
                query = list(groups)
                x = _dot_nt(p, p).astype(BF16)
            else:
                query = [g for g in groups if ((g * ROW_GROUP & (1 << l)) != 0) == (d == 0)]
                x = _dot_nt(jnp.concatenate([piece(p, g) for g in query], axis=0), p).astype(BF16)
            for j, g in enumerate(query):
                a[g] = jnp.where(mask_pieces[l][g], piece(x, j), a[g])
        a_ref[u, h] = jnp.concatenate(a, axis=0)


def _scan_outputs(u, d, rows, v_ref, cb_ref, p_ref, a_ref, st_ref, o_ref, n_heads, dk, dv):
    edge = CHUNK - 1 if d == 0 else 0
    for h in range(n_heads):
        ks = slice(h * dk, (h + 1) * dk)
        vs = slice(h * dv, (h + 1) * dv)
        v = v_ref[rows, vs]
        st = st_ref[d, h]
        o = _dot(a_ref[u, h], v) + _dot_nt(p_ref[u, START_SLOT, :, ks], st.astype(BF16))
        o_ref[rows, vs] = o.astype(o_ref.dtype)
        st_ref[d, h] = st * jnp.exp2(cb_ref[u, edge:edge + 1, ks]) + _dot_tn(v, p_ref[u, END_SLOT, :, ks])


def _scan_body(*refs, n_heads, dk, dv, derive_k, zero_init, emit_state):
    refs = list(refs)
    q_refs = (refs.pop(0), refs.pop(0))
    k_refs = (None, None) if derive_k else (refs.pop(0), refs.pop(0))
    v_refs = (refs.pop(0), refs.pop(0))
    g_refs = (refs.pop(0), refs.pop(0))
    m_ref, lvl_ref = refs.pop(0), refs.pop(0)
    s0_ref = None if zero_init else refs.pop(0)
    o_refs = (refs.pop(0), refs.pop(0))
    sout_ref = refs.pop(0) if emit_state else None
    st_ref, cb_ref, kf_ref, kb_ref, p_ref, a_ref = refs
    n = pl.program_id(1)

    @pl.when(n == 0)
    def _():
        for d in range(2):
            for h in range(n_heads):
                st_ref[d, h] = jnp.zeros((dv, dk), F32) if zero_init else s0_ref[d, h].T

    cps = CHUNKS_PER_STEP
    rows = lambda c: slice(c * CHUNK, (c + 1) * CHUNK)
    masks = [[lvl_ref[d].astype(F32).astype(BF16) == l for l in range(N_LEVELS + 1)] for d in range(2)]
    row = lax.broadcasted_iota(jnp.int32, (ROW_GROUP, q_refs[0].shape[-1]), 0)
    side = [((row >> l) & 1).astype(F32).astype(BF16) for l in range(N_LEVELS) if (1 << l) < ROW_GROUP]
    units = [(d, c) for c in range(cps) for d in range(2)]
    for i in range(len(units) + 1):
        if i < len(units):
            d, c = units[i]
            _scan_prepare(d * cps + c, d, rows(c), q_refs[d], k_refs[d], g_refs[d], m_ref, side, cb_ref, kf_ref,
                          kb_ref, p_ref)
        if i >= 1:
            d, c = units[i - 1]
            _scan_scores(d * cps + c, d, rows(c), q_refs[d], masks[d], kb_ref, p_ref, a_ref, n_heads, dk)
    for d in range(2):
        for c in (range(cps) if d == 0 else reversed(range(cps))):
            _scan_outputs(d * cps + c, d, rows(c), v_refs[d], cb_ref, p_ref, a_ref, st_ref, o_refs[d], n_heads,
                          dk, dv)

    if emit_state:
        @pl.when(n == pl.num_programs(1) - 1)
        def _():
            for d in range(2):
                for h in range(n_heads):
                    sout_ref[d, h] = st_ref[d, h].T


def _scan(q, k, v, g_f, g_b, s0, consts, n_heads, dk, dv, emit_state):
    b, l, _ = q.shape
    step = CHUNKS_PER_STEP * CHUNK
    n = l // step
    slots = 2 * CHUNKS_PER_STEP
    m_all, lvl = consts
    fwd = lambda w: pl.BlockSpec((None, step, w), lambda i, j: (i, j, 0))
    bwd = lambda w: pl.BlockSpec((None, step, w), lambda i, j: (i, n - 1 - j, 0))
    const = lambda shape: pl.BlockSpec(shape, lambda i, j: (0,) * len(shape))
    state = pl.BlockSpec((None, 2, n_heads, dk, dv), lambda i, j: (i, 0, 0, 0, 0))
    wk, wv = n_heads * dk, n_heads * dv
    args, in_specs = [q, q], [fwd(wk), bwd(wk)]
    if k is not None:
        args += [k, k]
        in_specs += [fwd(wk), bwd(wk)]
    args += [v, v, g_f, g_b, m_all, lvl]
    in_specs += [fwd(wv), bwd(wv), fwd(wk), bwd(wk), const(m_all.shape), const(lvl.shape)]
    if s0 is not None:
        args.append(s0)
        in_specs.append(state)
    out_specs = [fwd(wv), bwd(wv)]
    out_shape = [jax.ShapeDtypeStruct((b, l, wv), BF16)] * 2
    if emit_state:
        out_specs.append(state)
        out_shape.append(jax.ShapeDtypeStruct((b, 2, n_heads, dk, dv), F32))
    body = functools.partial(_scan_body, n_heads=n_heads, dk=dk, dv=dv, derive_k=k is None,
                             zero_init=s0 is None, emit_state=emit_state)
    return pl.pallas_call(
        body,
        grid=(b, n),
        in_specs=in_specs,
        out_specs=out_specs,
        out_shape=out_shape,
        scratch_shapes=[pltpu.VMEM((2, n_heads, dv, dk), F32),
                        pltpu.VMEM((slots, (1 + N_MXU_LEVELS) * CHUNK, wk), F32),
                        pltpu.VMEM((slots, CHUNK, wk), F32), pltpu.VMEM((slots, CHUNK, wk), BF16),
                        pltpu.VMEM((slots, N_LEVELS + 2, CHUNK, wk), BF16),
                        pltpu.VMEM((slots, n_heads, CHUNK, CHUNK), BF16)],
        compiler_params=pltpu.CompilerParams(dimension_semantics=("arbitrary", "arbitrary"),
                                             vmem_limit_bytes=VMEM_LIMIT),
        name="scan_hgrn" if k is None else "scan_gla",
    )(*args)


def _post_body(ohf_ref, ohb_ref, ogf_ref, ogb_ref, hgate_ref, gr_ref, sa_ref, sb_ref, x_ref, mod_ref,
               hgn_ref, glan_ref, n2_ref, wa_ref, wb_ref, wo_ref, wr_ref,
               x1_ref, rows_ref, comb_ref, yh_ref, yg_ref):
    tm = x_ref.shape[0]
    for r0 in range(0, tm, POST_PART):
        _post_part(slice(r0, r0 + POST_PART), ohf_ref, ohb_ref, ogf_ref, ogb_ref, hgate_ref, gr_ref, sa_ref, sb_ref,
                   x_ref, mod_ref, hgn_ref, glan_ref, n2_ref, wa_ref, wb_ref, wo_ref, wr_ref,
                   x1_ref, rows_ref, comb_ref, yh_ref, yg_ref)


def _post_part(rs, ohf_ref, ohb_ref, ogf_ref, ogb_ref, hgate_ref, gr_ref, sa_ref, sb_ref, x_ref, mod_ref,
               hgn_ref, glan_ref, n2_ref, wa_ref, wb_ref, wo_ref, wr_ref,
               x1_ref, rows_ref, comb_ref, yh_ref, yg_ref):
    n = rs.stop - rs.start
    for h in range(HG_H):
        s = slice(h * HG_DV, (h + 1) * HG_DV)
        o = ohf_ref[rs, s].astype(F32) + ohb_ref[rs, s].astype(F32)
        yh_ref[rs, s] = (_rmsnorm(o, hgn_ref[...]) * hgate_ref[rs, s].astype(F32)).astype(BF16)
    for h in range(GLA_H):
        s = slice(h * GLA_DV, (h + 1) * GLA_DV)
        o = ogf_ref[rs, s].astype(F32) + ogb_ref[rs, s].astype(F32)
        yg_ref[rs, s] = (_rmsnorm(o, glan_ref[...]) * gr_ref[rs, s].astype(F32)).astype(BF16)
    merged = (sa_ref[rs, :].astype(F32) * _dot(yh_ref[rs, :], wa_ref[...])
              + sb_ref[rs, :].astype(F32) * _dot(yg_ref[rs, :], wb_ref[...]))
    mix = _dot(merged.astype(BF16), wo_ref[...])
    x1 = x_ref[rs, :] + mod_ref[2:3, :] * mix
    x1_ref[rs, :] = x1
    h2 = _rmsnorm(x1, n2_ref[...]) * (1.0 + mod_ref[4:5, :]) + mod_ref[3:4, :]
    h2_hi = h2.astype(BF16)
    token_rows = lambda s: pl.ds(rs.start * ROW_SUBLANES + s, n, stride=ROW_SUBLANES)
    for s in range(D_MODEL // 128):
        rows_ref[token_rows(s), :] = h2[:, s * 128:(s + 1) * 128]
    for s in range(COMB_SUBLANE + 1, ROW_SUBLANES):
        rows_ref[token_rows(s), :] = jnp.zeros((n, 128), F32)

    h2_lo = (h2 - h2_hi.astype(F32)).astype(BF16)
    logits = _dot(h2_hi, wr_ref[0]) + _dot(h2_hi, wr_ref[1]) + _dot(h2_lo, wr_ref[0])
    lane = lax.broadcasted_iota(jnp.int32, logits.shape, 1)
    neg = -jnp.inf
    first = lambda hit: jnp.min(jnp.where(hit, lane, ROUTER_LANES), axis=-1, keepdims=True)
    gl = jnp.where(lane < N_GROUPS, logits, neg)
    gmax = jnp.max(gl, axis=-1, keepdims=True)
    p_top = 1.0 / jnp.sum(jnp.exp(gl - gmax), axis=-1, keepdims=True)
    g_idx = first(gl == gmax)
    in_group = ((lane >= EXPERT_LANE0) & (lane < EXPERT_LANE0 + N_EXPERTS)
                & (((lane - EXPERT_LANE0) >> 3) == g_idx))
    sl = jnp.where(in_group, logits, neg)
    m1 = jnp.max(sl, axis=-1, keepdims=True)
    i1 = first(sl == m1)
    sl2 = jnp.where(lane == i1, neg, sl)
    m2 = jnp.max(sl2, axis=-1, keepdims=True)
    i2 = first(sl2 == m2)
    r = jnp.exp(m2 - m1)
    w1 = p_top / (1.0 + r)
    comb = jnp.where(lane == i1, w1, jnp.where(lane == i2, w1 * r, 0.0))
    rows_ref[token_rows(COMB_SUBLANE), :] = comb
    comb_ref[rs, :] = jnp.where(lane == 0, g_idx.astype(F32), comb)


def _post(ohf, ohb, ogf, ogb, hgate, gr, sa, sb, x, mod, tiles_per_row, hgn, glan, n2, wa, wb, wo, wr):
    t = x.shape[0]
    tm = TM_POST
    const = lambda shape: pl.BlockSpec(shape, lambda i: (0,) * len(shape))
    tok = lambda w: pl.BlockSpec((tm, w), lambda i: (i, 0))
    return pl.pallas_call(
        _post_body,
        grid=(t // tm,),
        in_specs=[tok(1024)] * 9
                 + [pl.BlockSpec((None, 6, D_MODEL), lambda i: (i // tiles_per_row, 0, 0)),
                    const((1, HG_DV)), const((1, GLA_DV)), const((1, D_MODEL)),
                    const((1024, 1024)), const((1024, 1024)), const((1024, 1024)),
                    const((2, D_MODEL, ROUTER_LANES))],
        out_specs=[tok(1024), pl.BlockSpec((tm * ROW_SUBLANES, 128), lambda i: (i, 0)), tok(ROUTER_LANES)],
        out_shape=[jax.ShapeDtypeStruct((t, D_MODEL), F32), jax.ShapeDtypeStruct((t * ROW_SUBLANES, 128), F32),
                   jax.ShapeDtypeStruct((t, ROUTER_LANES), F32)],
        scratch_shapes=[pltpu.VMEM((tm, 1024), BF16), pltpu.VMEM((tm, 1024), BF16)],
        compiler_params=pltpu.CompilerParams(vmem_limit_bytes=VMEM_LIMIT),
        name="post",
    )(ohf, ohb, ogf, ogb, hgate, gr, sa, sb, x, mod, hgn, glan, n2, wa, wb, wo, wr)


def _route_plan(group, t):
    tr, blk = TR_MOE, PLAN_BLOCK
    nt, nb = t // tr + N_GROUPS, t // blk
    i32 = jnp.int32
    gids = jnp.arange(N_GROUPS, dtype=i32)
    onehot = (group[:, None] == gids[None, :]).astype(i32).reshape(nb, blk, N_GROUPS)
    within = jnp.cumsum(onehot, axis=1)
    blk_end = jnp.cumsum(within[:, -1, :], axis=0)
    blk_start = blk_end - within[:, -1, :]
    counts = blk_end[-1]
    tiles = (counts + tr - 1) // tr
    tile_end = jnp.cumsum(tiles)
    tile_group = jnp.minimum(jnp.sum(jnp.arange(nt, dtype=i32)[:, None] >= tile_end[None, :], axis=1),
                             N_GROUPS - 1).astype(i32)
    row = jnp.arange(nt * tr, dtype=i32)
    gsel = (jnp.repeat(tile_group, tr)[:, None] == gids[None, :]).astype(i32)
    pick = lambda per_group: jnp.sum(gsel * per_group[None, :], axis=1)
    k = row - pick(tile_end - tiles) * tr
    valid = k < pick(counts)
    per_block = lambda table: jnp.sum(gsel[:, None, :] * table[None, :, :], axis=2)
    b_row = jnp.minimum(jnp.sum(per_block(blk_end) <= k[:, None], axis=1), nb - 1)
    bsel = (b_row[:, None] == jnp.arange(nb, dtype=i32)[None, :]).astype(i32)
    k_local = k - jnp.sum(bsel * per_block(blk_start), axis=1)
    sel = (bsel[:, :, None] * gsel[:, None, :]).reshape(nt * tr, nb * N_GROUPS).astype(F32)
    table = jnp.transpose(within, (0, 2, 1)).reshape(nb * N_GROUPS, blk).astype(F32)
    ranks = jnp.dot(sel, table)
    t_local = jnp.sum(ranks <= k_local[:, None].astype(F32), axis=1).astype(i32)
    row_src = jnp.where(valid, b_row * blk + t_local, 0)
    spare = t + ((row // tr) % 2) * tr + row % tr
    return tile_group, row_src, jnp.where(valid, row_src, spare)


def _moe_body(tg_ref, src_ref, dst_ref, rows_hbm, wg_ref, wu_ref, wd_ref, out_hbm, xbuf, ybuf, gsem, ssem):
    tr = TR_MOE
    i = pl.program_id(0)
    nt = pl.num_programs(0)
    slot = i % 2

    def gather_row(tile, slot_, r):
        tok = src_ref[tile * tr + r]
        return pltpu.make_async_copy(rows_hbm.at[pl.ds(pl.multiple_of(tok * ROW_SUBLANES, ROW_SUBLANES), ROW_SUBLANES)],
                                     xbuf.at[slot_, pl.ds(r * ROW_SUBLANES, ROW_SUBLANES)], gsem.at[slot_])

    def scatter_row(tile, slot_, r):
        tok = dst_ref[tile * tr + r]
        return pltpu.make_async_copy(ybuf.at[slot_, pl.ds(r * OUT_SUBLANES, OUT_SUBLANES)],
                                     out_hbm.at[pl.ds(pl.multiple_of(tok * OUT_SUBLANES, OUT_SUBLANES), OUT_SUBLANES)],
                                     ssem.at[slot_])

    def for_rows(fn):
        def body(r, carry):
            fn(r)
            return carry
        lax.fori_loop(0, tr, body, 0, unroll=8)

    def gathered(slot_):
        return pltpu.make_async_copy(rows_hbm.at[pl.ds(0, tr * ROW_SUBLANES)], xbuf.at[slot_], gsem.at[slot_])

    def scattered(slot_):
        return pltpu.make_async_copy(ybuf.at[slot_], out_hbm.at[pl.ds(0, tr * OUT_SUBLANES)], ssem.at[slot_])

    @pl.when(i == 0)
    def _():
        for_rows(lambda r: gather_row(0, 0, r).start())
        n_tok = out_hbm.shape[0] // OUT_SUBLANES - 2 * tr
        ybuf[...] = jnp.zeros_like(ybuf)
        for s in range(2):
            spare = pltpu.make_async_copy(
                ybuf.at[s], out_hbm.at[pl.ds((n_tok + s * tr) * OUT_SUBLANES, tr * OUT_SUBLANES)], ssem.at[s])
            spare.start()
            spare.wait()

    @pl.when(i + 1 < nt)
    def _():
        for_rows(lambda r: gather_row(i + 1, 1 - slot, r).start())

    gathered(slot).wait()

    @pl.when(i >= 2)
    def _():
        scattered(slot).wait()

    x = jnp.concatenate([xbuf[slot, pl.ds(s, tr, stride=ROW_SUBLANES), :] for s in range(D_MODEL // 128)],
                        axis=1).astype(BF16)
    comb = xbuf[slot, pl.ds(COMB_SUBLANE, tr, stride=ROW_SUBLANES), :]
    lane = lax.broadcasted_iota(jnp.int32, comb.shape, 1)
    lane0 = EXPERT_LANE0 + tg_ref[i] * EXPERTS_PER_GROUP
    acc = jnp.zeros((tr, D_MODEL), F32)
    for e in range(EXPERTS_PER_GROUP):
        col = jnp.sum(jnp.where(lane == lane0 + e, comb, 0.0), axis=-1, keepdims=True)
        hid = (_silu(_dot(x, wg_ref[e])) * _dot(x, wu_ref[e]) * col).astype(BF16)
        acc = acc + _dot(hid, wd_ref[e])

    for s in range(D_MODEL // 128):
        ybuf[slot, pl.ds(s, tr, stride=OUT_SUBLANES), :] = acc[:, s * 128:(s + 1) * 128]

    for_rows(lambda r: scatter_row(i, slot, r).start())

    @pl.when(i == nt - 1)
    def _():
        scattered(slot).wait()
        scattered(1 - slot).wait()


def _moe(rows, plan, wg, wu, wd, t):
    tr = TR_MOE
    tile_group, row_src, row_dst = plan
    nt = tile_group.shape[0]
    group_w = lambda shape: pl.BlockSpec((EXPERTS_PER_GROUP,) + shape, lambda i, tg, src, dst: (tg[i], 0, 0))
    grid_spec = pltpu.PrefetchScalarGridSpec(
        num_scalar_prefetch=3,
        grid=(nt,),
        in_specs=[pl.BlockSpec(memory_space=pl.ANY),
                  group_w((D_MODEL, D_EXPERT)), group_w((D_MODEL, D_EXPERT)), group_w((D_EXPERT, D_MODEL))],
        out_specs=pl.BlockSpec(memory_space=pl.ANY),
        scratch_shapes=[pltpu.VMEM((2, tr * ROW_SUBLANES, 128), F32), pltpu.VMEM((2, tr * OUT_SUBLANES, 128), F32),
                        pltpu.SemaphoreType.DMA((2,)), pltpu.SemaphoreType.DMA((2,))])
    return pl.pallas_call(
        _moe_body,
        grid_spec=grid_spec,
        out_shape=jax.ShapeDtypeStruct(((t + 2 * tr) * OUT_SUBLANES, 128), F32),
        compiler_params=pltpu.CompilerParams(dimension_semantics=("arbitrary",), vmem_limit_bytes=VMEM_LIMIT),
        name="moe",
    )(tile_group, row_src, row_dst, rows, wg, wu, wd)


def _final_body(x1_ref, moe_ref, mod_ref, fn_ref, y_ref):
    tm = x1_ref.shape[0]
    moe = jnp.concatenate([moe_ref[pl.ds(s, tm, stride=OUT_SUBLANES), :] for s in range(D_MODEL // 128)], axis=1)
    y_ref[...] = _rmsnorm(x1_ref[...] + mod_ref[5:6, :] * moe, fn_ref[...])


def _final(x1, moe, mod, tiles_per_row, fn):
    t = x1.shape[0]
    tm = TM_FINAL
    return pl.pallas_call(
        _final_body,
        grid=(t // tm,),
        in_specs=[pl.BlockSpec((tm, D_MODEL), lambda i: (i, 0)),
                  pl.BlockSpec((tm * OUT_SUBLANES, 128), lambda i: (i, 0)),
                  pl.BlockSpec((None, 6, D_MODEL), lambda i: (i // tiles_per_row, 0, 0)),
                  pl.BlockSpec((1, D_MODEL), lambda i: (0, 0))],
        out_specs=pl.BlockSpec((tm, D_MODEL), lambda i: (i, 0)),
        out_shape=jax.ShapeDtypeStruct((t, D_MODEL), F32),
        compiler_params=pltpu.CompilerParams(vmem_limit_bytes=VMEM_LIMIT),
        name="final",
    )(x1, moe, mod, fn)


def _trunk_layer(x, mod, s0_hg, s0_gla, emit_state, w):
    b, l, _ = x.shape
    xt = x.reshape(b * l, D_MODEL)
    per_row = lambda tm: (l // tm) if mod.shape[0] > 1 else (b * l // tm)
    (hq, gf, gb, hv, hgate, gq, gk, gv, gr, laf, lab, sa, sb) = _inproj(
        xt, mod, per_row(TM_PROJ), w["n1"], w["lbp"], w["wmain"], w["waup"], w["ba"])
    seq = lambda a: a.reshape(b, l, a.shape[-1])
    hg = _scan(seq(hq), None, seq(hv), seq(gf), seq(gb), s0_hg, w["scan_consts"], HG_H, HG_DK, HG_DV,
               emit_state)
    gla = _scan(seq(gq), seq(gk), seq(gv), seq(laf), seq(lab), s0_gla, w["scan_consts"], GLA_H, GLA_DK,
                GLA_DV, emit_state)
    flat = lambda a: a.reshape(b * l, a.shape[-1])
    x1, rows, comb = _post(flat(hg[0]), flat(hg[1]), flat(gla[0]), flat(gla[1]), hgate, gr, sa, sb, xt, mod,
                           per_row(TM_POST), w["hgn"], w["glan"], w["n2"], w["wa"], w["wb"], w["wo"], w["wr"])
    plan = _route_plan(comb[:, 0].astype(jnp.int32), b * l)
    moe = _moe(rows, plan, w["wg"], w["wu"], w["wd"], b * l)
    y = _final(x1, moe, mod, per_row(TM_FINAL), w["fn"])
    states = (hg[2], gla[2]) if emit_state else (None, None)
    return y.reshape(b, l, D_MODEL), states


def kernel(x_prompt, x_sample, state_hgrn, state_gla, c, c_ctx, ada_w, ada_b, norm1_g, norm2_g, w_in,
           hg_lb_param, hg_norm_g, gla_wa_up, gla_ba, gla_norm_g, w_br_a, w_br_b, w_out,
           w_router_group, w_router_expert, w_exp_gate, w_exp_up, w_exp_down, final_norm_g):
    nb = c.shape[0]
    cond = jnp.concatenate([c, c_ctx[None, :], jnp.zeros((8 - nb - 1, D_MODEL), F32)], axis=0)
    mod = _modulation(cond, ada_w[0], ada_b).reshape(8, 6, D_MODEL)

    zeros = jnp.zeros((GLA_RANK, GLA_H * GLA_DK), F32)
    router = jnp.concatenate(
        [w_router_group[0], jnp.transpose(w_router_expert[0], (1, 0, 2)).reshape(D_MODEL, N_EXPERTS),
         jnp.zeros((D_MODEL, ROUTER_LANES - N_GROUPS - N_EXPERTS), F32)], axis=1)
    w = {
        "n1": norm1_g, "n2": norm2_g, "fn": final_norm_g[None, :], "lbp": hg_lb_param,
        "wmain": w_in[0].astype(BF16),
        "waup": jnp.concatenate([jnp.concatenate([gla_wa_up[0, 0], zeros], axis=1),
                                 jnp.concatenate([zeros, gla_wa_up[0, 1]], axis=1)], axis=0).astype(BF16),
        "ba": gla_ba[0].reshape(1, 2 * GLA_H * GLA_DK),
        "hgn": hg_norm_g, "glan": gla_norm_g,
        "wa": w_br_a[0].astype(BF16), "wb": w_br_b[0].astype(BF16), "wo": w_out[0].astype(BF16),
        "wr": jnp.stack([router.astype(BF16), (router - router.astype(BF16).astype(F32)).astype(BF16)]),
        "wg": w_exp_gate[0].astype(BF16), "wu": w_exp_up[0].astype(BF16), "wd": w_exp_down[0].astype(BF16),
        "scan_consts": _scan_constants(),
    }
    y_prompt, (st_hg, st_gla) = _trunk_layer(x_prompt, mod[nb:nb + 1], None, None, True, w)
    y_sample, _ = _trunk_layer(x_sample, mod[:nb], state_hgrn[:, 0], state_gla[:, 0], False, w)
    return y_prompt, y_sample, st_hg[:, None], st_gla[:, None]
```

```python
import functools

import numpy as np
import jax
import jax.numpy as jnp
from jax import lax
from jax.experimental import pallas as pl
from jax.experimental.pallas import tpu as pltpu

F32 = jnp.float32
BF16 = jnp.bfloat16

D_MODEL = 1024
CHUNK = 64
N_LEVELS = 6
CHUNKS_PER_STEP = 4
HG_H, HG_DK, HG_DV = 8, 128, 128
GLA_H, GLA_DK, GLA_DV = 4, 128, 256
GLA_RANK = 16
GLA_GATE_NORM = 16.0
N_GROUPS = 4
EXPERTS_PER_GROUP = 8
N_EXPERTS = N_GROUPS * EXPERTS_PER_GROUP
D_EXPERT = D_MODEL // 4
EPS = 1e-6
ROUTER_LANES = 128
EXPERT_LANE0 = N_GROUPS
VMEM_LIMIT = 56 * 1024 * 1024
MXU_DEPTH = 256

TM_PROJ = 256
TM_POST = 256
POST_PART = 256
TM_FINAL = 512
TR_MOE = 256
PLAN_BLOCK = 256
OUT_SUBLANES = D_MODEL // 128
COMB_SUBLANE = OUT_SUBLANES
ROW_SUBLANES = 16


def _sigmoid(x):
    return 1.0 / (1.0 + jnp.exp(-x))


def _silu(x):
    return x * _sigmoid(x)


def _log_sigmoid(x):
    return jnp.minimum(x, 0.0) - jnp.log(1.0 + jnp.exp(-jnp.abs(x)))


def _rmsnorm(x, g):
    return x * lax.rsqrt(jnp.mean(x * x, axis=-1, keepdims=True) + EPS) * g


def _dot(a, b):
    return jnp.dot(a, b, preferred_element_type=F32)


def _dot_nt(a, b):
    return lax.dot_general(a, b, (((1,), (1,)), ((), ())), preferred_element_type=F32)


def _dot_tn(a, b):
    return lax.dot_general(a, b, (((0,), (0,)), ((), ())), preferred_element_type=F32)


def _mod_body(c_ref, w_ref, b_ref, o_ref):
    s, w = _silu(c_ref[...]), w_ref[...]
    s_hi, w_hi = s.astype(BF16), w.astype(BF16)
    s_lo, w_lo = (s - s_hi.astype(F32)).astype(BF16), (w - w_hi.astype(F32)).astype(BF16)
    o_ref[...] = _dot(s_hi, w_hi) + _dot(s_hi, w_lo) + _dot(s_lo, w_hi) + b_ref[...]


def _modulation(cond, ada_w, ada_b):
    rows, n = cond.shape[0], ada_w.shape[1]
    tn = n // 4
    return pl.pallas_call(
        _mod_body,
        grid=(n // tn,),
        in_specs=[pl.BlockSpec((rows, D_MODEL), lambda j: (0, 0)),
                  pl.BlockSpec((D_MODEL, tn), lambda j: (0, j)),
                  pl.BlockSpec((1, tn), lambda j: (0, j))],
        out_specs=pl.BlockSpec((rows, tn), lambda j: (0, j)),
        out_shape=jax.ShapeDtypeStruct((rows, n), F32),
        compiler_params=pltpu.CompilerParams(vmem_limit_bytes=VMEM_LIMIT),
        name="modulation",
    )(cond, ada_w, ada_b)


_C_HQ, _C_HFF, _C_HFB, _C_HI, _C_HGATE = 0, 1024, 2048, 3072, 4096
_C_GQ, _C_GK, _C_GV, _C_GR, _C_TAIL = 5120, 5632, 6144, 7168, 8192
GA_WIDTH = 2 * GLA_RANK
W_TAIL = GA_WIDTH + 2 * D_MODEL
W_MAIN = _C_TAIL + W_TAIL


def _inproj_body(x_ref, mod_ref, n1_ref, lbp_ref, wmain_ref, waup_ref, ba_ref,
                 hq_ref, gf_ref, gb_ref, hv_ref, hgate_ref, gq_ref, gk_ref, gv_ref, gr_ref,
                 laf_ref, lab_ref, sa_ref, sb_ref):
    x = x_ref[...]
    h = _rmsnorm(x, n1_ref[...]) * (1.0 + mod_ref[1:2, :]) + mod_ref[0:1, :]
    hb = h.astype(BF16)

    def proj(c0, w):
        return _dot(hb, wmain_ref[:, c0:c0 + w])

    p0, p1 = lbp_ref[0:1, :], lbp_ref[1:2, :]
    pm = jnp.maximum(p0, p1)
    e0, e1 = jnp.exp(p0 - pm), jnp.exp(p1 - pm)
    lb = e0 / (e0 + e1)

    hq_ref[...] = (proj(_C_HQ, 1024) * HG_DK ** -0.5).astype(BF16)
    gf_ref[...] = jnp.log(lb + (1.0 - lb) * _sigmoid(proj(_C_HFF, 1024)))
    gb_ref[...] = jnp.log(lb + (1.0 - lb) * _sigmoid(proj(_C_HFB, 1024)))
    hv_ref[...] = proj(_C_HI, 1024).astype(BF16)
    hgate_ref[...] = _silu(proj(_C_HGATE, 1024)).astype(BF16)
    gq_ref[...] = (proj(_C_GQ, 512) * GLA_DK ** -0.5).astype(BF16)
    gk_ref[...] = proj(_C_GK, 512).astype(BF16)
    gv_ref[...] = proj(_C_GV, 1024).astype(BF16)
    gr_ref[...] = _silu(proj(_C_GR, 1024)).astype(BF16)
    tail = proj(_C_TAIL, W_TAIL)
    sa_ref[...] = _sigmoid(tail[:, GA_WIDTH:GA_WIDTH + D_MODEL]).astype(BF16)
    sb_ref[...] = _sigmoid(tail[:, GA_WIDTH + D_MODEL:]).astype(BF16)

    ga = tail[:, :GA_WIDTH].astype(BF16)
    la = _log_sigmoid(_dot(ga, waup_ref[...]) + ba_ref[...]) * (1.0 / GLA_GATE_NORM)
    laf_ref[...] = la[:, :512]
    lab_ref[...] = la[:, 512:]


def _inproj(x, mod, tiles_per_row, n1, lbp, wmain, waup, ba):
    t = x.shape[0]
    tm = TM_PROJ
    const = lambda shape: pl.BlockSpec(shape, lambda i: (0,) * len(shape))
    tok = lambda w: pl.BlockSpec((tm, w), lambda i: (i, 0))
    widths = (1024, 1024, 1024, 1024, 1024, 512, 512, 1024, 1024, 512, 512, 1024, 1024)
    dtypes = (BF16, F32, F32, BF16, BF16, BF16, BF16, BF16, BF16, F32, F32, BF16, BF16)
    return pl.pallas_call(
        _inproj_body,
        grid=(t // tm,),
        in_specs=[tok(D_MODEL),
                  pl.BlockSpec((None, 6, D_MODEL), lambda i: (i // tiles_per_row, 0, 0)),
                  const((1, D_MODEL)), const((2, D_MODEL)),
                  pl.BlockSpec((D_MODEL, W_MAIN), lambda i: (0, 0), pipeline_mode=pl.Buffered(1)),
                  const((GA_WIDTH, 1024)), const((1, 1024))],
        out_specs=[tok(w) for w in widths],
        out_shape=[jax.ShapeDtypeStruct((t, w), dt) for w, dt in zip(widths, dtypes)],
        compiler_params=pltpu.CompilerParams(vmem_limit_bytes=VMEM_LIMIT),
        name="inproj",
    )(x, mod, n1, lbp, wmain, waup, ba)


N_MXU_LEVELS = 3
ROW_GROUP = 16
START_SLOT, END_SLOT = N_LEVELS, N_LEVELS + 1
LOG2E = 1.4426950408889634


def _boundary_row(t, h, d):
    base = (t // (2 * h)) * 2 * h
    return base + h - 1 if d == 0 else base + h


def _scan_constants():
    c = CHUNK
    tri = np.tril(np.ones((c, c), np.float32))
    m = [tri] + [np.abs(tri - tri[[_boundary_row(t, 1 << l, 0) for t in range(c)]]) for l in range(N_MXU_LEVELS)]
    m = np.stack(m)
    lvl = np.full((c, c), N_LEVELS + 1, np.int32)
    for t in range(c):
        lvl[t, t] = N_LEVELS
        for s in range(t):
            lvl[t, s] = int(np.floor(np.log2(t ^ s)))
    m_fwd = m.reshape(-1, c)
    m_bwd = m[:, ::-1, ::-1].reshape(-1, c)
    m_both = np.stack([m_fwd, m_bwd])
    return (jnp.asarray(np.concatenate([m_both, m_both], axis=2), BF16), jnp.asarray(np.stack([lvl, lvl.T])))


def _scan_prepare(u, d, rows, q_ref, k_ref, g_ref, m_ref, side, cb_ref, kf_ref, kb_ref, p_ref):
    c = CHUNK
    w = g_ref.shape[-1]
    g = g_ref[rows, :]
    gs = g * LOG2E
    g1 = gs.astype(BF16)
    g2 = (gs - g1.astype(F32)).astype(BF16)
    cb_ref[u] = _dot(m_ref[d], jnp.concatenate([g1, g2], axis=0))
    k32 = (1.0 - jnp.exp(g)) if k_ref is None else k_ref[rows, :].astype(F32)
    kf_ref[u] = k32
    kb_ref[u] = k32.astype(BF16)
    bcast = lambda r, n: jnp.broadcast_to(cb_ref[u, r:r + 1, :], (n, w))
    for i in range(c // ROW_GROUP):
        t0 = ROW_GROUP * i
        grp = slice(t0, t0 + ROW_GROUP)
        cum = cb_ref[u, grp, :]
        q = q_ref[rows.start + t0:rows.start + t0 + ROW_GROUP, :]
        k = kb_ref[u, grp, :]
        for l in range(N_LEVELS):
            h = 1 << l
            if l < N_MXU_LEVELS:
                ex = cb_ref[u, c * (l + 1) + t0:c * (l + 1) + t0 + ROW_GROUP, :]
            else:
                ex = []
                for t in (t0, t0 + 8):
                    half = slice(t - t0, t - t0 + 8)
                    bnd = bcast(_boundary_row(t, h, d), 8)
                    ex.append(cum[half] - bnd if ((t & h) != 0) == (d == 0) else bnd - cum[half])
                ex = jnp.concatenate(ex, axis=0)
            if h >= ROW_GROUP:
                src = q if ((t0 & h) != 0) == (d == 0) else k
            else:
                upper = side[l] > 0
                src = jnp.where(upper, q, k) if d == 0 else jnp.where(upper, k, q)
            p_ref[u, l, grp, :] = src * jnp.exp2(ex).astype(BF16)
        p_ref[u, START_SLOT, grp, :] = q * jnp.exp2(cum).astype(BF16)
        edge = c - 1 if d == 0 else 0
        p_ref[u, END_SLOT, grp, :] = (kf_ref[u, grp, :] * jnp.exp2(bcast(edge, ROW_GROUP) - cum)).astype(BF16)


def _scan_scores(u, d, rows, q_ref, masks, kb_ref, p_ref, a_ref, n_heads, dk):
    groups = range(CHUNK // ROW_GROUP)
    piece = lambda x, g: x[g * ROW_GROUP:(g + 1) * ROW_GROUP]
    mask_pieces = [[piece(m, g) for g in groups] for m in masks]
    for h in range(n_heads):
        ks = slice(h * dk, (h + 1) * dk)
        diag = _dot_nt(q_ref[rows, ks], kb_ref[u, :, ks]).astype(BF16)
        a = [jnp.where(mask_pieces[N_LEVELS][g], piece(diag, g), jnp.zeros((), BF16)) for g in groups]
        for l in range(N_LEVELS):
            p = p_ref[u, l, :, ks]
            if (1 << l) < ROW_GROUP:
                query = list(groups)
                x = _dot_nt(p, p).astype(BF16)
            else:
                query = [g for g in groups if ((g * ROW_GROUP & (1 << l)) != 0) == (d == 0)]
                x = _dot_nt(jnp.concatenate([piece(p, g) for g in query], axis=0), p).astype(BF16)
            for j, g in enumerate(query):
                a[g] = jnp.where(mask_pieces[l][g], piece(x, j), a[g])
        a_ref[u, h] = jnp.concatenate(a, axis=0)


def _scan_outputs(u, d, rows, v_ref, cb_ref, p_ref, a_ref, st_ref, o_ref, n_heads, dk, dv):
    edge = CHUNK - 1 if d == 0 else 0
    for h in range(n_heads):
        ks = slice(h * dk, (h + 1) * dk)
        vs = slice(h * dv, (h + 1) * dv)
        v = v_ref[rows, vs]
        st = st_ref[d, h]
        o = _dot(a_ref[u, h], v) + _dot_nt(p_ref[u, START_SLOT, :, ks], st.astype(BF16))
        o_ref[rows, vs] = o.astype(o_ref.dtype)
        st_ref[d, h] = st * jnp.exp2(cb_ref[u, edge:edge + 1, ks]) + _dot_tn(v, p_ref[u, END_SLOT, :, ks])


def _scan_body(*refs, n_heads, dk, dv, derive_k, zero_init, emit_state):
    refs = list(refs)
    q_refs = (refs.pop(0), refs.pop(0))
    k_refs = (None, None) if derive_k else (refs.pop(0), refs.pop(0))
    v_refs = (refs.pop(0), refs.pop(0))
    g_refs = (refs.pop(0), refs.pop(0))
    m_ref, lvl_ref = refs.pop(0), refs.pop(0)
    s0_ref = None if zero_init else refs.pop(0)
    o_refs = (refs.pop(0), refs.pop(0))
    sout_ref = refs.pop(0) if emit_state else None
    st_ref, cb_ref, kf_ref, kb_ref, p_ref, a_ref = refs
    n = pl.program_id(1)

    @pl.when(n == 0)
    def _():
        for d in range(2):
            for h in range(n_heads):
                st_ref[d, h] = jnp.zeros((dv, dk), F32) if zero_init else s0_ref[d, h].T

    cps = CHUNKS_PER_STEP
    rows = lambda c: slice(c * CHUNK, (c + 1) * CHUNK)
    masks = [[lvl_ref[d].astype(F32).astype(BF16) == l for l in range(N_LEVELS + 1)] for d in range(2)]
    row = lax.broadcasted_iota(jnp.int32, (ROW_GROUP, q_refs[0].shape[-1]), 0)
    side = [((row >> l) & 1).astype(F32).astype(BF16) for l in range(N_LEVELS) if (1 << l) < ROW_GROUP]
    units = [(d, c) for c in range(cps) for d in range(2)]
    for i in range(len(units) + 1):
        if i < len(units):
            d, c = units[i]
            _scan_prepare(d * cps + c, d, rows(c), q_refs[d], k_refs[d], g_refs[d], m_ref, side, cb_ref, kf_ref,
                          kb_ref, p_ref)
        if i >= 1:
            d, c = units[i - 1]
            _scan_scores(d * cps + c, d, rows(c), q_refs[d], masks[d], kb_ref, p_ref, a_ref, n_heads, dk)
    for d in range(2):
        for c in (range(cps) if d == 0 else reversed(range(cps))):
            _scan_outputs(d * cps + c, d, rows(c), v_refs[d], cb_ref, p_ref, a_ref, st_ref, o_refs[d], n_heads,
                          dk, dv)

    if emit_state:
        @pl.when(n == pl.num_programs(1) - 1)
        def _():
            for d in range(2):
                for h in range(n_heads):
                    sout_ref[d, h] = st_ref[d, h].T


def _scan(q, k, v, g_f, g_b, s0, consts, n_heads, dk, dv, emit_state):
    b, l, _ = q.shape
    step = CHUNKS_PER_STEP * CHUNK
    n = l // step
    slots = 2 * CHUNKS_PER_STEP
    m_all, lvl = consts
    fwd = lambda w: pl.BlockSpec((None, step, w), lambda i, j: (i, j, 0))
    bwd = lambda w: pl.BlockSpec((None, step, w), lambda i, j: (i, n - 1 - j, 0))
    const = lambda shape: pl.BlockSpec(shape, lambda i, j: (0,) * len(shape))
    state = pl.BlockSpec((None, 2, n_heads, dk, dv), lambda i, j: (i, 0, 0, 0, 0))
    wk, wv = n_heads * dk, n_heads * dv
    args, in_specs = [q, q], [fwd(wk), bwd(wk)]
    if k is not None:
        args += [k, k]
        in_specs += [fwd(wk), bwd(wk)]
    args += [v, v, g_f, g_b, m_all, lvl]
    in_specs += [fwd(wv), bwd(wv), fwd(wk), bwd(wk), const(m_all.shape), const(lvl.shape)]
    if s0 is not None:
        args.append(s0)
        in_specs.append(state)
    out_specs = [fwd(wv), bwd(wv)]
    out_shape = [jax.ShapeDtypeStruct((b, l, wv), BF16)] * 2
    if emit_state:
        out_specs.append(state)
        out_shape.append(jax.ShapeDtypeStruct((b, 2, n_heads, dk, dv), F32))
    body = functools.partial(_scan_body, n_heads=n_heads, dk=dk, dv=dv, derive_k=k is None,
                             zero_init=s0 is None, emit_state=emit_state)
    return pl.pallas_call(
        body,
        grid=(b, n),
        in_specs=in_specs,
        out_specs=out_specs,
        out_shape=out_shape,
        scratch_shapes=[pltpu.VMEM((2, n_heads, dv, dk), F32),
                        pltpu.VMEM((slots, (1 + N_MXU_LEVELS) * CHUNK, wk), F32),
                        pltpu.VMEM((slots, CHUNK, wk), F32), pltpu.VMEM((slots, CHUNK, wk), BF16),
                        pltpu.VMEM((slots, N_LEVELS + 2, CHUNK, wk), BF16),
                        pltpu.VMEM((slots, n_heads, CHUNK, CHUNK), BF16)],
        compiler_params=pltpu.CompilerParams(dimension_semantics=("arbitrary", "arbitrary"),
                                             vmem_limit_bytes=VMEM_LIMIT),
        name="scan_hgrn" if k is None else "scan_gla",
    )(*args)


def _post_body(ohf_ref, ohb_ref, ogf_ref, ogb_ref, hgate_ref, gr_ref, sa_ref, sb_ref, x_ref, mod_ref,
               hgn_ref, glan_ref, n2_ref, wa_ref, wb_ref, wo_ref, wr_ref,
               x1_ref, rows_ref, group_ref):
    tm = x_ref.shape[0]
    for r0 in range(0, tm, POST_PART):
        _post_part(slice(r0, r0 + POST_PART), ohf_ref, ohb_ref, ogf_ref, ogb_ref, hgate_ref, gr_ref, sa_ref, sb_ref,
                   x_ref, mod_ref, hgn_ref, glan_ref, n2_ref, wa_ref, wb_ref, wo_ref, wr_ref,
                   x1_ref, rows_ref, group_ref)


def _post_part(rs, ohf_ref, ohb_ref, ogf_ref, ogb_ref, hgate_ref, gr_ref, sa_ref, sb_ref, x_ref, mod_ref,
               hgn_ref, glan_ref, n2_ref, wa_ref, wb_ref, wo_ref, wr_ref,
               x1_ref, rows_ref, group_ref):
    n = rs.stop - rs.start

    def normed(of_ref, ob_ref, g_ref, gate_ref, s):
        o = of_ref[rs, s].astype(F32) + ob_ref[rs, s].astype(F32)
        return (_rmsnorm(o, g_ref[...]) * gate_ref[rs, s].astype(F32)).astype(BF16)

    chunks = [slice(c * MXU_DEPTH, (c + 1) * MXU_DEPTH) for c in range(D_MODEL // MXU_DEPTH)]
    acc = lambda total, part: part if total is None else total + part
    t_hg = t_gla = mix = None
    for cs in chunks:
        heads = range(cs.start // HG_DV, cs.stop // HG_DV)
        y_hg = jnp.concatenate([normed(ohf_ref, ohb_ref, hgn_ref, hgate_ref, slice(h * HG_DV, (h + 1) * HG_DV))
                                for h in heads], axis=1)
        t_hg = acc(t_hg, _dot(y_hg, wa_ref[cs, :]))
        heads = range(cs.start // GLA_DV, cs.stop // GLA_DV)
        y_gla = jnp.concatenate([normed(ogf_ref, ogb_ref, glan_ref, gr_ref, slice(h * GLA_DV, (h + 1) * GLA_DV))
                                 for h in heads], axis=1)
        t_gla = acc(t_gla, _dot(y_gla, wb_ref[cs, :]))
    for cs in chunks:
        merged = sa_ref[rs, cs].astype(F32) * t_hg[:, cs] + sb_ref[rs, cs].astype(F32) * t_gla[:, cs]
        mix = acc(mix, _dot(merged.astype(BF16), wo_ref[cs, :]))
    x1 =x_ref[rs, :] + mod_ref[2:3, :] * mix
    x1_ref[rs, :] = x1
    h2 = _rmsnorm(x1, n2_ref[...]) * (1.0 + mod_ref[4:5, :]) + mod_ref[3:4, :]
    h2_hi = h2.astype(BF16)
    token_rows = lambda s: pl.ds(rs.start * ROW_SUBLANES + s, n, stride=ROW_SUBLANES)
    for s in range(D_MODEL // 128):
        rows_ref[token_rows(s), :] = h2[:, s * 128:(s + 1) * 128]
    for s in range(COMB_SUBLANE + 1, ROW_SUBLANES):
        rows_ref[token_rows(s), :] = jnp.zeros((n, 128), F32)

    h2_lo = (h2 - h2_hi.astype(F32)).astype(BF16)
    lt = _dot_nt(wr_ref[0], h2_hi) + _dot_nt(wr_ref[1], h2_hi) + _dot_nt(wr_ref[0], h2_lo)
    row = lambda r: lt[r:r + 1, :]
    top = lambda vals: functools.reduce(jnp.maximum, vals)

    def first_at(vals, m):
        idx = jnp.full(m.shape, len(vals), jnp.int32)
        for j in reversed(range(len(vals))):
            idx = jnp.where(vals[j] == m, j, idx)
        return idx

    gl = [row(g) for g in range(N_GROUPS)]
    gmax = top(gl)
    p_top = 1.0 / functools.reduce(jnp.add, [jnp.exp(v - gmax) for v in gl])
    g_idx = first_at(gl, gmax)
    sel = []
    for j in range(EXPERTS_PER_GROUP):
        v = row(EXPERT_LANE0 + j)
        for g in range(1, N_GROUPS):
            v = jnp.where(g_idx == g, row(EXPERT_LANE0 + g * EXPERTS_PER_GROUP + j), v)
        sel.append(v)
    m1 = top(sel)
    i1 = first_at(sel, m1)
    sel2 = [jnp.where(i1 == j, -jnp.inf, v) for j, v in enumerate(sel)]
    m2 = top(sel2)
    i2 = first_at(sel2, m2)
    r = jnp.exp(m2 - m1)
    w1 = p_top / (1.0 + r)
    lane0 = EXPERT_LANE0 + g_idx * EXPERTS_PER_GROUP
    rid = lax.broadcasted_iota(jnp.int32, lt.shape, 0)
    comb_t = jnp.where(rid == lane0 + i1, w1, jnp.where(rid == lane0 + i2, w1 * r, 0.0))
    rows_ref[token_rows(COMB_SUBLANE), :] = comb_t.T
    group_ref[:, rs] = g_idx


def _post(ohf, ohb, ogf, ogb, hgate, gr, sa, sb, x, mod, tiles_per_row, hgn, glan, n2, wa, wb, wo, wr):
    t = x.shape[0]
    tm = TM_POST
    const = lambda shape: pl.BlockSpec(shape, lambda i: (0,) * len(shape))
    tok = lambda w: pl.BlockSpec((tm, w), lambda i: (i, 0))
    return pl.pallas_call(
        _post_body,
        grid=(t // tm,),
        in_specs=[tok(1024)] * 9
                 + [pl.BlockSpec((None, 6, D_MODEL), lambda i: (i // tiles_per_row, 0, 0)),
                    const((1, HG_DV)), const((1, GLA_DV)), const((1, D_MODEL)),
                    const((1024, 1024)), const((1024, 1024)), const((1024, 1024)),
                    const((2, ROUTER_LANES, D_MODEL))],
        out_specs=[tok(1024), pl.BlockSpec((tm * ROW_SUBLANES, 128), lambda i: (i, 0)),
                   pl.BlockSpec((None, 1, tm), lambda i: (i, 0, 0))],
        out_shape=[jax.ShapeDtypeStruct((t, D_MODEL), F32), jax.ShapeDtypeStruct((t * ROW_SUBLANES, 128), F32),
                   jax.ShapeDtypeStruct((t // tm, 1, tm), jnp.int32)],
        compiler_params=pltpu.CompilerParams(vmem_limit_bytes=VMEM_LIMIT),
        name="post",
    )(ohf, ohb, ogf, ogb, hgate, gr, sa, sb, x, mod, hgn, glan, n2, wa, wb, wo, wr)


def _route_plan(group, t):
    tr, blk = TR_MOE, PLAN_BLOCK
    nt, nb = t // tr + N_GROUPS, t // blk
    i32 = jnp.int32
    gids = jnp.arange(N_GROUPS, dtype=i32)
    onehot = (group[:, None] == gids[None, :]).astype(i32).reshape(nb, blk, N_GROUPS)
    within = jnp.cumsum(onehot, axis=1)
    blk_end = jnp.cumsum(within[:, -1, :], axis=0)
    blk_start = blk_end - within[:, -1, :]
    counts = blk_end[-1]
    tiles = (counts + tr - 1) // tr
    tile_end = jnp.cumsum(tiles)
    tile_group = jnp.minimum(jnp.sum(jnp.arange(nt, dtype=i32)[:, None] >= tile_end[None, :], axis=1),
                             N_GROUPS - 1).astype(i32)
    row = jnp.arange(nt * tr, dtype=i32)
    gsel = (jnp.repeat(tile_group, tr)[:, None] == gids[None, :]).astype(i32)
    pick = lambda per_group: jnp.sum(gsel * per_group[None, :], axis=1)
    k = row - pick(tile_end - tiles) * tr
    valid = k < pick(counts)
    per_block = lambda table: jnp.sum(gsel[:, None, :] * table[None, :, :], axis=2)
    b_row = jnp.minimum(jnp.sum(per_block(blk_end) <= k[:, None], axis=1), nb - 1)
    bsel = (b_row[:, None] == jnp.arange(nb, dtype=i32)[None, :]).astype(i32)
    k_local = k - jnp.sum(bsel * per_block(blk_start), axis=1)
    sel = (bsel[:, :, None] * gsel[:, None, :]).reshape(nt * tr, nb * N_GROUPS).astype(F32)
    table = jnp.transpose(within, (0, 2, 1)).reshape(nb * N_GROUPS, blk).astype(F32)
    ranks = jnp.dot(sel, table)
    t_local = jnp.sum(ranks <= k_local[:, None].astype(F32), axis=1).astype(i32)
    row_src = jnp.where(valid, b_row * blk + t_local, 0)
    spare = t + ((row // tr) % 2) * tr + row % tr
    return tile_group, row_src, jnp.where(valid, row_src, spare)


def _moe_body(tg_ref, src_ref, dst_ref, rows_hbm, wg_ref, wu_ref, wd_ref, out_hbm, xbuf, ybuf, gsem, ssem):
    tr = TR_MOE
    i = pl.program_id(0)
    nt = pl.num_programs(0)
    slot = i % 2

    def gather_row(tile, slot_, r):
        tok = src_ref[tile * tr + r]
        return pltpu.make_async_copy(rows_hbm.at[pl.ds(pl.multiple_of(tok * ROW_SUBLANES, ROW_SUBLANES), ROW_SUBLANES)],
                                     xbuf.at[slot_, pl.ds(r * ROW_SUBLANES, ROW_SUBLANES)], gsem.at[slot_])

    def scatter_row(tile, slot_, r):
        tok = dst_ref[tile * tr + r]
        return pltpu.make_async_copy(ybuf.at[slot_, pl.ds(r * OUT_SUBLANES, OUT_SUBLANES)],
                                     out_hbm.at[pl.ds(pl.multiple_of(tok * OUT_SUBLANES, OUT_SUBLANES), OUT_SUBLANES)],
                                     ssem.at[slot_])

    def for_rows(fn):
        def body(r, carry):
            fn(r)
            return carry
        lax.fori_loop(0, tr, body, 0, unroll=8)

    def gathered(slot_):
        return pltpu.make_async_copy(rows_hbm.at[pl.ds(0, tr * ROW_SUBLANES)], xbuf.at[slot_], gsem.at[slot_])

    def scattered(slot_):
        return pltpu.make_async_copy(ybuf.at[slot_], out_hbm.at[pl.ds(0, tr * OUT_SUBLANES)], ssem.at[slot_])

    @pl.when(i == 0)
    def _():
        for_rows(lambda r: gather_row(0, 0, r).start())
        n_tok = out_hbm.shape[0] // OUT_SUBLANES - 2 * tr
        ybuf[...] = jnp.zeros_like(ybuf)
        for s in range(2):
            spare = pltpu.make_async_copy(
                ybuf.at[s], out_hbm.at[pl.ds((n_tok + s * tr) * OUT_SUBLANES, tr * OUT_SUBLANES)], ssem.at[s])
            spare.start()
            spare.wait()

    @pl.when(i + 1 < nt)
    def _():
        for_rows(lambda r: gather_row(i + 1, 1 - slot, r).start())

    gathered(slot).wait()

    @pl.when(i >= 2)
    def _():
        scattered(slot).wait()

    x = jnp.concatenate([xbuf[slot, pl.ds(s, tr, stride=ROW_SUBLANES), :] for s in range(D_MODEL // 128)],
                        axis=1).astype(BF16)
    comb = xbuf[slot, pl.ds(COMB_SUBLANE, tr, stride=ROW_SUBLANES), :]
    lane = lax.broadcasted_iota(jnp.int32, comb.shape, 1)
    lane0 = EXPERT_LANE0 + tg_ref[i] * EXPERTS_PER_GROUP
    acc = jnp.zeros((tr, D_MODEL), F32)
    for e in range(EXPERTS_PER_GROUP):
        col = jnp.sum(jnp.where(lane == lane0 + e, comb, 0.0), axis=-1, keepdims=True)
        hid = (_silu(_dot(x, wg_ref[e])) * _dot(x, wu_ref[e]) * col).astype(BF16)
        acc = acc + _dot(hid, wd_ref[e])

    for s in range(D_MODEL // 128):
        ybuf[slot, pl.ds(s, tr, stride=OUT_SUBLANES), :] = acc[:, s * 128:(s + 1) * 128]

    for_rows(lambda r: scatter_row(i, slot, r).start())

    @pl.when(i == nt - 1)
    def _():
        scattered(slot).wait()
        scattered(1 - slot).wait()


def _moe(rows, plan, wg, wu, wd, t):
    tr = TR_MOE
    tile_group, row_src, row_dst = plan
    nt = tile_group.shape[0]
    group_w = lambda shape: pl.BlockSpec((EXPERTS_PER_GROUP,) + shape, lambda i, tg, src, dst: (tg[i], 0, 0))
    grid_spec = pltpu.PrefetchScalarGridSpec(
        num_scalar_prefetch=3,
        grid=(nt,),
        in_specs=[pl.BlockSpec(memory_space=pl.ANY),
                  group_w((D_MODEL, D_EXPERT)), group_w((D_MODEL, D_EXPERT)), group_w((D_EXPERT, D_MODEL))],
        out_specs=pl.BlockSpec(memory_space=pl.ANY),
        scratch_shapes=[pltpu.VMEM((2, tr * ROW_SUBLANES, 128), F32), pltpu.VMEM((2, tr * OUT_SUBLANES, 128), F32),
                        pltpu.SemaphoreType.DMA((2,)), pltpu.SemaphoreType.DMA((2,))])
    return pl.pallas_call(
        _moe_body,
        grid_spec=grid_spec,
        out_shape=jax.ShapeDtypeStruct(((t + 2 * tr) * OUT_SUBLANES, 128), F32),
        compiler_params=pltpu.CompilerParams(dimension_semantics=("arbitrary",), vmem_limit_bytes=VMEM_LIMIT),
        name="moe",
    )(tile_group, row_src, row_dst, rows, wg, wu, wd)


def _final_body(x1_ref, moe_ref, mod_ref, fn_ref, y_ref):
    tm = x1_ref.shape[0]
    moe = jnp.concatenate([moe_ref[pl.ds(s, tm, stride=OUT_SUBLANES), :] for s in range(D_MODEL // 128)], axis=1)
    y_ref[...] = _rmsnorm(x1_ref[...] + mod_ref[5:6, :] * moe, fn_ref[...])


def _final(x1, moe, mod, tiles_per_row, fn):
    t = x1.shape[0]
    tm = TM_FINAL
    return pl.pallas_call(
        _final_body,
        grid=(t // tm,),
        in_specs=[pl.BlockSpec((tm, D_MODEL), lambda i: (i, 0)),
                  pl.BlockSpec((tm * OUT_SUBLANES, 128), lambda i: (i, 0)),
                  pl.BlockSpec((None, 6, D_MODEL), lambda i: (i // tiles_per_row, 0, 0)),
                  pl.BlockSpec((1, D_MODEL), lambda i: (0, 0))],
        out_specs=pl.BlockSpec((tm, D_MODEL), lambda i: (i, 0)),
        out_shape=jax.ShapeDtypeStruct((t, D_MODEL), F32),
        compiler_params=pltpu.CompilerParams(vmem_limit_bytes=VMEM_LIMIT),
        name="final",
    )(x1, moe, mod, fn)


def _trunk_layer(x, mod, s0_hg, s0_gla, emit_state, w):
    b, l, _ = x.shape
    xt = x.reshape(b * l, D_MODEL)
    per_row = lambda tm: (l // tm) if mod.shape[0] > 1 else (b * l // tm)
    (hq, gf, gb, hv, hgate, gq, gk, gv, gr, laf, lab, sa, sb) = _inproj(
        xt, mod, per_row(TM_PROJ), w["n1"], w["lbp"], w["wmain"], w["waup"], w["ba"])
    seq = lambda a: a.reshape(b, l, a.shape[-1])
    hg = _scan(seq(hq), None, seq(hv), seq(gf), seq(gb), s0_hg, w["scan_consts"], HG_H, HG_DK, HG_DV,
               emit_state)
    gla = _scan(seq(gq), seq(gk), seq(gv), seq(laf), seq(lab), s0_gla, w["scan_consts"], GLA_H, GLA_DK,
                GLA_DV, emit_state)
    flat = lambda a: a.reshape(b * l, a.shape[-1])
    x1, rows, group = _post(flat(hg[0]), flat(hg[1]), flat(gla[0]), flat(gla[1]), hgate, gr, sa, sb, xt, mod,
                           per_row(TM_POST), w["hgn"], w["glan"], w["n2"], w["wa"], w["wb"], w["wo"], w["wr"])
    plan = _route_plan(group.reshape(b * l), b * l)
    moe = _moe(rows, plan, w["wg"], w["wu"], w["wd"], b * l)
    y = _final(x1, moe, mod, per_row(TM_FINAL), w["fn"])
    states = (hg[2], gla[2]) if emit_state else (None, None)
    return y.reshape(b, l, D_MODEL), states


def kernel(x_prompt, x_sample, state_hgrn, state_gla, c, c_ctx, ada_w, ada_b, norm1_g, norm2_g, w_in,
           hg_lb_param, hg_norm_g, gla_wa_up, gla_ba, gla_norm_g, w_br_a, w_br_b, w_out,
           w_router_group, w_router_expert, w_exp_gate, w_exp_up, w_exp_down, final_norm_g):
    nb = c.shape[0]
    cond = jnp.concatenate([c, c_ctx[None, :], jnp.zeros((8 - nb - 1, D_MODEL), F32)], axis=0)
    mod = _modulation(cond, ada_w[0], ada_b).reshape(8, 6, D_MODEL)

    zeros = jnp.zeros((GLA_RANK, GLA_H * GLA_DK), F32)
    router = jnp.concatenate(
        [w_router_group[0], jnp.transpose(w_router_expert[0], (1, 0, 2)).reshape(D_MODEL, N_EXPERTS),
         jnp.zeros((D_MODEL, ROUTER_LANES - N_GROUPS - N_EXPERTS), F32)], axis=1)
    w = {
        "n1": norm1_g, "n2": norm2_g, "fn": final_norm_g[None, :], "lbp": hg_lb_param,
        "wmain": w_in[0].astype(BF16),
        "waup": jnp.concatenate([jnp.concatenate([gla_wa_up[0, 0], zeros], axis=1),
                                 jnp.concatenate([zeros, gla_wa_up[0, 1]], axis=1)], axis=0).astype(BF16),
        "ba": gla_ba[0].reshape(1, 2 * GLA_H * GLA_DK),
        "hgn": hg_norm_g, "glan": gla_norm_g,
        "wa": w_br_a[0].astype(BF16), "wb": w_br_b[0].astype(BF16), "wo": w_out[0].astype(BF16),
        "wr": jnp.stack([router.astype(BF16), (router - router.astype(BF16).astype(F32)).astype(BF16)]
                        ).transpose(0, 2, 1),
        "wg": w_exp_gate[0].astype(BF16), "wu": w_exp_up[0].astype(BF16), "wd": w_exp_down[0].astype(BF16),
        "scan_consts": _scan_constants(),
    }
    y_prompt, (st_hg, st_gla) = _trunk_layer(x_prompt, mod[nb:nb + 1], None, None, True, w)
    y_sample, _ = _trunk_layer(x_sample, mod[:nb], state_hgrn[:, 0], state_gla[:, 0], False, w)
    return y_prompt, y_sample, st_hg[:, None], st_gla[:, None]
```

```python
import functools

import numpy as np
import jax
import jax.numpy as jnp
from jax import lax
from jax.experimental import pallas as pl
from jax.experimental.pallas import tpu as pltpu

F32 = jnp.float32
BF16 = jnp.bfloat16

D_MODEL = 1024
CHUNK = 64
N_LEVELS = 6
CHUNKS_PER_STEP = 4
HG_H, HG_DK, HG_DV = 8, 128, 128
GLA_H, GLA_DK, GLA_DV = 4, 128, 256
GLA_RANK = 16
GLA_GATE_NORM = 16.0
N_GROUPS = 4
EXPERTS_PER_GROUP = 8
N_EXPERTS = N_GROUPS * EXPERTS_PER_GROUP
D_EXPERT = D_MODEL // 4
EPS = 1e-6
ROUTER_LANES = 128
EXPERT_LANE0 = N_GROUPS
VMEM_LIMIT = 56 * 1024 * 1024
MXU_DEPTH = 256

TM_PROJ = 256
TM_POST = 256
POST_PART = 256
TM_FINAL = 512
TR_MOE = 256
PLAN_BLOCK = 256
OUT_SUBLANES = D_MODEL // 128
COMB_SUBLANE = OUT_SUBLANES
ROW_SUBLANES = 16


def _sigmoid(x):
    return 1.0 / (1.0 + jnp.exp(-x))


def _silu(x):
    return x * _sigmoid(x)


def _log_sigmoid(x):
    return jnp.minimum(x, 0.0) - jnp.log(1.0 + jnp.exp(-jnp.abs(x)))


def _rmsnorm(x, g):
    return x * lax.rsqrt(jnp.mean(x * x, axis=-1, keepdims=True) + EPS) * g


def _dot(a, b):
    return jnp.dot(a, b, preferred_element_type=F32)


def _dot_nt(a, b):
    return lax.dot_general(a, b, (((1,), (1,)), ((), ())), preferred_element_type=F32)


def _dot_tn(a, b):
    return lax.dot_general(a, b, (((0,), (0,)), ((), ())), preferred_element_type=F32)


def _mod_body(c_ref, w_ref, b_ref, o_ref):
    s, w = _silu(c_ref[...]), w_ref[...]
    s_hi, w_hi = s.astype(BF16), w.astype(BF16)
    s_lo, w_lo = (s - s_hi.astype(F32)).astype(BF16), (w - w_hi.astype(F32)).astype(BF16)
    o_ref[...] = _dot(s_hi, w_hi) + _dot(s_hi, w_lo) + _dot(s_lo, w_hi) + b_ref[...]


def _modulation(cond, ada_w, ada_b):
    rows, n = cond.shape[0], ada_w.shape[1]
    tn = n // 4
    return pl.pallas_call(
        _mod_body,
        grid=(n // tn,),
        in_specs=[pl.BlockSpec((rows, D_MODEL), lambda j: (0, 0)),
                  pl.BlockSpec((D_MODEL, tn), lambda j: (0, j)),
                  pl.BlockSpec((1, tn), lambda j: (0, j))],
        out_specs=pl.BlockSpec((rows, tn), lambda j: (0, j)),
        out_shape=jax.ShapeDtypeStruct((rows, n), F32),
        compiler_params=pltpu.CompilerParams(vmem_limit_bytes=VMEM_LIMIT),
        name="modulation",
    )(cond, ada_w, ada_b)


_C_HQ, _C_HFF, _C_HFB, _C_HI, _C_HGATE = 0, 1024, 2048, 3072, 4096
_C_GQ, _C_GK, _C_GV, _C_GR, _C_TAIL = 5120, 5632, 6144, 7168, 8192
GA_WIDTH = 2 * GLA_RANK
W_TAIL = GA_WIDTH + 2 * D_MODEL
W_MAIN = _C_TAIL + W_TAIL


def _inproj_body(x_ref, mod_ref, n1_ref, lbp_ref, wmain_ref, waup_ref, ba_ref,
                 hq_ref, gf_ref, gb_ref, hv_ref, hgate_ref, gq_ref, gk_ref, gv_ref, gr_ref,
                 laf_ref, lab_ref, sa_ref, sb_ref):
    x = x_ref[...]
    h = _rmsnorm(x, n1_ref[...]) * (1.0 + mod_ref[1:2, :]) + mod_ref[0:1, :]
    hb = h.astype(BF16)

    def proj(c0, w):
        return _dot(hb, wmain_ref[:, c0:c0 + w])

    p0, p1 = lbp_ref[0:1, :], lbp_ref[1:2, :]
    pm = jnp.maximum(p0, p1)
    e0, e1 = jnp.exp(p0 - pm), jnp.exp(p1 - pm)
    lb = e0 / (e0 + e1)

    hq_ref[...] = (proj(_C_HQ, 1024) * HG_DK ** -0.5).astype(BF16)
    gf_ref[...] = jnp.log(lb + (1.0 - lb) * _sigmoid(proj(_C_HFF, 1024)))
    gb_ref[...] = jnp.log(lb + (1.0 - lb) * _sigmoid(proj(_C_HFB, 1024)))
    hv_ref[...] = proj(_C_HI, 1024).astype(BF16)
    hgate_ref[...] = _silu(proj(_C_HGATE, 1024)).astype(BF16)
    gq_ref[...] = (proj(_C_GQ, 512) * GLA_DK ** -0.5).astype(BF16)
    gk_ref[...] = proj(_C_GK, 512).astype(BF16)
    gv_ref[...] = proj(_C_GV, 1024).astype(BF16)
    gr_ref[...] = _silu(proj(_C_GR, 1024)).astype(BF16)
    tail = proj(_C_TAIL, W_TAIL)
    sa_ref[...] = _sigmoid(tail[:, GA_WIDTH:GA_WIDTH + D_MODEL]).astype(BF16)
    sb_ref[...] = _sigmoid(tail[:, GA_WIDTH + D_MODEL:]).astype(BF16)

    ga = tail[:, :GA_WIDTH].astype(BF16)
    la = _log_sigmoid(_dot(ga, waup_ref[...]) + ba_ref[...]) * (1.0 / GLA_GATE_NORM)
    laf_ref[...] = la[:, :512]
    lab_ref[...] = la[:, 512:]


def _inproj(x, mod, tiles_per_row, n1, lbp, wmain, waup, ba):
    t = x.shape[0]
    tm = TM_PROJ
    const = lambda shape: pl.BlockSpec(shape, lambda i: (0,) * len(shape))
    tok = lambda w: pl.BlockSpec((tm, w), lambda i: (i, 0))
    widths = (1024, 1024, 1024, 1024, 1024, 512, 512, 1024, 1024, 512, 512, 1024, 1024)
    dtypes = (BF16, F32, F32, BF16, BF16, BF16, BF16, BF16, BF16, F32, F32, BF16, BF16)
    return pl.pallas_call(
        _inproj_body,
        grid=(t // tm,),
        in_specs=[tok(D_MODEL),
                  pl.BlockSpec((None, 6, D_MODEL), lambda i: (i // tiles_per_row, 0, 0)),
                  const((1, D_MODEL)), const((2, D_MODEL)),
                  pl.BlockSpec((D_MODEL, W_MAIN), lambda i: (0, 0), pipeline_mode=pl.Buffered(1)),
                  const((GA_WIDTH, 1024)), const((1, 1024))],
        out_specs=[tok(w) for w in widths],
        out_shape=[jax.ShapeDtypeStruct((t, w), dt) for w, dt in zip(widths, dtypes)],
        compiler_params=pltpu.CompilerParams(vmem_limit_bytes=VMEM_LIMIT),
        name="inproj",
    )(x, mod, n1, lbp, wmain, waup, ba)


N_MXU_LEVELS = 3
ROW_GROUP = 16
START_SLOT, END_SLOT = N_LEVELS, N_LEVELS + 1
LOG2E = 1.4426950408889634


def _boundary_row(t, h, d):
    base = (t // (2 * h)) * 2 * h
    return base + h - 1 if d == 0 else base + h


def _scan_constants():
    c = CHUNK
    tri = np.tril(np.ones((c, c), np.float32))
    m = [tri] + [np.abs(tri - tri[[_boundary_row(t, 1 << l, 0) for t in range(c)]]) for l in range(N_MXU_LEVELS)]
    m = np.stack(m)
    lvl = np.full((c, c), N_LEVELS + 1, np.int32)
    for t in range(c):
        lvl[t, t] = N_LEVELS
        for s in range(t):
            lvl[t, s] = int(np.floor(np.log2(t ^ s)))
    m_fwd = m.reshape(-1, c)
    m_bwd = m[:, ::-1, ::-1].reshape(-1, c)
    m_both = np.stack([m_fwd, m_bwd])
    return (jnp.asarray(np.concatenate([m_both, m_both], axis=2), BF16), jnp.asarray(np.stack([lvl, lvl.T])))


def _scan_prepare(u, d, rows, q_ref, k_ref, g_ref, m_ref, side, cb_ref, kf_ref, kb_ref, p_ref):
    c = CHUNK
    w = g_ref.shape[-1]
    g = g_ref[rows, :]
    gs = g * LOG2E
    g1 = gs.astype(BF16)
    g2 = (gs - g1.astype(F32)).astype(BF16)
    cb_ref[u] = _dot(m_ref[d], jnp.concatenate([g1, g2], axis=0))
    k32 = (1.0 - jnp.exp(g)) if k_ref is None else k_ref[rows, :].astype(F32)
    kf_ref[u] = k32
    kb_ref[u] = k32.astype(BF16)
    bcast = lambda r, n: jnp.broadcast_to(cb_ref[u, r:r + 1, :], (n, w))
    for i in range(c // ROW_GROUP):
        t0 = ROW_GROUP * i
        grp = slice(t0, t0 + ROW_GROUP)
        cum = cb_ref[u, grp, :]
        q = q_ref[rows.start + t0:rows.start + t0 + ROW_GROUP, :]
        k = kb_ref[u, grp, :]
        for l in range(N_LEVELS):
            h = 1 << l
            if l < N_MXU_LEVELS:
                ex = cb_ref[u, c * (l + 1) + t0:c * (l + 1) + t0 + ROW_GROUP, :]
            else:
                ex = []
                for t in (t0, t0 + 8):
                    half = slice(t - t0, t - t0 + 8)
                    bnd = bcast(_boundary_row(t, h, d), 8)
                    ex.append(cum[half] - bnd if ((t & h) != 0) == (d == 0) else bnd - cum[half])
                ex = jnp.concatenate(ex, axis=0)
            if h >= ROW_GROUP:
                src = q if ((t0 & h) != 0) == (d == 0) else k
            else:
                upper = side[l] > 0
                src = jnp.where(upper, q, k) if d == 0 else jnp.where(upper, k, q)
            p_ref[u, l, grp, :] = src * jnp.exp2(ex).astype(BF16)
        p_ref[u, START_SLOT, grp, :] = q * jnp.exp2(cum).astype(BF16)
        edge = c - 1 if d == 0 else 0
        p_ref[u, END_SLOT, grp, :] = (kf_ref[u, grp, :] * jnp.exp2(bcast(edge, ROW_GROUP) - cum)).astype(BF16)


def _scan_scores(u, d, rows, q_ref, masks, kb_ref, p_ref, a_ref, n_heads, dk):
    groups = range(CHUNK // ROW_GROUP)
    piece = lambda x, g: x[g * ROW_GROUP:(g + 1) * ROW_GROUP]
    mask_pieces = [[piece(m, g) for g in groups] for m in masks]
    for h in range(n_heads):
        ks = slice(h * dk, (h + 1) * dk)
        diag = _dot_nt(q_ref[rows, ks], kb_ref[u, :, ks]).astype(BF16)
        a = [jnp.where(mask_pieces[N_LEVELS][g], piece(diag, g), jnp.zeros((), BF16)) for g in groups]
        for l in range(N_LEVELS):
            p = p_ref[u, l, :, ks]
            if (1 << l) < ROW_GROUP:
                query = list(groups)
                x = _dot_nt(p, p).astype(BF16)
            else:
                query = [g for g in groups if ((g * ROW_GROUP & (1 << l)) != 0) == (d == 0)]
                x = _dot_nt(jnp.concatenate([piece(p, g) for g in query], axis=0), p).astype(BF16)
            for j, g in enumerate(query):
                a[g] = jnp.where(mask_pieces[l][g], piece(x, j), a[g])
        a_ref[u, h] = jnp.concatenate(a, axis=0)


def _scan_outputs(u, d, rows, v_ref, cb_ref, p_ref, a_ref, st_ref, o_ref, n_heads, dk, dv):
    edge = CHUNK - 1 if d == 0 else 0
    for h in range(n_heads):
        ks = slice(h * dk, (h + 1) * dk)
        vs = slice(h * dv, (h + 1) * dv)
        v = v_ref[rows, vs]
        st = st_ref[d, h]
        o = _dot(a_ref[u, h], v) + _dot_nt(p_ref[u, START_SLOT, :, ks], st.astype(BF16))
        o_ref[rows, vs] = o.astype(o_ref.dtype)
        st_ref[d, h] = st * jnp.exp2(cb_ref[u, edge:edge + 1, ks]) + _dot_tn(v, p_ref[u, END_SLOT, :, ks])


def _scan_body(*refs, n_heads, dk, dv, derive_k, zero_init, emit_state):
    refs = list(refs)
    q_refs = (refs.pop(0), refs.pop(0))
    k_refs = (None, None) if derive_k else (refs.pop(0), refs.pop(0))
    v_refs = (refs.pop(0), refs.pop(0))
    g_refs = (refs.pop(0), refs.pop(0))
    m_ref, lvl_ref = refs.pop(0), refs.pop(0)
    s0_ref = None if zero_init else refs.pop(0)
    o_refs = (refs.pop(0), refs.pop(0))
    sout_ref = refs.pop(0) if emit_state else None
    st_ref, cb_ref, kf_ref, kb_ref, p_ref, a_ref = refs
    n = pl.program_id(1)

    @pl.when(n == 0)
    def _():
        for d in range(2):
            for h in range(n_heads):
                st_ref[d, h] = jnp.zeros((dv, dk), F32) if zero_init else s0_ref[d, h].T

    cps = CHUNKS_PER_STEP
    rows = lambda c: slice(c * CHUNK, (c + 1) * CHUNK)
    masks = [[lvl_ref[d].astype(F32).astype(BF16) == l for l in range(N_LEVELS + 1)] for d in range(2)]
    row = lax.broadcasted_iota(jnp.int32, (ROW_GROUP, q_refs[0].shape[-1]), 0)
    side = [((row >> l) & 1).astype(F32).astype(BF16) for l in range(N_LEVELS) if (1 << l) < ROW_GROUP]
    units = [(d, c) for c in range(cps) for d in range(2)]
    for i in range(len(units) + 1):
        if i < len(units):
            d, c = units[i]
            _scan_prepare(d * cps + c, d, rows(c), q_refs[d], k_refs[d], g_refs[d], m_ref, side, cb_ref, kf_ref,
                          kb_ref, p_ref)
        if i >= 1:
            d, c = units[i - 1]
            _scan_scores(d * cps + c, d, rows(c), q_refs[d], masks[d], kb_ref, p_ref, a_ref, n_heads, dk)
    for d in range(2):
        for c in (range(cps) if d == 0 else reversed(range(cps))):
            _scan_outputs(d * cps + c, d, rows(c), v_refs[d], cb_ref, p_ref, a_ref, st_ref, o_refs[d], n_heads,
                          dk, dv)

    if emit_state:
        @pl.when(n == pl.num_programs(1) - 1)
        def _():
            for d in range(2):
                for h in range(n_heads):
                    sout_ref[d, h] = st_ref[d, h].T


def _scan(q, k, v, g_f, g_b, s0, consts, n_heads, dk, dv, emit_state):
    b, l, _ = q.shape
    step = CHUNKS_PER_STEP * CHUNK
    n = l // step
    slots = 2 * CHUNKS_PER_STEP
    m_all, lvl = consts
    fwd = lambda w: pl.BlockSpec((None, step, w), lambda i, j: (i, j, 0))
    bwd = lambda w: pl.BlockSpec((None, step, w), lambda i, j: (i, n - 1 - j, 0))
    const = lambda shape: pl.BlockSpec(shape, lambda i, j: (0,) * len(shape))
    state = pl.BlockSpec((None, 2, n_heads, dk, dv), lambda i, j: (i, 0, 0, 0, 0))
    wk, wv = n_heads * dk, n_heads * dv
    args, in_specs = [q, q], [fwd(wk), bwd(wk)]
    if k is not None:
        args += [k, k]
        in_specs += [fwd(wk), bwd(wk)]
    args += [v, v, g_f, g_b, m_all, lvl]
    in_specs += [fwd(wv), bwd(wv), fwd(wk), bwd(wk), const(m_all.shape), const(lvl.shape)]
    if s0 is not None:
        args.append(s0)
        in_specs.append(state)
    out_specs = [fwd(wv), bwd(wv)]
    out_shape = [jax.ShapeDtypeStruct((b, l, wv), BF16)] * 2
    if emit_state:
        out_specs.append(state)
        out_shape.append(jax.ShapeDtypeStruct((b, 2, n_heads, dk, dv), F32))
    body = functools.partial(_scan_body, n_heads=n_heads, dk=dk, dv=dv, derive_k=k is None,
                             zero_init=s0 is None, emit_state=emit_state)
    return pl.pallas_call(
        body,
        grid=(b, n),
        in_specs=in_specs,
        out_specs=out_specs,
        out_shape=out_shape,
        scratch_shapes=[pltpu.VMEM((2, n_heads, dv, dk), F32),
                        pltpu.VMEM((slots, (1 + N_MXU_LEVELS) * CHUNK, wk), F32),
                        pltpu.VMEM((slots, CHUNK, wk), F32), pltpu.VMEM((slots, CHUNK, wk), BF16),
                        pltpu.VMEM((slots, N_LEVELS + 2, CHUNK, wk), BF16),
                        pltpu.VMEM((slots, n_heads, CHUNK, CHUNK), BF16)],
        compiler_params=pltpu.CompilerParams(dimension_semantics=("arbitrary", "arbitrary"),
                                             vmem_limit_bytes=VMEM_LIMIT),
        name="scan_hgrn" if k is None else "scan_gla",
    )(*args)


def _post_body(ohf_ref, ohb_ref, ogf_ref, ogb_ref, hgate_ref, gr_ref, sa_ref, sb_ref, x_ref, mod_ref,
               hgn_ref, glan_ref, n2_ref, wa_ref, wb_ref, wo_ref, wr_ref,
               x1_ref, rows_ref, group_ref):
    tm = x_ref.shape[0]
    for r0 in range(0, tm, POST_PART):
        _post_part(slice(r0, r0 + POST_PART), ohf_ref, ohb_ref, ogf_ref, ogb_ref, hgate_ref, gr_ref, sa_ref, sb_ref,
                   x_ref, mod_ref, hgn_ref, glan_ref, n2_ref, wa_ref, wb_ref, wo_ref, wr_ref,
                   x1_ref, rows_ref, group_ref)


def _post_part(rs, ohf_ref, ohb_ref, ogf_ref, ogb_ref, hgate_ref, gr_ref, sa_ref, sb_ref, x_ref, mod_ref,
               hgn_ref, glan_ref, n2_ref, wa_ref, wb_ref, wo_ref, wr_ref,
               x1_ref, rows_ref, group_ref):
    n = rs.stop - rs.start

    def normed(of_ref, ob_ref, g_ref, gate_ref, s):
        o = of_ref[rs, s].astype(F32) + ob_ref[rs, s].astype(F32)
        return (_rmsnorm(o, g_ref[...]) * gate_ref[rs, s].astype(F32)).astype(BF16)

    chunks = [slice(c * MXU_DEPTH, (c + 1) * MXU_DEPTH) for c in range(D_MODEL // MXU_DEPTH)]
    acc = lambda total, part: part if total is None else total + part
    t_hg = t_gla = mix = None
    for cs in chunks:
        heads = range(cs.start // HG_DV, cs.stop // HG_DV)
        y_hg = jnp.concatenate([normed(ohf_ref, ohb_ref, hgn_ref, hgate_ref, slice(h * HG_DV, (h + 1) * HG_DV))
                                for h in heads], axis=1)
        t_hg = acc(t_hg, _dot(y_hg, wa_ref[cs, :]))
        heads = range(cs.start // GLA_DV, cs.stop // GLA_DV)
        y_gla = jnp.concatenate([normed(ogf_ref, ogb_ref, glan_ref, gr_ref, slice(h * GLA_DV, (h + 1) * GLA_DV))
                                 for h in heads], axis=1)
        t_gla = acc(t_gla, _dot(y_gla, wb_ref[cs, :]))
    for cs in chunks:
        merged = sa_ref[rs, cs].astype(F32) * t_hg[:, cs] + sb_ref[rs, cs].astype(F32) * t_gla[:, cs]
        mix = acc(mix, _dot(merged.astype(BF16), wo_ref[cs, :]))
    x1 =x_ref[rs, :] + mod_ref[2:3, :] * mix
    x1_ref[rs, :] = x1
    h2 = _rmsnorm(x1, n2_ref[...]) * (1.0 + mod_ref[4:5, :]) + mod_ref[3:4, :]
    h2_hi = h2.astype(BF16)
    token_rows = lambda s: pl.ds(rs.start * ROW_SUBLANES + s, n, stride=ROW_SUBLANES)
    for s in range(D_MODEL // 128):
        rows_ref[token_rows(s), :] = h2[:, s * 128:(s + 1) * 128]
    for s in range(COMB_SUBLANE + 1, ROW_SUBLANES):
        rows_ref[token_rows(s), :] = jnp.zeros((n, 128), F32)

    h2_lo = (h2 - h2_hi.astype(F32)).astype(BF16)
    lt = _dot_nt(wr_ref[0], h2_hi) + _dot_nt(wr_ref[1], h2_hi) + _dot_nt(wr_ref[0], h2_lo)
    row = lambda r: lt[r:r + 1, :]
    top = lambda vals: functools.reduce(jnp.maximum, vals)

    def first_at(vals, m):
        idx = jnp.full(m.shape, len(vals), jnp.int32)
        for j in reversed(range(len(vals))):
            idx = jnp.where(vals[j] == m, j, idx)
        return idx

    gl = [row(g) for g in range(N_GROUPS)]
    gmax = top(gl)
    p_top = 1.0 / functools.reduce(jnp.add, [jnp.exp(v - gmax) for v in gl])
    g_idx = first_at(gl, gmax)
    sel = []
    for j in range(EXPERTS_PER_GROUP):
        v = row(EXPERT_LANE0 + j)
        for g in range(1, N_GROUPS):
            v = jnp.where(g_idx == g, row(EXPERT_LANE0 + g * EXPERTS_PER_GROUP + j), v)
        sel.append(v)
    m1 = top(sel)
    i1 = first_at(sel, m1)
    sel2 = [jnp.where(i1 == j, -jnp.inf, v) for j, v in enumerate(sel)]
    m2 = top(sel2)
    i2 = first_at(sel2, m2)
    r = jnp.exp(m2 - m1)
    w1 = p_top / (1.0 + r)
    lane0 = EXPERT_LANE0 + g_idx * EXPERTS_PER_GROUP
    rid = lax.broadcasted_iota(jnp.int32, lt.shape, 0)
    comb_t = jnp.where(rid == lane0 + i1, w1, jnp.where(rid == lane0 + i2, w1 * r, 0.0))
    rows_ref[token_rows(COMB_SUBLANE), :] = comb_t.T
    group_ref[:, rs] = g_idx


def _post(ohf, ohb, ogf, ogb, hgate, gr, sa, sb, x, mod, tiles_per_row, hgn, glan, n2, wa, wb, wo, wr):
    t = x.shape[0]
    tm = TM_POST
    const = lambda shape: pl.BlockSpec(shape, lambda i: (0,) * len(shape))
    tok = lambda w: pl.BlockSpec((tm, w), lambda i: (i, 0))
    return pl.pallas_call(
        _post_body,
        grid=(t // tm,),
        in_specs=[tok(1024)] * 9
                 + [pl.BlockSpec((None, 6, D_MODEL), lambda i: (i // tiles_per_row, 0, 0)),
                    const((1, HG_DV)), const((1, GLA_DV)), const((1, D_MODEL)),
                    const((1024, 1024)), const((1024, 1024)), const((1024, 1024)),
                    const((2, ROUTER_LANES, D_MODEL))],
        out_specs=[tok(1024), pl.BlockSpec((tm * ROW_SUBLANES, 128), lambda i: (i, 0)),
                   pl.BlockSpec((None, 1, tm), lambda i: (i, 0, 0))],
        out_shape=[jax.ShapeDtypeStruct((t, D_MODEL), F32), jax.ShapeDtypeStruct((t * ROW_SUBLANES, 128), F32),
                   jax.ShapeDtypeStruct((t // tm, 1, tm), jnp.int32)],
        compiler_params=pltpu.CompilerParams(vmem_limit_bytes=VMEM_LIMIT),
        name="post",
    )(ohf, ohb, ogf, ogb, hgate, gr, sa, sb, x, mod, hgn, glan, n2, wa, wb, wo, wr)


def _route_plan(group, t):
    tr, blk = TR_MOE, PLAN_BLOCK
    nt, nb = t // tr + N_GROUPS, t // blk
    i32 = jnp.int32
    gids = jnp.arange(N_GROUPS, dtype=i32)
    onehot = (group[:, None] == gids[None, :]).astype(i32).reshape(nb, blk, N_GROUPS)
    within = jnp.cumsum(onehot, axis=1)
    blk_end = jnp.cumsum(within[:, -1, :], axis=0)
    blk_start = blk_end - within[:, -1, :]
    counts = blk_end[-1]
    tiles = (counts + tr - 1) // tr
    tile_end = jnp.cumsum(tiles)
    tile_group = jnp.minimum(jnp.sum(jnp.arange(nt, dtype=i32)[:, None] >= tile_end[None, :], axis=1),
                             N_GROUPS - 1).astype(i32)
    row = jnp.arange(nt * tr, dtype=i32)
    gsel = (jnp.repeat(tile_group, tr)[:, None] == gids[None, :]).astype(i32)
    pick = lambda per_group: jnp.sum(gsel * per_group[None, :], axis=1)
    k = row - pick(tile_end - tiles) * tr
    valid = k < pick(counts)
    per_block = lambda table: jnp.sum(gsel[:, None, :] * table[None, :, :], axis=2)
    b_row = jnp.minimum(jnp.sum(per_block(blk_end) <= k[:, None], axis=1), nb - 1)
    bsel = (b_row[:, None] == jnp.arange(nb, dtype=i32)[None, :]).astype(i32)
    k_local = k - jnp.sum(bsel * per_block(blk_start), axis=1)
    sel = (bsel[:, :, None] * gsel[:, None, :]).reshape(nt * tr, nb * N_GROUPS).astype(F32)
    table = jnp.transpose(within, (0, 2, 1)).reshape(nb * N_GROUPS, blk).astype(F32)
    ranks = jnp.dot(sel, table)
    t_local = jnp.sum(ranks <= k_local[:, None].astype(F32), axis=1).astype(i32)
    row_src = jnp.where(valid, b_row * blk + t_local, 0)
    spare = t + ((row // tr) % 2) * tr + row % tr
    return tile_group, row_src, jnp.where(valid, row_src, spare), tile_end[-1:].astype(i32)


def _moe_body(tg_ref, src_ref, dst_ref, used_ref, rows_hbm, wg_ref, wu_ref, wd_ref, out_hbm, xbuf, ybuf, gsem, ssem):
    tr = TR_MOE
    i = pl.program_id(0)
    n_used = used_ref[0]
    slot = i % 2

    def gather_row(tile, slot_, r):
        tok = src_ref[tile * tr + r]
        return pltpu.make_async_copy(rows_hbm.at[pl.ds(pl.multiple_of(tok * ROW_SUBLANES, ROW_SUBLANES), ROW_SUBLANES)],
                                     xbuf.at[slot_, pl.ds(r * ROW_SUBLANES, ROW_SUBLANES)], gsem.at[slot_])

    def scatter_row(tile, slot_, r):
        tok = dst_ref[tile * tr + r]
        return pltpu.make_async_copy(ybuf.at[slot_, pl.ds(r * OUT_SUBLANES, OUT_SUBLANES)],
                                     out_hbm.at[pl.ds(pl.multiple_of(tok * OUT_SUBLANES, OUT_SUBLANES), OUT_SUBLANES)],
                                     ssem.at[slot_])

    def for_rows(fn):
        def body(r, carry):
            fn(r)
            return carry
        lax.fori_loop(0, tr, body, 0, unroll=8)

    def gathered(slot_):
        return pltpu.make_async_copy(rows_hbm.at[pl.ds(0, tr * ROW_SUBLANES)], xbuf.at[slot_], gsem.at[slot_])

    def scattered(slot_):
        return pltpu.make_async_copy(ybuf.at[slot_], out_hbm.at[pl.ds(0, tr * OUT_SUBLANES)], ssem.at[slot_])

    @pl.when(i == 0)
    def _():
        for_rows(lambda r: gather_row(0, 0, r).start())
        n_tok = out_hbm.shape[0] // OUT_SUBLANES - 2 * tr
        ybuf[...] = jnp.zeros_like(ybuf)
        for s in range(2):
            spare = pltpu.make_async_copy(
                ybuf.at[s], out_hbm.at[pl.ds((n_tok + s * tr) * OUT_SUBLANES, tr * OUT_SUBLANES)], ssem.at[s])
            spare.start()
            spare.wait()

    @pl.when(i < n_used)
    def _():
        @pl.when(i + 1 < n_used)
        def _():
            for_rows(lambda r: gather_row(i + 1, 1 - slot, r).start())

        gathered(slot).wait()

        @pl.when(i >= 2)
        def _():
            scattered(slot).wait()

        x = jnp.concatenate([xbuf[slot, pl.ds(s, tr, stride=ROW_SUBLANES), :] for s in range(D_MODEL // 128)],
                            axis=1).astype(BF16)
        comb = xbuf[slot, pl.ds(COMB_SUBLANE, tr, stride=ROW_SUBLANES), :]
        lane = lax.broadcasted_iota(jnp.int32, comb.shape, 1)
        lane0 = EXPERT_LANE0 + tg_ref[i] * EXPERTS_PER_GROUP
        acc = jnp.zeros((tr, D_MODEL), F32)
        for e in range(EXPERTS_PER_GROUP):
            col = jnp.sum(jnp.where(lane == lane0 + e, comb, 0.0), axis=-1, keepdims=True)
            hid = (_silu(_dot(x, wg_ref[e])) * _dot(x, wu_ref[e]) * col).astype(BF16)
            acc = acc + _dot(hid, wd_ref[e])

        for s in range(D_MODEL // 128):
            ybuf[slot, pl.ds(s, tr, stride=OUT_SUBLANES), :] = acc[:, s * 128:(s + 1) * 128]

        for_rows(lambda r: scatter_row(i, slot, r).start())

        @pl.when(i == n_used - 1)
        def _():
            scattered(slot).wait()
            scattered(1 - slot).wait()


def _moe(rows, plan, wg, wu, wd, t):
    tr = TR_MOE
    assert t // tr >= 2
    nt = plan[0].shape[0]
    group_w = lambda shape: pl.BlockSpec((EXPERTS_PER_GROUP,) + shape, lambda i, tg, src, dst, used: (tg[i], 0, 0))
    grid_spec = pltpu.PrefetchScalarGridSpec(
        num_scalar_prefetch=4,
        grid=(nt,),
        in_specs=[pl.BlockSpec(memory_space=pl.ANY),
                  group_w((D_MODEL, D_EXPERT)), group_w((D_MODEL, D_EXPERT)), group_w((D_EXPERT, D_MODEL))],
        out_specs=pl.BlockSpec(memory_space=pl.ANY),
        scratch_shapes=[pltpu.VMEM((2, tr * ROW_SUBLANES, 128), F32), pltpu.VMEM((2, tr * OUT_SUBLANES, 128), F32),
                        pltpu.SemaphoreType.DMA((2,)), pltpu.SemaphoreType.DMA((2,))])
    return pl.pallas_call(
        _moe_body,
        grid_spec=grid_spec,
        out_shape=jax.ShapeDtypeStruct(((t + 2 * tr) * OUT_SUBLANES, 128), F32),
        compiler_params=pltpu.CompilerParams(dimension_semantics=("arbitrary",), vmem_limit_bytes=VMEM_LIMIT),
        name="moe",
    )(*plan, rows, wg, wu, wd)


def _final_body(x1_ref, moe_ref, mod_ref, fn_ref, y_ref):
    tm = x1_ref.shape[0]
    moe = jnp.concatenate([moe_ref[pl.ds(s, tm, stride=OUT_SUBLANES), :] for s in range(D_MODEL // 128)], axis=1)
    y_ref[...] = _rmsnorm(x1_ref[...] + mod_ref[5:6, :] * moe, fn_ref[...])


def _final(x1, moe, mod, tiles_per_row, fn):
    t = x1.shape[0]
    tm = TM_FINAL
    return pl.pallas_call(
        _final_body,
        grid=(t // tm,),
        in_specs=[pl.BlockSpec((tm, D_MODEL), lambda i: (i, 0)),
                  pl.BlockSpec((tm * OUT_SUBLANES, 128), lambda i: (i, 0)),
                  pl.BlockSpec((None, 6, D_MODEL), lambda i: (i // tiles_per_row, 0, 0)),
                  pl.BlockSpec((1, D_MODEL), lambda i: (0, 0))],
        out_specs=pl.BlockSpec((tm, D_MODEL), lambda i: (i, 0)),
        out_shape=jax.ShapeDtypeStruct((t, D_MODEL), F32),
        compiler_params=pltpu.CompilerParams(vmem_limit_bytes=VMEM_LIMIT),
        name="final",
    )(x1, moe, mod, fn)


def _trunk_layer(x, mod, s0_hg, s0_gla, emit_state, w):
    b, l, _ = x.shape
    xt = x.reshape(b * l, D_MODEL)
    per_row = lambda tm: (l // tm) if mod.shape[0] > 1 else (b * l // tm)
    (hq, gf, gb, hv, hgate, gq, gk, gv, gr, laf, lab, sa, sb) = _inproj(
        xt, mod, per_row(TM_PROJ), w["n1"], w["lbp"], w["wmain"], w["waup"], w["ba"])
    seq = lambda a: a.reshape(b, l, a.shape[-1])
    hg = _scan(seq(hq), None, seq(hv), seq(gf), seq(gb), s0_hg, w["scan_consts"], HG_H, HG_DK, HG_DV,
               emit_state)
    gla = _scan(seq(gq), seq(gk), seq(gv), seq(laf), seq(lab), s0_gla, w["scan_consts"], GLA_H, GLA_DK,
                GLA_DV, emit_state)
    flat = lambda a: a.reshape(b * l, a.shape[-1])
    x1, rows, group = _post(flat(hg[0]), flat(hg[1]), flat(gla[0]), flat(gla[1]), hgate, gr, sa, sb, xt, mod,
                           per_row(TM_POST), w["hgn"], w["glan"], w["n2"], w["wa"], w["wb"], w["wo"], w["wr"])
    plan = _route_plan(group.reshape(b * l), b * l)
    moe = _moe(rows, plan, w["wg"], w["wu"], w["wd"], b * l)
    y = _final(x1, moe, mod, per_row(TM_FINAL), w["fn"])
    states = (hg[2], gla[2]) if emit_state else (None, None)
    return y.reshape(b, l, D_MODEL), states


def kernel(x_prompt, x_sample, state_hgrn, state_gla, c, c_ctx, ada_w, ada_b, norm1_g, norm2_g, w_in,
           hg_lb_param, hg_norm_g, gla_wa_up, gla_ba, gla_norm_g, w_br_a, w_br_b, w_out,
           w_router_group, w_router_expert, w_exp_gate, w_exp_up, w_exp_down, final_norm_g):
    nb = c.shape[0]
    cond = jnp.concatenate([c, c_ctx[None, :], jnp.zeros((8 - nb - 1, D_MODEL), F32)], axis=0)
    mod = _modulation(cond, ada_w[0], ada_b).reshape(8, 6, D_MODEL)

    zeros = jnp.zeros((GLA_RANK, GLA_H * GLA_DK), F32)
    router = jnp.concatenate(
        [w_router_group[0], jnp.transpose(w_router_expert[0], (1, 0, 2)).reshape(D_MODEL, N_EXPERTS),
         jnp.zeros((D_MODEL, ROUTER_LANES - N_GROUPS - N_EXPERTS), F32)], axis=1)
    w = {
        "n1": norm1_g, "n2": norm2_g, "fn": final_norm_g[None, :], "lbp": hg_lb_param,
        "wmain": w_in[0].astype(BF16),
        "waup": jnp.concatenate([jnp.concatenate([gla_wa_up[0, 0], zeros], axis=1),
                                 jnp.concatenate([zeros, gla_wa_up[0, 1]], axis=1)], axis=0).astype(BF16),
        "ba": gla_ba[0].reshape(1, 2 * GLA_H * GLA_DK),
        "hgn": hg_norm_g, "glan": gla_norm_g,
        "wa": w_br_a[0].astype(BF16), "wb": w_br_b[0].astype(BF16), "wo": w_out[0].astype(BF16),
        "wr": jnp.stack([router.astype(BF16), (router - router.astype(BF16).astype(F32)).astype(BF16)]
                        ).transpose(0, 2, 1),
        "wg": w_exp_gate[0].astype(BF16), "wu": w_exp_up[0].astype(BF16), "wd": w_exp_down[0].astype(BF16),
        "scan_consts": _scan_constants(),
    }
    y_prompt, (st_hg, st_gla) = _trunk_layer(x_prompt, mod[nb:nb + 1], None, None, True, w)
    y_sample, _ = _trunk_layer(x_sample, mod[:nb], state_hgrn[:, 0], state_gla[:, 0], False, w)
    return y_prompt, y_sample, st_hg[:, None], st_gla[:, None]
```

```python
import functools

import numpy as np
import jax
import jax.numpy as jnp
from jax import lax
from jax.experimental import pallas as pl
from jax.experimental.pallas import tpu as pltpu

F32 = jnp.float32
BF16 = jnp.bfloat16

D_MODEL = 1024
CHUNK = 64
N_LEVELS = 6
CHUNKS_PER_STEP = 4
HG_H, HG_DK, HG_DV = 8, 128, 128
GLA_H, GLA_DK, GLA_DV = 4, 128, 256
GLA_RANK = 16
GLA_GATE_NORM = 16.0
N_GROUPS = 4
EXPERTS_PER_GROUP = 8
N_EXPERTS = N_GROUPS * EXPERTS_PER_GROUP
D_EXPERT = D_MODEL // 4
EPS = 1e-6
ROUTER_LANES = 128
EXPERT_LANE0 = N_GROUPS
VMEM_LIMIT = 56 * 1024 * 1024
MXU_DEPTH = 256

TM_PROJ = 256
TM_POST = 256
POST_PART = 256
TM_FINAL = 1024
TR_MOE = 256
PLAN_BLOCK = 256
OUT_SUBLANES = D_MODEL // 128
COMB_SUBLANE = OUT_SUBLANES
ROW_SUBLANES = 16


def _sigmoid(x):
    return 1.0 / (1.0 + jnp.exp(-x))


def _silu(x):
    return x * _sigmoid(x)


def _log_sigmoid(x):
    return jnp.minimum(x, 0.0) - jnp.log(1.0 + jnp.exp(-jnp.abs(x)))


def _rmsnorm(x, g):
    return x * lax.rsqrt(jnp.mean(x * x, axis=-1, keepdims=True) + EPS) * g


def _dot(a, b):
    return jnp.dot(a, b, preferred_element_type=F32)


def _dot_nt(a, b):
    return lax.dot_general(a, b, (((1,), (1,)), ((), ())), preferred_element_type=F32)


def _dot_tn(a, b):
    return lax.dot_general(a, b, (((0,), (0,)), ((), ())), preferred_element_type=F32)


def _mod_body(c_ref, w_ref, b_ref, o_ref):
    s, w = _silu(c_ref[...]), w_ref[...]
    s_hi, w_hi = s.astype(BF16), w.astype(BF16)
    s_lo, w_lo = (s - s_hi.astype(F32)).astype(BF16), (w - w_hi.astype(F32)).astype(BF16)
    o_ref[...] = _dot(s_hi, w_hi) + _dot(s_hi, w_lo) + _dot(s_lo, w_hi) + b_ref[...]


def _modulation(cond, ada_w, ada_b):
    rows, n = cond.shape[0], ada_w.shape[1]
    tn = n // 4
    return pl.pallas_call(
        _mod_body,
        grid=(n // tn,),
        in_specs=[pl.BlockSpec((rows, D_MODEL), lambda j: (0, 0)),
                  pl.BlockSpec((D_MODEL, tn), lambda j: (0, j)),
                  pl.BlockSpec((1, tn), lambda j: (0, j))],
        out_specs=pl.BlockSpec((rows, tn), lambda j: (0, j)),
        out_shape=jax.ShapeDtypeStruct((rows, n), F32),
        compiler_params=pltpu.CompilerParams(vmem_limit_bytes=VMEM_LIMIT),
        name="modulation",
    )(cond, ada_w, ada_b)


_C_HQ, _C_HFF, _C_HFB, _C_HI, _C_HGATE = 0, 1024, 2048, 3072, 4096
_C_GQ, _C_GK, _C_GV, _C_GR, _C_TAIL = 5120, 5632, 6144, 7168, 8192
GA_WIDTH = 2 * GLA_RANK
W_TAIL = GA_WIDTH + 2 * D_MODEL
W_MAIN = _C_TAIL + W_TAIL


def _inproj_body(x_ref, mod_ref, n1_ref, lbp_ref, wmain_ref, waup_ref, ba_ref,
                 hq_ref, gf_ref, gb_ref, hv_ref, hgate_ref, gq_ref, gk_ref, gv_ref, gr_ref,
                 laf_ref, lab_ref, sa_ref, sb_ref):
    x = x_ref[...]
    h = _rmsnorm(x, n1_ref[...]) * (1.0 + mod_ref[1:2, :]) + mod_ref[0:1, :]
    hb = h.astype(BF16)

    def proj(c0, w):
        return _dot(hb, wmain_ref[:, c0:c0 + w])

    p0, p1 = lbp_ref[0:1, :], lbp_ref[1:2, :]
    pm = jnp.maximum(p0, p1)
    e0, e1 = jnp.exp(p0 - pm), jnp.exp(p1 - pm)
    lb = e0 / (e0 + e1)

    hq_ref[...] = (proj(_C_HQ, 1024) * HG_DK ** -0.5).astype(BF16)
    gf_ref[...] = jnp.log(lb + (1.0 - lb) * _sigmoid(proj(_C_HFF, 1024)))
    gb_ref[...] = jnp.log(lb + (1.0 - lb) * _sigmoid(proj(_C_HFB, 1024)))
    hv_ref[...] = proj(_C_HI, 1024).astype(BF16)
    hgate_ref[...] = _silu(proj(_C_HGATE, 1024)).astype(BF16)
    gq_ref[...] = (proj(_C_GQ, 512) * GLA_DK ** -0.5).astype(BF16)
    gk_ref[...] = proj(_C_GK, 512).astype(BF16)
    gv_ref[...] = proj(_C_GV, 1024).astype(BF16)
    gr_ref[...] = _silu(proj(_C_GR, 1024)).astype(BF16)
    tail = proj(_C_TAIL, W_TAIL)
    sa_ref[...] = _sigmoid(tail[:, GA_WIDTH:GA_WIDTH + D_MODEL]).astype(BF16)
    sb_ref[...] = _sigmoid(tail[:, GA_WIDTH + D_MODEL:]).astype(BF16)

    ga = tail[:, :GA_WIDTH].astype(BF16)
    la = _log_sigmoid(_dot(ga, waup_ref[...]) + ba_ref[...]) * (1.0 / GLA_GATE_NORM)
    laf_ref[...] = la[:, :512]
    lab_ref[...] = la[:, 512:]


def _inproj(x, mod, tiles_per_row, n1, lbp, wmain, waup, ba):
    t = x.shape[0]
    tm = TM_PROJ
    const = lambda shape: pl.BlockSpec(shape, lambda i: (0,) * len(shape))
    tok = lambda w: pl.BlockSpec((tm, w), lambda i: (i, 0))
    widths = (1024, 1024, 1024, 1024, 1024, 512, 512, 1024, 1024, 512, 512, 1024, 1024)
    dtypes = (BF16, F32, F32, BF16, BF16, BF16, BF16, BF16, BF16, F32, F32, BF16, BF16)
    return pl.pallas_call(
        _inproj_body,
        grid=(t // tm,),
        in_specs=[tok(D_MODEL),
                  pl.BlockSpec((None, 6, D_MODEL), lambda i: (i // tiles_per_row, 0, 0)),
                  const((1, D_MODEL)), const((2, D_MODEL)),
                  pl.BlockSpec((D_MODEL, W_MAIN), lambda i: (0, 0), pipeline_mode=pl.Buffered(1)),
                  const((GA_WIDTH, 1024)), const((1, 1024))],
        out_specs=[tok(w) for w in widths],
        out_shape=[jax.ShapeDtypeStruct((t, w), dt) for w, dt in zip(widths, dtypes)],
        compiler_params=pltpu.CompilerParams(vmem_limit_bytes=VMEM_LIMIT),
        name="inproj",
    )(x, mod, n1, lbp, wmain, waup, ba)


N_MXU_LEVELS = 3
ROW_GROUP = 16
START_SLOT, END_SLOT = N_LEVELS, N_LEVELS + 1
LOG2E = 1.4426950408889634


def _boundary_row(t, h, d):
    base = (t // (2 * h)) * 2 * h
    return base + h - 1 if d == 0 else base + h


def _scan_constants():
    c = CHUNK
    tri = np.tril(np.ones((c, c), np.float32))
    m = [tri] + [np.abs(tri - tri[[_boundary_row(t, 1 << l, 0) for t in range(c)]]) for l in range(N_MXU_LEVELS)]
    m = np.stack(m)
    lvl = np.full((c, c), N_LEVELS + 1, np.int32)
    for t in range(c):
        lvl[t, t] = N_LEVELS
        for s in range(t):
            lvl[t, s] = int(np.floor(np.log2(t ^ s)))
    m_fwd = m.reshape(-1, c)
    m_bwd = m[:, ::-1, ::-1].reshape(-1, c)
    m_both = np.stack([m_fwd, m_bwd])
    return (jnp.asarray(np.concatenate([m_both, m_both], axis=2), BF16), jnp.asarray(np.stack([lvl, lvl.T])))


def _scan_prepare(u, d, rows, q_ref, k_ref, g_ref, m_ref, side, cb_ref, kf_ref, kb_ref, p_ref):
    c = CHUNK
    w = g_ref.shape[-1]
    g = g_ref[rows, :]
    gs = g * LOG2E
    g1 = gs.astype(BF16)
    g2 = (gs - g1.astype(F32)).astype(BF16)
    cb_ref[u] = _dot(m_ref[d], jnp.concatenate([g1, g2], axis=0))
    k32 = (1.0 - jnp.exp(g)) if k_ref is None else k_ref[rows, :].astype(F32)
    kf_ref[u] = k32
    kb_ref[u] = k32.astype(BF16)
    bcast = lambda r, n: jnp.broadcast_to(cb_ref[u, r:r + 1, :], (n, w))
    for i in range(c // ROW_GROUP):
        t0 = ROW_GROUP * i
        grp = slice(t0, t0 + ROW_GROUP)
        cum = cb_ref[u, grp, :]
        q = q_ref[rows.start + t0:rows.start + t0 + ROW_GROUP, :]
        k = kb_ref[u, grp, :]
        for l in range(N_LEVELS):
            h = 1 << l
            if l < N_MXU_LEVELS:
                ex = cb_ref[u, c * (l + 1) + t0:c * (l + 1) + t0 + ROW_GROUP, :]
            else:
                ex = []
                for t in (t0, t0 + 8):
                    half = slice(t - t0, t - t0 + 8)
                    bnd = bcast(_boundary_row(t, h, d), 8)
                    ex.append(cum[half] - bnd if ((t & h) != 0) == (d == 0) else bnd - cum[half])
                ex = jnp.concatenate(ex, axis=0)
            if h >= ROW_GROUP:
                src = q if ((t0 & h) != 0) == (d == 0) else k
            else:
                upper = side[l] > 0
                src = jnp.where(upper, q, k) if d == 0 else jnp.where(upper, k, q)
            p_ref[u, l, grp, :] = src * jnp.exp2(ex).astype(BF16)
        p_ref[u, START_SLOT, grp, :] = q * jnp.exp2(cum).astype(BF16)
        edge = c - 1 if d == 0 else 0
        p_ref[u, END_SLOT, grp, :] = (kf_ref[u, grp, :] * jnp.exp2(bcast(edge, ROW_GROUP) - cum)).astype(BF16)


def _scan_scores(u, d, rows, q_ref, masks, kb_ref, p_ref, a_ref, n_heads, dk):
    groups = range(CHUNK // ROW_GROUP)
    piece = lambda x, g: x[g * ROW_GROUP:(g + 1) * ROW_GROUP]
    mask_pieces = [[piece(m, g) for g in groups] for m in masks]
    for h in range(n_heads):
        ks = slice(h * dk, (h + 1) * dk)
        diag = _dot_nt(q_ref[rows, ks], kb_ref[u, :, ks]).astype(BF16)
        a = [jnp.where(mask_pieces[N_LEVELS][g], piece(diag, g), jnp.zeros((), BF16)) for g in groups]
        for l in range(N_LEVELS):
            p = p_ref[u, l, :, ks]
            if (1 << l) < ROW_GROUP:
                query = list(groups)
                x = _dot_nt(p, p).astype(BF16)
            else:
                query = [g for g in groups if ((g * ROW_GROUP & (1 << l)) != 0) == (d == 0)]
                x = _dot_nt(jnp.concatenate([piece(p, g) for g in query], axis=0), p).astype(BF16)
            for j, g in enumerate(query):
                a[g] = jnp.where(mask_pieces[l][g], piece(x, j), a[g])
        a_ref[u, h] = jnp.concatenate(a, axis=0)


def _scan_outputs(u, d, rows, v_ref, cb_ref, p_ref, a_ref, st_ref, o_ref, n_heads, dk, dv):
    edge = CHUNK - 1 if d == 0 else 0
    for h in range(n_heads):
        ks = slice(h * dk, (h + 1) * dk)
        vs = slice(h * dv, (h + 1) * dv)
        v = v_ref[rows, vs]
        st = st_ref[d, h]
        o = _dot(a_ref[u, h], v) + _dot_nt(p_ref[u, START_SLOT, :, ks], st.astype(BF16))
        o_ref[rows, vs] = o.astype(o_ref.dtype)
        st_ref[d, h] = st * jnp.exp2(cb_ref[u, edge:edge + 1, ks]) + _dot_tn(v, p_ref[u, END_SLOT, :, ks])


def _scan_body(*refs, n_heads, dk, dv, derive_k, zero_init, emit_state):
    refs = list(refs)
    q_refs = (refs.pop(0), refs.pop(0))
    k_refs = (None, None) if derive_k else (refs.pop(0), refs.pop(0))
    v_refs = (refs.pop(0), refs.pop(0))
    g_refs = (refs.pop(0), refs.pop(0))
    m_ref, lvl_ref = refs.pop(0), refs.pop(0)
    s0_ref = None if zero_init else refs.pop(0)
    o_refs = (refs.pop(0), refs.pop(0))
    sout_ref = refs.pop(0) if emit_state else None
    st_ref, cb_ref, kf_ref, kb_ref, p_ref, a_ref = refs
    n = pl.program_id(1)

    @pl.when(n == 0)
    def _():
        for d in range(2):
            for h in range(n_heads):
                st_ref[d, h] = jnp.zeros((dv, dk), F32) if zero_init else s0_ref[d, h].T

    cps = CHUNKS_PER_STEP
    rows = lambda c: slice(c * CHUNK, (c + 1) * CHUNK)
    masks = [[lvl_ref[d].astype(F32).astype(BF16) == l for l in range(N_LEVELS + 1)] for d in range(2)]
    row = lax.broadcasted_iota(jnp.int32, (ROW_GROUP, q_refs[0].shape[-1]), 0)
    side = [((row >> l) & 1).astype(F32).astype(BF16) for l in range(N_LEVELS) if (1 << l) < ROW_GROUP]
    units = [(d, c) for c in range(cps) for d in range(2)]
    for i in range(len(units) + 1):
        if i < len(units):
            d, c = units[i]
            _scan_prepare(d * cps + c, d, rows(c), q_refs[d], k_refs[d], g_refs[d], m_ref, side, cb_ref, kf_ref,
                          kb_ref, p_ref)
        if i >= 1:
            d, c = units[i - 1]
            _scan_scores(d * cps + c, d, rows(c), q_refs[d], masks[d], kb_ref, p_ref, a_ref, n_heads, dk)
    for d in range(2):
        for c in (range(cps) if d == 0 else reversed(range(cps))):
            _scan_outputs(d * cps + c, d, rows(c), v_refs[d], cb_ref, p_ref, a_ref, st_ref, o_refs[d], n_heads,
                          dk, dv)

    if emit_state:
        @pl.when(n == pl.num_programs(1) - 1)
        def _():
            for d in range(2):
                for h in range(n_heads):
                    sout_ref[d, h] = st_ref[d, h].T


def _scan(q, k, v, g_f, g_b, s0, consts, n_heads, dk, dv, emit_state):
    b, l, _ = q.shape
    step = CHUNKS_PER_STEP * CHUNK
    n = l // step
    slots = 2 * CHUNKS_PER_STEP
    m_all, lvl = consts
    fwd = lambda w: pl.BlockSpec((None, step, w), lambda i, j: (i, j, 0))
    bwd = lambda w: pl.BlockSpec((None, step, w), lambda i, j: (i, n - 1 - j, 0))
    const = lambda shape: pl.BlockSpec(shape, lambda i, j: (0,) * len(shape))
    state = pl.BlockSpec((None, 2, n_heads, dk, dv), lambda i, j: (i, 0, 0, 0, 0))
    wk, wv = n_heads * dk, n_heads * dv
    args, in_specs = [q, q], [fwd(wk), bwd(wk)]
    if k is not None:
        args += [k, k]
        in_specs += [fwd(wk), bwd(wk)]
    args += [v, v, g_f, g_b, m_all, lvl]
    in_specs += [fwd(wv), bwd(wv), fwd(wk), bwd(wk), const(m_all.shape), const(lvl.shape)]
    if s0 is not None:
        args.append(s0)
        in_specs.append(state)
    out_specs = [fwd(wv), bwd(wv)]
    out_shape = [jax.ShapeDtypeStruct((b, l, wv), BF16)] * 2
    if emit_state:
        out_specs.append(state)
        out_shape.append(jax.ShapeDtypeStruct((b, 2, n_heads, dk, dv), F32))
    body = functools.partial(_scan_body, n_heads=n_heads, dk=dk, dv=dv, derive_k=k is None,
                             zero_init=s0 is None, emit_state=emit_state)
    return pl.pallas_call(
        body,
        grid=(b, n),
        in_specs=in_specs,
        out_specs=out_specs,
        out_shape=out_shape,
        scratch_shapes=[pltpu.VMEM((2, n_heads, dv, dk), F32),
                        pltpu.VMEM((slots, (1 + N_MXU_LEVELS) * CHUNK, wk), F32),
                        pltpu.VMEM((slots, CHUNK, wk), F32), pltpu.VMEM((slots, CHUNK, wk), BF16),
                        pltpu.VMEM((slots, N_LEVELS + 2, CHUNK, wk), BF16),
                        pltpu.VMEM((slots, n_heads, CHUNK, CHUNK), BF16)],
        compiler_params=pltpu.CompilerParams(dimension_semantics=("arbitrary", "arbitrary"),
                                             vmem_limit_bytes=VMEM_LIMIT),
        name="scan_hgrn" if k is None else "scan_gla",
    )(*args)


def _post_body(ohf_ref, ohb_ref, ogf_ref, ogb_ref, hgate_ref, gr_ref, sa_ref, sb_ref, x_ref, mod_ref,
               hgn_ref, glan_ref, n2_ref, wa_ref, wb_ref, wo_ref, wr_ref,
               x1_ref, rows_ref, group_ref):
    tm = x_ref.shape[0]
    for r0 in range(0, tm, POST_PART):
        _post_part(slice(r0, r0 + POST_PART), ohf_ref, ohb_ref, ogf_ref, ogb_ref, hgate_ref, gr_ref, sa_ref, sb_ref,
                   x_ref, mod_ref, hgn_ref, glan_ref, n2_ref, wa_ref, wb_ref, wo_ref, wr_ref,
                   x1_ref, rows_ref, group_ref)


def _post_part(rs, ohf_ref, ohb_ref, ogf_ref, ogb_ref, hgate_ref, gr_ref, sa_ref, sb_ref, x_ref, mod_ref,
               hgn_ref, glan_ref, n2_ref, wa_ref, wb_ref, wo_ref, wr_ref,
               x1_ref, rows_ref, group_ref):
    n = rs.stop - rs.start

    def normed(of_ref, ob_ref, g_ref, gate_ref, s):
        o = of_ref[rs, s].astype(F32) + ob_ref[rs, s].astype(F32)
        return (_rmsnorm(o, g_ref[...]) * gate_ref[rs, s].astype(F32)).astype(BF16)

    chunks = [slice(c * MXU_DEPTH, (c + 1) * MXU_DEPTH) for c in range(D_MODEL // MXU_DEPTH)]
    acc = lambda total, part: part if total is None else total + part
    t_hg = t_gla = mix = None
    for cs in chunks:
        heads = range(cs.start // HG_DV, cs.stop // HG_DV)
        y_hg = jnp.concatenate([normed(ohf_ref, ohb_ref, hgn_ref, hgate_ref, slice(h * HG_DV, (h + 1) * HG_DV))
                                for h in heads], axis=1)
        t_hg = acc(t_hg, _dot(y_hg, wa_ref[cs, :]))
        heads = range(cs.start // GLA_DV, cs.stop // GLA_DV)
        y_gla = jnp.concatenate([normed(ogf_ref, ogb_ref, glan_ref, gr_ref, slice(h * GLA_DV, (h + 1) * GLA_DV))
                                 for h in heads], axis=1)
        t_gla = acc(t_gla, _dot(y_gla, wb_ref[cs, :]))
    for cs in chunks:
        merged = sa_ref[rs, cs].astype(F32) * t_hg[:, cs] + sb_ref[rs, cs].astype(F32) * t_gla[:, cs]
        mix = acc(mix, _dot(merged.astype(BF16), wo_ref[cs, :]))
    x1 =x_ref[rs, :] + mod_ref[2:3, :] * mix
    x1_ref[rs, :] = x1
    h2 = _rmsnorm(x1, n2_ref[...]) * (1.0 + mod_ref[4:5, :]) + mod_ref[3:4, :]
    h2_hi = h2.astype(BF16)
    token_rows = lambda s: pl.ds(rs.start * ROW_SUBLANES + s, n, stride=ROW_SUBLANES)
    for s in range(D_MODEL // 128):
        rows_ref[token_rows(s), :] = h2[:, s * 128:(s + 1) * 128]
    for s in range(COMB_SUBLANE + 1, ROW_SUBLANES):
        rows_ref[token_rows(s), :] = jnp.zeros((n, 128), F32)

    h2_lo = (h2 - h2_hi.astype(F32)).astype(BF16)
    lt = (_dot(h2_hi, wr_ref[0]) + _dot(h2_hi, wr_ref[1]) + _dot(h2_lo, wr_ref[0])).T
    row = lambda r: lt[r:r + 1, :]
    top = lambda vals: functools.reduce(jnp.maximum, vals)

    def first_at(vals, m):
        idx = jnp.full(m.shape, len(vals), jnp.int32)
        for j in reversed(range(len(vals))):
            idx = jnp.where(vals[j] == m, j, idx)
        return idx

    gl = [row(g) for g in range(N_GROUPS)]
    gmax = top(gl)
    p_top = 1.0 / functools.reduce(jnp.add, [jnp.exp(v - gmax) for v in gl])
    g_idx = first_at(gl, gmax)
    sel = []
    for j in range(EXPERTS_PER_GROUP):
        v = row(EXPERT_LANE0 + j)
        for g in range(1, N_GROUPS):
            v = jnp.where(g_idx == g, row(EXPERT_LANE0 + g * EXPERTS_PER_GROUP + j), v)
        sel.append(v)
    m1 = top(sel)
    i1 = first_at(sel, m1)
    sel2 = [jnp.where(i1 == j, -jnp.inf, v) for j, v in enumerate(sel)]
    m2 = top(sel2)
    i2 = first_at(sel2, m2)
    r = jnp.exp(m2 - m1)
    w1 = p_top / (1.0 + r)
    lane0 = EXPERT_LANE0 + g_idx * EXPERTS_PER_GROUP
    rid = lax.broadcasted_iota(jnp.int32, lt.shape, 0)
    comb_t = jnp.where(rid == lane0 + i1, w1, jnp.where(rid == lane0 + i2, w1 * r, 0.0))
    rows_ref[token_rows(COMB_SUBLANE), :] = comb_t.T
    group_ref[:, rs] = g_idx


def _post(ohf, ohb, ogf, ogb, hgate, gr, sa, sb, x, mod, tiles_per_row, hgn, glan, n2, wa, wb, wo, wr):
    t = x.shape[0]
    tm = TM_POST
    const = lambda shape: pl.BlockSpec(shape, lambda i: (0,) * len(shape))
    tok = lambda w: pl.BlockSpec((tm, w), lambda i: (i, 0))
    return pl.pallas_call(
        _post_body,
        grid=(t // tm,),
        in_specs=[tok(1024)] * 9
                 + [pl.BlockSpec((None, 6, D_MODEL), lambda i: (i // tiles_per_row, 0, 0)),
                    const((1, HG_DV)), const((1, GLA_DV)), const((1, D_MODEL)),
                    const((1024, 1024)), const((1024, 1024)), const((1024, 1024)),
                    const((2, D_MODEL, ROUTER_LANES))],
        out_specs=[tok(1024), pl.BlockSpec((tm * ROW_SUBLANES, 128), lambda i: (i, 0)),
                   pl.BlockSpec((None, 1, tm), lambda i: (i, 0, 0))],
        out_shape=[jax.ShapeDtypeStruct((t, D_MODEL), F32), jax.ShapeDtypeStruct((t * ROW_SUBLANES, 128), F32),
                   jax.ShapeDtypeStruct((t // tm, 1, tm), jnp.int32)],
        compiler_params=pltpu.CompilerParams(vmem_limit_bytes=VMEM_LIMIT),
        name="post",
    )(ohf, ohb, ogf, ogb, hgate, gr, sa, sb, x, mod, hgn, glan, n2, wa, wb, wo, wr)


def _route_plan(group, t):
    tr, blk = TR_MOE, PLAN_BLOCK
    nt, nb = t // tr + N_GROUPS, t // blk
    i32 = jnp.int32
    gids = jnp.arange(N_GROUPS, dtype=i32)
    onehot = (group[:, None] == gids[None, :]).astype(i32).reshape(nb, blk, N_GROUPS)
    within = jnp.cumsum(onehot, axis=1)
    blk_end = jnp.cumsum(within[:, -1, :], axis=0)
    blk_start = blk_end - within[:, -1, :]
    counts = blk_end[-1]
    tiles = (counts + tr - 1) // tr
    tile_end = jnp.cumsum(tiles)
    tile_group = jnp.minimum(jnp.sum(jnp.arange(nt, dtype=i32)[:, None] >= tile_end[None, :], axis=1),
                             N_GROUPS - 1).astype(i32)
    row = jnp.arange(nt * tr, dtype=i32)
    gsel = (jnp.repeat(tile_group, tr)[:, None] == gids[None, :]).astype(i32)
    pick = lambda per_group: jnp.sum(gsel * per_group[None, :], axis=1)
    k = row - pick(tile_end - tiles) * tr
    valid = k < pick(counts)
    per_block = lambda table: jnp.sum(gsel[:, None, :] * table[None, :, :], axis=2)
    b_row = jnp.minimum(jnp.sum(per_block(blk_end) <= k[:, None], axis=1), nb - 1)
    bsel = (b_row[:, None] == jnp.arange(nb, dtype=i32)[None, :]).astype(i32)
    k_local = k - jnp.sum(bsel * per_block(blk_start), axis=1)
    sel = (bsel[:, :, None] * gsel[:, None, :]).reshape(nt * tr, nb * N_GROUPS).astype(F32)
    table = jnp.transpose(within, (0, 2, 1)).reshape(nb * N_GROUPS, blk).astype(F32)
    ranks = jnp.dot(sel, table)
    t_local = jnp.sum(ranks <= k_local[:, None].astype(F32), axis=1).astype(i32)
    row_src = jnp.where(valid, b_row * blk + t_local, 0)
    spare = t + ((row // tr) % 2) * tr + row % tr
    return tile_group, row_src, jnp.where(valid, row_src, spare), tile_end[-1:].astype(i32)


def _moe_body(tg_ref, src_ref, dst_ref, used_ref, rows_hbm, wg_ref, wu_ref, wd_ref, out_hbm, xbuf, ybuf, gsem, ssem):
    tr = TR_MOE
    i = pl.program_id(0)
    n_used = used_ref[0]
    slot = i % 2

    def gather_row(tile, slot_, r):
        tok = src_ref[tile * tr + r]
        return pltpu.make_async_copy(rows_hbm.at[pl.ds(pl.multiple_of(tok * ROW_SUBLANES, ROW_SUBLANES), ROW_SUBLANES)],
                                     xbuf.at[slot_, pl.ds(r * ROW_SUBLANES, ROW_SUBLANES)], gsem.at[slot_])

    def scatter_row(tile, slot_, r):
        tok = dst_ref[tile * tr + r]
        return pltpu.make_async_copy(ybuf.at[slot_, pl.ds(r * OUT_SUBLANES, OUT_SUBLANES)],
                                     out_hbm.at[pl.ds(pl.multiple_of(tok * OUT_SUBLANES, OUT_SUBLANES), OUT_SUBLANES)],
                                     ssem.at[slot_])

    def for_rows(fn):
        def body(r, carry):
            fn(r)
            return carry
        lax.fori_loop(0, tr, body, 0, unroll=8)

    def gathered(slot_):
        return pltpu.make_async_copy(rows_hbm.at[pl.ds(0, tr * ROW_SUBLANES)], xbuf.at[slot_], gsem.at[slot_])

    def scattered(slot_):
        return pltpu.make_async_copy(ybuf.at[slot_], out_hbm.at[pl.ds(0, tr * OUT_SUBLANES)], ssem.at[slot_])

    @pl.when(i == 0)
    def _():
        for_rows(lambda r: gather_row(0, 0, r).start())
        n_tok = out_hbm.shape[0] // OUT_SUBLANES - 2 * tr
        ybuf[...] = jnp.zeros_like(ybuf)
        for s in range(2):
            spare = pltpu.make_async_copy(
                ybuf.at[s], out_hbm.at[pl.ds((n_tok + s * tr) * OUT_SUBLANES, tr * OUT_SUBLANES)], ssem.at[s])
            spare.start()
            spare.wait()

    @pl.when(i < n_used)
    def _():
        @pl.when(i + 1 < n_used)
        def _():
            for_rows(lambda r: gather_row(i + 1, 1 - slot, r).start())

        gathered(slot).wait()

        @pl.when(i >= 2)
        def _():
            scattered(slot).wait()

        x = jnp.concatenate([xbuf[slot, pl.ds(s, tr, stride=ROW_SUBLANES), :] for s in range(D_MODEL // 128)],
                            axis=1).astype(BF16)
        comb = xbuf[slot, pl.ds(COMB_SUBLANE, tr, stride=ROW_SUBLANES), :]
        lane = lax.broadcasted_iota(jnp.int32, comb.shape, 1)
        lane0 = EXPERT_LANE0 + tg_ref[i] * EXPERTS_PER_GROUP
        acc = jnp.zeros((tr, D_MODEL), F32)
        for e in range(EXPERTS_PER_GROUP):
            col = jnp.sum(jnp.where(lane == lane0 + e, comb, 0.0), axis=-1, keepdims=True)
            hid = (_silu(_dot(x, wg_ref[e])) * _dot(x, wu_ref[e]) * col).astype(BF16)
            acc = acc + _dot(hid, wd_ref[e])

        for s in range(D_MODEL // 128):
            ybuf[slot, pl.ds(s, tr, stride=OUT_SUBLANES), :] = acc[:, s * 128:(s + 1) * 128]

        for_rows(lambda r: scatter_row(i, slot, r).start())

        @pl.when(i == n_used - 1)
        def _():
            scattered(slot).wait()
            scattered(1 - slot).wait()


def _moe(rows, plan, wg, wu, wd, t):
    tr = TR_MOE
    assert t // tr >= 2
    nt = plan[0].shape[0]
    group_w = lambda shape: pl.BlockSpec((EXPERTS_PER_GROUP,) + shape, lambda i, tg, src, dst, used: (tg[i], 0, 0))
    grid_spec = pltpu.PrefetchScalarGridSpec(
        num_scalar_prefetch=4,
        grid=(nt,),
        in_specs=[pl.BlockSpec(memory_space=pl.ANY),
                  group_w((D_MODEL, D_EXPERT)), group_w((D_MODEL, D_EXPERT)), group_w((D_EXPERT, D_MODEL))],
        out_specs=pl.BlockSpec(memory_space=pl.ANY),
        scratch_shapes=[pltpu.VMEM((2, tr * ROW_SUBLANES, 128), F32), pltpu.VMEM((2, tr * OUT_SUBLANES, 128), F32),
                        pltpu.SemaphoreType.DMA((2,)), pltpu.SemaphoreType.DMA((2,))])
    return pl.pallas_call(
        _moe_body,
        grid_spec=grid_spec,
        out_shape=jax.ShapeDtypeStruct(((t + 2 * tr) * OUT_SUBLANES, 128), F32),
        compiler_params=pltpu.CompilerParams(dimension_semantics=("arbitrary",), vmem_limit_bytes=VMEM_LIMIT),
        name="moe",
    )(*plan, rows, wg, wu, wd)


def _final_body(x1_ref, moe_ref, mod_ref, fn_ref, y_ref):
    tm = x1_ref.shape[0]
    moe = jnp.concatenate([moe_ref[pl.ds(s, tm, stride=OUT_SUBLANES), :] for s in range(D_MODEL // 128)], axis=1)
    y_ref[...] = _rmsnorm(x1_ref[...] + mod_ref[5:6, :] * moe, fn_ref[...])


def _final(x1, moe, mod, tiles_per_row, fn):
    t = x1.shape[0]
    tm = TM_FINAL
    return pl.pallas_call(
        _final_body,
        grid=(t // tm,),
        in_specs=[pl.BlockSpec((tm, D_MODEL), lambda i: (i, 0)),
                  pl.BlockSpec((tm * OUT_SUBLANES, 128), lambda i: (i, 0)),
                  pl.BlockSpec((None, 6, D_MODEL), lambda i: (i // tiles_per_row, 0, 0)),
                  pl.BlockSpec((1, D_MODEL), lambda i: (0, 0))],
        out_specs=pl.BlockSpec((tm, D_MODEL), lambda i: (i, 0)),
        out_shape=jax.ShapeDtypeStruct((t, D_MODEL), F32),
        compiler_params=pltpu.CompilerParams(vmem_limit_bytes=VMEM_LIMIT),
        name="final",
    )(x1, moe, mod, fn)


def _trunk_layer(x, mod, s0_hg, s0_gla, emit_state, w):
    b, l, _ = x.shape
    xt = x.reshape(b * l, D_MODEL)
    per_row = lambda tm: (l // tm) if mod.shape[0] > 1 else (b * l // tm)
    (hq, gf, gb, hv, hgate, gq, gk, gv, gr, laf, lab, sa, sb) = _inproj(
        xt, mod, per_row(TM_PROJ), w["n1"], w["lbp"], w["wmain"], w["waup"], w["ba"])
    seq = lambda a: a.reshape(b, l, a.shape[-1])
    hg = _scan(seq(hq), None, seq(hv), seq(gf), seq(gb), s0_hg, w["scan_consts"], HG_H, HG_DK, HG_DV,
               emit_state)
    gla = _scan(seq(gq), seq(gk), seq(gv), seq(laf), seq(lab), s0_gla, w["scan_consts"], GLA_H, GLA_DK,
                GLA_DV, emit_state)
    flat = lambda a: a.reshape(b * l, a.shape[-1])
    x1, rows, group = _post(flat(hg[0]), flat(hg[1]), flat(gla[0]), flat(gla[1]), hgate, gr, sa, sb, xt, mod,
                           per_row(TM_POST), w["hgn"], w["glan"], w["n2"], w["wa"], w["wb"], w["wo"], w["wr"])
    plan = _route_plan(group.reshape(b * l), b * l)
    moe = _moe(rows, plan, w["wg"], w["wu"], w["wd"], b * l)
    y = _final(x1, moe, mod, per_row(TM_FINAL), w["fn"])
    states = (hg[2], gla[2]) if emit_state else (None, None)
    return y.reshape(b, l, D_MODEL), states


def kernel(x_prompt, x_sample, state_hgrn, state_gla, c, c_ctx, ada_w, ada_b, norm1_g, norm2_g, w_in,
           hg_lb_param, hg_norm_g, gla_wa_up, gla_ba, gla_norm_g, w_br_a, w_br_b, w_out,
           w_router_group, w_router_expert, w_exp_gate, w_exp_up, w_exp_down, final_norm_g):
    nb = c.shape[0]
    cond = jnp.concatenate([c, c_ctx[None, :], jnp.zeros((8 - nb - 1, D_MODEL), F32)], axis=0)
    mod = _modulation(cond, ada_w[0], ada_b).reshape(8, 6, D_MODEL)

    zeros = jnp.zeros((GLA_RANK, GLA_H * GLA_DK), F32)
    router = jnp.concatenate(
        [w_router_group[0], jnp.transpose(w_router_expert[0], (1, 0, 2)).reshape(D_MODEL, N_EXPERTS),
         jnp.zeros((D_MODEL, ROUTER_LANES - N_GROUPS - N_EXPERTS), F32)], axis=1)
    w = {
        "n1": norm1_g, "n2": norm2_g, "fn": final_norm_g[None, :], "lbp": hg_lb_param,
        "wmain": w_in[0].astype(BF16),
        "waup": jnp.concatenate([jnp.concatenate([gla_wa_up[0, 0], zeros], axis=1),
                                 jnp.concatenate([zeros, gla_wa_up[0, 1]], axis=1)], axis=0).astype(BF16),
        "ba": gla_ba[0].reshape(1, 2 * GLA_H * GLA_DK),
        "hgn": hg_norm_g, "glan": gla_norm_g,
        "wa": w_br_a[0].astype(BF16), "wb": w_br_b[0].astype(BF16), "wo": w_out[0].astype(BF16),
        "wr": jnp.stack([router.astype(BF16), (router - router.astype(BF16).astype(F32)).astype(BF16)]),
        "wg": w_exp_gate[0].astype(BF16), "wu": w_exp_up[0].astype(BF16), "wd": w_exp_down[0].astype(BF16),
        "scan_consts": _scan_constants(),
    }
    y_prompt, (st_hg, st_gla) = _trunk_layer(x_prompt, mod[nb:nb + 1], None, None, True, w)
    y_sample, _ = _trunk_layer(x_sample, mod[:nb], state_hgrn[:, 0], state_gla[:, 0], False, w)
    return y_prompt, y_sample, st_hg[:, None], st_gla[:, None]
```

```python
import functools

import numpy as np
import jax
import jax.numpy as jnp
from jax import lax
from jax.experimental import pallas as pl
from jax.experimental.pallas import tpu as pltpu

F32 = jnp.float32
BF16 = jnp.bfloat16

D_MODEL = 1024
CHUNK = 64
N_LEVELS = 6
CHUNKS_PER_STEP = 4
HG_H, HG_DK, HG_DV = 8, 128, 128
GLA_H, GLA_DK, GLA_DV = 4, 128, 256
GLA_RANK = 16
GLA_GATE_NORM = 16.0
N_GROUPS = 4
EXPERTS_PER_GROUP = 8
N_EXPERTS = N_GROUPS * EXPERTS_PER_GROUP
D_EXPERT = D_MODEL // 4
EPS = 1e-6
ROUTER_LANES = 128
EXPERT_LANE0 = N_GROUPS
VMEM_LIMIT = 56 * 1024 * 1024
MXU_DEPTH = 256

TM_PROJ = 256
TM_POST = 256
POST_PART = 256
TM_FINAL = 1024
TR_MOE = 256
PLAN_BLOCK = 256
OUT_SUBLANES = D_MODEL // 128
COMB_SUBLANE = OUT_SUBLANES
ROW_SUBLANES = 16


def _sigmoid(x):
    return 1.0 / (1.0 + jnp.exp(-x))


def _silu(x):
    return x * _sigmoid(x)


def _log_sigmoid(x):
    return jnp.minimum(x, 0.0) - jnp.log(1.0 + jnp.exp(-jnp.abs(x)))


def _rmsnorm(x, g):
    return x * lax.rsqrt(jnp.mean(x * x, axis=-1, keepdims=True) + EPS) * g


def _dot(a, b):
    return jnp.dot(a, b, preferred_element_type=F32)


def _dot_nt(a, b):
    return lax.dot_general(a, b, (((1,), (1,)), ((), ())), preferred_element_type=F32)


def _dot_tn(a, b):
    return lax.dot_general(a, b, (((0,), (0,)), ((), ())), preferred_element_type=F32)


def _mod_body(c_ref, w_ref, b_ref, o_ref):
    s, w = _silu(c_ref[...]), w_ref[...]
    s_hi, w_hi = s.astype(BF16), w.astype(BF16)
    s_lo, w_lo = (s - s_hi.astype(F32)).astype(BF16), (w - w_hi.astype(F32)).astype(BF16)
    o_ref[...] = _dot(s_hi, w_hi) + _dot(s_hi, w_lo) + _dot(s_lo, w_hi) + b_ref[...]


def _modulation(cond, ada_w, ada_b):
    rows, n = cond.shape[0], ada_w.shape[1]
    tn = n // 4
    return pl.pallas_call(
        _mod_body,
        grid=(n // tn,),
        in_specs=[pl.BlockSpec((rows, D_MODEL), lambda j: (0, 0)),
                  pl.BlockSpec((D_MODEL, tn), lambda j: (0, j)),
                  pl.BlockSpec((1, tn), lambda j: (0, j))],
        out_specs=pl.BlockSpec((rows, tn), lambda j: (0, j)),
        out_shape=jax.ShapeDtypeStruct((rows, n), F32),
        compiler_params=pltpu.CompilerParams(vmem_limit_bytes=VMEM_LIMIT),
        name="modulation",
    )(cond, ada_w, ada_b)


_C_HQ, _C_HFF, _C_HFB, _C_HI, _C_HGATE = 0, 1024, 2048, 3072, 4096
_C_GQ, _C_GK, _C_GV, _C_GR, _C_TAIL = 5120, 5632, 6144, 7168, 8192
GA_WIDTH = 2 * GLA_RANK
W_TAIL = GA_WIDTH + 2 * D_MODEL
W_MAIN = _C_TAIL + W_TAIL


def _inproj_body(x_ref, mod_ref, n1_ref, lbp_ref, wmain_ref, waup_ref, ba_ref,
                 hq_ref, gf_ref, gb_ref, hv_ref, hgate_ref, gq_ref, gk_ref, gv_ref, gr_ref,
                 laf_ref, lab_ref, sa_ref, sb_ref):
    x = x_ref[...]
    h = _rmsnorm(x, n1_ref[...]) * (1.0 + mod_ref[1:2, :]) + mod_ref[0:1, :]
    hb = h.astype(BF16)

    def proj(c0, w):
        return _dot(hb, wmain_ref[:, c0:c0 + w])

    p0, p1 = lbp_ref[0:1, :], lbp_ref[1:2, :]
    pm = jnp.maximum(p0, p1)
    e0, e1 = jnp.exp(p0 - pm), jnp.exp(p1 - pm)
    lb = e0 / (e0 + e1)

    tail = proj(_C_TAIL, W_TAIL)
    sa_ref[...] = _sigmoid(tail[:, GA_WIDTH:GA_WIDTH + D_MODEL]).astype(BF16)
    sb_ref[...] = _sigmoid(tail[:, GA_WIDTH + D_MODEL:]).astype(BF16)
    ga = tail[:, :GA_WIDTH].astype(BF16)
    la = _log_sigmoid(_dot(ga, waup_ref[...]) + ba_ref[...]) * (1.0 / GLA_GATE_NORM)
    laf_ref[...] = la[:, :512]
    lab_ref[...] = la[:, 512:]
    gf_ref[...] = jnp.log(lb + (1.0 - lb) * _sigmoid(proj(_C_HFF, 1024)))
    gb_ref[...] = jnp.log(lb + (1.0 - lb) * _sigmoid(proj(_C_HFB, 1024)))
    hgate_ref[...] = _silu(proj(_C_HGATE, 1024)).astype(BF16)
    gr_ref[...] = _silu(proj(_C_GR, 1024)).astype(BF16)
    hq_ref[...] = (proj(_C_HQ, 1024) * HG_DK ** -0.5).astype(BF16)
    gq_ref[...] = (proj(_C_GQ, 512) * GLA_DK ** -0.5).astype(BF16)
    gv_ref[...] = proj(_C_GV, 1024).astype(BF16)
    gk_ref[...] = proj(_C_GK, 512).astype(BF16)
    hv_ref[...] = proj(_C_HI, 1024).astype(BF16)


def _inproj(x, mod, tiles_per_row, n1, lbp, wmain, waup, ba):
    t = x.shape[0]
    tm = TM_PROJ
    const = lambda shape: pl.BlockSpec(shape, lambda i: (0,) * len(shape))
    tok = lambda w: pl.BlockSpec((tm, w), lambda i: (i, 0))
    widths = (1024, 1024, 1024, 1024, 1024, 512, 512, 1024, 1024, 512, 512, 1024, 1024)
    dtypes = (BF16, F32, F32, BF16, BF16, BF16, BF16, BF16, BF16, F32, F32, BF16, BF16)
    return pl.pallas_call(
        _inproj_body,
        grid=(t // tm,),
        in_specs=[tok(D_MODEL),
                  pl.BlockSpec((None, 6, D_MODEL), lambda i: (i // tiles_per_row, 0, 0)),
                  const((1, D_MODEL)), const((2, D_MODEL)),
                  pl.BlockSpec((D_MODEL, W_MAIN), lambda i: (0, 0), pipeline_mode=pl.Buffered(1)),
                  const((GA_WIDTH, 1024)), const((1, 1024))],
        out_specs=[tok(w) for w in widths],
        out_shape=[jax.ShapeDtypeStruct((t, w), dt) for w, dt in zip(widths, dtypes)],
        compiler_params=pltpu.CompilerParams(vmem_limit_bytes=VMEM_LIMIT),
        name="inproj",
    )(x, mod, n1, lbp, wmain, waup, ba)


N_MXU_LEVELS = 3
ROW_GROUP = 16
START_SLOT, END_SLOT = N_LEVELS, N_LEVELS + 1
LOG2E = 1.4426950408889634


def _boundary_row(t, h, d):
    base = (t // (2 * h)) * 2 * h
    return base + h - 1 if d == 0 else base + h


def _scan_constants():
    c = CHUNK
    tri = np.tril(np.ones((c, c), np.float32))
    m = [tri] + [np.abs(tri - tri[[_boundary_row(t, 1 << l, 0) for t in range(c)]]) for l in range(N_MXU_LEVELS)]
    m = np.stack(m)
    lvl = np.full((c, c), N_LEVELS + 1, np.int32)
    for t in range(c):
        lvl[t, t] = N_LEVELS
        for s in range(t):
            lvl[t, s] = int(np.floor(np.log2(t ^ s)))
    m_fwd = m.reshape(-1, c)
    m_bwd = m[:, ::-1, ::-1].reshape(-1, c)
    m_both = np.stack([m_fwd, m_bwd])
    return (jnp.asarray(np.concatenate([m_both, m_both], axis=2), BF16), jnp.asarray(np.stack([lvl, lvl.T])))


def _scan_prepare(u, d, rows, q_ref, k_ref, g_ref, m_ref, side, cb_ref, kf_ref, kb_ref, p_ref):
    c = CHUNK
    w = g_ref.shape[-1]
    g = g_ref[rows, :]
    gs = g * LOG2E
    g1 = gs.astype(BF16)
    g2 = (gs - g1.astype(F32)).astype(BF16)
    cb_ref[u] = _dot(m_ref[d], jnp.concatenate([g1, g2], axis=0))
    k32 = (1.0 - jnp.exp(g)) if k_ref is None else k_ref[rows, :].astype(F32)
    kf_ref[u] = k32
    kb_ref[u] = k32.astype(BF16)
    bcast = lambda r, n: jnp.broadcast_to(cb_ref[u, r:r + 1, :], (n, w))
    for i in range(c // ROW_GROUP):
        t0 = ROW_GROUP * i
        grp = slice(t0, t0 + ROW_GROUP)
        cum = cb_ref[u, grp, :]
        q = q_ref[rows.start + t0:rows.start + t0 + ROW_GROUP, :]
        k = kb_ref[u, grp, :]
        for l in range(N_LEVELS):
            h = 1 << l
            if l < N_MXU_LEVELS:
                ex = cb_ref[u, c * (l + 1) + t0:c * (l + 1) + t0 + ROW_GROUP, :]
            else:
                ex = []
                for t in (t0, t0 + 8):
                    half = slice(t - t0, t - t0 + 8)
                    bnd = bcast(_boundary_row(t, h, d), 8)
                    ex.append(cum[half] - bnd if ((t & h) != 0) == (d == 0) else bnd - cum[half])
                ex = jnp.concatenate(ex, axis=0)
            if h >= ROW_GROUP:
                src = q if ((t0 & h) != 0) == (d == 0) else k
            else:
                upper = side[l] > 0
                src = jnp.where(upper, q, k) if d == 0 else jnp.where(upper, k, q)
            p_ref[u, l, grp, :] = src * jnp.exp2(ex).astype(BF16)
        p_ref[u, START_SLOT, grp, :] = q * jnp.exp2(cum).astype(BF16)
        edge = c - 1 if d == 0 else 0
        p_ref[u, END_SLOT, grp, :] = (kf_ref[u, grp, :] * jnp.exp2(bcast(edge, ROW_GROUP) - cum)).astype(BF16)


def _scan_scores(u, d, rows, q_ref, masks, kb_ref, p_ref, a_ref, n_heads, dk):
    groups = range(CHUNK // ROW_GROUP)
    piece = lambda x, g: x[g * ROW_GROUP:(g + 1) * ROW_GROUP]
    mask_pieces = [[piece(m, g) for g in groups] for m in masks]
    for h in range(n_heads):
        ks = slice(h * dk, (h + 1) * dk)
        diag = _dot_nt(q_ref[rows, ks], kb_ref[u, :, ks]).astype(BF16)
        a = [jnp.where(mask_pieces[N_LEVELS][g], piece(diag, g), jnp.zeros((), BF16)) for g in groups]
        for l in range(N_LEVELS):
            p = p_ref[u, l, :, ks]
            if (1 << l) < ROW_GROUP:
                query = list(groups)
                x = _dot_nt(p, p).astype(BF16)
            else:
                query = [g for g in groups if ((g * ROW_GROUP & (1 << l)) != 0) == (d == 0)]
                x = _dot_nt(jnp.concatenate([piece(p, g) for g in query], axis=0), p).astype(BF16)
            for j, g in enumerate(query):
                a[g] = jnp.where(mask_pieces[l][g], piece(x, j), a[g])
        a_ref[u, h] = jnp.concatenate(a, axis=0)


def _scan_outputs(u, d, rows, v_ref, cb_ref, p_ref, a_ref, st_ref, o_ref, n_heads, dk, dv):
    edge = CHUNK - 1 if d == 0 else 0
    for h in range(n_heads):
        ks = slice(h * dk, (h + 1) * dk)
        vs = slice(h * dv, (h + 1) * dv)
        v = v_ref[rows, vs]
        st = st_ref[d, h]
        o = _dot(a_ref[u, h], v) + _dot_nt(p_ref[u, START_SLOT, :, ks], st.astype(BF16))
        o_ref[rows, vs] = o.astype(o_ref.dtype)
        st_ref[d, h] = st * jnp.exp2(cb_ref[u, edge:edge + 1, ks]) + _dot_tn(v, p_ref[u, END_SLOT, :, ks])


def _scan_body(*refs, n_heads, dk, dv, derive_k, zero_init, emit_state):
    refs = list(refs)
    q_refs = (refs.pop(0), refs.pop(0))
    k_refs = (None, None) if derive_k else (refs.pop(0), refs.pop(0))
    v_refs = (refs.pop(0), refs.pop(0))
    g_refs = (refs.pop(0), refs.pop(0))
    m_ref, lvl_ref = refs.pop(0), refs.pop(0)
    s0_ref = None if zero_init else refs.pop(0)
    o_refs = (refs.pop(0), refs.pop(0))
    sout_ref = refs.pop(0) if emit_state else None
    st_ref, cb_ref, kf_ref, kb_ref, p_ref, a_ref = refs
    n = pl.program_id(1)

    @pl.when(n == 0)
    def _():
        for d in range(2):
            for h in range(n_heads):
                st_ref[d, h] = jnp.zeros((dv, dk), F32) if zero_init else s0_ref[d, h].T

    cps = CHUNKS_PER_STEP
    rows = lambda c: slice(c * CHUNK, (c + 1) * CHUNK)
    masks = [[lvl_ref[d].astype(F32).astype(BF16) == l for l in range(N_LEVELS + 1)] for d in range(2)]
    row = lax.broadcasted_iota(jnp.int32, (ROW_GROUP, q_refs[0].shape[-1]), 0)
    side = [((row >> l) & 1).astype(F32).astype(BF16) for l in range(N_LEVELS) if (1 << l) < ROW_GROUP]
    units = [(d, c) for c in range(cps) for d in range(2)]
    for i in range(len(units) + 1):
        if i < len(units):
            d, c = units[i]
            _scan_prepare(d * cps + c, d, rows(c), q_refs[d], k_refs[d], g_refs[d], m_ref, side, cb_ref, kf_ref,
                          kb_ref, p_ref)
        if i >= 1:
            d, c = units[i - 1]
            _scan_scores(d * cps + c, d, rows(c), q_refs[d], masks[d], kb_ref, p_ref, a_ref, n_heads, dk)
    for d in range(2):
        for c in (range(cps) if d == 0 else reversed(range(cps))):
            _scan_outputs(d * cps + c, d, rows(c), v_refs[d], cb_ref, p_ref, a_ref, st_ref, o_refs[d], n_heads,
                          dk, dv)

    if emit_state:
        @pl.when(n == pl.num_programs(1) - 1)
        def _():
            for d in range(2):
                for h in range(n_heads):
                    sout_ref[d, h] = st_ref[d, h].T


def _scan(q, k, v, g_f, g_b, s0, consts, n_heads, dk, dv, emit_state):
    b, l, _ = q.shape
    step = CHUNKS_PER_STEP * CHUNK
    n = l // step
    slots = 2 * CHUNKS_PER_STEP
    m_all, lvl = consts
    fwd = lambda w: pl.BlockSpec((None, step, w), lambda i, j: (i, j, 0))
    bwd = lambda w: pl.BlockSpec((None, step, w), lambda i, j: (i, n - 1 - j, 0))
    const = lambda shape: pl.BlockSpec(shape, lambda i, j: (0,) * len(shape))
    state = pl.BlockSpec((None, 2, n_heads, dk, dv), lambda i, j: (i, 0, 0, 0, 0))
    wk, wv = n_heads * dk, n_heads * dv
    args, in_specs = [q, q], [fwd(wk), bwd(wk)]
    if k is not None:
        args += [k, k]
        in_specs += [fwd(wk), bwd(wk)]
    args += [v, v, g_f, g_b, m_all, lvl]
    in_specs += [fwd(wv), bwd(wv), fwd(wk), bwd(wk), const(m_all.shape), const(lvl.shape)]
    if s0 is not None:
        args.append(s0)
        in_specs.append(state)
    out_specs = [fwd(wv), bwd(wv)]
    out_shape = [jax.ShapeDtypeStruct((b, l, wv), BF16)] * 2
    if emit_state:
        out_specs.append(state)
        out_shape.append(jax.ShapeDtypeStruct((b, 2, n_heads, dk, dv), F32))
    body = functools.partial(_scan_body, n_heads=n_heads, dk=dk, dv=dv, derive_k=k is None,
                             zero_init=s0 is None, emit_state=emit_state)
    return pl.pallas_call(
        body,
        grid=(b, n),
        in_specs=in_specs,
        out_specs=out_specs,
        out_shape=out_shape,
        scratch_shapes=[pltpu.VMEM((2, n_heads, dv, dk), F32),
                        pltpu.VMEM((slots, (1 + N_MXU_LEVELS) * CHUNK, wk), F32),
                        pltpu.VMEM((slots, CHUNK, wk), F32), pltpu.VMEM((slots, CHUNK, wk), BF16),
                        pltpu.VMEM((slots, N_LEVELS + 2, CHUNK, wk), BF16),
                        pltpu.VMEM((slots, n_heads, CHUNK, CHUNK), BF16)],
        compiler_params=pltpu.CompilerParams(dimension_semantics=("arbitrary", "arbitrary"),
                                             vmem_limit_bytes=VMEM_LIMIT),
        name="scan_hgrn" if k is None else "scan_gla",
    )(*args)


def _post_body(ohf_ref, ohb_ref, ogf_ref, ogb_ref, hgate_ref, gr_ref, sa_ref, sb_ref, x_ref, mod_ref,
               hgn_ref, glan_ref, n2_ref, wa_ref, wb_ref, wo_ref, wr_ref,
               x1_ref, rows_ref, group_ref):
    tm = x_ref.shape[0]
    for r0 in range(0, tm, POST_PART):
        _post_part(slice(r0, r0 + POST_PART), ohf_ref, ohb_ref, ogf_ref, ogb_ref, hgate_ref, gr_ref, sa_ref, sb_ref,
                   x_ref, mod_ref, hgn_ref, glan_ref, n2_ref, wa_ref, wb_ref, wo_ref, wr_ref,
                   x1_ref, rows_ref, group_ref)


def _post_part(rs, ohf_ref, ohb_ref, ogf_ref, ogb_ref, hgate_ref, gr_ref, sa_ref, sb_ref, x_ref, mod_ref,
               hgn_ref, glan_ref, n2_ref, wa_ref, wb_ref, wo_ref, wr_ref,
               x1_ref, rows_ref, group_ref):
    n = rs.stop - rs.start

    def normed(of_ref, ob_ref, g_ref, gate_ref, s):
        o = of_ref[rs, s].astype(F32) + ob_ref[rs, s].astype(F32)
        return (_rmsnorm(o, g_ref[...]) * gate_ref[rs, s].astype(F32)).astype(BF16)

    chunks = [slice(c * MXU_DEPTH, (c + 1) * MXU_DEPTH) for c in range(D_MODEL // MXU_DEPTH)]
    acc = lambda total, part: part if total is None else total + part
    t_hg = t_gla = mix = None
    for cs in chunks:
        heads = range(cs.start // HG_DV, cs.stop // HG_DV)
        y_hg = jnp.concatenate([normed(ohf_ref, ohb_ref, hgn_ref, hgate_ref, slice(h * HG_DV, (h + 1) * HG_DV))
                                for h in heads], axis=1)
        t_hg = acc(t_hg, _dot(y_hg, wa_ref[cs, :]))
        heads = range(cs.start // GLA_DV, cs.stop // GLA_DV)
        y_gla = jnp.concatenate([normed(ogf_ref, ogb_ref, glan_ref, gr_ref, slice(h * GLA_DV, (h + 1) * GLA_DV))
                                 for h in heads], axis=1)
        t_gla = acc(t_gla, _dot(y_gla, wb_ref[cs, :]))
    for cs in chunks:
        merged = sa_ref[rs, cs].astype(F32) * t_hg[:, cs] + sb_ref[rs, cs].astype(F32) * t_gla[:, cs]
        mix = acc(mix, _dot(merged.astype(BF16), wo_ref[cs, :]))
    x1 =x_ref[rs, :] + mod_ref[2:3, :] * mix
    x1_ref[rs, :] = x1
    h2 = _rmsnorm(x1, n2_ref[...]) * (1.0 + mod_ref[4:5, :]) + mod_ref[3:4, :]
    h2_hi = h2.astype(BF16)
    token_rows = lambda s: pl.ds(rs.start * ROW_SUBLANES + s, n, stride=ROW_SUBLANES)
    for s in range(D_MODEL // 128):
        rows_ref[token_rows(s), :] = h2[:, s * 128:(s + 1) * 128]
    for s in range(COMB_SUBLANE + 1, ROW_SUBLANES):
        rows_ref[token_rows(s), :] = jnp.zeros((n, 128), F32)

    h2_lo = (h2 - h2_hi.astype(F32)).astype(BF16)
    lt = (_dot(h2_hi, wr_ref[0]) + _dot(h2_hi, wr_ref[1]) + _dot(h2_lo, wr_ref[0])).T
    row = lambda r: lt[r:r + 1, :]
    top = lambda vals: functools.reduce(jnp.maximum, vals)

    def first_at(vals, m):
        idx = jnp.full(m.shape, len(vals), jnp.int32)
        for j in reversed(range(len(vals))):
            idx = jnp.where(vals[j] == m, j, idx)
        return idx

    gl = [row(g) for g in range(N_GROUPS)]
    gmax = top(gl)
    p_top = 1.0 / functools.reduce(jnp.add, [jnp.exp(v - gmax) for v in gl])
    g_idx = first_at(gl, gmax)
    sel = []
    for j in range(EXPERTS_PER_GROUP):
        v = row(EXPERT_LANE0 + j)
        for g in range(1, N_GROUPS):
            v = jnp.where(g_idx == g, row(EXPERT_LANE0 + g * EXPERTS_PER_GROUP + j), v)
        sel.append(v)
    m1 = top(sel)
    i1 = first_at(sel, m1)
    sel2 = [jnp.where(i1 == j, -jnp.inf, v) for j, v in enumerate(sel)]
    m2 = top(sel2)
    i2 = first_at(sel2, m2)
    r = jnp.exp(m2 - m1)
    w1 = p_top / (1.0 + r)
    lane0 = EXPERT_LANE0 + g_idx * EXPERTS_PER_GROUP
    rid = lax.broadcasted_iota(jnp.int32, lt.shape, 0)
    comb_t = jnp.where(rid == lane0 + i1, w1, jnp.where(rid == lane0 + i2, w1 * r, 0.0))
    rows_ref[token_rows(COMB_SUBLANE), :] = comb_t.T
    group_ref[:, rs] = g_idx


def _post(ohf, ohb, ogf, ogb, hgate, gr, sa, sb, x, mod, tiles_per_row, hgn, glan, n2, wa, wb, wo, wr):
    t = x.shape[0]
    tm = TM_POST
    const = lambda shape: pl.BlockSpec(shape, lambda i: (0,) * len(shape))
    tok = lambda w: pl.BlockSpec((tm, w), lambda i: (i, 0))
    return pl.pallas_call(
        _post_body,
        grid=(t // tm,),
        in_specs=[tok(1024)] * 9
                 + [pl.BlockSpec((None, 6, D_MODEL), lambda i: (i // tiles_per_row, 0, 0)),
                    const((1, HG_DV)), const((1, GLA_DV)), const((1, D_MODEL)),
                    const((1024, 1024)), const((1024, 1024)), const((1024, 1024)),
                    const((2, D_MODEL, ROUTER_LANES))],
        out_specs=[tok(1024), pl.BlockSpec((tm * ROW_SUBLANES, 128), lambda i: (i, 0)),
                   pl.BlockSpec((None, 1, tm), lambda i: (i, 0, 0))],
        out_shape=[jax.ShapeDtypeStruct((t, D_MODEL), F32), jax.ShapeDtypeStruct((t * ROW_SUBLANES, 128), F32),
                   jax.ShapeDtypeStruct((t // tm, 1, tm), jnp.int32)],
        compiler_params=pltpu.CompilerParams(vmem_limit_bytes=VMEM_LIMIT),
        name="post",
    )(ohf, ohb, ogf, ogb, hgate, gr, sa, sb, x, mod, hgn, glan, n2, wa, wb, wo, wr)


def _route_plan(group, t):
    tr, blk = TR_MOE, PLAN_BLOCK
    nt, nb = t // tr + N_GROUPS, t // blk
    i32 = jnp.int32
    gids = jnp.arange(N_GROUPS, dtype=i32)
    onehot = (group[:, None] == gids[None, :]).astype(i32).reshape(nb, blk, N_GROUPS)
    within = jnp.cumsum(onehot, axis=1)
    blk_end = jnp.cumsum(within[:, -1, :], axis=0)
    blk_start = blk_end - within[:, -1, :]
    counts = blk_end[-1]
    tiles = (counts + tr - 1) // tr
    tile_end = jnp.cumsum(tiles)
    tile_group = jnp.minimum(jnp.sum(jnp.arange(nt, dtype=i32)[:, None] >= tile_end[None, :], axis=1),
                             N_GROUPS - 1).astype(i32)
    row = jnp.arange(nt * tr, dtype=i32)
    gsel = (jnp.repeat(tile_group, tr)[:, None] == gids[None, :]).astype(i32)
    pick = lambda per_group: jnp.sum(gsel * per_group[None, :], axis=1)
    k = row - pick(tile_end - tiles) * tr
    valid = k < pick(counts)
    per_block = lambda table: jnp.sum(gsel[:, None, :] * table[None, :, :], axis=2)
    b_row = jnp.minimum(jnp.sum(per_block(blk_end) <= k[:, None], axis=1), nb - 1)
    bsel = (b_row[:, None] == jnp.arange(nb, dtype=i32)[None, :]).astype(i32)
    k_local = k - jnp.sum(bsel * per_block(blk_start), axis=1)
    sel = (bsel[:, :, None] * gsel[:, None, :]).reshape(nt * tr, nb * N_GROUPS).astype(F32)
    table = jnp.transpose(within, (0, 2, 1)).reshape(nb * N_GROUPS, blk).astype(F32)
    ranks = jnp.dot(sel, table)
    t_local = jnp.sum(ranks <= k_local[:, None].astype(F32), axis=1).astype(i32)
    row_src = jnp.where(valid, b_row * blk + t_local, 0)
    spare = t + ((row // tr) % 2) * tr + row % tr
    return tile_group, row_src, jnp.where(valid, row_src, spare), tile_end[-1:].astype(i32)


def _moe_body(tg_ref, src_ref, dst_ref, used_ref, rows_hbm, wg_ref, wu_ref, wd_ref, out_hbm, xbuf, ybuf, gsem, ssem):
    tr = TR_MOE
    i = pl.program_id(0)
    n_used = used_ref[0]
    slot = i % 2

    def gather_row(tile, slot_, r):
        tok = src_ref[tile * tr + r]
        return pltpu.make_async_copy(rows_hbm.at[pl.ds(pl.multiple_of(tok * ROW_SUBLANES, ROW_SUBLANES), ROW_SUBLANES)],
                                     xbuf.at[slot_, pl.ds(r * ROW_SUBLANES, ROW_SUBLANES)], gsem.at[slot_])

    def scatter_row(tile, slot_, r):
        tok = dst_ref[tile * tr + r]
        return pltpu.make_async_copy(ybuf.at[slot_, pl.ds(r * OUT_SUBLANES, OUT_SUBLANES)],
                                     out_hbm.at[pl.ds(pl.multiple_of(tok * OUT_SUBLANES, OUT_SUBLANES), OUT_SUBLANES)],
                                     ssem.at[slot_])

    def for_rows(fn):
        def body(r, carry):
            fn(r)
            return carry
        lax.fori_loop(0, tr, body, 0, unroll=8)

    def gathered(slot_):
        return pltpu.make_async_copy(rows_hbm.at[pl.ds(0, tr * ROW_SUBLANES)], xbuf.at[slot_], gsem.at[slot_])

    def scattered(slot_):
        return pltpu.make_async_copy(ybuf.at[slot_], out_hbm.at[pl.ds(0, tr * OUT_SUBLANES)], ssem.at[slot_])

    @pl.when(i == 0)
    def _():
        for_rows(lambda r: gather_row(0, 0, r).start())
        n_tok = out_hbm.shape[0] // OUT_SUBLANES - 2 * tr
        ybuf[...] = jnp.zeros_like(ybuf)
        for s in range(2):
            spare = pltpu.make_async_copy(
                ybuf.at[s], out_hbm.at[pl.ds((n_tok + s * tr) * OUT_SUBLANES, tr * OUT_SUBLANES)], ssem.at[s])
            spare.start()
            spare.wait()

    @pl.when(i < n_used)
    def _():
        @pl.when(i + 1 < n_used)
        def _():
            for_rows(lambda r: gather_row(i + 1, 1 - slot, r).start())

        gathered(slot).wait()

        @pl.when(i >= 2)
        def _():
            scattered(slot).wait()

        x = jnp.concatenate([xbuf[slot, pl.ds(s, tr, stride=ROW_SUBLANES), :] for s in range(D_MODEL // 128)],
                            axis=1).astype(BF16)
        comb = xbuf[slot, pl.ds(COMB_SUBLANE, tr, stride=ROW_SUBLANES), :]
        lane = lax.broadcasted_iota(jnp.int32, comb.shape, 1)
        lane0 = EXPERT_LANE0 + tg_ref[i] * EXPERTS_PER_GROUP
        acc = jnp.zeros((tr, D_MODEL), F32)
        for e in range(EXPERTS_PER_GROUP):
            col = jnp.sum(jnp.where(lane == lane0 + e, comb, 0.0), axis=-1, keepdims=True)
            hid = (_silu(_dot(x, wg_ref[e])) * _dot(x, wu_ref[e]) * col).astype(BF16)
            acc = acc + _dot(hid, wd_ref[e])

        for s in range(D_MODEL // 128):
            ybuf[slot, pl.ds(s, tr, stride=OUT_SUBLANES), :] = acc[:, s * 128:(s + 1) * 128]

        for_rows(lambda r: scatter_row(i, slot, r).start())

        @pl.when(i == n_used - 1)
        def _():
            scattered(slot).wait()
            scattered(1 - slot).wait()


def _moe(rows, plan, wg, wu, wd, t):
    tr = TR_MOE
    assert t // tr >= 2
    nt = plan[0].shape[0]
    group_w = lambda shape: pl.BlockSpec((EXPERTS_PER_GROUP,) + shape, lambda i, tg, src, dst, used: (tg[i], 0, 0))
    grid_spec = pltpu.PrefetchScalarGridSpec(
        num_scalar_prefetch=4,
        grid=(nt,),
        in_specs=[pl.BlockSpec(memory_space=pl.ANY),
                  group_w((D_MODEL, D_EXPERT)), group_w((D_MODEL, D_EXPERT)), group_w((D_EXPERT, D_MODEL))],
        out_specs=pl.BlockSpec(memory_space=pl.ANY),
        scratch_shapes=[pltpu.VMEM((2, tr * ROW_SUBLANES, 128), F32), pltpu.VMEM((2, tr * OUT_SUBLANES, 128), F32),
                        pltpu.SemaphoreType.DMA((2,)), pltpu.SemaphoreType.DMA((2,))])
    return pl.pallas_call(
        _moe_body,
        grid_spec=grid_spec,
        out_shape=jax.ShapeDtypeStruct(((t + 2 * tr) * OUT_SUBLANES, 128), F32),
        compiler_params=pltpu.CompilerParams(dimension_semantics=("arbitrary",), vmem_limit_bytes=VMEM_LIMIT),
        name="moe",
    )(*plan, rows, wg, wu, wd)


def _final_body(x1_ref, moe_ref, mod_ref, fn_ref, y_ref):
    tm = x1_ref.shape[0]
    moe = jnp.concatenate([moe_ref[pl.ds(s, tm, stride=OUT_SUBLANES), :] for s in range(D_MODEL // 128)], axis=1)
    y_ref[...] = _rmsnorm(x1_ref[...] + mod_ref[5:6, :] * moe, fn_ref[...])


def _final(x1, moe, mod, tiles_per_row, fn):
    t = x1.shape[0]
    tm = TM_FINAL
    return pl.pallas_call(
        _final_body,
        grid=(t // tm,),
        in_specs=[pl.BlockSpec((tm, D_MODEL), lambda i: (i, 0)),
                  pl.BlockSpec((tm * OUT_SUBLANES, 128), lambda i: (i, 0)),
                  pl.BlockSpec((None, 6, D_MODEL), lambda i: (i // tiles_per_row, 0, 0)),
                  pl.BlockSpec((1, D_MODEL), lambda i: (0, 0))],
        out_specs=pl.BlockSpec((tm, D_MODEL), lambda i: (i, 0)),
        out_shape=jax.ShapeDtypeStruct((t, D_MODEL), F32),
        compiler_params=pltpu.CompilerParams(vmem_limit_bytes=VMEM_LIMIT),
        name="final",
    )(x1, moe, mod, fn)


def _trunk_layer(x, mod, s0_hg, s0_gla, emit_state, w):
    b, l, _ = x.shape
    xt = x.reshape(b * l, D_MODEL)
    per_row = lambda tm: (l // tm) if mod.shape[0] > 1 else (b * l // tm)
    (hq, gf, gb, hv, hgate, gq, gk, gv, gr, laf, lab, sa, sb) = _inproj(
        xt, mod, per_row(TM_PROJ), w["n1"], w["lbp"], w["wmain"], w["waup"], w["ba"])
    seq = lambda a: a.reshape(b, l, a.shape[-1])
    hg = _scan(seq(hq), None, seq(hv), seq(gf), seq(gb), s0_hg, w["scan_consts"], HG_H, HG_DK, HG_DV,
               emit_state)
    gla = _scan(seq(gq), seq(gk), seq(gv), seq(laf), seq(lab), s0_gla, w["scan_consts"], GLA_H, GLA_DK,
                GLA_DV, emit_state)
    flat = lambda a: a.reshape(b * l, a.shape[-1])
    x1, rows, group = _post(flat(hg[0]), flat(hg[1]), flat(gla[0]), flat(gla[1]), hgate, gr, sa, sb, xt, mod,
                           per_row(TM_POST), w["hgn"], w["glan"], w["n2"], w["wa"], w["wb"], w["wo"], w["wr"])
    plan = _route_plan(group.reshape(b * l), b * l)
    moe = _moe(rows, plan, w["wg"], w["wu"], w["wd"], b * l)
    y = _final(x1, moe, mod, per_row(TM_FINAL), w["fn"])
    states = (hg[2], gla[2]) if emit_state else (None, None)
    return y.reshape(b, l, D_MODEL), states


def kernel(x_prompt, x_sample, state_hgrn, state_gla, c, c_ctx, ada_w, ada_b, norm1_g, norm2_g, w_in,
           hg_lb_param, hg_norm_g, gla_wa_up, gla_ba, gla_norm_g, w_br_a, w_br_b, w_out,
           w_router_group, w_router_expert, w_exp_gate, w_exp_up, w_exp_down, final_norm_g):
    nb = c.shape[0]
    cond = jnp.concatenate([c, c_ctx[None, :], jnp.zeros((8 - nb - 1, D_MODEL), F32)], axis=0)
    mod = _modulation(cond, ada_w[0], ada_b).reshape(8, 6, D_MODEL)

    zeros = jnp.zeros((GLA_RANK, GLA_H * GLA_DK), F32)
    router = jnp.concatenate(
        [w_router_group[0], jnp.transpose(w_router_expert[0], (1, 0, 2)).reshape(D_MODEL, N_EXPERTS),
         jnp.zeros((D_MODEL, ROUTER_LANES - N_GROUPS - N_EXPERTS), F32)], axis=1)
    w = {
        "n1": norm1_g, "n2": norm2_g, "fn": final_norm_g[None, :], "lbp": hg_lb_param,
        "wmain": w_in[0].astype(BF16),
        "waup": jnp.concatenate([jnp.concatenate([gla_wa_up[0, 0], zeros], axis=1),
                                 jnp.concatenate([zeros, gla_wa_up[0, 1]], axis=1)], axis=0).astype(BF16),
        "ba": gla_ba[0].reshape(1, 2 * GLA_H * GLA_DK),
        "hgn": hg_norm_g, "glan": gla_norm_g,
        "wa": w_br_a[0].astype(BF16), "wb": w_br_b[0].astype(BF16), "wo": w_out[0].astype(BF16),
        "wr": jnp.stack([router.astype(BF16), (router - router.astype(BF16).astype(F32)).astype(BF16)]),
        "wg": w_exp_gate[0].astype(BF16), "wu": w_exp_up[0].astype(BF16), "wd": w_exp_down[0].astype(BF16),
        "scan_consts": _scan_constants(),
    }
    y_prompt, (st_hg, st_gla) = _trunk_layer(x_prompt, mod[nb:nb + 1], None, None, True, w)
    y_sample, _ = _trunk_layer(x_sample, mod[:nb], state_hgrn[:, 0], state_gla[:, 0], False, w)
    return y_prompt, y_sample, st_hg[:, None], st_gla[:, None]
```

```python
import functools

import numpy as np
import jax
import jax.numpy as jnp
from jax import lax
from jax.experimental import pallas as pl
from jax.experimental.pallas import tpu as pltpu

F32 = jnp.float32
BF16 = jnp.bfloat16

D_MODEL = 1024
CHUNK = 64
N_LEVELS = 6
CHUNKS_PER_STEP = 4
HG_H, HG_DK, HG_DV = 8, 128, 128
GLA_H, GLA_DK, GLA_DV = 4, 128, 256
GLA_RANK = 16
GLA_GATE_NORM = 16.0
N_GROUPS = 4
EXPERTS_PER_GROUP = 8
N_EXPERTS = N_GROUPS * EXPERTS_PER_GROUP
D_EXPERT = D_MODEL // 4
EPS = 1e-6
ROUTER_LANES = 128
EXPERT_LANE0 = N_GROUPS
VMEM_LIMIT = 56 * 1024 * 1024
MXU_DEPTH = 256

TM_PROJ = 256
TM_POST = 256
POST_PART = 256
TM_FINAL = 1024
TR_MOE = 512
PLAN_BLOCK = 256
OUT_SUBLANES = D_MODEL // 128
COMB_SUBLANE = OUT_SUBLANES
ROW_SUBLANES = 16


def _sigmoid(x):
    return 1.0 / (1.0 + jnp.exp(-x))


def _silu(x):
    return x * _sigmoid(x)


def _log_sigmoid(x):
    return jnp.minimum(x, 0.0) - jnp.log(1.0 + jnp.exp(-jnp.abs(x)))


def _rmsnorm(x, g):
    return x * lax.rsqrt(jnp.mean(x * x, axis=-1, keepdims=True) + EPS) * g


def _dot(a, b):
    return jnp.dot(a, b, preferred_element_type=F32)


def _dot_nt(a, b):
    return lax.dot_general(a, b, (((1,), (1,)), ((), ())), preferred_element_type=F32)


def _dot_tn(a, b):
    return lax.dot_general(a, b, (((0,), (0,)), ((), ())), preferred_element_type=F32)


def _mod_body(c_ref, w_ref, b_ref, o_ref):
    s, w = _silu(c_ref[...]), w_ref[...]
    s_hi, w_hi = s.astype(BF16), w.astype(BF16)
    s_lo, w_lo = (s - s_hi.astype(F32)).astype(BF16), (w - w_hi.astype(F32)).astype(BF16)
    o_ref[...] = _dot(s_hi, w_hi) + _dot(s_hi, w_lo) + _dot(s_lo, w_hi) + b_ref[...]


def _modulation(cond, ada_w, ada_b):
    rows, n = cond.shape[0], ada_w.shape[1]
    tn = n // 4
    return pl.pallas_call(
        _mod_body,
        grid=(n // tn,),
        in_specs=[pl.BlockSpec((rows, D_MODEL), lambda j: (0, 0)),
                  pl.BlockSpec((D_MODEL, tn), lambda j: (0, j)),
                  pl.BlockSpec((1, tn), lambda j: (0, j))],
        out_specs=pl.BlockSpec((rows, tn), lambda j: (0, j)),
        out_shape=jax.ShapeDtypeStruct((rows, n), F32),
        compiler_params=pltpu.CompilerParams(vmem_limit_bytes=VMEM_LIMIT),
        name="modulation",
    )(cond, ada_w, ada_b)


_C_HQ, _C_HFF, _C_HFB, _C_HI, _C_HGATE = 0, 1024, 2048, 3072, 4096
_C_GQ, _C_GK, _C_GV, _C_GR, _C_TAIL = 5120, 5632, 6144, 7168, 8192
GA_WIDTH = 2 * GLA_RANK
W_TAIL = GA_WIDTH + 2 * D_MODEL
W_MAIN = _C_TAIL + W_TAIL


def _inproj_body(x_ref, mod_ref, n1_ref, lbp_ref, wmain_ref, waup_ref, ba_ref,
                 hq_ref, gf_ref, gb_ref, hv_ref, hgate_ref, gq_ref, gk_ref, gv_ref, gr_ref,
                 laf_ref, lab_ref, sa_ref, sb_ref):
    x = x_ref[...]
    h = _rmsnorm(x, n1_ref[...]) * (1.0 + mod_ref[1:2, :]) + mod_ref[0:1, :]
    hb = h.astype(BF16)

    def proj(c0, w):
        return _dot(hb, wmain_ref[:, c0:c0 + w])

    p0, p1 = lbp_ref[0:1, :], lbp_ref[1:2, :]
    pm = jnp.maximum(p0, p1)
    e0, e1 = jnp.exp(p0 - pm), jnp.exp(p1 - pm)
    lb = e0 / (e0 + e1)

    tail = proj(_C_TAIL, W_TAIL)
    sa_ref[...] = _sigmoid(tail[:, GA_WIDTH:GA_WIDTH + D_MODEL]).astype(BF16)
    sb_ref[...] = _sigmoid(tail[:, GA_WIDTH + D_MODEL:]).astype(BF16)
    ga = tail[:, :GA_WIDTH].astype(BF16)
    la = _log_sigmoid(_dot(ga, waup_ref[...]) + ba_ref[...]) * (1.0 / GLA_GATE_NORM)
    laf_ref[...] = la[:, :512]
    lab_ref[...] = la[:, 512:]
    gf_ref[...] = jnp.log(lb + (1.0 - lb) * _sigmoid(proj(_C_HFF, 1024)))
    gb_ref[...] = jnp.log(lb + (1.0 - lb) * _sigmoid(proj(_C_HFB, 1024)))
    hgate_ref[...] = _silu(proj(_C_HGATE, 1024)).astype(BF16)
    gr_ref[...] = _silu(proj(_C_GR, 1024)).astype(BF16)
    hq_ref[...] = (proj(_C_HQ, 1024) * HG_DK ** -0.5).astype(BF16)
    gq_ref[...] = (proj(_C_GQ, 512) * GLA_DK ** -0.5).astype(BF16)
    gv_ref[...] = proj(_C_GV, 1024).astype(BF16)
    gk_ref[...] = proj(_C_GK, 512).astype(BF16)
    hv_ref[...] = proj(_C_HI, 1024).astype(BF16)


def _inproj(x, mod, tiles_per_row, n1, lbp, wmain, waup, ba):
    t = x.shape[0]
    tm = TM_PROJ
    const = lambda shape: pl.BlockSpec(shape, lambda i: (0,) * len(shape))
    tok = lambda w: pl.BlockSpec((tm, w), lambda i: (i, 0))
    widths = (1024, 1024, 1024, 1024, 1024, 512, 512, 1024, 1024, 512, 512, 1024, 1024)
    dtypes = (BF16, F32, F32, BF16, BF16, BF16, BF16, BF16, BF16, F32, F32, BF16, BF16)
    return pl.pallas_call(
        _inproj_body,
        grid=(t // tm,),
        in_specs=[tok(D_MODEL),
                  pl.BlockSpec((None, 6, D_MODEL), lambda i: (i // tiles_per_row, 0, 0)),
                  const((1, D_MODEL)), const((2, D_MODEL)),
                  pl.BlockSpec((D_MODEL, W_MAIN), lambda i: (0, 0), pipeline_mode=pl.Buffered(1)),
                  const((GA_WIDTH, 1024)), const((1, 1024))],
        out_specs=[tok(w) for w in widths],
        out_shape=[jax.ShapeDtypeStruct((t, w), dt) for w, dt in zip(widths, dtypes)],
        compiler_params=pltpu.CompilerParams(vmem_limit_bytes=VMEM_LIMIT),
        name="inproj",
    )(x, mod, n1, lbp, wmain, waup, ba)


N_MXU_LEVELS = 3
ROW_GROUP = 16
START_SLOT, END_SLOT = N_LEVELS, N_LEVELS + 1
LOG2E = 1.4426950408889634


def _boundary_row(t, h, d):
    base = (t // (2 * h)) * 2 * h
    return base + h - 1 if d == 0 else base + h


def _scan_constants():
    c = CHUNK
    tri = np.tril(np.ones((c, c), np.float32))
    m = [tri] + [np.abs(tri - tri[[_boundary_row(t, 1 << l, 0) for t in range(c)]]) for l in range(N_MXU_LEVELS)]
    m = np.stack(m)
    lvl = np.full((c, c), N_LEVELS + 1, np.int32)
    for t in range(c):
        lvl[t, t] = N_LEVELS
        for s in range(t):
            lvl[t, s] = int(np.floor(np.log2(t ^ s)))
    m_fwd = m.reshape(-1, c)
    m_bwd = m[:, ::-1, ::-1].reshape(-1, c)
    m_both = np.stack([m_fwd, m_bwd])
    return (jnp.asarray(np.concatenate([m_both, m_both], axis=2), BF16), jnp.asarray(np.stack([lvl, lvl.T])))


def _scan_prepare(u, d, rows, q_ref, k_ref, g_ref, m_ref, side, cb_ref, kf_ref, kb_ref, p_ref):
    c = CHUNK
    w = g_ref.shape[-1]
    g = g_ref[rows, :]
    gs = g * LOG2E
    g1 = gs.astype(BF16)
    g2 = (gs - g1.astype(F32)).astype(BF16)
    cb_ref[u] = _dot(m_ref[d], jnp.concatenate([g1, g2], axis=0))
    k32 = (1.0 - jnp.exp(g)) if k_ref is None else k_ref[rows, :].astype(F32)
    kf_ref[u] = k32
    kb_ref[u] = k32.astype(BF16)
    bcast = lambda r, n: jnp.broadcast_to(cb_ref[u, r:r + 1, :], (n, w))
    for i in range(c // ROW_GROUP):
        t0 = ROW_GROUP * i
        grp = slice(t0, t0 + ROW_GROUP)
        cum = cb_ref[u, grp, :]
        q = q_ref[rows.start + t0:rows.start + t0 + ROW_GROUP, :]
        k = kb_ref[u, grp, :]
        for l in range(N_LEVELS):
            h = 1 << l
            if l < N_MXU_LEVELS:
                ex = cb_ref[u, c * (l + 1) + t0:c * (l + 1) + t0 + ROW_GROUP, :]
            else:
                ex = []
                for t in (t0, t0 + 8):
                    half = slice(t - t0, t - t0 + 8)
                    bnd = bcast(_boundary_row(t, h, d), 8)
                    ex.append(cum[half] - bnd if ((t & h) != 0) == (d == 0) else bnd - cum[half])
                ex = jnp.concatenate(ex, axis=0)
            if h >= ROW_GROUP:
                src = q if ((t0 & h) != 0) == (d == 0) else k
            else:
                upper = side[l] > 0
                src = jnp.where(upper, q, k) if d == 0 else jnp.where(upper, k, q)
            p_ref[u, l, grp, :] = src * jnp.exp2(ex).astype(BF16)
        p_ref[u, START_SLOT, grp, :] = q * jnp.exp2(cum).astype(BF16)
        edge = c - 1 if d == 0 else 0
        p_ref[u, END_SLOT, grp, :] = (kf_ref[u, grp, :] * jnp.exp2(bcast(edge, ROW_GROUP) - cum)).astype(BF16)


def _scan_scores(u, d, rows, q_ref, masks, kb_ref, p_ref, a_ref, n_heads, dk):
    groups = range(CHUNK // ROW_GROUP)
    piece = lambda x, g: x[g * ROW_GROUP:(g + 1) * ROW_GROUP]
    mask_pieces = [[piece(m, g) for g in groups] for m in masks]
    for h in range(n_heads):
        ks = slice(h * dk, (h + 1) * dk)
        diag = _dot_nt(q_ref[rows, ks], kb_ref[u, :, ks]).astype(BF16)
        a = [jnp.where(mask_pieces[N_LEVELS][g], piece(diag, g), jnp.zeros((), BF16)) for g in groups]
        for l in range(N_LEVELS):
            p = p_ref[u, l, :, ks]
            if (1 << l) < ROW_GROUP:
                query = list(groups)
                x = _dot_nt(p, p).astype(BF16)
            else:
                query = [g for g in groups if ((g * ROW_GROUP & (1 << l)) != 0) == (d == 0)]
                x = _dot_nt(jnp.concatenate([piece(p, g) for g in query], axis=0), p).astype(BF16)
            for j, g in enumerate(query):
                a[g] = jnp.where(mask_pieces[l][g], piece(x, j), a[g])
        a_ref[u, h] = jnp.concatenate(a, axis=0)


def _scan_outputs(u, d, rows, v_ref, cb_ref, p_ref, a_ref, st_ref, o_ref, n_heads, dk, dv):
    edge = CHUNK - 1 if d == 0 else 0
    for h in range(n_heads):
        ks = slice(h * dk, (h + 1) * dk)
        vs = slice(h * dv, (h + 1) * dv)
        v = v_ref[rows, vs]
        st = st_ref[d, h]
        o = _dot(a_ref[u, h], v) + _dot_nt(p_ref[u, START_SLOT, :, ks], st.astype(BF16))
        o_ref[rows, vs] = o.astype(o_ref.dtype)
        st_ref[d, h] = st * jnp.exp2(cb_ref[u, edge:edge + 1, ks]) + _dot_tn(v, p_ref[u, END_SLOT, :, ks])


def _scan_body(*refs, n_heads, dk, dv, derive_k, zero_init, emit_state):
    refs = list(refs)
    q_refs = (refs.pop(0), refs.pop(0))
    k_refs = (None, None) if derive_k else (refs.pop(0), refs.pop(0))
    v_refs = (refs.pop(0), refs.pop(0))
    g_refs = (refs.pop(0), refs.pop(0))
    m_ref, lvl_ref = refs.pop(0), refs.pop(0)
    s0_ref = None if zero_init else refs.pop(0)
    o_refs = (refs.pop(0), refs.pop(0))
    sout_ref = refs.pop(0) if emit_state else None
    st_ref, cb_ref, kf_ref, kb_ref, p_ref, a_ref = refs
    n = pl.program_id(1)

    @pl.when(n == 0)
    def _():
        for d in range(2):
            for h in range(n_heads):
                st_ref[d, h] = jnp.zeros((dv, dk), F32) if zero_init else s0_ref[d, h].T

    cps = CHUNKS_PER_STEP
    rows = lambda c: slice(c * CHUNK, (c + 1) * CHUNK)
    masks = [[lvl_ref[d].astype(F32).astype(BF16) == l for l in range(N_LEVELS + 1)] for d in range(2)]
    row = lax.broadcasted_iota(jnp.int32, (ROW_GROUP, q_refs[0].shape[-1]), 0)
    side = [((row >> l) & 1).astype(F32).astype(BF16) for l in range(N_LEVELS) if (1 << l) < ROW_GROUP]
    units = [(d, c) for c in range(cps) for d in range(2)]
    for i in range(len(units) + 1):
        if i < len(units):
            d, c = units[i]
            _scan_prepare(d * cps + c, d, rows(c), q_refs[d], k_refs[d], g_refs[d], m_ref, side, cb_ref, kf_ref,
                          kb_ref, p_ref)
        if i >= 1:
            d, c = units[i - 1]
            _scan_scores(d * cps + c, d, rows(c), q_refs[d], masks[d], kb_ref, p_ref, a_ref, n_heads, dk)
    for d in range(2):
        for c in (range(cps) if d == 0 else reversed(range(cps))):
            _scan_outputs(d * cps + c, d, rows(c), v_refs[d], cb_ref, p_ref, a_ref, st_ref, o_refs[d], n_heads,
                          dk, dv)

    if emit_state:
        @pl.when(n == pl.num_programs(1) - 1)
        def _():
            for d in range(2):
                for h in range(n_heads):
                    sout_ref[d, h] = st_ref[d, h].T


def _scan(q, k, v, g_f, g_b, s0, consts, n_heads, dk, dv, emit_state):
    b, l, _ = q.shape
    step = CHUNKS_PER_STEP * CHUNK
    n = l // step
    slots = 2 * CHUNKS_PER_STEP
    m_all, lvl = consts
    fwd = lambda w: pl.BlockSpec((None, step, w), lambda i, j: (i, j, 0))
    bwd = lambda w: pl.BlockSpec((None, step, w), lambda i, j: (i, n - 1 - j, 0))
    const = lambda shape: pl.BlockSpec(shape, lambda i, j: (0,) * len(shape))
    state = pl.BlockSpec((None, 2, n_heads, dk, dv), lambda i, j: (i, 0, 0, 0, 0))
    wk, wv = n_heads * dk, n_heads * dv
    args, in_specs = [q, q], [fwd(wk), bwd(wk)]
    if k is not None:
        args += [k, k]
        in_specs += [fwd(wk), bwd(wk)]
    args += [v, v, g_f, g_b, m_all, lvl]
    in_specs += [fwd(wv), bwd(wv), fwd(wk), bwd(wk), const(m_all.shape), const(lvl.shape)]
    if s0 is not None:
        args.append(s0)
        in_specs.append(state)
    out_specs = [fwd(wv), bwd(wv)]
    out_shape = [jax.ShapeDtypeStruct((b, l, wv), BF16)] * 2
    if emit_state:
        out_specs.append(state)
        out_shape.append(jax.ShapeDtypeStruct((b, 2, n_heads, dk, dv), F32))
    body = functools.partial(_scan_body, n_heads=n_heads, dk=dk, dv=dv, derive_k=k is None,
                             zero_init=s0 is None, emit_state=emit_state)
    return pl.pallas_call(
        body,
        grid=(b, n),
        in_specs=in_specs,
        out_specs=out_specs,
        out_shape=out_shape,
        scratch_shapes=[pltpu.VMEM((2, n_heads, dv, dk), F32),
                        pltpu.VMEM((slots, (1 + N_MXU_LEVELS) * CHUNK, wk), F32),
                        pltpu.VMEM((slots, CHUNK, wk), F32), pltpu.VMEM((slots, CHUNK, wk), BF16),
                        pltpu.VMEM((slots, N_LEVELS + 2, CHUNK, wk), BF16),
                        pltpu.VMEM((slots, n_heads, CHUNK, CHUNK), BF16)],
        compiler_params=pltpu.CompilerParams(dimension_semantics=("arbitrary", "arbitrary"),
                                             vmem_limit_bytes=VMEM_LIMIT),
        name="scan_hgrn" if k is None else "scan_gla",
    )(*args)


def _post_body(ohf_ref, ohb_ref, ogf_ref, ogb_ref, hgate_ref, gr_ref, sa_ref, sb_ref, x_ref, mod_ref,
               hgn_ref, glan_ref, n2_ref, wa_ref, wb_ref, wo_ref, wr_ref,
               x1_ref, rows_ref, group_ref):
    tm = x_ref.shape[0]
    for r0 in range(0, tm, POST_PART):
        _post_part(slice(r0, r0 + POST_PART), ohf_ref, ohb_ref, ogf_ref, ogb_ref, hgate_ref, gr_ref, sa_ref, sb_ref,
                   x_ref, mod_ref, hgn_ref, glan_ref, n2_ref, wa_ref, wb_ref, wo_ref, wr_ref,
                   x1_ref, rows_ref, group_ref)


def _post_part(rs, ohf_ref, ohb_ref, ogf_ref, ogb_ref, hgate_ref, gr_ref, sa_ref, sb_ref, x_ref, mod_ref,
               hgn_ref, glan_ref, n2_ref, wa_ref, wb_ref, wo_ref, wr_ref,
               x1_ref, rows_ref, group_ref):
    n = rs.stop - rs.start

    def normed(of_ref, ob_ref, g_ref, gate_ref, s):
        o = of_ref[rs, s].astype(F32) + ob_ref[rs, s].astype(F32)
        return (_rmsnorm(o, g_ref[...]) * gate_ref[rs, s].astype(F32)).astype(BF16)

    chunks = [slice(c * MXU_DEPTH, (c + 1) * MXU_DEPTH) for c in range(D_MODEL // MXU_DEPTH)]
    acc = lambda total, part: part if total is None else total + part
    t_hg = t_gla = mix = None
    for cs in chunks:
        heads = range(cs.start // HG_DV, cs.stop // HG_DV)
        y_hg = jnp.concatenate([normed(ohf_ref, ohb_ref, hgn_ref, hgate_ref, slice(h * HG_DV, (h + 1) * HG_DV))
                                for h in heads], axis=1)
        t_hg = acc(t_hg, _dot(y_hg, wa_ref[cs, :]))
        heads = range(cs.start // GLA_DV, cs.stop // GLA_DV)
        y_gla = jnp.concatenate([normed(ogf_ref, ogb_ref, glan_ref, gr_ref, slice(h * GLA_DV, (h + 1) * GLA_DV))
                                 for h in heads], axis=1)
        t_gla = acc(t_gla, _dot(y_gla, wb_ref[cs, :]))
    for cs in chunks:
        merged = sa_ref[rs, cs].astype(F32) * t_hg[:, cs] + sb_ref[rs, cs].astype(F32) * t_gla[:, cs]
        mix = acc(mix, _dot(merged.astype(BF16), wo_ref[cs, :]))
    x1 =x_ref[rs, :] + mod_ref[2:3, :] * mix
    x1_ref[rs, :] = x1
    h2 = _rmsnorm(x1, n2_ref[...]) * (1.0 + mod_ref[4:5, :]) + mod_ref[3:4, :]
    h2_hi = h2.astype(BF16)
    token_rows = lambda s: pl.ds(rs.start * ROW_SUBLANES + s, n, stride=ROW_SUBLANES)
    for s in range(D_MODEL // 128):
        rows_ref[token_rows(s), :] = h2[:, s * 128:(s + 1) * 128]
    for s in range(COMB_SUBLANE + 1, ROW_SUBLANES):
        rows_ref[token_rows(s), :] = jnp.zeros((n, 128), F32)

    h2_lo = (h2 - h2_hi.astype(F32)).astype(BF16)
    lt = (_dot(h2_hi, wr_ref[0]) + _dot(h2_hi, wr_ref[1]) + _dot(h2_lo, wr_ref[0])).T
    row = lambda r: lt[r:r + 1, :]
    top = lambda vals: functools.reduce(jnp.maximum, vals)

    def first_at(vals, m):
        idx = jnp.full(m.shape, len(vals), jnp.int32)
        for j in reversed(range(len(vals))):
            idx = jnp.where(vals[j] == m, j, idx)
        return idx

    gl = [row(g) for g in range(N_GROUPS)]
    gmax = top(gl)
    p_top = 1.0 / functools.reduce(jnp.add, [jnp.exp(v - gmax) for v in gl])
    g_idx = first_at(gl, gmax)
    sel = []
    for j in range(EXPERTS_PER_GROUP):
        v = row(EXPERT_LANE0 + j)
        for g in range(1, N_GROUPS):
            v = jnp.where(g_idx == g, row(EXPERT_LANE0 + g * EXPERTS_PER_GROUP + j), v)
        sel.append(v)
    m1 = top(sel)
    i1 = first_at(sel, m1)
    sel2 = [jnp.where(i1 == j, -jnp.inf, v) for j, v in enumerate(sel)]
    m2 = top(sel2)
    i2 = first_at(sel2, m2)
    r = jnp.exp(m2 - m1)
    w1 = p_top / (1.0 + r)
    lane0 = EXPERT_LANE0 + g_idx * EXPERTS_PER_GROUP
    rid = lax.broadcasted_iota(jnp.int32, lt.shape, 0)
    comb_t = jnp.where(rid == lane0 + i1, w1, jnp.where(rid == lane0 + i2, w1 * r, 0.0))
    rows_ref[token_rows(COMB_SUBLANE), :] = comb_t.T
    group_ref[:, rs] = g_idx


def _post(ohf, ohb, ogf, ogb, hgate, gr, sa, sb, x, mod, tiles_per_row, hgn, glan, n2, wa, wb, wo, wr):
    t = x.shape[0]
    tm = TM_POST
    const = lambda shape: pl.BlockSpec(shape, lambda i: (0,) * len(shape))
    tok = lambda w: pl.BlockSpec((tm, w), lambda i: (i, 0))
    return pl.pallas_call(
        _post_body,
        grid=(t // tm,),
        in_specs=[tok(1024)] * 9
                 + [pl.BlockSpec((None, 6, D_MODEL), lambda i: (i // tiles_per_row, 0, 0)),
                    const((1, HG_DV)), const((1, GLA_DV)), const((1, D_MODEL)),
                    const((1024, 1024)), const((1024, 1024)), const((1024, 1024)),
                    const((2, D_MODEL, ROUTER_LANES))],
        out_specs=[tok(1024), pl.BlockSpec((tm * ROW_SUBLANES, 128), lambda i: (i, 0)),
                   pl.BlockSpec((None, 1, tm), lambda i: (i, 0, 0))],
        out_shape=[jax.ShapeDtypeStruct((t, D_MODEL), F32), jax.ShapeDtypeStruct((t * ROW_SUBLANES, 128), F32),
                   jax.ShapeDtypeStruct((t // tm, 1, tm), jnp.int32)],
        compiler_params=pltpu.CompilerParams(vmem_limit_bytes=VMEM_LIMIT),
        name="post",
    )(ohf, ohb, ogf, ogb, hgate, gr, sa, sb, x, mod, hgn, glan, n2, wa, wb, wo, wr)


def _route_plan(group, t):
    tr, blk = TR_MOE, PLAN_BLOCK
    nt, nb = t // tr + N_GROUPS, t // blk
    i32 = jnp.int32
    gids = jnp.arange(N_GROUPS, dtype=i32)
    onehot = (group[:, None] == gids[None, :]).astype(i32).reshape(nb, blk, N_GROUPS)
    within = jnp.cumsum(onehot, axis=1)
    blk_end = jnp.cumsum(within[:, -1, :], axis=0)
    blk_start = blk_end - within[:, -1, :]
    counts = blk_end[-1]
    tiles = (counts + tr - 1) // tr
    tile_end = jnp.cumsum(tiles)
    tile_group = jnp.minimum(jnp.sum(jnp.arange(nt, dtype=i32)[:, None] >= tile_end[None, :], axis=1),
                             N_GROUPS - 1).astype(i32)
    row = jnp.arange(nt * tr, dtype=i32)
    gsel = (jnp.repeat(tile_group, tr)[:, None] == gids[None, :]).astype(i32)
    pick = lambda per_group: jnp.sum(gsel * per_group[None, :], axis=1)
    k = row - pick(tile_end - tiles) * tr
    valid = k < pick(counts)
    per_block = lambda table: jnp.sum(gsel[:, None, :] * table[None, :, :], axis=2)
    b_row = jnp.minimum(jnp.sum(per_block(blk_end) <= k[:, None], axis=1), nb - 1)
    bsel = (b_row[:, None] == jnp.arange(nb, dtype=i32)[None, :]).astype(i32)
    k_local = k - jnp.sum(bsel * per_block(blk_start), axis=1)
    sel = (bsel[:, :, None] * gsel[:, None, :]).reshape(nt * tr, nb * N_GROUPS).astype(F32)
    table = jnp.transpose(within, (0, 2, 1)).reshape(nb * N_GROUPS, blk).astype(F32)
    ranks = jnp.dot(sel, table, precision=lax.Precision.HIGHEST)
    t_local = jnp.sum(ranks <= k_local[:, None].astype(F32), axis=1).astype(i32)
    row_src = jnp.where(valid, b_row * blk + t_local, 0)
    spare = t + ((row // tr) % 2) * tr + row % tr
    return tile_group, row_src, jnp.where(valid, row_src, spare), tile_end[-1:].astype(i32)


def _moe_body(tg_ref, src_ref, dst_ref, used_ref, rows_hbm, wg_ref, wu_ref, wd_ref, out_hbm, xbuf, ybuf, gsem, ssem):
    tr = TR_MOE
    i = pl.program_id(0)
    n_used = used_ref[0]
    slot = i % 2

    def gather_row(tile, slot_, r):
        tok = src_ref[tile * tr + r]
        return pltpu.make_async_copy(rows_hbm.at[pl.ds(pl.multiple_of(tok * ROW_SUBLANES, ROW_SUBLANES), ROW_SUBLANES)],
                                     xbuf.at[slot_, pl.ds(r * ROW_SUBLANES, ROW_SUBLANES)], gsem.at[slot_])

    def scatter_row(tile, slot_, r):
        tok = dst_ref[tile * tr + r]
        return pltpu.make_async_copy(ybuf.at[slot_, pl.ds(r * OUT_SUBLANES, OUT_SUBLANES)],
                                     out_hbm.at[pl.ds(pl.multiple_of(tok * OUT_SUBLANES, OUT_SUBLANES), OUT_SUBLANES)],
                                     ssem.at[slot_])

    def for_rows(fn):
        def body(r, carry):
            fn(r)
            return carry
        lax.fori_loop(0, tr, body, 0, unroll=32)

    def gathered(slot_):
        return pltpu.make_async_copy(rows_hbm.at[pl.ds(0, tr * ROW_SUBLANES)], xbuf.at[slot_], gsem.at[slot_])

    def scattered(slot_):
        return pltpu.make_async_copy(ybuf.at[slot_], out_hbm.at[pl.ds(0, tr * OUT_SUBLANES)], ssem.at[slot_])

    @pl.when(i == 0)
    def _():
        for_rows(lambda r: gather_row(0, 0, r).start())
        n_tok = out_hbm.shape[0] // OUT_SUBLANES - 2 * tr
        ybuf[...] = jnp.zeros_like(ybuf)
        for s in range(2):
            spare = pltpu.make_async_copy(
                ybuf.at[s], out_hbm.at[pl.ds((n_tok + s * tr) * OUT_SUBLANES, tr * OUT_SUBLANES)], ssem.at[s])
            spare.start()
            spare.wait()

    @pl.when(i < n_used)
    def _():
        @pl.when(i + 1 < n_used)
        def _():
            for_rows(lambda r: gather_row(i + 1, 1 - slot, r).start())

        gathered(slot).wait()

        @pl.when(i >= 2)
        def _():
            scattered(slot).wait()

        x = jnp.concatenate([xbuf[slot, pl.ds(s, tr, stride=ROW_SUBLANES), :] for s in range(D_MODEL // 128)],
                            axis=1).astype(BF16)
        comb = xbuf[slot, pl.ds(COMB_SUBLANE, tr, stride=ROW_SUBLANES), :]
        lane = lax.broadcasted_iota(jnp.int32, comb.shape, 1)
        lane0 = EXPERT_LANE0 + tg_ref[i] * EXPERTS_PER_GROUP
        acc = jnp.zeros((tr, D_MODEL), F32)
        for e in range(EXPERTS_PER_GROUP):
            col = jnp.sum(jnp.where(lane == lane0 + e, comb, 0.0), axis=-1, keepdims=True)
            hid = (_silu(_dot(x, wg_ref[e])) * _dot(x, wu_ref[e]) * col).astype(BF16)
            acc = acc + _dot(hid, wd_ref[e])

        for s in range(D_MODEL // 128):
            ybuf[slot, pl.ds(s, tr, stride=OUT_SUBLANES), :] = acc[:, s * 128:(s + 1) * 128]

        for_rows(lambda r: scatter_row(i, slot, r).start())

        @pl.when(i == n_used - 1)
        def _():
            scattered(slot).wait()
            scattered(1 - slot).wait()


def _moe(rows, plan, wg, wu, wd, t):
    tr = TR_MOE
    assert t // tr >= 2
    nt = plan[0].shape[0]
    group_w = lambda shape: pl.BlockSpec((EXPERTS_PER_GROUP,) + shape, lambda i, tg, src, dst, used: (tg[i], 0, 0))
    grid_spec = pltpu.PrefetchScalarGridSpec(
        num_scalar_prefetch=4,
        grid=(nt,),
        in_specs=[pl.BlockSpec(memory_space=pl.ANY),
                  group_w((D_MODEL, D_EXPERT)), group_w((D_MODEL, D_EXPERT)), group_w((D_EXPERT, D_MODEL))],
        out_specs=pl.BlockSpec(memory_space=pl.ANY),
        scratch_shapes=[pltpu.VMEM((2, tr * ROW_SUBLANES, 128), F32), pltpu.VMEM((2, tr * OUT_SUBLANES, 128), F32),
                        pltpu.SemaphoreType.DMA((2,)), pltpu.SemaphoreType.DMA((2,))])
    return pl.pallas_call(
        _moe_body,
        grid_spec=grid_spec,
        out_shape=jax.ShapeDtypeStruct(((t + 2 * tr) * OUT_SUBLANES, 128), F32),
        compiler_params=pltpu.CompilerParams(dimension_semantics=("arbitrary",), vmem_limit_bytes=VMEM_LIMIT),
        name="moe",
    )(*plan, rows, wg, wu, wd)


def _final_body(x1_ref, moe_ref, mod_ref, fn_ref, y_ref):
    tm = x1_ref.shape[0]
    moe = jnp.concatenate([moe_ref[pl.ds(s, tm, stride=OUT_SUBLANES), :] for s in range(D_MODEL // 128)], axis=1)
    y_ref[...] = _rmsnorm(x1_ref[...] + mod_ref[5:6, :] * moe, fn_ref[...])


def _final(x1, moe, mod, tiles_per_row, fn):
    t = x1.shape[0]
    tm = TM_FINAL
    return pl.pallas_call(
        _final_body,
        grid=(t // tm,),
        in_specs=[pl.BlockSpec((tm, D_MODEL), lambda i: (i, 0)),
                  pl.BlockSpec((tm * OUT_SUBLANES, 128), lambda i: (i, 0)),
                  pl.BlockSpec((None, 6, D_MODEL), lambda i: (i // tiles_per_row, 0, 0)),
                  pl.BlockSpec((1, D_MODEL), lambda i: (0, 0))],
        out_specs=pl.BlockSpec((tm, D_MODEL), lambda i: (i, 0)),
        out_shape=jax.ShapeDtypeStruct((t, D_MODEL), F32),
        compiler_params=pltpu.CompilerParams(vmem_limit_bytes=VMEM_LIMIT),
        name="final",
    )(x1, moe, mod, fn)


def _trunk_layer(x, mod, s0_hg, s0_gla, emit_state, w):
    b, l, _ = x.shape
    xt = x.reshape(b * l, D_MODEL)
    per_row = lambda tm: (l // tm) if mod.shape[0] > 1 else (b * l // tm)
    (hq, gf, gb, hv, hgate, gq, gk, gv, gr, laf, lab, sa, sb) = _inproj(
        xt, mod, per_row(TM_PROJ), w["n1"], w["lbp"], w["wmain"], w["waup"], w["ba"])
    seq = lambda a: a.reshape(b, l, a.shape[-1])
    hg = _scan(seq(hq), None, seq(hv), seq(gf), seq(gb), s0_hg, w["scan_consts"], HG_H, HG_DK, HG_DV,
               emit_state)
    gla = _scan(seq(gq), seq(gk), seq(gv), seq(laf), seq(lab), s0_gla, w["scan_consts"], GLA_H, GLA_DK,
                GLA_DV, emit_state)
    flat = lambda a: a.reshape(b * l, a.shape[-1])
    x1, rows, group = _post(flat(hg[0]), flat(hg[1]), flat(gla[0]), flat(gla[1]), hgate, gr, sa, sb, xt, mod,
                           per_row(TM_POST), w["hgn"], w["glan"], w["n2"], w["wa"], w["wb"], w["wo"], w["wr"])
    plan = _route_plan(group.reshape(b * l), b * l)
    moe = _moe(rows, plan, w["wg"], w["wu"], w["wd"], b * l)
    y = _final(x1, moe, mod, per_row(TM_FINAL), w["fn"])
    states = (hg[2], gla[2]) if emit_state else (None, None)
    return y.reshape(b, l, D_MODEL), states


def kernel(x_prompt, x_sample, state_hgrn, state_gla, c, c_ctx, ada_w, ada_b, norm1_g, norm2_g, w_in,
           hg_lb_param, hg_norm_g, gla_wa_up, gla_ba, gla_norm_g, w_br_a, w_br_b, w_out,
           w_router_group, w_router_expert, w_exp_gate, w_exp_up, w_exp_down, final_norm_g):
    nb = c.shape[0]
    cond = jnp.concatenate([c, c_ctx[None, :], jnp.zeros((8 - nb - 1, D_MODEL), F32)], axis=0)
    mod = _modulation(cond, ada_w[0], ada_b).reshape(8, 6, D_MODEL)

    zeros = jnp.zeros((GLA_RANK, GLA_H * GLA_DK), F32)
    router = jnp.concatenate(
        [w_router_group[0], jnp.transpose(w_router_expert[0], (1, 0, 2)).reshape(D_MODEL, N_EXPERTS),
         jnp.zeros((D_MODEL, ROUTER_LANES - N_GROUPS - N_EXPERTS), F32)], axis=1)
    w = {
        "n1": norm1_g, "n2": norm2_g, "fn": final_norm_g[None, :], "lbp": hg_lb_param,
        "wmain": w_in[0].astype(BF16),
        "waup": jnp.concatenate([jnp.concatenate([gla_wa_up[0, 0], zeros], axis=1),
                                 jnp.concatenate([zeros, gla_wa_up[0, 1]], axis=1)], axis=0).astype(BF16),
        "ba": gla_ba[0].reshape(1, 2 * GLA_H * GLA_DK),
        "hgn": hg_norm_g, "glan": gla_norm_g,
        "wa": w_br_a[0].astype(BF16), "wb": w_br_b[0].astype(BF16), "wo": w_out[0].astype(BF16),
        "wr": jnp.stack([router.astype(BF16), (router - router.astype(BF16).astype(F32)).astype(BF16)]),
        "wg": w_exp_gate[0].astype(BF16), "wu": w_exp_up[0].astype(BF16), "wd": w_exp_down[0].astype(BF16),
        "scan_consts": _scan_constants(),
    }
    y_prompt, (st_hg, st_gla) = _trunk_layer(x_prompt, mod[nb:nb + 1], None, None, True, w)
    y_sample, _ = _trunk_layer(x_sample, mod[:nb], state_hgrn[:, 0], state_gla[:, 0], False, w)
    return y_prompt, y_sample, st_hg[:, None], st_gla[:, None]
```

```python
import functools

import numpy as np
import jax
import jax.numpy as jnp
from jax import lax
from jax.experimental import pallas as pl
from jax.experimental.pallas import tpu as pltpu

F32 = jnp.float32
BF16 = jnp.bfloat16

D_MODEL = 1024
CHUNK = 64
N_LEVELS = 6
CHUNKS_PER_STEP = 4
HG_H, HG_DK, HG_DV = 8, 128, 128
GLA_H, GLA_DK, GLA_DV = 4, 128, 256
GLA_RANK = 16
GLA_GATE_NORM = 16.0
N_GROUPS = 4
EXPERTS_PER_GROUP = 8
N_EXPERTS = N_GROUPS * EXPERTS_PER_GROUP
D_EXPERT = D_MODEL // 4
EPS = 1e-6
ROUTER_LANES = 128
EXPERT_LANE0 = N_GROUPS
VMEM_LIMIT = 56 * 1024 * 1024
MXU_DEPTH = 256

TM_PROJ = 256
TM_POST = 256
POST_PART = 256
TM_FINAL = 1024
TR_MOE = (256, 512)
MOE_WIDE_MIN_TILES = 16
PLAN_BLOCK = 256
OUT_SUBLANES = D_MODEL // 128
COMB_SUBLANE = OUT_SUBLANES
ROW_SUBLANES = 16


def _sigmoid(x):
    return 1.0 / (1.0 + jnp.exp(-x))


def _silu(x):
    return x * _sigmoid(x)


def _log_sigmoid(x):
    return jnp.minimum(x, 0.0) - jnp.log(1.0 + jnp.exp(-jnp.abs(x)))


def _rmsnorm(x, g):
    return x * lax.rsqrt(jnp.mean(x * x, axis=-1, keepdims=True) + EPS) * g


def _dot(a, b):
    return jnp.dot(a, b, preferred_element_type=F32)


def _dot_nt(a, b):
    return lax.dot_general(a, b, (((1,), (1,)), ((), ())), preferred_element_type=F32)


def _dot_tn(a, b):
    return lax.dot_general(a, b, (((0,), (0,)), ((), ())), preferred_element_type=F32)


def _mod_body(c_ref, w_ref, b_ref, o_ref):
    s, w = _silu(c_ref[...]), w_ref[...]
    s_hi, w_hi = s.astype(BF16), w.astype(BF16)
    s_lo, w_lo = (s - s_hi.astype(F32)).astype(BF16), (w - w_hi.astype(F32)).astype(BF16)
    o_ref[...] = _dot(s_hi, w_hi) + _dot(s_hi, w_lo) + _dot(s_lo, w_hi) + b_ref[...]


def _modulation(cond, ada_w, ada_b):
    rows, n = cond.shape[0], ada_w.shape[1]
    tn = n // 4
    return pl.pallas_call(
        _mod_body,
        grid=(n // tn,),
        in_specs=[pl.BlockSpec((rows, D_MODEL), lambda j: (0, 0)),
                  pl.BlockSpec((D_MODEL, tn), lambda j: (0, j)),
                  pl.BlockSpec((1, tn), lambda j: (0, j))],
        out_specs=pl.BlockSpec((rows, tn), lambda j: (0, j)),
        out_shape=jax.ShapeDtypeStruct((rows, n), F32),
        compiler_params=pltpu.CompilerParams(vmem_limit_bytes=VMEM_LIMIT),
        name="modulation",
    )(cond, ada_w, ada_b)


_C_HQ, _C_HFF, _C_HFB, _C_HI, _C_HGATE = 0, 1024, 2048, 3072, 4096
_C_GQ, _C_GK, _C_GV, _C_GR, _C_TAIL = 5120, 5632, 6144, 7168, 8192
GA_WIDTH = 2 * GLA_RANK
W_TAIL = GA_WIDTH + 2 * D_MODEL
W_MAIN = _C_TAIL + W_TAIL


def _inproj_body(x_ref, mod_ref, n1_ref, lbp_ref, wmain_ref, waup_ref, ba_ref,
                 hq_ref, gf_ref, gb_ref, hv_ref, hgate_ref, gq_ref, gk_ref, gv_ref, gr_ref,
                 laf_ref, lab_ref, sa_ref, sb_ref):
    x = x_ref[...]
    h = _rmsnorm(x, n1_ref[...]) * (1.0 + mod_ref[1:2, :]) + mod_ref[0:1, :]
    hb = h.astype(BF16)

    def proj(c0, w):
        return _dot(hb, wmain_ref[:, c0:c0 + w])

    p0, p1 = lbp_ref[0:1, :], lbp_ref[1:2, :]
    pm = jnp.maximum(p0, p1)
    e0, e1 = jnp.exp(p0 - pm), jnp.exp(p1 - pm)
    lb = e0 / (e0 + e1)

    tail = proj(_C_TAIL, W_TAIL)
    sa_ref[...] = _sigmoid(tail[:, GA_WIDTH:GA_WIDTH + D_MODEL]).astype(BF16)
    sb_ref[...] = _sigmoid(tail[:, GA_WIDTH + D_MODEL:]).astype(BF16)
    ga = tail[:, :GA_WIDTH].astype(BF16)
    la = _log_sigmoid(_dot(ga, waup_ref[...]) + ba_ref[...]) * (1.0 / GLA_GATE_NORM)
    laf_ref[...] = la[:, :512]
    lab_ref[...] = la[:, 512:]
    gf_ref[...] = jnp.log(lb + (1.0 - lb) * _sigmoid(proj(_C_HFF, 1024)))
    gb_ref[...] = jnp.log(lb + (1.0 - lb) * _sigmoid(proj(_C_HFB, 1024)))
    hgate_ref[...] = _silu(proj(_C_HGATE, 1024)).astype(BF16)
    gr_ref[...] = _silu(proj(_C_GR, 1024)).astype(BF16)
    hq_ref[...] = (proj(_C_HQ, 1024) * HG_DK ** -0.5).astype(BF16)
    gq_ref[...] = (proj(_C_GQ, 512) * GLA_DK ** -0.5).astype(BF16)
    gv_ref[...] = proj(_C_GV, 1024).astype(BF16)
    gk_ref[...] = proj(_C_GK, 512).astype(BF16)
    hv_ref[...] = proj(_C_HI, 1024).astype(BF16)


def _inproj(x, mod, tiles_per_row, n1, lbp, wmain, waup, ba):
    t = x.shape[0]
    tm = TM_PROJ
    const = lambda shape: pl.BlockSpec(shape, lambda i: (0,) * len(shape))
    tok = lambda w: pl.BlockSpec((tm, w), lambda i: (i, 0))
    widths = (1024, 1024, 1024, 1024, 1024, 512, 512, 1024, 1024, 512, 512, 1024, 1024)
    dtypes = (BF16, F32, F32, BF16, BF16, BF16, BF16, BF16, BF16, F32, F32, BF16, BF16)
    return pl.pallas_call(
        _inproj_body,
        grid=(t // tm,),
        in_specs=[tok(D_MODEL),
                  pl.BlockSpec((None, 6, D_MODEL), lambda i: (i // tiles_per_row, 0, 0)),
                  const((1, D_MODEL)), const((2, D_MODEL)),
                  pl.BlockSpec((D_MODEL, W_MAIN), lambda i: (0, 0), pipeline_mode=pl.Buffered(1)),
                  const((GA_WIDTH, 1024)), const((1, 1024))],
        out_specs=[tok(w) for w in widths],
        out_shape=[jax.ShapeDtypeStruct((t, w), dt) for w, dt in zip(widths, dtypes)],
        compiler_params=pltpu.CompilerParams(vmem_limit_bytes=VMEM_LIMIT),
        name="inproj",
    )(x, mod, n1, lbp, wmain, waup, ba)


N_MXU_LEVELS = 3
ROW_GROUP = 16
START_SLOT, END_SLOT = N_LEVELS, N_LEVELS + 1
LOG2E = 1.4426950408889634


def _boundary_row(t, h, d):
    base = (t // (2 * h)) * 2 * h
    return base + h - 1 if d == 0 else base + h


def _scan_constants():
    c = CHUNK
    tri = np.tril(np.ones((c, c), np.float32))
    m = [tri] + [np.abs(tri - tri[[_boundary_row(t, 1 << l, 0) for t in range(c)]]) for l in range(N_MXU_LEVELS)]
    m = np.stack(m)
    lvl = np.full((c, c), N_LEVELS + 1, np.int32)
    for t in range(c):
        lvl[t, t] = N_LEVELS
        for s in range(t):
            lvl[t, s] = int(np.floor(np.log2(t ^ s)))
    m_fwd = m.reshape(-1, c)
    m_bwd = m[:, ::-1, ::-1].reshape(-1, c)
    m_both = np.stack([m_fwd, m_bwd])
    return (jnp.asarray(np.concatenate([m_both, m_both], axis=2), BF16), jnp.asarray(np.stack([lvl, lvl.T])))


def _scan_prepare(u, d, rows, q_ref, k_ref, g_ref, m_ref, side, cb_ref, kf_ref, kb_ref, p_ref):
    c = CHUNK
    w = g_ref.shape[-1]
    g = g_ref[rows, :]
    gs = g * LOG2E
    g1 = gs.astype(BF16)
    g2 = (gs - g1.astype(F32)).astype(BF16)
    cb_ref[u] = _dot(m_ref[d], jnp.concatenate([g1, g2], axis=0))
    k32 = (1.0 - jnp.exp(g)) if k_ref is None else k_ref[rows, :].astype(F32)
    kf_ref[u] = k32
    kb_ref[u] = k32.astype(BF16)
    bcast = lambda r, n: jnp.broadcast_to(cb_ref[u, r:r + 1, :], (n, w))
    for i in range(c // ROW_GROUP):
        t0 = ROW_GROUP * i
        grp = slice(t0, t0 + ROW_GROUP)
        cum = cb_ref[u, grp, :]
        q = q_ref[rows.start + t0:rows.start + t0 + ROW_GROUP, :]
        k = kb_ref[u, grp, :]
        for l in range(N_LEVELS):
            h = 1 << l
            if l < N_MXU_LEVELS:
                ex = cb_ref[u, c * (l + 1) + t0:c * (l + 1) + t0 + ROW_GROUP, :]
            else:
                ex = []
                for t in (t0, t0 + 8):
                    half = slice(t - t0, t - t0 + 8)
                    bnd = bcast(_boundary_row(t, h, d), 8)
                    ex.append(cum[half] - bnd if ((t & h) != 0) == (d == 0) else bnd - cum[half])
                ex = jnp.concatenate(ex, axis=0)
            if h >= ROW_GROUP:
                src = q if ((t0 & h) != 0) == (d == 0) else k
            else:
                upper = side[l] > 0
                src = jnp.where(upper, q, k) if d == 0 else jnp.where(upper, k, q)
            p_ref[u, l, grp, :] = src * jnp.exp2(ex).astype(BF16)
        p_ref[u, START_SLOT, grp, :] = q * jnp.exp2(cum).astype(BF16)
        edge = c - 1 if d == 0 else 0
        p_ref[u, END_SLOT, grp, :] = (kf_ref[u, grp, :] * jnp.exp2(bcast(edge, ROW_GROUP) - cum)).astype(BF16)


def _scan_scores(u, d, rows, q_ref, masks, kb_ref, p_ref, a_ref, n_heads, dk):
    groups = range(CHUNK // ROW_GROUP)
    piece = lambda x, g: x[g * ROW_GROUP:(g + 1) * ROW_GROUP]
    mask_pieces = [[piece(m, g) for g in groups] for m in masks]
    for h in range(n_heads):
        ks = slice(h * dk, (h + 1) * dk)
        diag = _dot_nt(q_ref[rows, ks], kb_ref[u, :, ks]).astype(BF16)
        a = [jnp.where(mask_pieces[N_LEVELS][g], piece(diag, g), jnp.zeros((), BF16)) for g in groups]
        for l in range(N_LEVELS):
            p = p_ref[u, l, :, ks]
            if (1 << l) < ROW_GROUP:
                query = list(groups)
                x = _dot_nt(p, p).astype(BF16)
            else:
                query = [g for g in groups if ((g * ROW_GROUP & (1 << l)) != 0) == (d == 0)]
                x = _dot_nt(jnp.concatenate([piece(p, g) for g in query], axis=0), p).astype(BF16)
            for j, g in enumerate(query):
                a[g] = jnp.where(mask_pieces[l][g], piece(x, j), a[g])
        a_ref[u, h] = jnp.concatenate(a, axis=0)


def _scan_outputs(u, d, rows, v_ref, cb_ref, p_ref, a_ref, st_ref, o_ref, n_heads, dk, dv):
    edge = CHUNK - 1 if d == 0 else 0
    for h in range(n_heads):
        ks = slice(h * dk, (h + 1) * dk)
        vs = slice(h * dv, (h + 1) * dv)
        v = v_ref[rows, vs]
        st = st_ref[d, h]
        o = _dot(a_ref[u, h], v) + _dot_nt(p_ref[u, START_SLOT, :, ks], st.astype(BF16))
        o_ref[rows, vs] = o.astype(o_ref.dtype)
        st_ref[d, h] = st * jnp.exp2(cb_ref[u, edge:edge + 1, ks]) + _dot_tn(v, p_ref[u, END_SLOT, :, ks])


def _scan_body(*refs, n_heads, dk, dv, derive_k, zero_init, emit_state):
    refs = list(refs)
    q_refs = (refs.pop(0), refs.pop(0))
    k_refs = (None, None) if derive_k else (refs.pop(0), refs.pop(0))
    v_refs = (refs.pop(0), refs.pop(0))
    g_refs = (refs.pop(0), refs.pop(0))
    m_ref, lvl_ref = refs.pop(0), refs.pop(0)
    s0_ref = None if zero_init else refs.pop(0)
    o_refs = (refs.pop(0), refs.pop(0))
    sout_ref = refs.pop(0) if emit_state else None
    st_ref, cb_ref, kf_ref, kb_ref, p_ref, a_ref = refs
    n = pl.program_id(1)

    @pl.when(n == 0)
    def _():
        for d in range(2):
            for h in range(n_heads):
                st_ref[d, h] = jnp.zeros((dv, dk), F32) if zero_init else s0_ref[d, h].T

    cps = CHUNKS_PER_STEP
    rows = lambda c: slice(c * CHUNK, (c + 1) * CHUNK)
    masks = [[lvl_ref[d].astype(F32).astype(BF16) == l for l in range(N_LEVELS + 1)] for d in range(2)]
    row = lax.broadcasted_iota(jnp.int32, (ROW_GROUP, q_refs[0].shape[-1]), 0)
    side = [((row >> l) & 1).astype(F32).astype(BF16) for l in range(N_LEVELS) if (1 << l) < ROW_GROUP]
    units = [(d, c) for c in range(cps) for d in range(2)]
    for i in range(len(units) + 1):
        if i < len(units):
            d, c = units[i]
            _scan_prepare(d * cps + c, d, rows(c), q_refs[d], k_refs[d], g_refs[d], m_ref, side, cb_ref, kf_ref,
                          kb_ref, p_ref)
        if i >= 1:
            d, c = units[i - 1]
            _scan_scores(d * cps + c, d, rows(c), q_refs[d], masks[d], kb_ref, p_ref, a_ref, n_heads, dk)
    for d in range(2):
        for c in (range(cps) if d == 0 else reversed(range(cps))):
            _scan_outputs(d * cps + c, d, rows(c), v_refs[d], cb_ref, p_ref, a_ref, st_ref, o_refs[d], n_heads,
                          dk, dv)

    if emit_state:
        @pl.when(n == pl.num_programs(1) - 1)
        def _():
            for d in range(2):
                for h in range(n_heads):
                    sout_ref[d, h] = st_ref[d, h].T


def _scan(q, k, v, g_f, g_b, s0, consts, n_heads, dk, dv, emit_state):
    b, l, _ = q.shape
    step = CHUNKS_PER_STEP * CHUNK
    n = l // step
    slots = 2 * CHUNKS_PER_STEP
    m_all, lvl = consts
    fwd = lambda w: pl.BlockSpec((None, step, w), lambda i, j: (i, j, 0))
    bwd = lambda w: pl.BlockSpec((None, step, w), lambda i, j: (i, n - 1 - j, 0))
    const = lambda shape: pl.BlockSpec(shape, lambda i, j: (0,) * len(shape))
    state = pl.BlockSpec((None, 2, n_heads, dk, dv), lambda i, j: (i, 0, 0, 0, 0))
    wk, wv = n_heads * dk, n_heads * dv
    args, in_specs = [q, q], [fwd(wk), bwd(wk)]
    if k is not None:
        args += [k, k]
        in_specs += [fwd(wk), bwd(wk)]
    args += [v, v, g_f, g_b, m_all, lvl]
    in_specs += [fwd(wv), bwd(wv), fwd(wk), bwd(wk), const(m_all.shape), const(lvl.shape)]
    if s0 is not None:
        args.append(s0)
        in_specs.append(state)
    out_specs = [fwd(wv), bwd(wv)]
    out_shape = [jax.ShapeDtypeStruct((b, l, wv), BF16)] * 2
    if emit_state:
        out_specs.append(state)
        out_shape.append(jax.ShapeDtypeStruct((b, 2, n_heads, dk, dv), F32))
    body = functools.partial(_scan_body, n_heads=n_heads, dk=dk, dv=dv, derive_k=k is None,
                             zero_init=s0 is None, emit_state=emit_state)
    return pl.pallas_call(
        body,
        grid=(b, n),
        in_specs=in_specs,
        out_specs=out_specs,
        out_shape=out_shape,
        scratch_shapes=[pltpu.VMEM((2, n_heads, dv, dk), F32),
                        pltpu.VMEM((slots, (1 + N_MXU_LEVELS) * CHUNK, wk), F32),
                        pltpu.VMEM((slots, CHUNK, wk), F32), pltpu.VMEM((slots, CHUNK, wk), BF16),
                        pltpu.VMEM((slots, N_LEVELS + 2, CHUNK, wk), BF16),
                        pltpu.VMEM((slots, n_heads, CHUNK, CHUNK), BF16)],
        compiler_params=pltpu.CompilerParams(dimension_semantics=("arbitrary", "arbitrary"),
                                             vmem_limit_bytes=VMEM_LIMIT),
        name="scan_hgrn" if k is None else "scan_gla",
    )(*args)


def _post_body(ohf_ref, ohb_ref, ogf_ref, ogb_ref, hgate_ref, gr_ref, sa_ref, sb_ref, x_ref, mod_ref,
               hgn_ref, glan_ref, n2_ref, wa_ref, wb_ref, wo_ref, wr_ref,
               x1_ref, rows_ref, group_ref):
    tm = x_ref.shape[0]
    for r0 in range(0, tm, POST_PART):
        _post_part(slice(r0, r0 + POST_PART), ohf_ref, ohb_ref, ogf_ref, ogb_ref, hgate_ref, gr_ref, sa_ref, sb_ref,
                   x_ref, mod_ref, hgn_ref, glan_ref, n2_ref, wa_ref, wb_ref, wo_ref, wr_ref,
                   x1_ref, rows_ref, group_ref)


def _post_part(rs, ohf_ref, ohb_ref, ogf_ref, ogb_ref, hgate_ref, gr_ref, sa_ref, sb_ref, x_ref, mod_ref,
               hgn_ref, glan_ref, n2_ref, wa_ref, wb_ref, wo_ref, wr_ref,
               x1_ref, rows_ref, group_ref):
    n = rs.stop - rs.start

    def normed(of_ref, ob_ref, g_ref, gate_ref, s):
        o = of_ref[rs, s].astype(F32) + ob_ref[rs, s].astype(F32)
        return (_rmsnorm(o, g_ref[...]) * gate_ref[rs, s].astype(F32)).astype(BF16)

    chunks = [slice(c * MXU_DEPTH, (c + 1) * MXU_DEPTH) for c in range(D_MODEL // MXU_DEPTH)]
    acc = lambda total, part: part if total is None else total + part
    t_hg = t_gla = mix = None
    for cs in chunks:
        heads = range(cs.start // HG_DV, cs.stop // HG_DV)
        y_hg = jnp.concatenate([normed(ohf_ref, ohb_ref, hgn_ref, hgate_ref, slice(h * HG_DV, (h + 1) * HG_DV))
                                for h in heads], axis=1)
        t_hg = acc(t_hg, _dot(y_hg, wa_ref[cs, :]))
        heads = range(cs.start // GLA_DV, cs.stop // GLA_DV)
        y_gla = jnp.concatenate([normed(ogf_ref, ogb_ref, glan_ref, gr_ref, slice(h * GLA_DV, (h + 1) * GLA_DV))
                                 for h in heads], axis=1)
        t_gla = acc(t_gla, _dot(y_gla, wb_ref[cs, :]))
    for cs in chunks:
        merged = sa_ref[rs, cs].astype(F32) * t_hg[:, cs] + sb_ref[rs, cs].astype(F32) * t_gla[:, cs]
        mix = acc(mix, _dot(merged.astype(BF16), wo_ref[cs, :]))
    x1 =x_ref[rs, :] + mod_ref[2:3, :] * mix
    x1_ref[rs, :] = x1
    h2 = _rmsnorm(x1, n2_ref[...]) * (1.0 + mod_ref[4:5, :]) + mod_ref[3:4, :]
    h2_hi = h2.astype(BF16)
    token_rows = lambda s: pl.ds(rs.start * ROW_SUBLANES + s, n, stride=ROW_SUBLANES)
    for s in range(D_MODEL // 128):
        rows_ref[token_rows(s), :] = h2[:, s * 128:(s + 1) * 128]
    for s in range(COMB_SUBLANE + 1, ROW_SUBLANES):
        rows_ref[token_rows(s), :] = jnp.zeros((n, 128), F32)

    h2_lo = (h2 - h2_hi.astype(F32)).astype(BF16)
    lt = (_dot(h2_hi, wr_ref[0]) + _dot(h2_hi, wr_ref[1]) + _dot(h2_lo, wr_ref[0])).T
    row = lambda r: lt[r:r + 1, :]
    top = lambda vals: functools.reduce(jnp.maximum, vals)

    def first_at(vals, m):
        idx = jnp.full(m.shape, len(vals), jnp.int32)
        for j in reversed(range(len(vals))):
            idx = jnp.where(vals[j] == m, j, idx)
        return idx

    gl = [row(g) for g in range(N_GROUPS)]
    gmax = top(gl)
    p_top = 1.0 / functools.reduce(jnp.add, [jnp.exp(v - gmax) for v in gl])
    g_idx = first_at(gl, gmax)
    sel = []
    for j in range(EXPERTS_PER_GROUP):
        v = row(EXPERT_LANE0 + j)
        for g in range(1, N_GROUPS):
            v = jnp.where(g_idx == g, row(EXPERT_LANE0 + g * EXPERTS_PER_GROUP + j), v)
        sel.append(v)
    m1 = top(sel)
    i1 = first_at(sel, m1)
    sel2 = [jnp.where(i1 == j, -jnp.inf, v) for j, v in enumerate(sel)]
    m2 = top(sel2)
    i2 = first_at(sel2, m2)
    r = jnp.exp(m2 - m1)
    w1 = p_top / (1.0 + r)
    lane0 = EXPERT_LANE0 + g_idx * EXPERTS_PER_GROUP
    rid = lax.broadcasted_iota(jnp.int32, lt.shape, 0)
    comb_t = jnp.where(rid == lane0 + i1, w1, jnp.where(rid == lane0 + i2, w1 * r, 0.0))
    rows_ref[token_rows(COMB_SUBLANE), :] = comb_t.T
    group_ref[:, rs] = g_idx


def _post(ohf, ohb, ogf, ogb, hgate, gr, sa, sb, x, mod, tiles_per_row, hgn, glan, n2, wa, wb, wo, wr):
    t = x.shape[0]
    tm = TM_POST
    const = lambda shape: pl.BlockSpec(shape, lambda i: (0,) * len(shape))
    tok = lambda w: pl.BlockSpec((tm, w), lambda i: (i, 0))
    return pl.pallas_call(
        _post_body,
        grid=(t // tm,),
        in_specs=[tok(1024)] * 9
                 + [pl.BlockSpec((None, 6, D_MODEL), lambda i: (i // tiles_per_row, 0, 0)),
                    const((1, HG_DV)), const((1, GLA_DV)), const((1, D_MODEL)),
                    const((1024, 1024)), const((1024, 1024)), const((1024, 1024)),
                    const((2, D_MODEL, ROUTER_LANES))],
        out_specs=[tok(1024), pl.BlockSpec((tm * ROW_SUBLANES, 128), lambda i: (i, 0)),
                   pl.BlockSpec((None, 1, tm), lambda i: (i, 0, 0))],
        out_shape=[jax.ShapeDtypeStruct((t, D_MODEL), F32), jax.ShapeDtypeStruct((t * ROW_SUBLANES, 128), F32),
                   jax.ShapeDtypeStruct((t // tm, 1, tm), jnp.int32)],
        compiler_params=pltpu.CompilerParams(vmem_limit_bytes=VMEM_LIMIT),
        name="post",
    )(ohf, ohb, ogf, ogb, hgate, gr, sa, sb, x, mod, hgn, glan, n2, wa, wb, wo, wr)


def _moe_tile_rows(t):
    narrow, wide = TR_MOE
    return wide if t // wide >= MOE_WIDE_MIN_TILES else narrow


def _route_plan(group, t):
    tr, blk = _moe_tile_rows(t), PLAN_BLOCK
    nt, nb = t // tr + N_GROUPS, t // blk
    i32 = jnp.int32
    gids = jnp.arange(N_GROUPS, dtype=i32)
    onehot = (group[:, None] == gids[None, :]).astype(i32).reshape(nb, blk, N_GROUPS)
    within = jnp.cumsum(onehot, axis=1)
    blk_end = jnp.cumsum(within[:, -1, :], axis=0)
    blk_start = blk_end - within[:, -1, :]
    counts = blk_end[-1]
    tiles = (counts + tr - 1) // tr
    tile_end = jnp.cumsum(tiles)
    tile_group = jnp.minimum(jnp.sum(jnp.arange(nt, dtype=i32)[:, None] >= tile_end[None, :], axis=1),
                             N_GROUPS - 1).astype(i32)
    row = jnp.arange(nt * tr, dtype=i32)
    gsel = (jnp.repeat(tile_group, tr)[:, None] == gids[None, :]).astype(i32)
    pick = lambda per_group: jnp.sum(gsel * per_group[None, :], axis=1)
    k = row - pick(tile_end - tiles) * tr
    valid = k < pick(counts)
    per_block = lambda table: jnp.sum(gsel[:, None, :] * table[None, :, :], axis=2)
    b_row = jnp.minimum(jnp.sum(per_block(blk_end) <= k[:, None], axis=1), nb - 1)
    bsel = (b_row[:, None] == jnp.arange(nb, dtype=i32)[None, :]).astype(i32)
    k_local = k - jnp.sum(bsel * per_block(blk_start), axis=1)
    sel = (bsel[:, :, None] * gsel[:, None, :]).reshape(nt * tr, nb * N_GROUPS).astype(F32)
    table = jnp.transpose(within, (0, 2, 1)).reshape(nb * N_GROUPS, blk).astype(F32)
    ranks = jnp.dot(sel, table)
    t_local = jnp.sum(ranks <= k_local[:, None].astype(F32), axis=1).astype(i32)
    row_src = jnp.where(valid, b_row * blk + t_local, 0)
    spare = t + ((row // tr) % 2) * tr + row % tr
    return tile_group, row_src, jnp.where(valid, row_src, spare), tile_end[-1:].astype(i32)


def _moe_body(tg_ref, src_ref, dst_ref, used_ref, rows_hbm, wg_ref, wu_ref, wd_ref, out_hbm, xbuf, ybuf, gsem, ssem):
    tr = xbuf.shape[1] // ROW_SUBLANES
    i = pl.program_id(0)
    n_used = used_ref[0]
    slot = i % 2

    def gather_row(tile, slot_, r):
        tok = src_ref[tile * tr + r]
        return pltpu.make_async_copy(rows_hbm.at[pl.ds(pl.multiple_of(tok * ROW_SUBLANES, ROW_SUBLANES), ROW_SUBLANES)],
                                     xbuf.at[slot_, pl.ds(r * ROW_SUBLANES, ROW_SUBLANES)], gsem.at[slot_])

    def scatter_row(tile, slot_, r):
        tok = dst_ref[tile * tr + r]
        return pltpu.make_async_copy(ybuf.at[slot_, pl.ds(r * OUT_SUBLANES, OUT_SUBLANES)],
                                     out_hbm.at[pl.ds(pl.multiple_of(tok * OUT_SUBLANES, OUT_SUBLANES), OUT_SUBLANES)],
                                     ssem.at[slot_])

    def for_rows(fn):
        def body(r, carry):
            fn(r)
            return carry
        lax.fori_loop(0, tr, body, 0, unroll=32)

    def gathered(slot_):
        return pltpu.make_async_copy(rows_hbm.at[pl.ds(0, tr * ROW_SUBLANES)], xbuf.at[slot_], gsem.at[slot_])

    def scattered(slot_):
        return pltpu.make_async_copy(ybuf.at[slot_], out_hbm.at[pl.ds(0, tr * OUT_SUBLANES)], ssem.at[slot_])

    @pl.when(i == 0)
    def _():
        for_rows(lambda r: gather_row(0, 0, r).start())
        n_tok = out_hbm.shape[0] // OUT_SUBLANES - 2 * tr
        ybuf[...] = jnp.zeros_like(ybuf)
        for s in range(2):
            spare = pltpu.make_async_copy(
                ybuf.at[s], out_hbm.at[pl.ds((n_tok + s * tr) * OUT_SUBLANES, tr * OUT_SUBLANES)], ssem.at[s])
            spare.start()
            spare.wait()

    @pl.when(i < n_used)
    def _():
        @pl.when(i + 1 < n_used)
        def _():
            for_rows(lambda r: gather_row(i + 1, 1 - slot, r).start())

        gathered(slot).wait()

        @pl.when(i >= 2)
        def _():
            scattered(slot).wait()

        x = jnp.concatenate([xbuf[slot, pl.ds(s, tr, stride=ROW_SUBLANES), :] for s in range(D_MODEL // 128)],
                            axis=1).astype(BF16)
        comb = xbuf[slot, pl.ds(COMB_SUBLANE, tr, stride=ROW_SUBLANES), :]
        lane = lax.broadcasted_iota(jnp.int32, comb.shape, 1)
        lane0 = EXPERT_LANE0 + tg_ref[i] * EXPERTS_PER_GROUP
        acc = jnp.zeros((tr, D_MODEL), F32)
        for e in range(EXPERTS_PER_GROUP):
            col = jnp.sum(jnp.where(lane == lane0 + e, comb, 0.0), axis=-1, keepdims=True)
            hid = (_silu(_dot(x, wg_ref[e])) * _dot(x, wu_ref[e]) * col).astype(BF16)
            acc = acc + _dot(hid, wd_ref[e])

        for s in range(D_MODEL // 128):
            ybuf[slot, pl.ds(s, tr, stride=OUT_SUBLANES), :] = acc[:, s * 128:(s + 1) * 128]

        for_rows(lambda r: scatter_row(i, slot, r).start())

        @pl.when(i == n_used - 1)
        def _():
            scattered(slot).wait()
            scattered(1 - slot).wait()


def _moe(rows, plan, wg, wu, wd, t):
    tr = _moe_tile_rows(t)
    assert t // tr >= 2
    nt = plan[0].shape[0]
    group_w = lambda shape: pl.BlockSpec((EXPERTS_PER_GROUP,) + shape, lambda i, tg, src, dst, used: (tg[i], 0, 0))
    grid_spec = pltpu.PrefetchScalarGridSpec(
        num_scalar_prefetch=4,
        grid=(nt,),
        in_specs=[pl.BlockSpec(memory_space=pl.ANY),
                  group_w((D_MODEL, D_EXPERT)), group_w((D_MODEL, D_EXPERT)), group_w((D_EXPERT, D_MODEL))],
        out_specs=pl.BlockSpec(memory_space=pl.ANY),
        scratch_shapes=[pltpu.VMEM((2, tr * ROW_SUBLANES, 128), F32), pltpu.VMEM((2, tr * OUT_SUBLANES, 128), F32),
                        pltpu.SemaphoreType.DMA((2,)), pltpu.SemaphoreType.DMA((2,))])
    return pl.pallas_call(
        _moe_body,
        grid_spec=grid_spec,
        out_shape=jax.ShapeDtypeStruct(((t + 2 * tr) * OUT_SUBLANES, 128), F32),
        compiler_params=pltpu.CompilerParams(dimension_semantics=("arbitrary",), vmem_limit_bytes=VMEM_LIMIT),
        name="moe",
    )(*plan, rows, wg, wu, wd)


def _final_body(x1_ref, moe_ref, mod_ref, fn_ref, y_ref):
    tm = x1_ref.shape[0]
    moe = jnp.concatenate([moe_ref[pl.ds(s, tm, stride=OUT_SUBLANES), :] for s in range(D_MODEL // 128)], axis=1)
    y_ref[...] = _rmsnorm(x1_ref[...] + mod_ref[5:6, :] * moe, fn_ref[...])


def _final(x1, moe, mod, tiles_per_row, fn):
    t = x1.shape[0]
    tm = TM_FINAL
    return pl.pallas_call(
        _final_body,
        grid=(t // tm,),
        in_specs=[pl.BlockSpec((tm, D_MODEL), lambda i: (i, 0)),
                  pl.BlockSpec((tm * OUT_SUBLANES, 128), lambda i: (i, 0)),
                  pl.BlockSpec((None, 6, D_MODEL), lambda i: (i // tiles_per_row, 0, 0)),
                  pl.BlockSpec((1, D_MODEL), lambda i: (0, 0))],
        out_specs=pl.BlockSpec((tm, D_MODEL), lambda i: (i, 0)),
        out_shape=jax.ShapeDtypeStruct((t, D_MODEL), F32),
        compiler_params=pltpu.CompilerParams(vmem_limit_bytes=VMEM_LIMIT),
        name="final",
    )(x1, moe, mod, fn)


def _trunk_layer(x, mod, s0_hg, s0_gla, emit_state, w):
    b, l, _ = x.shape
    xt = x.reshape(b * l, D_MODEL)
    per_row = lambda tm: (l // tm) if mod.shape[0] > 1 else (b * l // tm)
    (hq, gf, gb, hv, hgate, gq, gk, gv, gr, laf, lab, sa, sb) = _inproj(
        xt, mod, per_row(TM_PROJ), w["n1"], w["lbp"], w["wmain"], w["waup"], w["ba"])
    seq = lambda a: a.reshape(b, l, a.shape[-1])
    hg = _scan(seq(hq), None, seq(hv), seq(gf), seq(gb), s0_hg, w["scan_consts"], HG_H, HG_DK, HG_DV,
               emit_state)
    gla = _scan(seq(gq), seq(gk), seq(gv), seq(laf), seq(lab), s0_gla, w["scan_consts"], GLA_H, GLA_DK,
                GLA_DV, emit_state)
    flat = lambda a: a.reshape(b * l, a.shape[-1])
    x1, rows, group = _post(flat(hg[0]), flat(hg[1]), flat(gla[0]), flat(gla[1]), hgate, gr, sa, sb, xt, mod,
                           per_row(TM_POST), w["hgn"], w["glan"], w["n2"], w["wa"], w["wb"], w["wo"], w["wr"])
    plan = _route_plan(group.reshape(b * l), b * l)
    moe = _moe(rows, plan, w["wg"], w["wu"], w["wd"], b * l)
    y = _final(x1, moe, mod, per_row(TM_FINAL), w["fn"])
    states = (hg[2], gla[2]) if emit_state else (None, None)
    return y.reshape(b, l, D_MODEL), states


def kernel(x_prompt, x_sample, state_hgrn, state_gla, c, c_ctx, ada_w, ada_b, norm1_g, norm2_g, w_in,
           hg_lb_param, hg_norm_g, gla_wa_up, gla_ba, gla_norm_g, w_br_a, w_br_b, w_out,
           w_router_group, w_router_expert, w_exp_gate, w_exp_up, w_exp_down, final_norm_g):
    nb = c.shape[0]
    cond = jnp.concatenate([c, c_ctx[None, :], jnp.zeros((8 - nb - 1, D_MODEL), F32)], axis=0)
    mod = _modulation(cond, ada_w[0], ada_b).reshape(8, 6, D_MODEL)

    zeros = jnp.zeros((GLA_RANK, GLA_H * GLA_DK), F32)
    router = jnp.concatenate(
        [w_router_group[0], jnp.transpose(w_router_expert[0], (1, 0, 2)).reshape(D_MODEL, N_EXPERTS),
         jnp.zeros((D_MODEL, ROUTER_LANES - N_GROUPS - N_EXPERTS), F32)], axis=1)
    w = {
        "n1": norm1_g, "n2": norm2_g, "fn": final_norm_g[None, :], "lbp": hg_lb_param,
        "wmain": w_in[0].astype(BF16),
        "waup": jnp.concatenate([jnp.concatenate([gla_wa_up[0, 0], zeros], axis=1),
                                 jnp.concatenate([zeros, gla_wa_up[0, 1]], axis=1)], axis=0).astype(BF16),
        "ba": gla_ba[0].reshape(1, 2 * GLA_H * GLA_DK),
        "hgn": hg_norm_g, "glan": gla_norm_g,
        "wa": w_br_a[0].astype(BF16), "wb": w_br_b[0].astype(BF16), "wo": w_out[0].astype(BF16),
        "wr": jnp.stack([router.astype(BF16), (router - router.astype(BF16).astype(F32)).astype(BF16)]),
        "wg": w_exp_gate[0].astype(BF16), "wu": w_exp_up[0].astype(BF16), "wd": w_exp_down[0].astype(BF16),
        "scan_consts": _scan_constants(),
    }
    y_prompt, (st_hg, st_gla) = _trunk_layer(x_prompt, mod[nb:nb + 1], None, None, True, w)
    y_sample, _ = _trunk_layer(x_sample, mod[:nb], state_hgrn[:, 0], state_gla[:, 0], False, w)
    return y_prompt, y_sample, st_hg[:, None], st_gla[:, None]
```

```python
import functools

import numpy as np
import jax
import jax.numpy as jnp
from jax import lax
from jax.experimental import pallas as pl
from jax.experimental.pallas import tpu as pltpu

F32 = jnp.float32
BF16 = jnp.bfloat16

D_MODEL = 1024
CHUNK = 64
N_LEVELS = 6
CHUNKS_PER_STEP = 4
HG_H, HG_DK, HG_DV = 8, 128, 128
GLA_H, GLA_DK, GLA_DV = 4, 128, 256
GLA_RANK = 16
GLA_GATE_NORM = 16.0
N_GROUPS = 4
EXPERTS_PER_GROUP = 8
N_EXPERTS = N_GROUPS * EXPERTS_PER_GROUP
D_EXPERT = D_MODEL // 4
EPS = 1e-6
ROUTER_LANES = 128
EXPERT_LANE0 = N_GROUPS
VMEM_LIMIT = 56 * 1024 * 1024
MXU_DEPTH = 256

TM_PROJ = 256
TM_POST = 256
POST_PART = 256
TM_FINAL = 1024
TR_MOE = (256, 512)
MOE_WIDE_MIN_TILES = 16
PLAN_BLOCK = 256
OUT_SUBLANES = D_MODEL // 128
COMB_SUBLANE = OUT_SUBLANES
ROW_SUBLANES = 16


def _sigmoid(x):
    return 1.0 / (1.0 + jnp.exp(-x))


def _silu(x):
    return x * _sigmoid(x)


def _log_sigmoid(x):
    return jnp.minimum(x, 0.0) - jnp.log(1.0 + jnp.exp(-jnp.abs(x)))


def _rmsnorm(x, g):
    return x * lax.rsqrt(jnp.mean(x * x, axis=-1, keepdims=True) + EPS) * g


def _dot(a, b):
    return jnp.dot(a, b, preferred_element_type=F32)


def _dot_nt(a, b):
    return lax.dot_general(a, b, (((1,), (1,)), ((), ())), preferred_element_type=F32)


def _dot_tn(a, b):
    return lax.dot_general(a, b, (((0,), (0,)), ((), ())), preferred_element_type=F32)


def _mod_body(c_ref, w_ref, b_ref, o_ref):
    s, w = _silu(c_ref[...]), w_ref[...]
    s_hi, w_hi = s.astype(BF16), w.astype(BF16)
    s_lo, w_lo = (s - s_hi.astype(F32)).astype(BF16), (w - w_hi.astype(F32)).astype(BF16)
    o_ref[...] = _dot(s_hi, w_hi) + _dot(s_hi, w_lo) + _dot(s_lo, w_hi) + b_ref[...]


def _modulation(cond, ada_w, ada_b):
    rows, n = cond.shape[0], ada_w.shape[1]
    tn = n // 4
    return pl.pallas_call(
        _mod_body,
        grid=(n // tn,),
        in_specs=[pl.BlockSpec((rows, D_MODEL), lambda j: (0, 0)),
                  pl.BlockSpec((D_MODEL, tn), lambda j: (0, j)),
                  pl.BlockSpec((1, tn), lambda j: (0, j))],
        out_specs=pl.BlockSpec((rows, tn), lambda j: (0, j)),
        out_shape=jax.ShapeDtypeStruct((rows, n), F32),
        compiler_params=pltpu.CompilerParams(vmem_limit_bytes=VMEM_LIMIT),
        name="modulation",
    )(cond, ada_w, ada_b)


_C_HQ, _C_HFF, _C_HFB, _C_HI, _C_HGATE = 0, 1024, 2048, 3072, 4096
_C_GQ, _C_GK, _C_GV, _C_GR, _C_TAIL = 5120, 5632, 6144, 7168, 8192
GA_WIDTH = 2 * GLA_RANK
W_TAIL = GA_WIDTH + 2 * D_MODEL
W_MAIN = _C_TAIL + W_TAIL


def _inproj_body(x_ref, mod_ref, n1_ref, lbp_ref, wmain_ref, waup_ref, ba_ref,
                 hq_ref, gf_ref, gb_ref, hv_ref, hgate_ref, gq_ref, gk_ref, gv_ref, gr_ref,
                 laf_ref, lab_ref, sa_ref, sb_ref):
    x = x_ref[...]
    h = _rmsnorm(x, n1_ref[...]) * (1.0 + mod_ref[1:2, :]) + mod_ref[0:1, :]
    hb = h.astype(BF16)

    def proj(c0, w):
        return _dot(hb, wmain_ref[:, c0:c0 + w])

    p0, p1 = lbp_ref[0:1, :], lbp_ref[1:2, :]
    pm = jnp.maximum(p0, p1)
    e0, e1 = jnp.exp(p0 - pm), jnp.exp(p1 - pm)
    lb = e0 / (e0 + e1)

    tail = proj(_C_TAIL, W_TAIL)
    sa_ref[...] = _sigmoid(tail[:, GA_WIDTH:GA_WIDTH + D_MODEL]).astype(BF16)
    sb_ref[...] = _sigmoid(tail[:, GA_WIDTH + D_MODEL:]).astype(BF16)
    ga = tail[:, :GA_WIDTH].astype(BF16)
    la = _log_sigmoid(_dot(ga, waup_ref[...]) + ba_ref[...]) * (1.0 / GLA_GATE_NORM)
    laf_ref[...] = la[:, :512]
    lab_ref[...] = la[:, 512:]
    gf_ref[...] = jnp.log(lb + (1.0 - lb) * _sigmoid(proj(_C_HFF, 1024)))
    gb_ref[...] = jnp.log(lb + (1.0 - lb) * _sigmoid(proj(_C_HFB, 1024)))
    hgate_ref[...] = _silu(proj(_C_HGATE, 1024)).astype(BF16)
    gr_ref[...] = _silu(proj(_C_GR, 1024)).astype(BF16)
    hq_ref[...] = (proj(_C_HQ, 1024) * HG_DK ** -0.5).astype(BF16)
    gq_ref[...] = (proj(_C_GQ, 512) * GLA_DK ** -0.5).astype(BF16)
    gv_ref[...] = proj(_C_GV, 1024).astype(BF16)
    gk_ref[...] = proj(_C_GK, 512).astype(BF16)
    hv_ref[...] = proj(_C_HI, 1024).astype(BF16)


def _inproj(x, mod, tiles_per_row, n1, lbp, wmain, waup, ba):
    t = x.shape[0]
    tm = TM_PROJ
    const = lambda shape: pl.BlockSpec(shape, lambda i: (0,) * len(shape))
    tok = lambda w: pl.BlockSpec((tm, w), lambda i: (i, 0))
    widths = (1024, 1024, 1024, 1024, 1024, 512, 512, 1024, 1024, 512, 512, 1024, 1024)
    dtypes = (BF16, F32, F32, BF16, BF16, BF16, BF16, BF16, BF16, F32, F32, BF16, BF16)
    return pl.pallas_call(
        _inproj_body,
        grid=(t // tm,),
        in_specs=[tok(D_MODEL),
                  pl.BlockSpec((None, 6, D_MODEL), lambda i: (i // tiles_per_row, 0, 0)),
                  const((1, D_MODEL)), const((2, D_MODEL)),
                  pl.BlockSpec((D_MODEL, W_MAIN), lambda i: (0, 0), pipeline_mode=pl.Buffered(1)),
                  const((GA_WIDTH, 1024)), const((1, 1024))],
        out_specs=[tok(w) for w in widths],
        out_shape=[jax.ShapeDtypeStruct((t, w), dt) for w, dt in zip(widths, dtypes)],
        compiler_params=pltpu.CompilerParams(vmem_limit_bytes=VMEM_LIMIT),
        name="inproj",
    )(x, mod, n1, lbp, wmain, waup, ba)


N_MXU_LEVELS = 3
ROW_GROUP = 16
START_SLOT, END_SLOT = N_LEVELS, N_LEVELS + 1
LOG2E = 1.4426950408889634


def _boundary_row(t, h, d):
    base = (t // (2 * h)) * 2 * h
    return base + h - 1 if d == 0 else base + h


def _scan_constants():
    c = CHUNK
    tri = np.tril(np.ones((c, c), np.float32))
    m = [tri] + [np.abs(tri - tri[[_boundary_row(t, 1 << l, 0) for t in range(c)]]) for l in range(N_MXU_LEVELS)]
    m = np.stack(m)
    lvl = np.full((c, c), N_LEVELS + 1, np.int32)
    for t in range(c):
        lvl[t, t] = N_LEVELS
        for s in range(t):
            lvl[t, s] = int(np.floor(np.log2(t ^ s)))
    m_fwd = m.reshape(-1, c)
    m_bwd = m[:, ::-1, ::-1].reshape(-1, c)
    m_both = np.stack([m_fwd, m_bwd])
    return (jnp.asarray(np.concatenate([m_both, m_both], axis=2), BF16), jnp.asarray(np.stack([lvl, lvl.T])))


def _scan_prepare(u, d, rows, q_ref, k_ref, g_ref, m_ref, side, cb_ref, kf_ref, kb_ref, p_ref):
    c = CHUNK
    w = g_ref.shape[-1]
    g = g_ref[rows, :]
    gs = g * LOG2E
    g1 = gs.astype(BF16)
    g2 = (gs - g1.astype(F32)).astype(BF16)
    cb_ref[u] = _dot(m_ref[d], jnp.concatenate([g1, g2], axis=0))
    k32 = (1.0 - jnp.exp(g)) if k_ref is None else k_ref[rows, :].astype(F32)
    kf_ref[u] = k32
    kb_ref[u] = k32.astype(BF16)
    bcast = lambda r, n: jnp.broadcast_to(cb_ref[u, r:r + 1, :], (n, w))
    for i in range(c // ROW_GROUP):
        t0 = ROW_GROUP * i
        grp = slice(t0, t0 + ROW_GROUP)
        cum = cb_ref[u, grp, :]
        q = q_ref[rows.start + t0:rows.start + t0 + ROW_GROUP, :]
        k = kb_ref[u, grp, :]
        for l in range(N_LEVELS):
            h = 1 << l
            if l < N_MXU_LEVELS:
                ex = cb_ref[u, c * (l + 1) + t0:c * (l + 1) + t0 + ROW_GROUP, :]
            else:
                ex = []
                for t in (t0, t0 + 8):
                    half = slice(t - t0, t - t0 + 8)
                    bnd = bcast(_boundary_row(t, h, d), 8)
                    ex.append(cum[half] - bnd if ((t & h) != 0) == (d == 0) else bnd - cum[half])
                ex = jnp.concatenate(ex, axis=0)
            if h >= ROW_GROUP:
                src = q if ((t0 & h) != 0) == (d == 0) else k
            else:
                upper = side[l] > 0
                src = jnp.where(upper, q, k) if d == 0 else jnp.where(upper, k, q)
            p_ref[u, l, grp, :] = src * jnp.exp2(ex).astype(BF16)
        p_ref[u, START_SLOT, grp, :] = q * jnp.exp2(cum).astype(BF16)
        edge = c - 1 if d == 0 else 0
        p_ref[u, END_SLOT, grp, :] = (kf_ref[u, grp, :] * jnp.exp2(bcast(edge, ROW_GROUP) - cum)).astype(BF16)


def _scan_scores(u, d, rows, q_ref, masks, kb_ref, p_ref, a_ref, n_heads, dk):
    groups = range(CHUNK // ROW_GROUP)
    piece = lambda x, g: x[g * ROW_GROUP:(g + 1) * ROW_GROUP]
    mask_pieces = [[piece(m, g) for g in groups] for m in masks]
    for h in range(n_heads):
        ks = slice(h * dk, (h + 1) * dk)
        diag = _dot_nt(q_ref[rows, ks], kb_ref[u, :, ks]).astype(BF16)
        a = [jnp.where(mask_pieces[N_LEVELS][g], piece(diag, g), jnp.zeros((), BF16)) for g in groups]
        for l in range(N_LEVELS):
            p = p_ref[u, l, :, ks]
            if (1 << l) < ROW_GROUP:
                query = list(groups)
                x = _dot_nt(p, p).astype(BF16)
            else:
                query = [g for g in groups if ((g * ROW_GROUP & (1 << l)) != 0) == (d == 0)]
                x = _dot_nt(jnp.concatenate([piece(p, g) for g in query], axis=0), p).astype(BF16)
            for j, g in enumerate(query):
                a[g] = jnp.where(mask_pieces[l][g], piece(x, j), a[g])
        a_ref[u, h] = jnp.concatenate(a, axis=0)


def _scan_outputs(u, d, rows, v_ref, cb_ref, p_ref, a_ref, st_ref, o_ref, n_heads, dk, dv):
    edge = CHUNK - 1 if d == 0 else 0
    for h in range(n_heads):
        ks = slice(h * dk, (h + 1) * dk)
        vs = slice(h * dv, (h + 1) * dv)
        v = v_ref[rows, vs]
        st = st_ref[d, h]
        o = _dot(jnp.concatenate([p_ref[u, START_SLOT, :, ks], a_ref[u, h]], axis=1),
                 jnp.concatenate([st.astype(BF16), v], axis=0))
        o_ref[rows, vs] = o.astype(o_ref.dtype)
        decay = jnp.broadcast_to(jnp.exp2(cb_ref[u, edge:edge + 1, ks]), (dk, dk)).T
        decay = jnp.concatenate([decay] * (dv // dk), axis=1)
        st_ref[d, h] = st * decay + _dot_tn(p_ref[u, END_SLOT, :, ks], v)


def _scan_body(*refs, n_heads, dk, dv, derive_k, zero_init, emit_state):
    refs = list(refs)
    q_refs = (refs.pop(0), refs.pop(0))
    k_refs = (None, None) if derive_k else (refs.pop(0), refs.pop(0))
    v_refs = (refs.pop(0), refs.pop(0))
    g_refs = (refs.pop(0), refs.pop(0))
    m_ref, lvl_ref = refs.pop(0), refs.pop(0)
    s0_ref = None if zero_init else refs.pop(0)
    o_refs = (refs.pop(0), refs.pop(0))
    sout_ref = refs.pop(0) if emit_state else None
    st_ref, cb_ref, kf_ref, kb_ref, p_ref, a_ref = refs
    n = pl.program_id(1)

    @pl.when(n == 0)
    def _():
        for d in range(2):
            for h in range(n_heads):
                st_ref[d, h] = jnp.zeros((dk, dv), F32) if zero_init else s0_ref[d, h]

    cps = CHUNKS_PER_STEP
    rows = lambda c: slice(c * CHUNK, (c + 1) * CHUNK)
    masks = [[lvl_ref[d].astype(F32).astype(BF16) == l for l in range(N_LEVELS + 1)] for d in range(2)]
    row = lax.broadcasted_iota(jnp.int32, (ROW_GROUP, q_refs[0].shape[-1]), 0)
    side = [((row >> l) & 1).astype(F32).astype(BF16) for l in range(N_LEVELS) if (1 << l) < ROW_GROUP]
    units = [(d, c if d == 0 else cps - 1 - c) for c in range(cps) for d in range(2)]
    for i in range(len(units) + 2):
        if i < len(units):
            d, c = units[i]
            _scan_prepare(d * cps + c, d, rows(c), q_refs[d], k_refs[d], g_refs[d], m_ref, side, cb_ref, kf_ref,
                          kb_ref, p_ref)
        if 1 <= i <= len(units):
            d, c = units[i - 1]
            _scan_scores(d * cps + c, d, rows(c), q_refs[d], masks[d], kb_ref, p_ref, a_ref, n_heads, dk)
        if i >= 2:
            d, c = units[i - 2]
            _scan_outputs(d * cps + c, d, rows(c), v_refs[d], cb_ref, p_ref, a_ref, st_ref, o_refs[d], n_heads,
                          dk, dv)

    if emit_state:
        @pl.when(n == pl.num_programs(1) - 1)
        def _():
            for d in range(2):
                for h in range(n_heads):
                    sout_ref[d, h] = st_ref[d, h]


def _scan(q, k, v, g_f, g_b, s0, consts, n_heads, dk, dv, emit_state):
    b, l, _ = q.shape
    step = CHUNKS_PER_STEP * CHUNK
    n = l // step
    slots = 2 * CHUNKS_PER_STEP
    m_all, lvl = consts
    fwd = lambda w: pl.BlockSpec((None, step, w), lambda i, j: (i, j, 0))
    bwd = lambda w: pl.BlockSpec((None, step, w), lambda i, j: (i, n - 1 - j, 0))
    const = lambda shape: pl.BlockSpec(shape, lambda i, j: (0,) * len(shape))
    state = pl.BlockSpec((None, 2, n_heads, dk, dv), lambda i, j: (i, 0, 0, 0, 0))
    wk, wv = n_heads * dk, n_heads * dv
    args, in_specs = [q, q], [fwd(wk), bwd(wk)]
    if k is not None:
        args += [k, k]
        in_specs += [fwd(wk), bwd(wk)]
    args += [v, v, g_f, g_b, m_all, lvl]
    in_specs += [fwd(wv), bwd(wv), fwd(wk), bwd(wk), const(m_all.shape), const(lvl.shape)]
    if s0 is not None:
        args.append(s0)
        in_specs.append(state)
    out_specs = [fwd(wv), bwd(wv)]
    out_shape = [jax.ShapeDtypeStruct((b, l, wv), BF16)] * 2
    if emit_state:
        out_specs.append(state)
        out_shape.append(jax.ShapeDtypeStruct((b, 2, n_heads, dk, dv), F32))
    body = functools.partial(_scan_body, n_heads=n_heads, dk=dk, dv=dv, derive_k=k is None,
                             zero_init=s0 is None, emit_state=emit_state)
    return pl.pallas_call(
        body,
        grid=(b, n),
        in_specs=in_specs,
        out_specs=out_specs,
        out_shape=out_shape,
        scratch_shapes=[pltpu.VMEM((2, n_heads, dk, dv), F32),
                        pltpu.VMEM((slots, (1 + N_MXU_LEVELS) * CHUNK, wk), F32),
                        pltpu.VMEM((slots, CHUNK, wk), F32), pltpu.VMEM((slots, CHUNK, wk), BF16),
                        pltpu.VMEM((slots, N_LEVELS + 2, CHUNK, wk), BF16),
                        pltpu.VMEM((slots, n_heads, CHUNK, CHUNK), BF16)],
        compiler_params=pltpu.CompilerParams(dimension_semantics=("arbitrary", "arbitrary"),
                                             vmem_limit_bytes=VMEM_LIMIT),
        name="scan_hgrn" if k is None else "scan_gla",
    )(*args)


def _post_body(ohf_ref, ohb_ref, ogf_ref, ogb_ref, hgate_ref, gr_ref, sa_ref, sb_ref, x_ref, mod_ref,
               hgn_ref, glan_ref, n2_ref, wa_ref, wb_ref, wo_ref, wr_ref,
               x1_ref, rows_ref, group_ref):
    tm = x_ref.shape[0]
    for r0 in range(0, tm, POST_PART):
        _post_part(slice(r0, r0 + POST_PART), ohf_ref, ohb_ref, ogf_ref, ogb_ref, hgate_ref, gr_ref, sa_ref, sb_ref,
                   x_ref, mod_ref, hgn_ref, glan_ref, n2_ref, wa_ref, wb_ref, wo_ref, wr_ref,
                   x1_ref, rows_ref, group_ref)


def _post_part(rs, ohf_ref, ohb_ref, ogf_ref, ogb_ref, hgate_ref, gr_ref, sa_ref, sb_ref, x_ref, mod_ref,
               hgn_ref, glan_ref, n2_ref, wa_ref, wb_ref, wo_ref, wr_ref,
               x1_ref, rows_ref, group_ref):
    n = rs.stop - rs.start

    def normed(of_ref, ob_ref, g_ref, gate_ref, s):
        o = of_ref[rs, s].astype(F32) + ob_ref[rs, s].astype(F32)
        return (_rmsnorm(o, g_ref[...]) * gate_ref[rs, s].astype(F32)).astype(BF16)

    chunks = [slice(c * MXU_DEPTH, (c + 1) * MXU_DEPTH) for c in range(D_MODEL // MXU_DEPTH)]
    acc = lambda total, part: part if total is None else total + part
    t_hg = t_gla = mix = None
    for cs in chunks:
        heads = range(cs.start // HG_DV, cs.stop // HG_DV)
        y_hg = jnp.concatenate([normed(ohf_ref, ohb_ref, hgn_ref, hgate_ref, slice(h * HG_DV, (h + 1) * HG_DV))
                                for h in heads], axis=1)
        t_hg = acc(t_hg, _dot(y_hg, wa_ref[cs, :]))
        heads = range(cs.start // GLA_DV, cs.stop // GLA_DV)
        y_gla = jnp.concatenate([normed(ogf_ref, ogb_ref, glan_ref, gr_ref, slice(h * GLA_DV, (h + 1) * GLA_DV))
                                 for h in heads], axis=1)
        t_gla = acc(t_gla, _dot(y_gla, wb_ref[cs, :]))
    for cs in chunks:
        merged = sa_ref[rs, cs].astype(F32) * t_hg[:, cs] + sb_ref[rs, cs].astype(F32) * t_gla[:, cs]
        mix = acc(mix, _dot(merged.astype(BF16), wo_ref[cs, :]))
    x1 =x_ref[rs, :] + mod_ref[2:3, :] * mix
    x1_ref[rs, :] = x1
    h2 = _rmsnorm(x1, n2_ref[...]) * (1.0 + mod_ref[4:5, :]) + mod_ref[3:4, :]
    h2_hi = h2.astype(BF16)
    token_rows = lambda s: pl.ds(rs.start * ROW_SUBLANES + s, n, stride=ROW_SUBLANES)
    for s in range(D_MODEL // 128):
        rows_ref[token_rows(s), :] = h2[:, s * 128:(s + 1) * 128]
    for s in range(COMB_SUBLANE + 1, ROW_SUBLANES):
        rows_ref[token_rows(s), :] = jnp.zeros((n, 128), F32)

    h2_lo = (h2 - h2_hi.astype(F32)).astype(BF16)
    lt = (_dot(h2_hi, wr_ref[0]) + _dot(h2_hi, wr_ref[1]) + _dot(h2_lo, wr_ref[0])).T
    row = lambda r: lt[r:r + 1, :]
    top = lambda vals: functools.reduce(jnp.maximum, vals)

    def first_at(vals, m):
        idx = jnp.full(m.shape, len(vals), jnp.int32)
        for j in reversed(range(len(vals))):
            idx = jnp.where(vals[j] == m, j, idx)
        return idx

    gl = [row(g) for g in range(N_GROUPS)]
    gmax = top(gl)
    p_top = 1.0 / functools.reduce(jnp.add, [jnp.exp(v - gmax) for v in gl])
    g_idx = first_at(gl, gmax)
    sel = []
    for j in range(EXPERTS_PER_GROUP):
        v = row(EXPERT_LANE0 + j)
        for g in range(1, N_GROUPS):
            v = jnp.where(g_idx == g, row(EXPERT_LANE0 + g * EXPERTS_PER_GROUP + j), v)
        sel.append(v)
    m1 = top(sel)
    i1 = first_at(sel, m1)
    sel2 = [jnp.where(i1 == j, -jnp.inf, v) for j, v in enumerate(sel)]
    m2 = top(sel2)
    i2 = first_at(sel2, m2)
    r = jnp.exp(m2 - m1)
    w1 = p_top / (1.0 + r)
    lane0 = EXPERT_LANE0 + g_idx * EXPERTS_PER_GROUP
    rid = lax.broadcasted_iota(jnp.int32, lt.shape, 0)
    comb_t = jnp.where(rid == lane0 + i1, w1, jnp.where(rid == lane0 + i2, w1 * r, 0.0))
    rows_ref[token_rows(COMB_SUBLANE), :] = comb_t.T
    group_ref[:, rs] = g_idx


def _post(ohf, ohb, ogf, ogb, hgate, gr, sa, sb, x, mod, tiles_per_row, hgn, glan, n2, wa, wb, wo, wr):
    t = x.shape[0]
    tm = TM_POST
    const = lambda shape: pl.BlockSpec(shape, lambda i: (0,) * len(shape))
    tok = lambda w: pl.BlockSpec((tm, w), lambda i: (i, 0))
    return pl.pallas_call(
        _post_body,
        grid=(t // tm,),
        in_specs=[tok(1024)] * 9
                 + [pl.BlockSpec((None, 6, D_MODEL), lambda i: (i // tiles_per_row, 0, 0)),
                    const((1, HG_DV)), const((1, GLA_DV)), const((1, D_MODEL)),
                    const((1024, 1024)), const((1024, 1024)), const((1024, 1024)),
                    const((2, D_MODEL, ROUTER_LANES))],
        out_specs=[tok(1024), pl.BlockSpec((tm * ROW_SUBLANES, 128), lambda i: (i, 0)),
                   pl.BlockSpec((None, 1, tm), lambda i: (i, 0, 0))],
        out_shape=[jax.ShapeDtypeStruct((t, D_MODEL), F32), jax.ShapeDtypeStruct((t * ROW_SUBLANES, 128), F32),
                   jax.ShapeDtypeStruct((t // tm, 1, tm), jnp.int32)],
        compiler_params=pltpu.CompilerParams(vmem_limit_bytes=VMEM_LIMIT),
        name="post",
    )(ohf, ohb, ogf, ogb, hgate, gr, sa, sb, x, mod, hgn, glan, n2, wa, wb, wo, wr)


def _moe_tile_rows(t):
    narrow, wide = TR_MOE
    return wide if t // wide >= MOE_WIDE_MIN_TILES else narrow


def _route_plan(group, t):
    tr, blk = _moe_tile_rows(t), PLAN_BLOCK
    nt, nb = t // tr + N_GROUPS, t // blk
    i32 = jnp.int32
    gids = jnp.arange(N_GROUPS, dtype=i32)
    onehot = (group[:, None] == gids[None, :]).astype(i32).reshape(nb, blk, N_GROUPS)
    within = jnp.cumsum(onehot, axis=1)
    blk_end = jnp.cumsum(within[:, -1, :], axis=0)
    blk_start = blk_end - within[:, -1, :]
    counts = blk_end[-1]
    tiles = (counts + tr - 1) // tr
    tile_end = jnp.cumsum(tiles)
    tile_group = jnp.minimum(jnp.sum(jnp.arange(nt, dtype=i32)[:, None] >= tile_end[None, :], axis=1),
                             N_GROUPS - 1).astype(i32)
    row = jnp.arange(nt * tr, dtype=i32)
    gsel = (jnp.repeat(tile_group, tr)[:, None] == gids[None, :]).astype(i32)
    pick = lambda per_group: jnp.sum(gsel * per_group[None, :], axis=1)
    k = row - pick(tile_end - tiles) * tr
    valid = k < pick(counts)
    per_block = lambda table: jnp.sum(gsel[:, None, :] * table[None, :, :], axis=2)
    b_row = jnp.minimum(jnp.sum(per_block(blk_end) <= k[:, None], axis=1), nb - 1)
    bsel = (b_row[:, None] == jnp.arange(nb, dtype=i32)[None, :]).astype(i32)
    k_local = k - jnp.sum(bsel * per_block(blk_start), axis=1)
    sel = (bsel[:, :, None] * gsel[:, None, :]).reshape(nt * tr, nb * N_GROUPS).astype(F32)
    table = jnp.transpose(within, (0, 2, 1)).reshape(nb * N_GROUPS, blk).astype(F32)
    ranks = jnp.dot(sel, table)
    t_local = jnp.sum(ranks <= k_local[:, None].astype(F32), axis=1).astype(i32)
    row_src = jnp.where(valid, b_row * blk + t_local, 0)
    spare = t + ((row // tr) % 2) * tr + row % tr
    return tile_group, row_src, jnp.where(valid, row_src, spare), tile_end[-1:].astype(i32)


def _moe_body(tg_ref, src_ref, dst_ref, used_ref, rows_hbm, wg_ref, wu_ref, wd_ref, out_hbm, xbuf, ybuf, gsem, ssem):
    tr = xbuf.shape[1] // ROW_SUBLANES
    i = pl.program_id(0)
    n_used = used_ref[0]
    slot = i % 2

    def gather_row(tile, slot_, r):
        tok = src_ref[tile * tr + r]
        return pltpu.make_async_copy(rows_hbm.at[pl.ds(pl.multiple_of(tok * ROW_SUBLANES, ROW_SUBLANES), ROW_SUBLANES)],
                                     xbuf.at[slot_, pl.ds(r * ROW_SUBLANES, ROW_SUBLANES)], gsem.at[slot_])

    def scatter_row(tile, slot_, r):
        tok = dst_ref[tile * tr + r]
        return pltpu.make_async_copy(ybuf.at[slot_, pl.ds(r * OUT_SUBLANES, OUT_SUBLANES)],
                                     out_hbm.at[pl.ds(pl.multiple_of(tok * OUT_SUBLANES, OUT_SUBLANES), OUT_SUBLANES)],
                                     ssem.at[slot_])

    def for_rows(fn):
        def body(r, carry):
            fn(r)
            return carry
        lax.fori_loop(0, tr, body, 0, unroll=32)

    def gathered(slot_):
        return pltpu.make_async_copy(rows_hbm.at[pl.ds(0, tr * ROW_SUBLANES)], xbuf.at[slot_], gsem.at[slot_])

    def scattered(slot_):
        return pltpu.make_async_copy(ybuf.at[slot_], out_hbm.at[pl.ds(0, tr * OUT_SUBLANES)], ssem.at[slot_])

    @pl.when(i == 0)
    def _():
        for_rows(lambda r: gather_row(0, 0, r).start())
        n_tok = out_hbm.shape[0] // OUT_SUBLANES - 2 * tr
        ybuf[...] = jnp.zeros_like(ybuf)
        for s in range(2):
            spare = pltpu.make_async_copy(
                ybuf.at[s], out_hbm.at[pl.ds((n_tok + s * tr) * OUT_SUBLANES, tr * OUT_SUBLANES)], ssem.at[s])
            spare.start()
            spare.wait()

    @pl.when(i < n_used)
    def _():
        @pl.when(i + 1 < n_used)
        def _():
            for_rows(lambda r: gather_row(i + 1, 1 - slot, r).start())

        gathered(slot).wait()

        @pl.when(i >= 2)
        def _():
            scattered(slot).wait()

        x = jnp.concatenate([xbuf[slot, pl.ds(s, tr, stride=ROW_SUBLANES), :] for s in range(D_MODEL // 128)],
                            axis=1).astype(BF16)
        comb = xbuf[slot, pl.ds(COMB_SUBLANE, tr, stride=ROW_SUBLANES), :]
        lane = lax.broadcasted_iota(jnp.int32, comb.shape, 1)
        lane0 = EXPERT_LANE0 + tg_ref[i] * EXPERTS_PER_GROUP
        acc = jnp.zeros((tr, D_MODEL), F32)
        for e in range(EXPERTS_PER_GROUP):
            col = jnp.sum(jnp.where(lane == lane0 + e, comb, 0.0), axis=-1, keepdims=True)
            hid = (_silu(_dot(x, wg_ref[e])) * _dot(x, wu_ref[e]) * col).astype(BF16)
            acc = acc + _dot(hid, wd_ref[e])

        for s in range(D_MODEL // 128):
            ybuf[slot, pl.ds(s, tr, stride=OUT_SUBLANES), :] = acc[:, s * 128:(s + 1) * 128]

        for_rows(lambda r: scatter_row(i, slot, r).start())

        @pl.when(i == n_used - 1)
        def _():
            scattered(slot).wait()
            scattered(1 - slot).wait()


def _moe(rows, plan, wg, wu, wd, t):
    tr = _moe_tile_rows(t)
    assert t // tr >= 2
    nt = plan[0].shape[0]
    group_w = lambda shape: pl.BlockSpec((EXPERTS_PER_GROUP,) + shape, lambda i, tg, src, dst, used: (tg[i], 0, 0))
    grid_spec = pltpu.PrefetchScalarGridSpec(
        num_scalar_prefetch=4,
        grid=(nt,),
        in_specs=[pl.BlockSpec(memory_space=pl.ANY),
                  group_w((D_MODEL, D_EXPERT)), group_w((D_MODEL, D_EXPERT)), group_w((D_EXPERT, D_MODEL))],
        out_specs=pl.BlockSpec(memory_space=pl.ANY),
        scratch_shapes=[pltpu.VMEM((2, tr * ROW_SUBLANES, 128), F32), pltpu.VMEM((2, tr * OUT_SUBLANES, 128), F32),
                        pltpu.SemaphoreType.DMA((2,)), pltpu.SemaphoreType.DMA((2,))])
    return pl.pallas_call(
        _moe_body,
        grid_spec=grid_spec,
        out_shape=jax.ShapeDtypeStruct(((t + 2 * tr) * OUT_SUBLANES, 128), F32),
        compiler_params=pltpu.CompilerParams(dimension_semantics=("arbitrary",), vmem_limit_bytes=VMEM_LIMIT),
        name="moe",
    )(*plan, rows, wg, wu, wd)


def _final_body(x1_ref, moe_ref, mod_ref, fn_ref, y_ref):
    tm = x1_ref.shape[0]
    moe = jnp.concatenate([moe_ref[pl.ds(s, tm, stride=OUT_SUBLANES), :] for s in range(D_MODEL // 128)], axis=1)
    y_ref[...] = _rmsnorm(x1_ref[...] + mod_ref[5:6, :] * moe, fn_ref[...])


def _final(x1, moe, mod, tiles_per_row, fn):
    t = x1.shape[0]
    tm = TM_FINAL
    return pl.pallas_call(
        _final_body,
        grid=(t // tm,),
        in_specs=[pl.BlockSpec((tm, D_MODEL), lambda i: (i, 0)),
                  pl.BlockSpec((tm * OUT_SUBLANES, 128), lambda i: (i, 0)),
                  pl.BlockSpec((None, 6, D_MODEL), lambda i: (i // tiles_per_row, 0, 0)),
                  pl.BlockSpec((1, D_MODEL), lambda i: (0, 0))],
        out_specs=pl.BlockSpec((tm, D_MODEL), lambda i: (i, 0)),
        out_shape=jax.ShapeDtypeStruct((t, D_MODEL), F32),
        compiler_params=pltpu.CompilerParams(vmem_limit_bytes=VMEM_LIMIT),
        name="final",
    )(x1, moe, mod, fn)


def _trunk_layer(x, mod, s0_hg, s0_gla, emit_state, w):
    b, l, _ = x.shape
    xt = x.reshape(b * l, D_MODEL)
    per_row = lambda tm: (l // tm) if mod.shape[0] > 1 else (b * l // tm)
    (hq, gf, gb, hv, hgate, gq, gk, gv, gr, laf, lab, sa, sb) = _inproj(
        xt, mod, per_row(TM_PROJ), w["n1"], w["lbp"], w["wmain"], w["waup"], w["ba"])
    seq = lambda a: a.reshape(b, l, a.shape[-1])
    hg = _scan(seq(hq), None, seq(hv), seq(gf), seq(gb), s0_hg, w["scan_consts"], HG_H, HG_DK, HG_DV,
               emit_state)
    gla = _scan(seq(gq), seq(gk), seq(gv), seq(laf), seq(lab), s0_gla, w["scan_consts"], GLA_H, GLA_DK,
                GLA_DV, emit_state)
    flat = lambda a: a.reshape(b * l, a.shape[-1])
    x1, rows, group = _post(flat(hg[0]), flat(hg[1]), flat(gla[0]), flat(gla[1]), hgate, gr, sa, sb, xt, mod,
                           per_row(TM_POST), w["hgn"], w["glan"], w["n2"], w["wa"], w["wb"], w["wo"], w["wr"])
    plan = _route_plan(group.reshape(b * l), b * l)
    moe = _moe(rows, plan, w["wg"], w["wu"], w["wd"], b * l)
    y = _final(x1, moe, mod, per_row(TM_FINAL), w["fn"])
    states = (hg[2], gla[2]) if emit_state else (None, None)
    return y.reshape(b, l, D_MODEL), states


def kernel(x_prompt, x_sample, state_hgrn, state_gla, c, c_ctx, ada_w, ada_b, norm1_g, norm2_g, w_in,
           hg_lb_param, hg_norm_g, gla_wa_up, gla_ba, gla_norm_g, w_br_a, w_br_b, w_out,
           w_router_group, w_router_expert, w_exp_gate, w_exp_up, w_exp_down, final_norm_g):
    nb = c.shape[0]
    cond = jnp.concatenate([c, c_ctx[None, :], jnp.zeros((8 - nb - 1, D_MODEL), F32)], axis=0)
    mod = _modulation(cond, ada_w[0], ada_b).reshape(8, 6, D_MODEL)

    zeros = jnp.zeros((GLA_RANK, GLA_H * GLA_DK), F32)
    router = jnp.concatenate(
        [w_router_group[0], jnp.transpose(w_router_expert[0], (1, 0, 2)).reshape(D_MODEL, N_EXPERTS),
         jnp.zeros((D_MODEL, ROUTER_LANES - N_GROUPS - N_EXPERTS), F32)], axis=1)
    w = {
        "n1": norm1_g, "n2": norm2_g, "fn": final_norm_g[None, :], "lbp": hg_lb_param,
        "wmain": w_in[0].astype(BF16),
        "waup": jnp.concatenate([jnp.concatenate([gla_wa_up[0, 0], zeros], axis=1),
                                 jnp.concatenate([zeros, gla_wa_up[0, 1]], axis=1)], axis=0).astype(BF16),
        "ba": gla_ba[0].reshape(1, 2 * GLA_H * GLA_DK),
        "hgn": hg_norm_g, "glan": gla_norm_g,
        "wa": w_br_a[0].astype(BF16), "wb": w_br_b[0].astype(BF16), "wo": w_out[0].astype(BF16),
        "wr": jnp.stack([router.astype(BF16), (router - router.astype(BF16).astype(F32)).astype(BF16)]),
        "wg": w_exp_gate[0].astype(BF16), "wu": w_exp_up[0].astype(BF16), "wd": w_exp_down[0].astype(BF16),
        "scan_consts": _scan_constants(),
    }
    y_prompt, (st_hg, st_gla) = _trunk_layer(x_prompt, mod[nb:nb + 1], None, None, True, w)
    y_sample, _ = _trunk_layer(x_sample, mod[:nb], state_hgrn[:, 0], state_gla[:, 0], False, w)
    return y_prompt, y_sample, st_hg[:, None], st_gla[:, None]
```

```python
import functools

import numpy as np
import jax
import jax.numpy as jnp
from jax import lax
from jax.experimental import pallas as pl
from jax.experimental.pallas import tpu as pltpu

F32 = jnp.float32
BF16 = jnp.bfloat16

D_MODEL = 1024
CHUNK = 64
N_LEVELS = 6
CHUNKS_PER_STEP = 4
HG_H, HG_DK, HG_DV = 8, 128, 128
GLA_H, GLA_DK, GLA_DV = 4, 128, 256
GLA_RANK = 16
GLA_GATE_NORM = 16.0
N_GROUPS = 4
EXPERTS_PER_GROUP = 8
N_EXPERTS = N_GROUPS * EXPERTS_PER_GROUP
D_EXPERT = D_MODEL // 4
EPS = 1e-6
ROUTER_LANES = 128
EXPERT_LANE0 = N_GROUPS
VMEM_LIMIT = 56 * 1024 * 1024
MXU_DEPTH = 256

TM_PROJ = 256
TM_POST = 256
POST_PART = 256
TM_FINAL = 1024
TR_MOE = (256, 512)
MOE_WIDE_MIN_TILES = 16
PLAN_BLOCK = 256
OUT_SUBLANES = D_MODEL // 128
COMB_SUBLANE = OUT_SUBLANES
ROW_SUBLANES = 16


def _sigmoid(x):
    return 1.0 / (1.0 + jnp.exp(-x))


def _silu(x):
    return x * _sigmoid(x)


def _log_sigmoid(x):
    return jnp.minimum(x, 0.0) - jnp.log(1.0 + jnp.exp(-jnp.abs(x)))


def _rmsnorm(x, g):
    return x * lax.rsqrt(jnp.mean(x * x, axis=-1, keepdims=True) + EPS) * g


def _dot(a, b):
    return jnp.dot(a, b, preferred_element_type=F32)


def _dot_nt(a, b):
    return lax.dot_general(a, b, (((1,), (1,)), ((), ())), preferred_element_type=F32)


def _dot_tn(a, b):
    return lax.dot_general(a, b, (((0,), (0,)), ((), ())), preferred_element_type=F32)


def _mod_body(c_ref, w_ref, b_ref, o_ref):
    s, w = _silu(c_ref[...]), w_ref[...]
    s_hi, w_hi = s.astype(BF16), w.astype(BF16)
    s_lo, w_lo = (s - s_hi.astype(F32)).astype(BF16), (w - w_hi.astype(F32)).astype(BF16)
    o_ref[...] = _dot(s_hi, w_hi) + _dot(s_hi, w_lo) + _dot(s_lo, w_hi) + b_ref[...]


def _modulation(cond, ada_w, ada_b):
    rows, n = cond.shape[0], ada_w.shape[1]
    tn = n // 4
    return pl.pallas_call(
        _mod_body,
        grid=(n // tn,),
        in_specs=[pl.BlockSpec((rows, D_MODEL), lambda j: (0, 0)),
                  pl.BlockSpec((D_MODEL, tn), lambda j: (0, j)),
                  pl.BlockSpec((1, tn), lambda j: (0, j))],
        out_specs=pl.BlockSpec((rows, tn), lambda j: (0, j)),
        out_shape=jax.ShapeDtypeStruct((rows, n), F32),
        compiler_params=pltpu.CompilerParams(vmem_limit_bytes=VMEM_LIMIT),
        name="modulation",
    )(cond, ada_w, ada_b)


_C_HQ, _C_HFF, _C_HFB, _C_HI, _C_HGATE = 0, 1024, 2048, 3072, 4096
_C_GQ, _C_GK, _C_GV, _C_GR, _C_TAIL = 5120, 5632, 6144, 7168, 8192
GA_WIDTH = 2 * GLA_RANK
W_TAIL = GA_WIDTH + 2 * D_MODEL
W_MAIN = _C_TAIL + W_TAIL


def _inproj_body(x_ref, mod_ref, n1_ref, lbp_ref, wmain_ref, waup_ref, ba_ref,
                 hq_ref, gf_ref, gb_ref, hv_ref, hgate_ref, gq_ref, gk_ref, gv_ref, gr_ref,
                 laf_ref, lab_ref, sa_ref, sb_ref):
    x = x_ref[...]
    h = _rmsnorm(x, n1_ref[...]) * (1.0 + mod_ref[1:2, :]) + mod_ref[0:1, :]
    hb = h.astype(BF16)

    def proj(c0, w):
        return _dot(hb, wmain_ref[:, c0:c0 + w])

    p0, p1 = lbp_ref[0:1, :], lbp_ref[1:2, :]
    pm = jnp.maximum(p0, p1)
    e0, e1 = jnp.exp(p0 - pm), jnp.exp(p1 - pm)
    lb = e0 / (e0 + e1)

    tail = proj(_C_TAIL, W_TAIL)
    sa_ref[...] = _sigmoid(tail[:, GA_WIDTH:GA_WIDTH + D_MODEL]).astype(BF16)
    sb_ref[...] = _sigmoid(tail[:, GA_WIDTH + D_MODEL:]).astype(BF16)
    ga = tail[:, :GA_WIDTH].astype(BF16)
    la = _log_sigmoid(_dot(ga, waup_ref[...]) + ba_ref[...]) * (1.0 / GLA_GATE_NORM)
    laf_ref[...] = la[:, :512]
    lab_ref[...] = la[:, 512:]
    gf_ref[...] = jnp.log(lb + (1.0 - lb) * _sigmoid(proj(_C_HFF, 1024)))
    gb_ref[...] = jnp.log(lb + (1.0 - lb) * _sigmoid(proj(_C_HFB, 1024)))
    hgate_ref[...] = _silu(proj(_C_HGATE, 1024)).astype(BF16)
    gr_ref[...] = _silu(proj(_C_GR, 1024)).astype(BF16)
    hq_ref[...] = (proj(_C_HQ, 1024) * HG_DK ** -0.5).astype(BF16)
    gq_ref[...] = (proj(_C_GQ, 512) * GLA_DK ** -0.5).astype(BF16)
    gv_ref[...] = proj(_C_GV, 1024).astype(BF16)
    gk_ref[...] = proj(_C_GK, 512).astype(BF16)
    hv_ref[...] = proj(_C_HI, 1024).astype(BF16)


def _inproj(x, mod, tiles_per_row, n1, lbp, wmain, waup, ba):
    t = x.shape[0]
    tm = TM_PROJ
    const = lambda shape: pl.BlockSpec(shape, lambda i: (0,) * len(shape))
    tok = lambda w: pl.BlockSpec((tm, w), lambda i: (i, 0))
    widths = (1024, 1024, 1024, 1024, 1024, 512, 512, 1024, 1024, 512, 512, 1024, 1024)
    dtypes = (BF16, F32, F32, BF16, BF16, BF16, BF16, BF16, BF16, F32, F32, BF16, BF16)
    return pl.pallas_call(
        _inproj_body,
        grid=(t // tm,),
        in_specs=[tok(D_MODEL),
                  pl.BlockSpec((None, 6, D_MODEL), lambda i: (i // tiles_per_row, 0, 0)),
                  const((1, D_MODEL)), const((2, D_MODEL)),
                  pl.BlockSpec((D_MODEL, W_MAIN), lambda i: (0, 0), pipeline_mode=pl.Buffered(1)),
                  const((GA_WIDTH, 1024)), const((1, 1024))],
        out_specs=[tok(w) for w in widths],
        out_shape=[jax.ShapeDtypeStruct((t, w), dt) for w, dt in zip(widths, dtypes)],
        compiler_params=pltpu.CompilerParams(vmem_limit_bytes=VMEM_LIMIT),
        name="inproj",
    )(x, mod, n1, lbp, wmain, waup, ba)


N_MXU_LEVELS = 3
ROW_GROUP = 16
PART = 8
FACTOR_KINDS = ("start", "end") + tuple(l for l in range(N_LEVELS) if (1 << l) > PART)
SCAN_EXPONENT_ROWS = (1 + N_MXU_LEVELS) * CHUNK + len(FACTOR_KINDS) * (CHUNK // PART)
START_SLOT, END_SLOT = N_LEVELS, N_LEVELS + 1
LOG2E = 1.4426950408889634


def _boundary_row(t, h, d):
    base = (t // (2 * h)) * 2 * h
    return base + h - 1 if d == 0 else base + h


def _scan_constants():
    c = CHUNK
    tri = np.tril(np.ones((c, c), np.float32))
    m = [tri] + [np.abs(tri - tri[[_boundary_row(t, 1 << l, 0) for t in range(c)]]) for l in range(N_MXU_LEVELS)]
    m = np.stack(m)
    lvl = np.full((c, c), N_LEVELS + 1, np.int32)
    for t in range(c):
        lvl[t, t] = N_LEVELS
        for s in range(t):
            lvl[t, s] = int(np.floor(np.log2(t ^ s)))
    cum = lambda r: tri[r] if r >= 0 else np.zeros(c, np.float32)
    f = np.zeros((len(FACTOR_KINDS), c // PART, c), np.float32)
    for j in range(c // PART):
        entry, exit_ = PART * j - 1, PART * j + PART - 1
        f[FACTOR_KINDS.index("start"), j] = cum(entry)
        f[FACTOR_KINDS.index("end"), j] = cum(c - 1) - cum(exit_)
        for l in range(N_LEVELS):
            if (1 << l) > PART:
                bnd = _boundary_row(PART * j, 1 << l, 0)
                query = (PART * j) & (1 << l) != 0
                f[FACTOR_KINDS.index(l), j] = cum(entry) - cum(bnd) if query else cum(bnd) - cum(exit_)
    m_fwd = np.concatenate([m.reshape(-1, c), f.reshape(-1, c)])
    m_bwd = np.concatenate([m[:, ::-1, ::-1].reshape(-1, c), f[:, ::-1, ::-1].reshape(-1, c)])
    m_both = np.stack([m_fwd, m_bwd])
    return (jnp.asarray(np.concatenate([m_both, m_both], axis=2), BF16), jnp.asarray(np.stack([lvl, lvl.T])))


def _scan_prepare(u, d, rows, q_ref, k_ref, g_ref, m_ref, side, cb_ref, kf_ref, kb_ref, f_ref, p_ref):
    c = CHUNK
    w = g_ref.shape[-1]
    g = g_ref[rows, :]
    gs = g * LOG2E
    g1 = gs.astype(BF16)
    g2 = (gs - g1.astype(F32)).astype(BF16)
    cb_ref[u] = _dot(m_ref[d], jnp.concatenate([g1, g2], axis=0))
    k32 = (1.0 - jnp.exp(g)) if k_ref is None else k_ref[rows, :].astype(F32)
    kf_ref[u] = k32
    kb_ref[u] = k32.astype(BF16)
    n_parts = c // PART
    f_base = c * (1 + N_MXU_LEVELS)
    f_ref[u] = jnp.exp2(cb_ref[u, f_base:f_base + len(FACTOR_KINDS) * n_parts, :])

    def per_part(i, row_of):
        return jnp.concatenate([jnp.broadcast_to(row_of(j), (PART, w))
                                for j in range(i * ROW_GROUP // PART, (i + 1) * ROW_GROUP // PART)], axis=0)

    def cum_row(r):
        return cb_ref[u, r:r + 1, :] if 0 <= r < c else jnp.zeros((1, w), F32)

    factor = lambda kind: lambda j: f_ref[u, FACTOR_KINDS.index(kind) * n_parts + j:
                                          FACTOR_KINDS.index(kind) * n_parts + j + 1, :]
    entry = (lambda j: cum_row(PART * j - 1)) if d == 0 else (lambda j: cum_row(PART * j + PART))
    exit_ = (lambda j: cum_row(PART * j + PART - 1)) if d == 0 else (lambda j: cum_row(PART * j))
    for i in range(c // ROW_GROUP):
        t0 = ROW_GROUP * i
        grp = slice(t0, t0 + ROW_GROUP)
        cum = cb_ref[u, grp, :]
        q = q_ref[rows.start + t0:rows.start + t0 + ROW_GROUP, :]
        k = kb_ref[u, grp, :]
        from_entry = jnp.exp2(cum - per_part(i, entry))
        to_exit = jnp.exp2(per_part(i, exit_) - cum)
        qu = q * from_entry.astype(BF16)
        kw = k * to_exit.astype(BF16)
        for l in range(N_LEVELS):
            h = 1 << l
            upper = side[l] > 0 if h < ROW_GROUP else None
            if l < N_MXU_LEVELS:
                src = jnp.where(upper, q, k) if d == 0 else jnp.where(upper, k, q)
                ex = cb_ref[u, c * (l + 1) + t0:c * (l + 1) + t0 + ROW_GROUP, :]
                p = src * jnp.exp2(ex).astype(BF16)
            elif h == PART:
                p = jnp.where(upper, qu, kw) if d == 0 else jnp.where(upper, kw, qu)
            else:
                src = qu if ((t0 & h) != 0) == (d == 0) else kw
                p = src * per_part(i, factor(l)).astype(BF16)
            p_ref[u, l, grp, :] = p
        p_ref[u, START_SLOT, grp, :] = qu * per_part(i, factor("start")).astype(BF16)
        p_ref[u, END_SLOT, grp, :] = (kf_ref[u, grp, :] * to_exit * per_part(i, factor("end"))).astype(BF16)


def _scan_scores(u, d, rows, q_ref, masks, kb_ref, p_ref, a_ref, n_heads, dk):
    groups = range(CHUNK // ROW_GROUP)
    piece = lambda x, g: x[g * ROW_GROUP:(g + 1) * ROW_GROUP]
    mask_pieces = [[piece(m, g) for g in groups] for m in masks]
    for h in range(n_heads):
        ks = slice(h * dk, (h + 1) * dk)
        diag = _dot_nt(q_ref[rows, ks], kb_ref[u, :, ks]).astype(BF16)
        a = [jnp.where(mask_pieces[N_LEVELS][g], piece(diag, g), jnp.zeros((), BF16)) for g in groups]
        for l in range(N_LEVELS):
            p = p_ref[u, l, :, ks]
            if (1 << l) < ROW_GROUP:
                query = list(groups)
                x = _dot_nt(p, p).astype(BF16)
            else:
                query = [g for g in groups if ((g * ROW_GROUP & (1 << l)) != 0) == (d == 0)]
                x = _dot_nt(jnp.concatenate([piece(p, g) for g in query], axis=0), p).astype(BF16)
            for j, g in enumerate(query):
                a[g] = jnp.where(mask_pieces[l][g], piece(x, j), a[g])
        a_ref[u, h] = jnp.concatenate(a, axis=0)


def _scan_outputs(u, d, rows, v_ref, cb_ref, p_ref, a_ref, st_ref, o_ref, n_heads, dk, dv):
    edge = CHUNK - 1 if d == 0 else 0
    for h in range(n_heads):
        ks = slice(h * dk, (h + 1) * dk)
        vs = slice(h * dv, (h + 1) * dv)
        v = v_ref[rows, vs]
        st = st_ref[d, h]
        o = _dot(jnp.concatenate([p_ref[u, START_SLOT, :, ks], a_ref[u, h]], axis=1),
                 jnp.concatenate([st.astype(BF16), v], axis=0))
        o_ref[rows, vs] = o.astype(o_ref.dtype)
        decay = jnp.broadcast_to(jnp.exp2(cb_ref[u, edge:edge + 1, ks]), (dk, dk)).T
        decay = jnp.concatenate([decay] * (dv // dk), axis=1)
        st_ref[d, h] = st * decay + _dot_tn(p_ref[u, END_SLOT, :, ks], v)


def _scan_body(*refs, n_heads, dk, dv, derive_k, zero_init, emit_state):
    refs = list(refs)
    q_refs = (refs.pop(0), refs.pop(0))
    k_refs = (None, None) if derive_k else (refs.pop(0), refs.pop(0))
    v_refs = (refs.pop(0), refs.pop(0))
    g_refs = (refs.pop(0), refs.pop(0))
    m_ref, lvl_ref = refs.pop(0), refs.pop(0)
    s0_ref = None if zero_init else refs.pop(0)
    o_refs = (refs.pop(0), refs.pop(0))
    sout_ref = refs.pop(0) if emit_state else None
    st_ref, cb_ref, kf_ref, kb_ref, f_ref, p_ref, a_ref = refs
    n = pl.program_id(1)

    @pl.when(n == 0)
    def _():
        for d in range(2):
            for h in range(n_heads):
                st_ref[d, h] = jnp.zeros((dk, dv), F32) if zero_init else s0_ref[d, h]

    cps = CHUNKS_PER_STEP
    rows = lambda c: slice(c * CHUNK, (c + 1) * CHUNK)
    masks = [[lvl_ref[d].astype(F32).astype(BF16) == l for l in range(N_LEVELS + 1)] for d in range(2)]
    row = lax.broadcasted_iota(jnp.int32, (ROW_GROUP, q_refs[0].shape[-1]), 0)
    side = [((row >> l) & 1).astype(F32).astype(BF16) for l in range(N_LEVELS) if (1 << l) < ROW_GROUP]
    units = [(d, c if d == 0 else cps - 1 - c) for c in range(cps) for d in range(2)]
    for i in range(len(units) + 2):
        if i < len(units):
            d, c = units[i]
            _scan_prepare(d * cps + c, d, rows(c), q_refs[d], k_refs[d], g_refs[d], m_ref, side, cb_ref, kf_ref,
                          kb_ref, f_ref, p_ref)
        if 1 <= i <= len(units):
            d, c = units[i - 1]
            _scan_scores(d * cps + c, d, rows(c), q_refs[d], masks[d], kb_ref, p_ref, a_ref, n_heads, dk)
        if i >= 2:
            d, c = units[i - 2]
            _scan_outputs(d * cps + c, d, rows(c), v_refs[d], cb_ref, p_ref, a_ref, st_ref, o_refs[d], n_heads,
                          dk, dv)

    if emit_state:
        @pl.when(n == pl.num_programs(1) - 1)
        def _():
            for d in range(2):
                for h in range(n_heads):
                    sout_ref[d, h] = st_ref[d, h]


def _scan(q, k, v, g_f, g_b, s0, consts, n_heads, dk, dv, emit_state):
    b, l, _ = q.shape
    step = CHUNKS_PER_STEP * CHUNK
    n = l // step
    slots = 2 * CHUNKS_PER_STEP
    m_all, lvl = consts
    fwd = lambda w: pl.BlockSpec((None, step, w), lambda i, j: (i, j, 0))
    bwd = lambda w: pl.BlockSpec((None, step, w), lambda i, j: (i, n - 1 - j, 0))
    const = lambda shape: pl.BlockSpec(shape, lambda i, j: (0,) * len(shape))
    state = pl.BlockSpec((None, 2, n_heads, dk, dv), lambda i, j: (i, 0, 0, 0, 0))
    wk, wv = n_heads * dk, n_heads * dv
    args, in_specs = [q, q], [fwd(wk), bwd(wk)]
    if k is not None:
        args += [k, k]
        in_specs += [fwd(wk), bwd(wk)]
    args += [v, v, g_f, g_b, m_all, lvl]
    in_specs += [fwd(wv), bwd(wv), fwd(wk), bwd(wk), const(m_all.shape), const(lvl.shape)]
    if s0 is not None:
        args.append(s0)
        in_specs.append(state)
    out_specs = [fwd(wv), bwd(wv)]
    out_shape = [jax.ShapeDtypeStruct((b, l, wv), BF16)] * 2
    if emit_state:
        out_specs.append(state)
        out_shape.append(jax.ShapeDtypeStruct((b, 2, n_heads, dk, dv), F32))
    body = functools.partial(_scan_body, n_heads=n_heads, dk=dk, dv=dv, derive_k=k is None,
                             zero_init=s0 is None, emit_state=emit_state)
    return pl.pallas_call(
        body,
        grid=(b, n),
        in_specs=in_specs,
        out_specs=out_specs,
        out_shape=out_shape,
        scratch_shapes=[pltpu.VMEM((2, n_heads, dk, dv), F32),
                        pltpu.VMEM((slots, SCAN_EXPONENT_ROWS, wk), F32),
                        pltpu.VMEM((slots, CHUNK, wk), F32), pltpu.VMEM((slots, CHUNK, wk), BF16),
                        pltpu.VMEM((slots, len(FACTOR_KINDS) * (CHUNK // PART), wk), F32),
                        pltpu.VMEM((slots, N_LEVELS + 2, CHUNK, wk), BF16),
                        pltpu.VMEM((slots, n_heads, CHUNK, CHUNK), BF16)],
        compiler_params=pltpu.CompilerParams(dimension_semantics=("arbitrary", "arbitrary"),
                                             vmem_limit_bytes=VMEM_LIMIT),
        name="scan_hgrn" if k is None else "scan_gla",
    )(*args)


def _post_body(ohf_ref, ohb_ref, ogf_ref, ogb_ref, hgate_ref, gr_ref, sa_ref, sb_ref, x_ref, mod_ref,
               hgn_ref, glan_ref, n2_ref, wa_ref, wb_ref, wo_ref, wr_ref,
               x1_ref, rows_ref, group_ref):
    tm = x_ref.shape[0]
    for r0 in range(0, tm, POST_PART):
        _post_part(slice(r0, r0 + POST_PART), ohf_ref, ohb_ref, ogf_ref, ogb_ref, hgate_ref, gr_ref, sa_ref, sb_ref,
                   x_ref, mod_ref, hgn_ref, glan_ref, n2_ref, wa_ref, wb_ref, wo_ref, wr_ref,
                   x1_ref, rows_ref, group_ref)


def _post_part(rs, ohf_ref, ohb_ref, ogf_ref, ogb_ref, hgate_ref, gr_ref, sa_ref, sb_ref, x_ref, mod_ref,
               hgn_ref, glan_ref, n2_ref, wa_ref, wb_ref, wo_ref, wr_ref,
               x1_ref, rows_ref, group_ref):
    n = rs.stop - rs.start

    def normed(of_ref, ob_ref, g_ref, gate_ref, s):
        o = of_ref[rs, s].astype(F32) + ob_ref[rs, s].astype(F32)
        return (_rmsnorm(o, g_ref[...]) * gate_ref[rs, s].astype(F32)).astype(BF16)

    chunks = [slice(c * MXU_DEPTH, (c + 1) * MXU_DEPTH) for c in range(D_MODEL // MXU_DEPTH)]
    acc = lambda total, part: part if total is None else total + part
    t_hg = t_gla = mix = None
    for cs in chunks:
        heads = range(cs.start // HG_DV, cs.stop // HG_DV)
        y_hg = jnp.concatenate([normed(ohf_ref, ohb_ref, hgn_ref, hgate_ref, slice(h * HG_DV, (h + 1) * HG_DV))
                                for h in heads], axis=1)
        t_hg = acc(t_hg, _dot(y_hg, wa_ref[cs, :]))
        heads = range(cs.start // GLA_DV, cs.stop // GLA_DV)
        y_gla = jnp.concatenate([normed(ogf_ref, ogb_ref, glan_ref, gr_ref, slice(h * GLA_DV, (h + 1) * GLA_DV))
                                 for h in heads], axis=1)
        t_gla = acc(t_gla, _dot(y_gla, wb_ref[cs, :]))
    for cs in chunks:
        merged = sa_ref[rs, cs].astype(F32) * t_hg[:, cs] + sb_ref[rs, cs].astype(F32) * t_gla[:, cs]
        mix = acc(mix, _dot(merged.astype(BF16), wo_ref[cs, :]))
    x1 =x_ref[rs, :] + mod_ref[2:3, :] * mix
    x1_ref[rs, :] = x1
    h2 = _rmsnorm(x1, n2_ref[...]) * (1.0 + mod_ref[4:5, :]) + mod_ref[3:4, :]
    h2_hi = h2.astype(BF16)
    token_rows = lambda s: pl.ds(rs.start * ROW_SUBLANES + s, n, stride=ROW_SUBLANES)
    for s in range(D_MODEL // 128):
        rows_ref[token_rows(s), :] = h2[:, s * 128:(s + 1) * 128]
    for s in range(COMB_SUBLANE + 1, ROW_SUBLANES):
        rows_ref[token_rows(s), :] = jnp.zeros((n, 128), F32)

    h2_lo = (h2 - h2_hi.astype(F32)).astype(BF16)
    lt = (_dot(h2_hi, wr_ref[0]) + _dot(h2_hi, wr_ref[1]) + _dot(h2_lo, wr_ref[0])).T
    row = lambda r: lt[r:r + 1, :]
    top = lambda vals: functools.reduce(jnp.maximum, vals)

    def first_at(vals, m):
        idx = jnp.full(m.shape, len(vals), jnp.int32)
        for j in reversed(range(len(vals))):
            idx = jnp.where(vals[j] == m, j, idx)
        return idx

    gl = [row(g) for g in range(N_GROUPS)]
    gmax = top(gl)
    p_top = 1.0 / functools.reduce(jnp.add, [jnp.exp(v - gmax) for v in gl])
    g_idx = first_at(gl, gmax)
    sel = []
    for j in range(EXPERTS_PER_GROUP):
        v = row(EXPERT_LANE0 + j)
        for g in range(1, N_GROUPS):
            v = jnp.where(g_idx == g, row(EXPERT_LANE0 + g * EXPERTS_PER_GROUP + j), v)
        sel.append(v)
    m1 = top(sel)
    i1 = first_at(sel, m1)
    sel2 = [jnp.where(i1 == j, -jnp.inf, v) for j, v in enumerate(sel)]
    m2 = top(sel2)
    i2 = first_at(sel2, m2)
    r = jnp.exp(m2 - m1)
    w1 = p_top / (1.0 + r)
    lane0 = EXPERT_LANE0 + g_idx * EXPERTS_PER_GROUP
    rid = lax.broadcasted_iota(jnp.int32, lt.shape, 0)
    comb_t = jnp.where(rid == lane0 + i1, w1, jnp.where(rid == lane0 + i2, w1 * r, 0.0))
    rows_ref[token_rows(COMB_SUBLANE), :] = comb_t.T
    group_ref[:, rs] = g_idx


def _post(ohf, ohb, ogf, ogb, hgate, gr, sa, sb, x, mod, tiles_per_row, hgn, glan, n2, wa, wb, wo, wr):
    t = x.shape[0]
    tm = TM_POST
    const = lambda shape: pl.BlockSpec(shape, lambda i: (0,) * len(shape))
    tok = lambda w: pl.BlockSpec((tm, w), lambda i: (i, 0))
    return pl.pallas_call(
        _post_body,
        grid=(t // tm,),
        in_specs=[tok(1024)] * 9
                 + [pl.BlockSpec((None, 6, D_MODEL), lambda i: (i // tiles_per_row, 0, 0)),
                    const((1, HG_DV)), const((1, GLA_DV)), const((1, D_MODEL)),
                    const((1024, 1024)), const((1024, 1024)), const((1024, 1024)),
                    const((2, D_MODEL, ROUTER_LANES))],
        out_specs=[tok(1024), pl.BlockSpec((tm * ROW_SUBLANES, 128), lambda i: (i, 0)),
                   pl.BlockSpec((None, 1, tm), lambda i: (i, 0, 0))],
        out_shape=[jax.ShapeDtypeStruct((t, D_MODEL), F32), jax.ShapeDtypeStruct((t * ROW_SUBLANES, 128), F32),
                   jax.ShapeDtypeStruct((t // tm, 1, tm), jnp.int32)],
        compiler_params=pltpu.CompilerParams(vmem_limit_bytes=VMEM_LIMIT),
        name="post",
    )(ohf, ohb, ogf, ogb, hgate, gr, sa, sb, x, mod, hgn, glan, n2, wa, wb, wo, wr)


def _moe_tile_rows(t):
    narrow, wide = TR_MOE
    return wide if t // wide >= MOE_WIDE_MIN_TILES else narrow


def _route_plan(group, t):
    tr, blk = _moe_tile_rows(t), PLAN_BLOCK
    nt, nb = t // tr + N_GROUPS, t // blk
    i32 = jnp.int32
    gids = jnp.arange(N_GROUPS, dtype=i32)
    onehot = (group[:, None] == gids[None, :]).astype(i32).reshape(nb, blk, N_GROUPS)
    within = jnp.cumsum(onehot, axis=1)
    blk_end = jnp.cumsum(within[:, -1, :], axis=0)
    blk_start = blk_end - within[:, -1, :]
    counts = blk_end[-1]
    tiles = (counts + tr - 1) // tr
    tile_end = jnp.cumsum(tiles)
    tile_group = jnp.minimum(jnp.sum(jnp.arange(nt, dtype=i32)[:, None] >= tile_end[None, :], axis=1),
                             N_GROUPS - 1).astype(i32)
    row = jnp.arange(nt * tr, dtype=i32)
    gsel = (jnp.repeat(tile_group, tr)[:, None] == gids[None, :]).astype(i32)
    pick = lambda per_group: jnp.sum(gsel * per_group[None, :], axis=1)
    k = row - pick(tile_end - tiles) * tr
    valid = k < pick(counts)
    per_block = lambda table: jnp.sum(gsel[:, None, :] * table[None, :, :], axis=2)
    b_row = jnp.minimum(jnp.sum(per_block(blk_end) <= k[:, None], axis=1), nb - 1)
    bsel = (b_row[:, None] == jnp.arange(nb, dtype=i32)[None, :]).astype(i32)
    k_local = k - jnp.sum(bsel * per_block(blk_start), axis=1)
    sel = (bsel[:, :, None] * gsel[:, None, :]).reshape(nt * tr, nb * N_GROUPS).astype(F32)
    table = jnp.transpose(within, (0, 2, 1)).reshape(nb * N_GROUPS, blk).astype(F32)
    ranks = jnp.dot(sel, table)
    t_local = jnp.sum(ranks <= k_local[:, None].astype(F32), axis=1).astype(i32)
    row_src = jnp.where(valid, b_row * blk + t_local, 0)
    spare = t + ((row // tr) % 2) * tr + row % tr
    return tile_group, row_src, jnp.where(valid, row_src, spare), tile_end[-1:].astype(i32)


def _moe_body(tg_ref, src_ref, dst_ref, used_ref, rows_hbm, wg_ref, wu_ref, wd_ref, out_hbm, xbuf, ybuf, gsem, ssem):
    tr = xbuf.shape[1] // ROW_SUBLANES
    i = pl.program_id(0)
    n_used = used_ref[0]
    slot = i % 2

    def gather_row(tile, slot_, r):
        tok = src_ref[tile * tr + r]
        return pltpu.make_async_copy(rows_hbm.at[pl.ds(pl.multiple_of(tok * ROW_SUBLANES, ROW_SUBLANES), ROW_SUBLANES)],
                                     xbuf.at[slot_, pl.ds(r * ROW_SUBLANES, ROW_SUBLANES)], gsem.at[slot_])

    def scatter_row(tile, slot_, r):
        tok = dst_ref[tile * tr + r]
        return pltpu.make_async_copy(ybuf.at[slot_, pl.ds(r * OUT_SUBLANES, OUT_SUBLANES)],
                                     out_hbm.at[pl.ds(pl.multiple_of(tok * OUT_SUBLANES, OUT_SUBLANES), OUT_SUBLANES)],
                                     ssem.at[slot_])

    def for_rows(fn):
        def body(r, carry):
            fn(r)
            return carry
        lax.fori_loop(0, tr, body, 0, unroll=32)

    def gathered(slot_):
        return pltpu.make_async_copy(rows_hbm.at[pl.ds(0, tr * ROW_SUBLANES)], xbuf.at[slot_], gsem.at[slot_])

    def scattered(slot_):
        return pltpu.make_async_copy(ybuf.at[slot_], out_hbm.at[pl.ds(0, tr * OUT_SUBLANES)], ssem.at[slot_])

    @pl.when(i == 0)
    def _():
        for_rows(lambda r: gather_row(0, 0, r).start())
        n_tok = out_hbm.shape[0] // OUT_SUBLANES - 2 * tr
        ybuf[...] = jnp.zeros_like(ybuf)
        for s in range(2):
            spare = pltpu.make_async_copy(
                ybuf.at[s], out_hbm.at[pl.ds((n_tok + s * tr) * OUT_SUBLANES, tr * OUT_SUBLANES)], ssem.at[s])
            spare.start()
            spare.wait()

    @pl.when(i < n_used)
    def _():
        @pl.when(i + 1 < n_used)
        def _():
            for_rows(lambda r: gather_row(i + 1, 1 - slot, r).start())

        gathered(slot).wait()

        @pl.when(i >= 2)
        def _():
            scattered(slot).wait()

        x = jnp.concatenate([xbuf[slot, pl.ds(s, tr, stride=ROW_SUBLANES), :] for s in range(D_MODEL // 128)],
                            axis=1).astype(BF16)
        comb = xbuf[slot, pl.ds(COMB_SUBLANE, tr, stride=ROW_SUBLANES), :]
        lane = lax.broadcasted_iota(jnp.int32, comb.shape, 1)
        lane0 = EXPERT_LANE0 + tg_ref[i] * EXPERTS_PER_GROUP
        acc = jnp.zeros((tr, D_MODEL), F32)
        for e in range(EXPERTS_PER_GROUP):
            col = jnp.sum(jnp.where(lane == lane0 + e, comb, 0.0), axis=-1, keepdims=True)
            hid = (_silu(_dot(x, wg_ref[e])) * _dot(x, wu_ref[e]) * col).astype(BF16)
            acc = acc + _dot(hid, wd_ref[e])

        for s in range(D_MODEL // 128):
            ybuf[slot, pl.ds(s, tr, stride=OUT_SUBLANES), :] = acc[:, s * 128:(s + 1) * 128]

        for_rows(lambda r: scatter_row(i, slot, r).start())

        @pl.when(i == n_used - 1)
        def _():
            scattered(slot).wait()
            scattered(1 - slot).wait()


def _moe(rows, plan, wg, wu, wd, t):
    tr = _moe_tile_rows(t)
    assert t // tr >= 2
    nt = plan[0].shape[0]
    group_w = lambda shape: pl.BlockSpec((EXPERTS_PER_GROUP,) + shape, lambda i, tg, src, dst, used: (tg[i], 0, 0))
    grid_spec = pltpu.PrefetchScalarGridSpec(
        num_scalar_prefetch=4,
        grid=(nt,),
        in_specs=[pl.BlockSpec(memory_space=pl.ANY),
                  group_w((D_MODEL, D_EXPERT)), group_w((D_MODEL, D_EXPERT)), group_w((D_EXPERT, D_MODEL))],
        out_specs=pl.BlockSpec(memory_space=pl.ANY),
        scratch_shapes=[pltpu.VMEM((2, tr * ROW_SUBLANES, 128), F32), pltpu.VMEM((2, tr * OUT_SUBLANES, 128), F32),
                        pltpu.SemaphoreType.DMA((2,)), pltpu.SemaphoreType.DMA((2,))])
    return pl.pallas_call(
        _moe_body,
        grid_spec=grid_spec,
        out_shape=jax.ShapeDtypeStruct(((t + 2 * tr) * OUT_SUBLANES, 128), F32),
        compiler_params=pltpu.CompilerParams(dimension_semantics=("arbitrary",), vmem_limit_bytes=VMEM_LIMIT),
        name="moe",
    )(*plan, rows, wg, wu, wd)


def _final_body(x1_ref, moe_ref, mod_ref, fn_ref, y_ref):
    tm = x1_ref.shape[0]
    moe = jnp.concatenate([moe_ref[pl.ds(s, tm, stride=OUT_SUBLANES), :] for s in range(D_MODEL // 128)], axis=1)
    y_ref[...] = _rmsnorm(x1_ref[...] + mod_ref[5:6, :] * moe, fn_ref[...])


def _final(x1, moe, mod, tiles_per_row, fn):
    t = x1.shape[0]
    tm = TM_FINAL
    return pl.pallas_call(
        _final_body,
        grid=(t // tm,),
        in_specs=[pl.BlockSpec((tm, D_MODEL), lambda i: (i, 0)),
                  pl.BlockSpec((tm * OUT_SUBLANES, 128), lambda i: (i, 0)),
                  pl.BlockSpec((None, 6, D_MODEL), lambda i: (i // tiles_per_row, 0, 0)),
                  pl.BlockSpec((1, D_MODEL), lambda i: (0, 0))],
        out_specs=pl.BlockSpec((tm, D_MODEL), lambda i: (i, 0)),
        out_shape=jax.ShapeDtypeStruct((t, D_MODEL), F32),
        compiler_params=pltpu.CompilerParams(vmem_limit_bytes=VMEM_LIMIT),
        name="final",
    )(x1, moe, mod, fn)


def _trunk_layer(x, mod, s0_hg, s0_gla, emit_state, w):
    b, l, _ = x.shape
    xt = x.reshape(b * l, D_MODEL)
    per_row = lambda tm: (l // tm) if mod.shape[0] > 1 else (b * l // tm)
    (hq, gf, gb, hv, hgate, gq, gk, gv, gr, laf, lab, sa, sb) = _inproj(
        xt, mod, per_row(TM_PROJ), w["n1"], w["lbp"], w["wmain"], w["waup"], w["ba"])
    seq = lambda a: a.reshape(b, l, a.shape[-1])
    hg = _scan(seq(hq), None, seq(hv), seq(gf), seq(gb), s0_hg, w["scan_consts"], HG_H, HG_DK, HG_DV,
               emit_state)
    gla = _scan(seq(gq), seq(gk), seq(gv), seq(laf), seq(lab), s0_gla, w["scan_consts"], GLA_H, GLA_DK,
                GLA_DV, emit_state)
    flat = lambda a: a.reshape(b * l, a.shape[-1])
    x1, rows, group = _post(flat(hg[0]), flat(hg[1]), flat(gla[0]), flat(gla[1]), hgate, gr, sa, sb, xt, mod,
                           per_row(TM_POST), w["hgn"], w["glan"], w["n2"], w["wa"], w["wb"], w["wo"], w["wr"])
    plan = _route_plan(group.reshape(b * l), b * l)
    moe = _moe(rows, plan, w["wg"], w["wu"], w["wd"], b * l)
    y = _final(x1, moe, mod, per_row(TM_FINAL), w["fn"])
    states = (hg[2], gla[2]) if emit_state else (None, None)
    return y.reshape(b, l, D_MODEL), states


def kernel(x_prompt, x_sample, state_hgrn, state_gla, c, c_ctx, ada_w, ada_b, norm1_g, norm2_g, w_in,
           hg_lb_param, hg_norm_g, gla_wa_up, gla_ba, gla_norm_g, w_br_a, w_br_b, w_out,
           w_router_group, w_router_expert, w_exp_gate, w_exp_up, w_exp_down, final_norm_g):
    nb = c.shape[0]
    cond = jnp.concatenate([c, c_ctx[None, :], jnp.zeros((8 - nb - 1, D_MODEL), F32)], axis=0)
    mod = _modulation(cond, ada_w[0], ada_b).reshape(8, 6, D_MODEL)

    zeros = jnp.zeros((GLA_RANK, GLA_H * GLA_DK), F32)
    router = jnp.concatenate(
        [w_router_group[0], jnp.transpose(w_router_expert[0], (1, 0, 2)).reshape(D_MODEL, N_EXPERTS),
         jnp.zeros((D_MODEL, ROUTER_LANES - N_GROUPS - N_EXPERTS), F32)], axis=1)
    w = {
        "n1": norm1_g, "n2": norm2_g, "fn": final_norm_g[None, :], "lbp": hg_lb_param,
        "wmain": w_in[0].astype(BF16),
        "waup": jnp.concatenate([jnp.concatenate([gla_wa_up[0, 0], zeros], axis=1),
                                 jnp.concatenate([zeros, gla_wa_up[0, 1]], axis=1)], axis=0).astype(BF16),
        "ba": gla_ba[0].reshape(1, 2 * GLA_H * GLA_DK),
        "hgn": hg_norm_g, "glan": gla_norm_g,
        "wa": w_br_a[0].astype(BF16), "wb": w_br_b[0].astype(BF16), "wo": w_out[0].astype(BF16),
        "wr": jnp.stack([router.astype(BF16), (router - router.astype(BF16).astype(F32)).astype(BF16)]),
        "wg": w_exp_gate[0].astype(BF16), "wu": w_exp_up[0].astype(BF16), "wd": w_exp_down[0].astype(BF16),
        "scan_consts": _scan_constants(),
    }
    y_prompt, (st_hg, st_gla) = _trunk_layer(x_prompt, mod[nb:nb + 1], None, None, True, w)
    y_sample, _ = _trunk_layer(x_sample, mod[:nb], state_hgrn[:, 0], state_gla[:, 0], False, w)
    return y_prompt, y_sample, st_hg[:, None], st_gla[:, None]
```

```python
import functools

import numpy as np
import jax
import jax.numpy as jnp
from jax import lax
from jax.experimental import pallas as pl
from jax.experimental.pallas import tpu as pltpu

F32 = jnp.float32
BF16 = jnp.bfloat16

D_MODEL = 1024
CHUNK = 64
N_LEVELS = 6
CHUNKS_PER_STEP = 4
GLA_CHUNKS_PER_STEP = 8
HG_H, HG_DK, HG_DV = 8, 128, 128
GLA_H, GLA_DK, GLA_DV = 4, 128, 256
GLA_RANK = 16
GLA_GATE_NORM = 16.0
N_GROUPS = 4
EXPERTS_PER_GROUP = 8
N_EXPERTS = N_GROUPS * EXPERTS_PER_GROUP
D_EXPERT = D_MODEL // 4
EPS = 1e-6
ROUTER_LANES = 128
EXPERT_LANE0 = N_GROUPS
VMEM_LIMIT = 56 * 1024 * 1024
MXU_DEPTH = 256

TM_PROJ = 256
TM_POST = 256
POST_PART = 256
TM_FINAL = 1024
TR_MOE = (256, 512)
MOE_WIDE_MIN_TILES = 16
PLAN_BLOCK = 256
OUT_SUBLANES = D_MODEL // 128
COMB_SUBLANE = OUT_SUBLANES
ROW_SUBLANES = 16


def _sigmoid(x):
    return 1.0 / (1.0 + jnp.exp(-x))


def _silu(x):
    return x * _sigmoid(x)


def _log_sigmoid(x):
    return jnp.minimum(x, 0.0) - jnp.log(1.0 + jnp.exp(-jnp.abs(x)))


def _rmsnorm(x, g):
    return x * lax.rsqrt(jnp.mean(x * x, axis=-1, keepdims=True) + EPS) * g


def _dot(a, b):
    return jnp.dot(a, b, preferred_element_type=F32)


def _dot_nt(a, b):
    return lax.dot_general(a, b, (((1,), (1,)), ((), ())), preferred_element_type=F32)


def _dot_tn(a, b):
    return lax.dot_general(a, b, (((0,), (0,)), ((), ())), preferred_element_type=F32)


def _mod_body(c_ref, w_ref, b_ref, o_ref):
    s, w = _silu(c_ref[...]), w_ref[...]
    s_hi, w_hi = s.astype(BF16), w.astype(BF16)
    s_lo, w_lo = (s - s_hi.astype(F32)).astype(BF16), (w - w_hi.astype(F32)).astype(BF16)
    o_ref[...] = _dot(s_hi, w_hi) + _dot(s_hi, w_lo) + _dot(s_lo, w_hi) + b_ref[...]


def _modulation(cond, ada_w, ada_b):
    rows, n = cond.shape[0], ada_w.shape[1]
    tn = n // 4
    return pl.pallas_call(
        _mod_body,
        grid=(n // tn,),
        in_specs=[pl.BlockSpec((rows, D_MODEL), lambda j: (0, 0)),
                  pl.BlockSpec((D_MODEL, tn), lambda j: (0, j)),
                  pl.BlockSpec((1, tn), lambda j: (0, j))],
        out_specs=pl.BlockSpec((rows, tn), lambda j: (0, j)),
        out_shape=jax.ShapeDtypeStruct((rows, n), F32),
        compiler_params=pltpu.CompilerParams(vmem_limit_bytes=VMEM_LIMIT),
        name="modulation",
    )(cond, ada_w, ada_b)


_C_HQ, _C_HFF, _C_HFB, _C_HI, _C_HGATE = 0, 1024, 2048, 3072, 4096
_C_GQ, _C_GK, _C_GV, _C_GR, _C_TAIL = 5120, 5632, 6144, 7168, 8192
GA_WIDTH = 2 * GLA_RANK
W_TAIL = GA_WIDTH + 2 * D_MODEL
W_MAIN = _C_TAIL + W_TAIL


def _inproj_body(x_ref, mod_ref, n1_ref, lbp_ref, wmain_ref, waup_ref, ba_ref,
                 hq_ref, gf_ref, gb_ref, hv_ref, hgate_ref, gq_ref, gk_ref, gv_ref, gr_ref,
                 laf_ref, lab_ref, sa_ref, sb_ref):
    x = x_ref[...]
    h = _rmsnorm(x, n1_ref[...]) * (1.0 + mod_ref[1:2, :]) + mod_ref[0:1, :]
    hb = h.astype(BF16)

    def proj(c0, w):
        return _dot(hb, wmain_ref[:, c0:c0 + w])

    p0, p1 = lbp_ref[0:1, :], lbp_ref[1:2, :]
    pm = jnp.maximum(p0, p1)
    e0, e1 = jnp.exp(p0 - pm), jnp.exp(p1 - pm)
    lb = e0 / (e0 + e1)

    tail = proj(_C_TAIL, W_TAIL)
    sa_ref[...] = _sigmoid(tail[:, GA_WIDTH:GA_WIDTH + D_MODEL]).astype(BF16)
    sb_ref[...] = _sigmoid(tail[:, GA_WIDTH + D_MODEL:]).astype(BF16)
    ga = tail[:, :GA_WIDTH].astype(BF16)
    la = _log_sigmoid(_dot(ga, waup_ref[...]) + ba_ref[...]) * (1.0 / GLA_GATE_NORM)
    laf_ref[...] = la[:, :512]
    lab_ref[...] = la[:, 512:]
    gf_ref[...] = jnp.log(lb + (1.0 - lb) * _sigmoid(proj(_C_HFF, 1024)))
    gb_ref[...] = jnp.log(lb + (1.0 - lb) * _sigmoid(proj(_C_HFB, 1024)))
    hgate_ref[...] = _silu(proj(_C_HGATE, 1024)).astype(BF16)
    gr_ref[...] = _silu(proj(_C_GR, 1024)).astype(BF16)
    hq_ref[...] = (proj(_C_HQ, 1024) * HG_DK ** -0.5).astype(BF16)
    gq_ref[...] = (proj(_C_GQ, 512) * GLA_DK ** -0.5).astype(BF16)
    gv_ref[...] = proj(_C_GV, 1024).astype(BF16)
    gk_ref[...] = proj(_C_GK, 512).astype(BF16)
    hv_ref[...] = proj(_C_HI, 1024).astype(BF16)


def _inproj(x, mod, tiles_per_row, n1, lbp, wmain, waup, ba):
    t = x.shape[0]
    tm = TM_PROJ
    const = lambda shape: pl.BlockSpec(shape, lambda i: (0,) * len(shape))
    tok = lambda w: pl.BlockSpec((tm, w), lambda i: (i, 0))
    widths = (1024, 1024, 1024, 1024, 1024, 512, 512, 1024, 1024, 512, 512, 1024, 1024)
    dtypes = (BF16, F32, F32, BF16, BF16, BF16, BF16, BF16, BF16, F32, F32, BF16, BF16)
    return pl.pallas_call(
        _inproj_body,
        grid=(t // tm,),
        in_specs=[tok(D_MODEL),
                  pl.BlockSpec((None, 6, D_MODEL), lambda i: (i // tiles_per_row, 0, 0)),
                  const((1, D_MODEL)), const((2, D_MODEL)),
                  pl.BlockSpec((D_MODEL, W_MAIN), lambda i: (0, 0), pipeline_mode=pl.Buffered(1)),
                  const((GA_WIDTH, 1024)), const((1, 1024))],
        out_specs=[tok(w) for w in widths],
        out_shape=[jax.ShapeDtypeStruct((t, w), dt) for w, dt in zip(widths, dtypes)],
        compiler_params=pltpu.CompilerParams(vmem_limit_bytes=VMEM_LIMIT),
        name="inproj",
    )(x, mod, n1, lbp, wmain, waup, ba)


N_MXU_LEVELS = 3
ROW_GROUP = 16
START_SLOT, END_SLOT = N_LEVELS, N_LEVELS + 1
LOG2E = 1.4426950408889634


def _boundary_row(t, h, d):
    base = (t // (2 * h)) * 2 * h
    return base + h - 1 if d == 0 else base + h


def _scan_constants():
    c = CHUNK
    tri = np.tril(np.ones((c, c), np.float32))
    m = [tri] + [np.abs(tri - tri[[_boundary_row(t, 1 << l, 0) for t in range(c)]]) for l in range(N_MXU_LEVELS)]
    m = np.stack(m)
    lvl = np.full((c, c), N_LEVELS + 1, np.int32)
    for t in range(c):
        lvl[t, t] = N_LEVELS
        for s in range(t):
            lvl[t, s] = int(np.floor(np.log2(t ^ s)))
    m_fwd = m.reshape(-1, c)
    m_bwd = m[:, ::-1, ::-1].reshape(-1, c)
    m_both = np.stack([m_fwd, m_bwd])
    return (jnp.asarray(np.concatenate([m_both, m_both], axis=2), BF16), jnp.asarray(np.stack([lvl, lvl.T])))


def _scan_prepare(u, d, rows, q_ref, k_ref, g_ref, m_ref, side, cb_ref, kf_ref, kb_ref, p_ref):
    c = CHUNK
    w = g_ref.shape[-1]
    g = g_ref[rows, :]
    gs = g * LOG2E
    g1 = gs.astype(BF16)
    g2 = (gs - g1.astype(F32)).astype(BF16)
    cb_ref[u] = _dot(m_ref[d], jnp.concatenate([g1, g2], axis=0))
    k32 = (1.0 - jnp.exp(g)) if k_ref is None else k_ref[rows, :].astype(F32)
    kf_ref[u] = k32
    kb_ref[u] = k32.astype(BF16)
    bcast = lambda r, n: jnp.broadcast_to(cb_ref[u, r:r + 1, :], (n, w))
    for i in range(c // ROW_GROUP):
        t0 = ROW_GROUP * i
        grp = slice(t0, t0 + ROW_GROUP)
        cum = cb_ref[u, grp, :]
        q = q_ref[rows.start + t0:rows.start + t0 + ROW_GROUP, :]
        k = kb_ref[u, grp, :]
        for l in range(N_LEVELS):
            h = 1 << l
            if l < N_MXU_LEVELS:
                ex = cb_ref[u, c * (l + 1) + t0:c * (l + 1) + t0 + ROW_GROUP, :]
            else:
                ex = []
                for t in (t0, t0 + 8):
                    half = slice(t - t0, t - t0 + 8)
                    bnd = bcast(_boundary_row(t, h, d), 8)
                    ex.append(cum[half] - bnd if ((t & h) != 0) == (d == 0) else bnd - cum[half])
                ex = jnp.concatenate(ex, axis=0)
            if h >= ROW_GROUP:
                src = q if ((t0 & h) != 0) == (d == 0) else k
            else:
                upper = side[l] > 0
                src = jnp.where(upper, q, k) if d == 0 else jnp.where(upper, k, q)
            p_ref[u, l, grp, :] = src * jnp.exp2(ex).astype(BF16)
        p_ref[u, START_SLOT, grp, :] = q * jnp.exp2(cum).astype(BF16)
        edge = c - 1 if d == 0 else 0
        p_ref[u, END_SLOT, grp, :] = (kf_ref[u, grp, :] * jnp.exp2(bcast(edge, ROW_GROUP) - cum)).astype(BF16)


def _scan_scores(u, d, rows, q_ref, masks, kb_ref, p_ref, a_ref, n_heads, dk):
    groups = range(CHUNK // ROW_GROUP)
    piece = lambda x, g: x[g * ROW_GROUP:(g + 1) * ROW_GROUP]
    mask_pieces = [[piece(m, g) for g in groups] for m in masks]
    for h in range(n_heads):
        ks = slice(h * dk, (h + 1) * dk)
        diag = _dot_nt(q_ref[rows, ks], kb_ref[u, :, ks]).astype(BF16)
        a = [jnp.where(mask_pieces[N_LEVELS][g], piece(diag, g), jnp.zeros((), BF16)) for g in groups]
        for l in range(N_LEVELS):
            p = p_ref[u, l, :, ks]
            if (1 << l) < ROW_GROUP:
                query = list(groups)
                x = _dot_nt(p, p).astype(BF16)
            else:
                query = [g for g in groups if ((g * ROW_GROUP & (1 << l)) != 0) == (d == 0)]
                x = _dot_nt(jnp.concatenate([piece(p, g) for g in query], axis=0), p).astype(BF16)
            for j, g in enumerate(query):
                a[g] = jnp.where(mask_pieces[l][g], piece(x, j), a[g])
        a_ref[u, h] = jnp.concatenate(a, axis=0)


def _scan_outputs(u, d, rows, v_ref, cb_ref, p_ref, a_ref, st_ref, o_ref, n_heads, dk, dv):
    edge = CHUNK - 1 if d == 0 else 0
    for h in range(n_heads):
        ks = slice(h * dk, (h + 1) * dk)
        vs = slice(h * dv, (h + 1) * dv)
        v = v_ref[rows, vs]
        st = st_ref[d, h]
        o = _dot(jnp.concatenate([p_ref[u, START_SLOT, :, ks], a_ref[u, h]], axis=1),
                 jnp.concatenate([st.astype(BF16), v], axis=0))
        o_ref[rows, vs] = o.astype(o_ref.dtype)
        decay = jnp.broadcast_to(jnp.exp2(cb_ref[u, edge:edge + 1, ks]), (dk, dk)).T
        decay = jnp.concatenate([decay] * (dv // dk), axis=1)
        st_ref[d, h] = st * decay + _dot_tn(p_ref[u, END_SLOT, :, ks], v)


def _scan_body(*refs, n_heads, dk, dv, cps, derive_k, zero_init, emit_state):
    refs = list(refs)
    q_refs = (refs.pop(0), refs.pop(0))
    k_refs = (None, None) if derive_k else (refs.pop(0), refs.pop(0))
    v_refs = (refs.pop(0), refs.pop(0))
    g_refs = (refs.pop(0), refs.pop(0))
    m_ref, lvl_ref = refs.pop(0), refs.pop(0)
    s0_ref = None if zero_init else refs.pop(0)
    o_refs = (refs.pop(0), refs.pop(0))
    sout_ref = refs.pop(0) if emit_state else None
    st_ref, cb_ref, kf_ref, kb_ref, p_ref, a_ref = refs
    n = pl.program_id(1)

    @pl.when(n == 0)
    def _():
        for d in range(2):
            for h in range(n_heads):
                st_ref[d, h] = jnp.zeros((dk, dv), F32) if zero_init else s0_ref[d, h]

    rows = lambda c: slice(c * CHUNK, (c + 1) * CHUNK)
    masks = [[lvl_ref[d].astype(F32).astype(BF16) == l for l in range(N_LEVELS + 1)] for d in range(2)]
    row = lax.broadcasted_iota(jnp.int32, (ROW_GROUP, q_refs[0].shape[-1]), 0)
    side = [((row >> l) & 1).astype(F32).astype(BF16) for l in range(N_LEVELS) if (1 << l) < ROW_GROUP]
    units = [(d, c if d == 0 else cps - 1 - c) for c in range(cps) for d in range(2)]
    for i in range(len(units) + 2):
        if i < len(units):
            d, c = units[i]
            _scan_prepare(d * cps + c, d, rows(c), q_refs[d], k_refs[d], g_refs[d], m_ref, side, cb_ref, kf_ref,
                          kb_ref, p_ref)
        if 1 <= i <= len(units):
            d, c = units[i - 1]
            _scan_scores(d * cps + c, d, rows(c), q_refs[d], masks[d], kb_ref, p_ref, a_ref, n_heads, dk)
        if i >= 2:
            d, c = units[i - 2]
            _scan_outputs(d * cps + c, d, rows(c), v_refs[d], cb_ref, p_ref, a_ref, st_ref, o_refs[d], n_heads,
                          dk, dv)

    if emit_state:
        @pl.when(n == pl.num_programs(1) - 1)
        def _():
            for d in range(2):
                for h in range(n_heads):
                    sout_ref[d, h] = st_ref[d, h]


def _scan(q, k, v, g_f, g_b, s0, consts, n_heads, dk, dv, emit_state):
    b, l, _ = q.shape
    cps = min(CHUNKS_PER_STEP if k is None else GLA_CHUNKS_PER_STEP, l // CHUNK)
    step = cps * CHUNK
    n = l // step
    slots = 2 * cps
    m_all, lvl = consts
    fwd = lambda w: pl.BlockSpec((None, step, w), lambda i, j: (i, j, 0))
    bwd = lambda w: pl.BlockSpec((None, step, w), lambda i, j: (i, n - 1 - j, 0))
    const = lambda shape: pl.BlockSpec(shape, lambda i, j: (0,) * len(shape))
    state = pl.BlockSpec((None, 2, n_heads, dk, dv), lambda i, j: (i, 0, 0, 0, 0))
    wk, wv = n_heads * dk, n_heads * dv
    args, in_specs = [q, q], [fwd(wk), bwd(wk)]
    if k is not None:
        args += [k, k]
        in_specs += [fwd(wk), bwd(wk)]
    args += [v, v, g_f, g_b, m_all, lvl]
    in_specs += [fwd(wv), bwd(wv), fwd(wk), bwd(wk), const(m_all.shape), const(lvl.shape)]
    if s0 is not None:
        args.append(s0)
        in_specs.append(state)
    out_specs = [fwd(wv), bwd(wv)]
    out_shape = [jax.ShapeDtypeStruct((b, l, wv), BF16)] * 2
    if emit_state:
        out_specs.append(state)
        out_shape.append(jax.ShapeDtypeStruct((b, 2, n_heads, dk, dv), F32))
    body = functools.partial(_scan_body, n_heads=n_heads, dk=dk, dv=dv, cps=cps, derive_k=k is None,
                             zero_init=s0 is None, emit_state=emit_state)
    return pl.pallas_call(
        body,
        grid=(b, n),
        in_specs=in_specs,
        out_specs=out_specs,
        out_shape=out_shape,
        scratch_shapes=[pltpu.VMEM((2, n_heads, dk, dv), F32),
                        pltpu.VMEM((slots, (1 + N_MXU_LEVELS) * CHUNK, wk), F32),
                        pltpu.VMEM((slots, CHUNK, wk), F32), pltpu.VMEM((slots, CHUNK, wk), BF16),
                        pltpu.VMEM((slots, N_LEVELS + 2, CHUNK, wk), BF16),
                        pltpu.VMEM((slots, n_heads, CHUNK, CHUNK), BF16)],
        compiler_params=pltpu.CompilerParams(dimension_semantics=("arbitrary", "arbitrary"),
                                             vmem_limit_bytes=VMEM_LIMIT),
        name="scan_hgrn" if k is None else "scan_gla",
    )(*args)


def _post_body(ohf_ref, ohb_ref, ogf_ref, ogb_ref, hgate_ref, gr_ref, sa_ref, sb_ref, x_ref, mod_ref,
               hgn_ref, glan_ref, n2_ref, wa_ref, wb_ref, wo_ref, wr_ref,
               x1_ref, rows_ref, group_ref):
    tm = x_ref.shape[0]
    for r0 in range(0, tm, POST_PART):
        _post_part(slice(r0, r0 + POST_PART), ohf_ref, ohb_ref, ogf_ref, ogb_ref, hgate_ref, gr_ref, sa_ref, sb_ref,
                   x_ref, mod_ref, hgn_ref, glan_ref, n2_ref, wa_ref, wb_ref, wo_ref, wr_ref,
                   x1_ref, rows_ref, group_ref)


def _post_part(rs, ohf_ref, ohb_ref, ogf_ref, ogb_ref, hgate_ref, gr_ref, sa_ref, sb_ref, x_ref, mod_ref,
               hgn_ref, glan_ref, n2_ref, wa_ref, wb_ref, wo_ref, wr_ref,
               x1_ref, rows_ref, group_ref):
    n = rs.stop - rs.start

    def normed(of_ref, ob_ref, g_ref, gate_ref, s):
        o = of_ref[rs, s].astype(F32) + ob_ref[rs, s].astype(F32)
        return (_rmsnorm(o, g_ref[...]) * gate_ref[rs, s].astype(F32)).astype(BF16)

    chunks = [slice(c * MXU_DEPTH, (c + 1) * MXU_DEPTH) for c in range(D_MODEL // MXU_DEPTH)]
    acc = lambda total, part: part if total is None else total + part
    t_hg = t_gla = mix = None
    for cs in chunks:
        heads = range(cs.start // HG_DV, cs.stop // HG_DV)
        y_hg = jnp.concatenate([normed(ohf_ref, ohb_ref, hgn_ref, hgate_ref, slice(h * HG_DV, (h + 1) * HG_DV))
                                for h in heads], axis=1)
        t_hg = acc(t_hg, _dot(y_hg, wa_ref[cs, :]))
        heads = range(cs.start // GLA_DV, cs.stop // GLA_DV)
        y_gla = jnp.concatenate([normed(ogf_ref, ogb_ref, glan_ref, gr_ref, slice(h * GLA_DV, (h + 1) * GLA_DV))
                                 for h in heads], axis=1)
        t_gla = acc(t_gla, _dot(y_gla, wb_ref[cs, :]))
    for cs in chunks:
        merged = sa_ref[rs, cs].astype(F32) * t_hg[:, cs] + sb_ref[rs, cs].astype(F32) * t_gla[:, cs]
        mix = acc(mix, _dot(merged.astype(BF16), wo_ref[cs, :]))
    x1 = x_ref[rs, :] + mod_ref[2:3, :] * mix
    x1_ref[rs, :] = x1
    h2 = _rmsnorm(x1, n2_ref[...]) * (1.0 + mod_ref[4:5, :]) + mod_ref[3:4, :]
    h2_hi = h2.astype(BF16)
    token_rows = lambda s: pl.ds(rs.start * ROW_SUBLANES + s, n, stride=ROW_SUBLANES)
    for s in range(D_MODEL // 128):
        rows_ref[token_rows(s), :] = h2[:, s * 128:(s + 1) * 128]
    for s in range(COMB_SUBLANE + 1, ROW_SUBLANES):
        rows_ref[token_rows(s), :] = jnp.zeros((n, 128), F32)

    h2_lo = (h2 - h2_hi.astype(F32)).astype(BF16)
    lt = (_dot(h2_hi, wr_ref[0]) + _dot(h2_hi, wr_ref[1]) + _dot(h2_lo, wr_ref[0])).T
    row = lambda r: lt[r:r + 1, :]
    top = lambda vals: functools.reduce(jnp.maximum, vals)

    def first_at(vals, m):
        idx = jnp.full(m.shape, len(vals), jnp.int32)
        for j in reversed(range(len(vals))):
            idx = jnp.where(vals[j] == m, j, idx)
        return idx

    gl = [row(g) for g in range(N_GROUPS)]
    gmax = top(gl)
    p_top = 1.0 / functools.reduce(jnp.add, [jnp.exp(v - gmax) for v in gl])
    g_idx = first_at(gl, gmax)
    sel = []
    for j in range(EXPERTS_PER_GROUP):
        v = row(EXPERT_LANE0 + j)
        for g in range(1, N_GROUPS):
            v = jnp.where(g_idx == g, row(EXPERT_LANE0 + g * EXPERTS_PER_GROUP + j), v)
        sel.append(v)
    m1 = top(sel)
    i1 = first_at(sel, m1)
    sel2 = [jnp.where(i1 == j, -jnp.inf, v) for j, v in enumerate(sel)]
    m2 = top(sel2)
    i2 = first_at(sel2, m2)
    r = jnp.exp(m2 - m1)
    w1 = p_top / (1.0 + r)
    lane0 = EXPERT_LANE0 + g_idx * EXPERTS_PER_GROUP
    rid = lax.broadcasted_iota(jnp.int32, lt.shape, 0)
    comb_t = jnp.where(rid == lane0 + i1, w1, jnp.where(rid == lane0 + i2, w1 * r, 0.0))
    rows_ref[token_rows(COMB_SUBLANE), :] = comb_t.T
    group_ref[:, rs] = g_idx


def _post(ohf, ohb, ogf, ogb, hgate, gr, sa, sb, x, mod, tiles_per_row, hgn, glan, n2, wa, wb, wo, wr):
    t = x.shape[0]
    tm = TM_POST
    const = lambda shape: pl.BlockSpec(shape, lambda i: (0,) * len(shape))
    tok = lambda w: pl.BlockSpec((tm, w), lambda i: (i, 0))
    return pl.pallas_call(
        _post_body,
        grid=(t // tm,),
        in_specs=[tok(1024)] * 9
                 + [pl.BlockSpec((None, 6, D_MODEL), lambda i: (i // tiles_per_row, 0, 0)),
                    const((1, HG_DV)), const((1, GLA_DV)), const((1, D_MODEL)),
                    const((1024, 1024)), const((1024, 1024)), const((1024, 1024)),
                    const((2, D_MODEL, ROUTER_LANES))],
        out_specs=[tok(1024), pl.BlockSpec((tm * ROW_SUBLANES, 128), lambda i: (i, 0)),
                   pl.BlockSpec((None, 1, tm), lambda i: (i, 0, 0))],
        out_shape=[jax.ShapeDtypeStruct((t, D_MODEL), F32), jax.ShapeDtypeStruct((t * ROW_SUBLANES, 128), F32),
                   jax.ShapeDtypeStruct((t // tm, 1, tm), jnp.int32)],
        compiler_params=pltpu.CompilerParams(vmem_limit_bytes=VMEM_LIMIT),
        name="post",
    )(ohf, ohb, ogf, ogb, hgate, gr, sa, sb, x, mod, hgn, glan, n2, wa, wb, wo, wr)


def _moe_tile_rows(t):
    narrow, wide = TR_MOE
    return wide if t // wide >= MOE_WIDE_MIN_TILES else narrow


def _route_plan(group, t):
    tr, blk = _moe_tile_rows(t), PLAN_BLOCK
    nt, nb = t // tr + N_GROUPS, t // blk
    i32 = jnp.int32
    gids = jnp.arange(N_GROUPS, dtype=i32)
    onehot = (group[:, None] == gids[None, :]).astype(i32).reshape(nb, blk, N_GROUPS)
    within = jnp.cumsum(onehot, axis=1)
    blk_end = jnp.cumsum(within[:, -1, :], axis=0)
    blk_start = blk_end - within[:, -1, :]
    counts = blk_end[-1]
    tiles = (counts + tr - 1) // tr
    tile_end = jnp.cumsum(tiles)
    tile_group = jnp.minimum(jnp.sum(jnp.arange(nt, dtype=i32)[:, None] >= tile_end[None, :], axis=1),
                             N_GROUPS - 1).astype(i32)
    row = jnp.arange(nt * tr, dtype=i32)
    gsel = (jnp.repeat(tile_group, tr)[:, None] == gids[None, :]).astype(i32)
    pick = lambda per_group: jnp.sum(gsel * per_group[None, :], axis=1)
    k = row - pick(tile_end - tiles) * tr
    valid = k < pick(counts)
    per_block = lambda table: jnp.sum(gsel[:, None, :] * table[None, :, :], axis=2)
    b_row = jnp.minimum(jnp.sum(per_block(blk_end) <= k[:, None], axis=1), nb - 1)
    bsel = (b_row[:, None] == jnp.arange(nb, dtype=i32)[None, :]).astype(i32)
    k_local = k - jnp.sum(bsel * per_block(blk_start), axis=1)
    sel = (bsel[:, :, None] * gsel[:, None, :]).reshape(nt * tr, nb * N_GROUPS).astype(F32)
    table = jnp.transpose(within, (0, 2, 1)).reshape(nb * N_GROUPS, blk).astype(F32)
    ranks = jnp.dot(sel, table)
    t_local = jnp.sum(ranks <= k_local[:, None].astype(F32), axis=1).astype(i32)
    row_src = jnp.where(valid, b_row * blk + t_local, 0)
    spare = t + ((row // tr) % 2) * tr + row % tr
    return tile_group, row_src, jnp.where(valid, row_src, spare), tile_end[-1:].astype(i32)


def _moe_body(tg_ref, src_ref, dst_ref, used_ref, rows_hbm, wg_ref, wu_ref, wd_ref, out_hbm, xbuf, ybuf, gsem, ssem):
    tr = xbuf.shape[1] // ROW_SUBLANES
    i = pl.program_id(0)
    n_used = used_ref[0]
    slot = i % 2

    def gather_row(tile, slot_, r):
        tok = src_ref[tile * tr + r]
        return pltpu.make_async_copy(rows_hbm.at[pl.ds(pl.multiple_of(tok * ROW_SUBLANES, ROW_SUBLANES), ROW_SUBLANES)],
                                     xbuf.at[slot_, pl.ds(r * ROW_SUBLANES, ROW_SUBLANES)], gsem.at[slot_])

    def scatter_row(tile, slot_, r):
        tok = dst_ref[tile * tr + r]
        return pltpu.make_async_copy(ybuf.at[slot_, pl.ds(r * OUT_SUBLANES, OUT_SUBLANES)],
                                     out_hbm.at[pl.ds(pl.multiple_of(tok * OUT_SUBLANES, OUT_SUBLANES), OUT_SUBLANES)],
                                     ssem.at[slot_])

    def for_rows(fn):
        def body(r, carry):
            fn(r)
            return carry
        lax.fori_loop(0, tr, body, 0, unroll=32)

    def gathered(slot_):
        return pltpu.make_async_copy(rows_hbm.at[pl.ds(0, tr * ROW_SUBLANES)], xbuf.at[slot_], gsem.at[slot_])

    def scattered(slot_):
        return pltpu.make_async_copy(ybuf.at[slot_], out_hbm.at[pl.ds(0, tr * OUT_SUBLANES)], ssem.at[slot_])

    @pl.when(i == 0)
    def _():
        for_rows(lambda r: gather_row(0, 0, r).start())
        n_tok = out_hbm.shape[0] // OUT_SUBLANES - 2 * tr
        ybuf[...] = jnp.zeros_like(ybuf)
        for s in range(2):
            spare = pltpu.make_async_copy(
                ybuf.at[s], out_hbm.at[pl.ds((n_tok + s * tr) * OUT_SUBLANES, tr * OUT_SUBLANES)], ssem.at[s])
            spare.start()
            spare.wait()

    @pl.when(i < n_used)
    def _():
        @pl.when(i + 1 < n_used)
        def _():
            for_rows(lambda r: gather_row(i + 1, 1 - slot, r).start())

        gathered(slot).wait()

        @pl.when(i >= 2)
        def _():
            scattered(slot).wait()

        x = jnp.concatenate([xbuf[slot, pl.ds(s, tr, stride=ROW_SUBLANES), :] for s in range(D_MODEL // 128)],
                            axis=1).astype(BF16)
        comb = xbuf[slot, pl.ds(COMB_SUBLANE, tr, stride=ROW_SUBLANES), :]
        lane = lax.broadcasted_iota(jnp.int32, comb.shape, 1)
        lane0 = EXPERT_LANE0 + tg_ref[i] * EXPERTS_PER_GROUP
        acc = jnp.zeros((tr, D_MODEL), F32)
        for e in range(EXPERTS_PER_GROUP):
            col = jnp.sum(jnp.where(lane == lane0 + e, comb, 0.0), axis=-1, keepdims=True)
            hid = (_silu(_dot(x, wg_ref[e])) * _dot(x, wu_ref[e]) * col).astype(BF16)
            acc = acc + _dot(hid, wd_ref[e])

        for s in range(D_MODEL // 128):
            ybuf[slot, pl.ds(s, tr, stride=OUT_SUBLANES), :] = acc[:, s * 128:(s + 1) * 128]

        for_rows(lambda r: scatter_row(i, slot, r).start())

        @pl.when(i == n_used - 1)
        def _():
            scattered(slot).wait()
            scattered(1 - slot).wait()


def _moe(rows, plan, wg, wu, wd, t):
    tr = _moe_tile_rows(t)
    assert t // tr >= 2
    nt = plan[0].shape[0]
    group_w = lambda shape: pl.BlockSpec((EXPERTS_PER_GROUP,) + shape, lambda i, tg, src, dst, used: (tg[i], 0, 0))
    grid_spec = pltpu.PrefetchScalarGridSpec(
        num_scalar_prefetch=4,
        grid=(nt,),
        in_specs=[pl.BlockSpec(memory_space=pl.ANY),
                  group_w((D_MODEL, D_EXPERT)), group_w((D_MODEL, D_EXPERT)), group_w((D_EXPERT, D_MODEL))],
        out_specs=pl.BlockSpec(memory_space=pl.ANY),
        scratch_shapes=[pltpu.VMEM((2, tr * ROW_SUBLANES, 128), F32), pltpu.VMEM((2, tr * OUT_SUBLANES, 128), F32),
                        pltpu.SemaphoreType.DMA((2,)), pltpu.SemaphoreType.DMA((2,))])
    return pl.pallas_call(
        _moe_body,
        grid_spec=grid_spec,
        out_shape=jax.ShapeDtypeStruct(((t + 2 * tr) * OUT_SUBLANES, 128), F32),
        compiler_params=pltpu.CompilerParams(dimension_semantics=("arbitrary",), vmem_limit_bytes=VMEM_LIMIT),
        name="moe",
    )(*plan, rows, wg, wu, wd)


def _final_body(x1_ref, moe_ref, mod_ref, fn_ref, y_ref):
    tm = x1_ref.shape[0]
    moe = jnp.concatenate([moe_ref[pl.ds(s, tm, stride=OUT_SUBLANES), :] for s in range(D_MODEL // 128)], axis=1)
    y_ref[...] = _rmsnorm(x1_ref[...] + mod_ref[5:6, :] * moe, fn_ref[...])


def _final(x1, moe, mod, tiles_per_row, fn):
    t = x1.shape[0]
    tm = TM_FINAL
    return pl.pallas_call(
        _final_body,
        grid=(t // tm,),
        in_specs=[pl.BlockSpec((tm, D_MODEL), lambda i: (i, 0)),
                  pl.BlockSpec((tm * OUT_SUBLANES, 128), lambda i: (i, 0)),
                  pl.BlockSpec((None, 6, D_MODEL), lambda i: (i // tiles_per_row, 0, 0)),
                  pl.BlockSpec((1, D_MODEL), lambda i: (0, 0))],
        out_specs=pl.BlockSpec((tm, D_MODEL), lambda i: (i, 0)),
        out_shape=jax.ShapeDtypeStruct((t, D_MODEL), F32),
        compiler_params=pltpu.CompilerParams(vmem_limit_bytes=VMEM_LIMIT),
        name="final",
    )(x1, moe, mod, fn)


def _trunk_layer(x, mod, s0_hg, s0_gla, emit_state, w):
    b, l, _ = x.shape
    xt = x.reshape(b * l, D_MODEL)
    per_row = lambda tm: (l // tm) if mod.shape[0] > 1 else (b * l // tm)
    (hq, gf, gb, hv, hgate, gq, gk, gv, gr, laf, lab, sa, sb) = _inproj(
        xt, mod, per_row(TM_PROJ), w["n1"], w["lbp"], w["wmain"], w["waup"], w["ba"])
    seq = lambda a: a.reshape(b, l, a.shape[-1])
    hg = _scan(seq(hq), None, seq(hv), seq(gf), seq(gb), s0_hg, w["scan_consts"], HG_H, HG_DK, HG_DV,
               emit_state)
    gla = _scan(seq(gq), seq(gk), seq(gv), seq(laf), seq(lab), s0_gla, w["scan_consts"], GLA_H, GLA_DK,
                GLA_DV, emit_state)
    flat = lambda a: a.reshape(b * l, a.shape[-1])
    x1, rows, group = _post(flat(hg[0]), flat(hg[1]), flat(gla[0]), flat(gla[1]), hgate, gr, sa, sb, xt, mod,
                           per_row(TM_POST), w["hgn"], w["glan"], w["n2"], w["wa"], w["wb"], w["wo"], w["wr"])
    plan = _route_plan(group.reshape(b * l), b * l)
    moe = _moe(rows, plan, w["wg"], w["wu"], w["wd"], b * l)
    y = _final(x1, moe, mod, per_row(TM_FINAL), w["fn"])
    states = (hg[2], gla[2]) if emit_state else (None, None)
    return y.reshape(b, l, D_MODEL), states


def kernel(x_prompt, x_sample, state_hgrn, state_gla, c, c_ctx, ada_w, ada_b, norm1_g, norm2_g, w_in,
           hg_lb_param, hg_norm_g, gla_wa_up, gla_ba, gla_norm_g, w_br_a, w_br_b, w_out,
           w_router_group, w_router_expert, w_exp_gate, w_exp_up, w_exp_down, final_norm_g):
    nb = c.shape[0]
    cond = jnp.concatenate([c, c_ctx[None, :], jnp.zeros((8 - nb - 1, D_MODEL), F32)], axis=0)
    mod = _modulation(cond, ada_w[0], ada_b).reshape(8, 6, D_MODEL)

    zeros = jnp.zeros((GLA_RANK, GLA_H * GLA_DK), F32)
    router = jnp.concatenate(
        [w_router_group[0], jnp.transpose(w_router_expert[0], (1, 0, 2)).reshape(D_MODEL, N_EXPERTS),
         jnp.zeros((D_MODEL, ROUTER_LANES - N_GROUPS - N_EXPERTS), F32)], axis=1)
    w = {
        "n1": norm1_g, "n2": norm2_g, "fn": final_norm_g[None, :], "lbp": hg_lb_param,
        "wmain": w_in[0].astype(BF16),
        "waup": jnp.concatenate([jnp.concatenate([gla_wa_up[0, 0], zeros], axis=1),
                                 jnp.concatenate([zeros, gla_wa_up[0, 1]], axis=1)], axis=0).astype(BF16),
        "ba": gla_ba[0].reshape(1, 2 * GLA_H * GLA_DK),
        "hgn": hg_norm_g, "glan": gla_norm_g,
        "wa": w_br_a[0].astype(BF16), "wb": w_br_b[0].astype(BF16), "wo": w_out[0].astype(BF16),
        "wr": jnp.stack([router.astype(BF16), (router - router.astype(BF16).astype(F32)).astype(BF16)]),
        "wg": w_exp_gate[0].astype(BF16), "wu": w_exp_up[0].astype(BF16), "wd": w_exp_down[0].astype(BF16),
        "scan_consts": _scan_constants(),
    }
    y_prompt, (st_hg, st_gla) = _trunk_layer(x_prompt, mod[nb:nb + 1], None, None, True, w)
    y_sample, _ = _trunk_layer(x_sample, mod[:nb], state_hgrn[:, 0], state_gla[:, 0], False, w)
    return y_prompt, y_sample, st_hg[:, None], st_gla[:, None]
```

```python
import functools

import numpy as np
import jax
import jax.numpy as jnp
from jax import lax
from jax.experimental import pallas as pl
from jax.experimental.pallas import tpu as pltpu

F32 = jnp.float32
BF16 = jnp.bfloat16

D_MODEL = 1024
CHUNK = 64
N_LEVELS = 6
CHUNKS_PER_STEP = 4
GLA_CHUNKS_PER_STEP = 8
HG_H, HG_DK, HG_DV = 8, 128, 128
GLA_H, GLA_DK, GLA_DV = 4, 128, 256
GLA_RANK = 16
GLA_GATE_NORM = 16.0
N_GROUPS = 4
EXPERTS_PER_GROUP = 8
N_EXPERTS = N_GROUPS * EXPERTS_PER_GROUP
D_EXPERT = D_MODEL // 4
EPS = 1e-6
ROUTER_LANES = 128
EXPERT_LANE0 = N_GROUPS
VMEM_LIMIT = 56 * 1024 * 1024
MXU_DEPTH = 256

TM_PROJ = 256
TM_POST = 256
POST_PART = 256
TM_FINAL = 1024
TR_MOE = (256, 512)
MOE_WIDE_MIN_TILES = 16
PLAN_BLOCK = 256
OUT_SUBLANES = D_MODEL // 128
COMB_SUBLANE = OUT_SUBLANES
ROW_SUBLANES = 16


def _sigmoid(x):
    return 1.0 / (1.0 + jnp.exp(-x))


def _silu(x):
    return x * _sigmoid(x)


def _log_sigmoid(x):
    return jnp.minimum(x, 0.0) - jnp.log(1.0 + jnp.exp(-jnp.abs(x)))


def _rmsnorm(x, g):
    return x * lax.rsqrt(jnp.mean(x * x, axis=-1, keepdims=True) + EPS) * g


def _dot(a, b):
    return jnp.dot(a, b, preferred_element_type=F32)


def _dot_nt(a, b):
    return lax.dot_general(a, b, (((1,), (1,)), ((), ())), preferred_element_type=F32)


def _dot_tn(a, b):
    return lax.dot_general(a, b, (((0,), (0,)), ((), ())), preferred_element_type=F32)


def _mod_body(c_ref, w_ref, b_ref, o_ref):
    s, w = _silu(c_ref[...]), w_ref[...]
    s_hi, w_hi = s.astype(BF16), w.astype(BF16)
    s_lo, w_lo = (s - s_hi.astype(F32)).astype(BF16), (w - w_hi.astype(F32)).astype(BF16)
    o_ref[...] = _dot(s_hi, w_hi) + _dot(s_hi, w_lo) + _dot(s_lo, w_hi) + b_ref[...]


def _modulation(cond, ada_w, ada_b):
    rows, n = cond.shape[0], ada_w.shape[1]
    tn = n // 4
    return pl.pallas_call(
        _mod_body,
        grid=(n // tn,),
        in_specs=[pl.BlockSpec((rows, D_MODEL), lambda j: (0, 0)),
                  pl.BlockSpec((D_MODEL, tn), lambda j: (0, j)),
                  pl.BlockSpec((1, tn), lambda j: (0, j))],
        out_specs=pl.BlockSpec((rows, tn), lambda j: (0, j)),
        out_shape=jax.ShapeDtypeStruct((rows, n), F32),
        compiler_params=pltpu.CompilerParams(vmem_limit_bytes=VMEM_LIMIT),
        name="modulation",
    )(cond, ada_w, ada_b)


_C_HQ, _C_HFF, _C_HFB, _C_HI, _C_HGATE = 0, 1024, 2048, 3072, 4096
_C_GQ, _C_GK, _C_GV, _C_GR, _C_TAIL = 5120, 5632, 6144, 7168, 8192
GA_WIDTH = 2 * GLA_RANK
W_TAIL = GA_WIDTH + 2 * D_MODEL
W_MAIN = _C_TAIL + W_TAIL


def _inproj_body(x_ref, mod_ref, n1_ref, lbp_ref, wmain_ref, waup_ref, ba_ref,
                 hq_ref, gf_ref, gb_ref, hv_ref, hgate_ref, gq_ref, gk_ref, gv_ref, gr_ref,
                 laf_ref, lab_ref, sa_ref, sb_ref):
    x = x_ref[...]
    h = _rmsnorm(x, n1_ref[...]) * (1.0 + mod_ref[1:2, :]) + mod_ref[0:1, :]
    hb = h.astype(BF16)

    def proj(c0, w):
        return _dot(hb, wmain_ref[:, c0:c0 + w])

    p0, p1 = lbp_ref[0:1, :], lbp_ref[1:2, :]
    pm = jnp.maximum(p0, p1)
    e0, e1 = jnp.exp(p0 - pm), jnp.exp(p1 - pm)
    lb = e0 / (e0 + e1)

    tail = proj(_C_TAIL, W_TAIL)
    sa_ref[...] = _sigmoid(tail[:, GA_WIDTH:GA_WIDTH + D_MODEL]).astype(BF16)
    sb_ref[...] = _sigmoid(tail[:, GA_WIDTH + D_MODEL:]).astype(BF16)
    ga = tail[:, :GA_WIDTH].astype(BF16)
    la = _log_sigmoid(_dot(ga, waup_ref[...]) + ba_ref[...]) * (1.0 / GLA_GATE_NORM)
    laf_ref[...] = la[:, :512]
    lab_ref[...] = la[:, 512:]
    gf_ref[...] = jnp.log(lb + (1.0 - lb) * _sigmoid(proj(_C_HFF, 1024)))
    gb_ref[...] = jnp.log(lb + (1.0 - lb) * _sigmoid(proj(_C_HFB, 1024)))
    hgate_ref[...] = _silu(proj(_C_HGATE, 1024)).astype(BF16)
    gr_ref[...] = _silu(proj(_C_GR, 1024)).astype(BF16)
    hq_ref[...] = (proj(_C_HQ, 1024) * HG_DK ** -0.5).astype(BF16)
    gq_ref[...] = (proj(_C_GQ, 512) * GLA_DK ** -0.5).astype(BF16)
    gv_ref[...] = proj(_C_GV, 1024).astype(BF16)
    gk_ref[...] = proj(_C_GK, 512).astype(BF16)
    hv_ref[...] = proj(_C_HI, 1024).astype(BF16)


def _inproj(x, mod, tiles_per_row, n1, lbp, wmain, waup, ba):
    t = x.shape[0]
    tm = TM_PROJ
    const = lambda shape: pl.BlockSpec(shape, lambda i: (0,) * len(shape))
    tok = lambda w: pl.BlockSpec((tm, w), lambda i: (i, 0))
    widths = (1024, 1024, 1024, 1024, 1024, 512, 512, 1024, 1024, 512, 512, 1024, 1024)
    dtypes = (BF16, F32, F32, BF16, BF16, BF16, BF16, BF16, BF16, F32, F32, BF16, BF16)
    return pl.pallas_call(
        _inproj_body,
        grid=(t // tm,),
        in_specs=[tok(D_MODEL),
                  pl.BlockSpec((None, 6, D_MODEL), lambda i: (i // tiles_per_row, 0, 0)),
                  const((1, D_MODEL)), const((2, D_MODEL)),
                  pl.BlockSpec((D_MODEL, W_MAIN), lambda i: (0, 0), pipeline_mode=pl.Buffered(1)),
                  const((GA_WIDTH, 1024)), const((1, 1024))],
        out_specs=[tok(w) for w in widths],
        out_shape=[jax.ShapeDtypeStruct((t, w), dt) for w, dt in zip(widths, dtypes)],
        compiler_params=pltpu.CompilerParams(vmem_limit_bytes=VMEM_LIMIT),
        name="inproj",
    )(x, mod, n1, lbp, wmain, waup, ba)


N_MXU_LEVELS = 3
ROW_GROUP = 16
START_SLOT, END_SLOT = N_LEVELS, N_LEVELS + 1
LOG2E = 1.4426950408889634


def _boundary_row(t, h, d):
    base = (t // (2 * h)) * 2 * h
    return base + h - 1 if d == 0 else base + h


def _scan_constants():
    c = CHUNK
    tri = np.tril(np.ones((c, c), np.float32))
    m = [tri] + [np.abs(tri - tri[[_boundary_row(t, 1 << l, 0) for t in range(c)]]) for l in range(N_MXU_LEVELS)]
    m = np.stack(m)
    lvl = np.full((c, c), N_LEVELS + 1, np.int32)
    for t in range(c):
        lvl[t, t] = N_LEVELS
        for s in range(t):
            lvl[t, s] = int(np.floor(np.log2(t ^ s)))
    m_fwd = m.reshape(-1, c)
    m_bwd = m[:, ::-1, ::-1].reshape(-1, c)
    m_both = np.stack([m_fwd, m_bwd])
    return (jnp.asarray(np.concatenate([m_both, m_both], axis=2), BF16), jnp.asarray(np.stack([lvl, lvl.T])))


def _scan_prepare(u, d, rows, q_ref, k_ref, g_ref, m_ref, side, cb_ref, kf_ref, kb_ref, p_ref):
    c = CHUNK
    w = g_ref.shape[-1]
    g = g_ref[rows, :]
    gs = g * LOG2E
    g1 = gs.astype(BF16)
    g2 = (gs - g1.astype(F32)).astype(BF16)
    cb_ref[u] = _dot(m_ref[d], jnp.concatenate([g1, g2], axis=0))
    k32 = (1.0 - jnp.exp(g)) if k_ref is None else k_ref[rows, :].astype(F32)
    kf_ref[u] = k32
    kb_ref[u] = k32.astype(BF16)
    bcast = lambda r, n: jnp.broadcast_to(cb_ref[u, r:r + 1, :], (n, w))
    for i in range(c // ROW_GROUP):
        t0 = ROW_GROUP * i
        grp = slice(t0, t0 + ROW_GROUP)
        cum = cb_ref[u, grp, :]
        q = q_ref[rows.start + t0:rows.start + t0 + ROW_GROUP, :]
        k = kb_ref[u, grp, :]
        for l in range(N_LEVELS):
            h = 1 << l
            if l < N_MXU_LEVELS:
                ex = cb_ref[u, c * (l + 1) + t0:c * (l + 1) + t0 + ROW_GROUP, :]
            else:
                ex = []
                for t in (t0, t0 + 8):
                    half = slice(t - t0, t - t0 + 8)
                    bnd = bcast(_boundary_row(t, h, d), 8)
                    ex.append(cum[half] - bnd if ((t & h) != 0) == (d == 0) else bnd - cum[half])
                ex = jnp.concatenate(ex, axis=0)
            if h >= ROW_GROUP:
                src = q if ((t0 & h) != 0) == (d == 0) else k
            else:
                upper = side[l] > 0
                src = jnp.where(upper, q, k) if d == 0 else jnp.where(upper, k, q)
            p_ref[u, l, grp, :] = src * jnp.exp2(ex).astype(BF16)
        p_ref[u, START_SLOT, grp, :] = q * jnp.exp2(cum).astype(BF16)
        edge = c - 1 if d == 0 else 0
        p_ref[u, END_SLOT, grp, :] = (kf_ref[u, grp, :] * jnp.exp2(bcast(edge, ROW_GROUP) - cum)).astype(BF16)


def _scan_scores(u, d, rows, q_ref, masks, kb_ref, p_ref, a_ref, n_heads, dk):
    groups = range(CHUNK // ROW_GROUP)
    piece = lambda x, g: x[g * ROW_GROUP:(g + 1) * ROW_GROUP]
    mask_pieces = [[piece(m, g) for g in groups] for m in masks]
    for h in range(n_heads):
        ks = slice(h * dk, (h + 1) * dk)
        diag = _dot_nt(q_ref[rows, ks], kb_ref[u, :, ks]).astype(BF16)
        a = [jnp.where(mask_pieces[N_LEVELS][g], piece(diag, g), jnp.zeros((), BF16)) for g in groups]
        for l in range(N_LEVELS):
            p = p_ref[u, l, :, ks]
            if (1 << l) < ROW_GROUP:
                query = list(groups)
                x = _dot_nt(p, p).astype(BF16)
            else:
                query = [g for g in groups if ((g * ROW_GROUP & (1 << l)) != 0) == (d == 0)]
                x = _dot_nt(jnp.concatenate([piece(p, g) for g in query], axis=0), p).astype(BF16)
            for j, g in enumerate(query):
                a[g] = jnp.where(mask_pieces[l][g], piece(x, j), a[g])
        a_ref[u, h] = jnp.concatenate(a, axis=0)


def _scan_outputs(u, d, rows, v_ref, cb_ref, p_ref, a_ref, st_ref, o_ref, n_heads, dk, dv):
    edge = CHUNK - 1 if d == 0 else 0
    for h in range(n_heads):
        ks = slice(h * dk, (h + 1) * dk)
        vs = slice(h * dv, (h + 1) * dv)
        v = v_ref[rows, vs]
        st = st_ref[d, h]
        o = _dot(jnp.concatenate([p_ref[u, START_SLOT, :, ks], a_ref[u, h]], axis=1),
                 jnp.concatenate([st.astype(BF16), v], axis=0))
        o_ref[rows, vs] = o.astype(o_ref.dtype)
        decay = jnp.broadcast_to(jnp.exp2(cb_ref[u, edge:edge + 1, ks]), (dk, dk)).T
        decay = jnp.concatenate([decay] * (dv // dk), axis=1)
        st_ref[d, h] = st * decay + _dot_tn(p_ref[u, END_SLOT, :, ks], v)


def _scan_body(*refs, n_heads, dk, dv, cps, derive_k, zero_init, emit_state):
    refs = list(refs)
    q_refs = (refs.pop(0), refs.pop(0))
    k_refs = (None, None) if derive_k else (refs.pop(0), refs.pop(0))
    v_refs = (refs.pop(0), refs.pop(0))
    g_refs = (refs.pop(0), refs.pop(0))
    m_ref, lvl_ref = refs.pop(0), refs.pop(0)
    s0_ref = None if zero_init else refs.pop(0)
    o_refs = (refs.pop(0), refs.pop(0))
    sout_ref = refs.pop(0) if emit_state else None
    st_ref, cb_ref, kf_ref, kb_ref, p_ref, a_ref = refs
    n = pl.program_id(1)

    @pl.when(n == 0)
    def _():
        for d in range(2):
            for h in range(n_heads):
                st_ref[d, h] = jnp.zeros((dk, dv), F32) if zero_init else s0_ref[d, h]

    rows = lambda c: slice(c * CHUNK, (c + 1) * CHUNK)
    masks = [[lvl_ref[d].astype(F32).astype(BF16) == l for l in range(N_LEVELS + 1)] for d in range(2)]
    row = lax.broadcasted_iota(jnp.int32, (ROW_GROUP, q_refs[0].shape[-1]), 0)
    side = [((row >> l) & 1).astype(F32).astype(BF16) for l in range(N_LEVELS) if (1 << l) < ROW_GROUP]
    units = [(d, c if d == 0 else cps - 1 - c) for c in range(cps) for d in range(2)]
    for i in range(len(units) + 2):
        if i < len(units):
            d, c = units[i]
            _scan_prepare(d * cps + c, d, rows(c), q_refs[d], k_refs[d], g_refs[d], m_ref, side, cb_ref, kf_ref,
                          kb_ref, p_ref)
        if 1 <= i <= len(units):
            d, c = units[i - 1]
            _scan_scores(d * cps + c, d, rows(c), q_refs[d], masks[d], kb_ref, p_ref, a_ref, n_heads, dk)
        if i >= 2:
            d, c = units[i - 2]
            _scan_outputs(d * cps + c, d, rows(c), v_refs[d], cb_ref, p_ref, a_ref, st_ref, o_refs[d], n_heads,
                          dk, dv)

    if emit_state:
        @pl.when(n == pl.num_programs(1) - 1)
        def _():
            for d in range(2):
                for h in range(n_heads):
                    sout_ref[d, h] = st_ref[d, h]


def _scan(q, k, v, g_f, g_b, s0, consts, n_heads, dk, dv, emit_state):
    b, l, _ = q.shape
    cps = min(CHUNKS_PER_STEP if k is None else GLA_CHUNKS_PER_STEP, l // CHUNK)
    step = cps * CHUNK
    n = l // step
    slots = 2 * cps
    m_all, lvl = consts
    fwd = lambda w: pl.BlockSpec((None, step, w), lambda i, j: (i, j, 0))
    bwd = lambda w: pl.BlockSpec((None, step, w), lambda i, j: (i, n - 1 - j, 0))
    const = lambda shape: pl.BlockSpec(shape, lambda i, j: (0,) * len(shape))
    state = pl.BlockSpec((None, 2, n_heads, dk, dv), lambda i, j: (i, 0, 0, 0, 0))
    wk, wv = n_heads * dk, n_heads * dv
    args, in_specs = [q, q], [fwd(wk), bwd(wk)]
    if k is not None:
        args += [k, k]
        in_specs += [fwd(wk), bwd(wk)]
    args += [v, v, g_f, g_b, m_all, lvl]
    in_specs += [fwd(wv), bwd(wv), fwd(wk), bwd(wk), const(m_all.shape), const(lvl.shape)]
    if s0 is not None:
        args.append(s0)
        in_specs.append(state)
    out_specs = [fwd(wv), bwd(wv)]
    out_shape = [jax.ShapeDtypeStruct((b, l, wv), BF16)] * 2
    if emit_state:
        out_specs.append(state)
        out_shape.append(jax.ShapeDtypeStruct((b, 2, n_heads, dk, dv), F32))
    body = functools.partial(_scan_body, n_heads=n_heads, dk=dk, dv=dv, cps=cps, derive_k=k is None,
                             zero_init=s0 is None, emit_state=emit_state)
    return pl.pallas_call(
        body,
        grid=(b, n),
        in_specs=in_specs,
        out_specs=out_specs,
        out_shape=out_shape,
        scratch_shapes=[pltpu.VMEM((2, n_heads, dk, dv), F32),
                        pltpu.VMEM((slots, (1 + N_MXU_LEVELS) * CHUNK, wk), F32),
                        pltpu.VMEM((slots, CHUNK, wk), F32), pltpu.VMEM((slots, CHUNK, wk), BF16),
                        pltpu.VMEM((slots, N_LEVELS + 2, CHUNK, wk), BF16),
                        pltpu.VMEM((slots, n_heads, CHUNK, CHUNK), BF16)],
        compiler_params=pltpu.CompilerParams(dimension_semantics=("arbitrary", "arbitrary"),
                                             vmem_limit_bytes=VMEM_LIMIT),
        name="scan_hgrn" if k is None else "scan_gla",
    )(*args)


def _post_body(ohf_ref, ohb_ref, ogf_ref, ogb_ref, hgate_ref, gr_ref, sa_ref, sb_ref, x_ref, mod_ref,
               hgn_ref, glan_ref, n2_ref, wa_ref, wb_ref, wo_ref, wr_ref,
               x1_ref, rows_ref, group_ref):
    tm = x_ref.shape[0]
    for r0 in range(0, tm, POST_PART):
        _post_part(slice(r0, r0 + POST_PART), ohf_ref, ohb_ref, ogf_ref, ogb_ref, hgate_ref, gr_ref, sa_ref, sb_ref,
                   x_ref, mod_ref, hgn_ref, glan_ref, n2_ref, wa_ref, wb_ref, wo_ref, wr_ref,
                   x1_ref, rows_ref, group_ref)


def _post_part(rs, ohf_ref, ohb_ref, ogf_ref, ogb_ref, hgate_ref, gr_ref, sa_ref, sb_ref, x_ref, mod_ref,
               hgn_ref, glan_ref, n2_ref, wa_ref, wb_ref, wo_ref, wr_ref,
               x1_ref, rows_ref, group_ref):
    n = rs.stop - rs.start

    def normed(of_ref, ob_ref, g_ref, gate_ref, s):
        o = of_ref[rs, s].astype(F32) + ob_ref[rs, s].astype(F32)
        return (_rmsnorm(o, g_ref[...]) * gate_ref[rs, s].astype(F32)).astype(BF16)

    chunks = [slice(c * MXU_DEPTH, (c + 1) * MXU_DEPTH) for c in range(D_MODEL // MXU_DEPTH)]
    acc = lambda total, part: part if total is None else total + part
    t_hg = t_gla = mix = None
    for cs in chunks:
        heads = range(cs.start // HG_DV, cs.stop // HG_DV)
        y_hg = jnp.concatenate([normed(ohf_ref, ohb_ref, hgn_ref, hgate_ref, slice(h * HG_DV, (h + 1) * HG_DV))
                                for h in heads], axis=1)
        t_hg = acc(t_hg, _dot(y_hg, wa_ref[cs, :]))
        heads = range(cs.start // GLA_DV, cs.stop // GLA_DV)
        y_gla = jnp.concatenate([normed(ogf_ref, ogb_ref, glan_ref, gr_ref, slice(h * GLA_DV, (h + 1) * GLA_DV))
                                 for h in heads], axis=1)
        t_gla = acc(t_gla, _dot(y_gla, wb_ref[cs, :]))
    for cs in chunks:
        merged = sa_ref[rs, cs].astype(F32) * t_hg[:, cs] + sb_ref[rs, cs].astype(F32) * t_gla[:, cs]
        mix = acc(mix, _dot(merged.astype(BF16), wo_ref[cs, :]))
    x1 = x_ref[rs, :] + mod_ref[2:3, :] * mix
    x1_ref[rs, :] = x1
    h2 = _rmsnorm(x1, n2_ref[...]) * (1.0 + mod_ref[4:5, :]) + mod_ref[3:4, :]
    h2_hi = h2.astype(BF16)
    token_rows = lambda s: pl.ds(rs.start * ROW_SUBLANES + s, n, stride=ROW_SUBLANES)
    for s in range(D_MODEL // 128):
        rows_ref[token_rows(s), :] = h2[:, s * 128:(s + 1) * 128]
    for s in range(COMB_SUBLANE + 1, ROW_SUBLANES):
        rows_ref[token_rows(s), :] = jnp.zeros((n, 128), F32)

    h2_lo = (h2 - h2_hi.astype(F32)).astype(BF16)
    lt = (_dot(h2_hi, wr_ref[0]) + _dot(h2_hi, wr_ref[1]) + _dot(h2_lo, wr_ref[0])).T
    row = lambda r: lt[r:r + 1, :]
    top = lambda vals: functools.reduce(jnp.maximum, vals)

    def first_at(vals, m):
        idx = jnp.full(m.shape, len(vals), jnp.int32)
        for j in reversed(range(len(vals))):
            idx = jnp.where(vals[j] == m, j, idx)
        return idx

    gl = [row(g) for g in range(N_GROUPS)]
    gmax = top(gl)
    p_top = 1.0 / functools.reduce(jnp.add, [jnp.exp(v - gmax) for v in gl])
    g_idx = first_at(gl, gmax)
    sel = []
    for j in range(EXPERTS_PER_GROUP):
        v = row(EXPERT_LANE0 + j)
        for g in range(1, N_GROUPS):
            v = jnp.where(g_idx == g, row(EXPERT_LANE0 + g * EXPERTS_PER_GROUP + j), v)
        sel.append(v)
    m1 = top(sel)
    i1 = first_at(sel, m1)
    sel2 = [jnp.where(i1 == j, -jnp.inf, v) for j, v in enumerate(sel)]
    m2 = top(sel2)
    i2 = first_at(sel2, m2)
    r = jnp.exp(m2 - m1)
    w1 = p_top / (1.0 + r)
    lane0 = EXPERT_LANE0 + g_idx * EXPERTS_PER_GROUP
    rid = lax.broadcasted_iota(jnp.int32, lt.shape, 0)
    comb_t = jnp.where(rid == lane0 + i1, w1, jnp.where(rid == lane0 + i2, w1 * r, 0.0))
    rows_ref[token_rows(COMB_SUBLANE), :] = comb_t.T
    group_ref[:, rs] = g_idx


def _post(ohf, ohb, ogf, ogb, hgate, gr, sa, sb, x, mod, tiles_per_row, hgn, glan, n2, wa, wb, wo, wr):
    t = x.shape[0]
    tm = TM_POST
    const = lambda shape: pl.BlockSpec(shape, lambda i: (0,) * len(shape))
    tok = lambda w: pl.BlockSpec((tm, w), lambda i: (i, 0))
    return pl.pallas_call(
        _post_body,
        grid=(t // tm,),
        in_specs=[tok(1024)] * 9
                 + [pl.BlockSpec((None, 6, D_MODEL), lambda i: (i // tiles_per_row, 0, 0)),
                    const((1, HG_DV)), const((1, GLA_DV)), const((1, D_MODEL)),
                    const((1024, 1024)), const((1024, 1024)), const((1024, 1024)),
                    const((2, D_MODEL, ROUTER_LANES))],
        out_specs=[tok(1024), pl.BlockSpec((tm * ROW_SUBLANES, 128), lambda i: (i, 0)),
                   pl.BlockSpec((None, 1, tm), lambda i: (i, 0, 0))],
        out_shape=[jax.ShapeDtypeStruct((t, D_MODEL), F32), jax.ShapeDtypeStruct((t * ROW_SUBLANES, 128), F32),
                   jax.ShapeDtypeStruct((t // tm, 1, tm), jnp.int32)],
        compiler_params=pltpu.CompilerParams(vmem_limit_bytes=VMEM_LIMIT),
        name="post",
    )(ohf, ohb, ogf, ogb, hgate, gr, sa, sb, x, mod, hgn, glan, n2, wa, wb, wo, wr)


def _moe_tile_rows(t):
    narrow, wide = TR_MOE
    return wide if t // wide >= MOE_WIDE_MIN_TILES else narrow


def _route_plan(group, t):
    tr, blk = _moe_tile_rows(t), PLAN_BLOCK
    nt, nb = t // tr + N_GROUPS, t // blk
    i32 = jnp.int32
    gids = jnp.arange(N_GROUPS, dtype=i32)
    onehot = (group[:, None] == gids[None, :]).astype(i32).reshape(nb, blk, N_GROUPS)
    within = jnp.cumsum(onehot, axis=1)
    blk_end = jnp.cumsum(within[:, -1, :], axis=0)
    blk_start = blk_end - within[:, -1, :]
    counts = blk_end[-1]
    tiles = (counts + tr - 1) // tr
    tile_end = jnp.cumsum(tiles)
    tile_group = jnp.minimum(jnp.sum(jnp.arange(nt, dtype=i32)[:, None] >= tile_end[None, :], axis=1),
                             N_GROUPS - 1).astype(i32)
    row = jnp.arange(nt * tr, dtype=i32)
    gsel = (jnp.repeat(tile_group, tr)[:, None] == gids[None, :]).astype(i32)
    pick = lambda per_group: jnp.sum(gsel * per_group[None, :], axis=1)
    k = row - pick(tile_end - tiles) * tr
    valid = k < pick(counts)
    per_block = lambda table: jnp.sum(gsel[:, None, :] * table[None, :, :], axis=2)
    b_row = jnp.minimum(jnp.sum(per_block(blk_end) <= k[:, None], axis=1), nb - 1)
    bsel = (b_row[:, None] == jnp.arange(nb, dtype=i32)[None, :]).astype(i32)
    k_local = k - jnp.sum(bsel * per_block(blk_start), axis=1)
    sel = (bsel[:, :, None] * gsel[:, None, :]).reshape(nt * tr, nb * N_GROUPS).astype(F32)
    table = jnp.transpose(within, (0, 2, 1)).reshape(nb * N_GROUPS, blk).astype(F32)
    ranks = jnp.dot(sel, table)
    t_local = jnp.sum(ranks <= k_local[:, None].astype(F32), axis=1).astype(i32)
    row_src = jnp.where(valid, b_row * blk + t_local, 0)
    spare = t + ((row // tr) % 2) * tr + row % tr
    return tile_group, row_src, jnp.where(valid, row_src, spare), tile_end[-1:].astype(i32)


def _moe_body(tg_ref, src_ref, dst_ref, used_ref, rows_hbm, wg_ref, wu_ref, wd_ref, out_hbm, xbuf, ybuf, gsem, ssem):
    tr = xbuf.shape[1] // ROW_SUBLANES
    i = pl.program_id(0)
    n_used = used_ref[0]
    slot = i % 2

    def gather_row(tile, slot_, r):
        tok = src_ref[tile * tr + r]
        return pltpu.make_async_copy(rows_hbm.at[pl.ds(pl.multiple_of(tok * ROW_SUBLANES, ROW_SUBLANES), ROW_SUBLANES)],
                                     xbuf.at[slot_, pl.ds(r * ROW_SUBLANES, ROW_SUBLANES)], gsem.at[slot_])

    def scatter_row(tile, slot_, r):
        tok = dst_ref[tile * tr + r]
        return pltpu.make_async_copy(ybuf.at[slot_, pl.ds(r * OUT_SUBLANES, OUT_SUBLANES)],
                                     out_hbm.at[pl.ds(pl.multiple_of(tok * OUT_SUBLANES, OUT_SUBLANES), OUT_SUBLANES)],
                                     ssem.at[slot_])

    def for_rows(fn):
        def body(r, carry):
            fn(r)
            return carry
        lax.fori_loop(0, tr, body, 0, unroll=32)

    def gathered(slot_):
        return pltpu.make_async_copy(rows_hbm.at[pl.ds(0, tr * ROW_SUBLANES)], xbuf.at[slot_], gsem.at[slot_])

    def scattered(slot_):
        return pltpu.make_async_copy(ybuf.at[slot_], out_hbm.at[pl.ds(0, tr * OUT_SUBLANES)], ssem.at[slot_])

    @pl.when(i == 0)
    def _():
        for_rows(lambda r: gather_row(0, 0, r).start())
        n_tok = out_hbm.shape[0] // OUT_SUBLANES - 2 * tr
        ybuf[...] = jnp.zeros_like(ybuf)
        for s in range(2):
            spare = pltpu.make_async_copy(
                ybuf.at[s], out_hbm.at[pl.ds((n_tok + s * tr) * OUT_SUBLANES, tr * OUT_SUBLANES)], ssem.at[s])
            spare.start()
            spare.wait()

    @pl.when(i < n_used)
    def _():
        @pl.when(i + 1 < n_used)
        def _():
            for_rows(lambda r: gather_row(i + 1, 1 - slot, r).start())

        gathered(slot).wait()

        @pl.when(i >= 2)
        def _():
            scattered(slot).wait()

        x = jnp.concatenate([xbuf[slot, pl.ds(s, tr, stride=ROW_SUBLANES), :] for s in range(D_MODEL // 128)],
                            axis=1).astype(BF16)
        comb = xbuf[slot, pl.ds(COMB_SUBLANE, tr, stride=ROW_SUBLANES), :]
        lane = lax.broadcasted_iota(jnp.int32, comb.shape, 1)
        lane0 = EXPERT_LANE0 + tg_ref[i] * EXPERTS_PER_GROUP
        hid = []
        for e in range(EXPERTS_PER_GROUP):
            col = jnp.sum(jnp.where(lane == lane0 + e, comb, 0.0), axis=-1, keepdims=True)
            hid.append((_silu(_dot(x, wg_ref[e])) * _dot(x, wu_ref[e]) * col).astype(BF16))
        acc = _dot(jnp.concatenate(hid, axis=1), wd_ref[...])

        for s in range(D_MODEL // 128):
            ybuf[slot, pl.ds(s, tr, stride=OUT_SUBLANES), :] = acc[:, s * 128:(s + 1) * 128]

        for_rows(lambda r: scatter_row(i, slot, r).start())

        @pl.when(i == n_used - 1)
        def _():
            scattered(slot).wait()
            scattered(1 - slot).wait()


def _moe(rows, plan, wg, wu, wd, t):
    tr = _moe_tile_rows(t)
    assert t // tr >= 2
    nt = plan[0].shape[0]
    group_w = lambda shape: pl.BlockSpec((EXPERTS_PER_GROUP,) + shape, lambda i, tg, src, dst, used: (tg[i], 0, 0))
    grid_spec = pltpu.PrefetchScalarGridSpec(
        num_scalar_prefetch=4,
        grid=(nt,),
        in_specs=[pl.BlockSpec(memory_space=pl.ANY),
                  group_w((D_MODEL, D_EXPERT)), group_w((D_MODEL, D_EXPERT)),
                  pl.BlockSpec((None, EXPERTS_PER_GROUP * D_EXPERT, D_MODEL),
                               lambda i, tg, src, dst, used: (tg[i], 0, 0))],
        out_specs=pl.BlockSpec(memory_space=pl.ANY),
        scratch_shapes=[pltpu.VMEM((2, tr * ROW_SUBLANES, 128), F32), pltpu.VMEM((2, tr * OUT_SUBLANES, 128), F32),
                        pltpu.SemaphoreType.DMA((2,)), pltpu.SemaphoreType.DMA((2,))])
    return pl.pallas_call(
        _moe_body,
        grid_spec=grid_spec,
        out_shape=jax.ShapeDtypeStruct(((t + 2 * tr) * OUT_SUBLANES, 128), F32),
        compiler_params=pltpu.CompilerParams(dimension_semantics=("arbitrary",), vmem_limit_bytes=VMEM_LIMIT),
        name="moe",
    )(*plan, rows, wg, wu, wd)


def _final_body(x1_ref, moe_ref, mod_ref, fn_ref, y_ref):
    tm = x1_ref.shape[0]
    moe = jnp.concatenate([moe_ref[pl.ds(s, tm, stride=OUT_SUBLANES), :] for s in range(D_MODEL // 128)], axis=1)
    y_ref[...] = _rmsnorm(x1_ref[...] + mod_ref[5:6, :] * moe, fn_ref[...])


def _final(x1, moe, mod, tiles_per_row, fn):
    t = x1.shape[0]
    tm = TM_FINAL
    return pl.pallas_call(
        _final_body,
        grid=(t // tm,),
        in_specs=[pl.BlockSpec((tm, D_MODEL), lambda i: (i, 0)),
                  pl.BlockSpec((tm * OUT_SUBLANES, 128), lambda i: (i, 0)),
                  pl.BlockSpec((None, 6, D_MODEL), lambda i: (i // tiles_per_row, 0, 0)),
                  pl.BlockSpec((1, D_MODEL), lambda i: (0, 0))],
        out_specs=pl.BlockSpec((tm, D_MODEL), lambda i: (i, 0)),
        out_shape=jax.ShapeDtypeStruct((t, D_MODEL), F32),
        compiler_params=pltpu.CompilerParams(vmem_limit_bytes=VMEM_LIMIT),
        name="final",
    )(x1, moe, mod, fn)


def _trunk_layer(x, mod, s0_hg, s0_gla, emit_state, w):
    b, l, _ = x.shape
    xt = x.reshape(b * l, D_MODEL)
    per_row = lambda tm: (l // tm) if mod.shape[0] > 1 else (b * l // tm)
    (hq, gf, gb, hv, hgate, gq, gk, gv, gr, laf, lab, sa, sb) = _inproj(
        xt, mod, per_row(TM_PROJ), w["n1"], w["lbp"], w["wmain"], w["waup"], w["ba"])
    seq = lambda a: a.reshape(b, l, a.shape[-1])
    hg = _scan(seq(hq), None, seq(hv), seq(gf), seq(gb), s0_hg, w["scan_consts"], HG_H, HG_DK, HG_DV,
               emit_state)
    gla = _scan(seq(gq), seq(gk), seq(gv), seq(laf), seq(lab), s0_gla, w["scan_consts"], GLA_H, GLA_DK,
                GLA_DV, emit_state)
    flat = lambda a: a.reshape(b * l, a.shape[-1])
    x1, rows, group = _post(flat(hg[0]), flat(hg[1]), flat(gla[0]), flat(gla[1]), hgate, gr, sa, sb, xt, mod,
                           per_row(TM_POST), w["hgn"], w["glan"], w["n2"], w["wa"], w["wb"], w["wo"], w["wr"])
    plan = _route_plan(group.reshape(b * l), b * l)
    moe = _moe(rows, plan, w["wg"], w["wu"], w["wd"], b * l)
    y = _final(x1, moe, mod, per_row(TM_FINAL), w["fn"])
    states = (hg[2], gla[2]) if emit_state else (None, None)
    return y.reshape(b, l, D_MODEL), states


def kernel(x_prompt, x_sample, state_hgrn, state_gla, c, c_ctx, ada_w, ada_b, norm1_g, norm2_g, w_in,
           hg_lb_param, hg_norm_g, gla_wa_up, gla_ba, gla_norm_g, w_br_a, w_br_b, w_out,
           w_router_group, w_router_expert, w_exp_gate, w_exp_up, w_exp_down, final_norm_g):
    nb = c.shape[0]
    cond = jnp.concatenate([c, c_ctx[None, :], jnp.zeros((8 - nb - 1, D_MODEL), F32)], axis=0)
    mod = _modulation(cond, ada_w[0], ada_b).reshape(8, 6, D_MODEL)

    zeros = jnp.zeros((GLA_RANK, GLA_H * GLA_DK), F32)
    router = jnp.concatenate(
        [w_router_group[0], jnp.transpose(w_router_expert[0], (1, 0, 2)).reshape(D_MODEL, N_EXPERTS),
         jnp.zeros((D_MODEL, ROUTER_LANES - N_GROUPS - N_EXPERTS), F32)], axis=1)
    w = {
        "n1": norm1_g, "n2": norm2_g, "fn": final_norm_g[None, :], "lbp": hg_lb_param,
        "wmain": w_in[0].astype(BF16),
        "waup": jnp.concatenate([jnp.concatenate([gla_wa_up[0, 0], zeros], axis=1),
                                 jnp.concatenate([zeros, gla_wa_up[0, 1]], axis=1)], axis=0).astype(BF16),
        "ba": gla_ba[0].reshape(1, 2 * GLA_H * GLA_DK),
        "hgn": hg_norm_g, "glan": gla_norm_g,
        "wa": w_br_a[0].astype(BF16), "wb": w_br_b[0].astype(BF16), "wo": w_out[0].astype(BF16),
        "wr": jnp.stack([router.astype(BF16), (router - router.astype(BF16).astype(F32)).astype(BF16)]),
        "wg": w_exp_gate[0].astype(BF16), "wu": w_exp_up[0].astype(BF16),
        "wd": w_exp_down[0].astype(BF16).reshape(N_GROUPS, EXPERTS_PER_GROUP * D_EXPERT, D_MODEL),
        "scan_consts": _scan_constants(),
    }
    y_prompt, (st_hg, st_gla) = _trunk_layer(x_prompt, mod[nb:nb + 1], None, None, True, w)
    y_sample, _ = _trunk_layer(x_sample, mod[:nb], state_hgrn[:, 0], state_gla[:, 0], False, w)
    return y_prompt, y_sample, st_hg[:, None], st_gla[:, None]
```
